```python
import math
import jax, jax.numpy as jnp
from jax import lax
import numpy as np

D_MODEL = 1024
BATCH = 8
SEQ = 4096
DEPTH = 1

HEAD_DIM = 128
MOBA_HEADS = 4
DSA_HEADS = 4
MOBA_W = MOBA_HEADS * HEAD_DIM
DSA_W = DSA_HEADS * HEAD_DIM
D_MIX = MOBA_W + DSA_W
MOBA_BLOCK = 256
MOBA_TOPK = 3
MOBA_Q_CHUNK = 64
DSA_MAX_TOPK = 256
DSA_Q_CHUNK = 128
KV_LORA = 256
IDX_HEADS = 8
IDX_DIM = 64
ROPE_THETA = 500000.0
ROPE_FRACTION_DIV = 4
N_EXPERTS = 256
EXPERT_TOPK = 8
N_GROUPS = 8
TOPK_GROUPS = 4
D_EXPERT = 256
D_SHARED = 256
ROUTED_SCALE = 2.5
MOE_BLOCK = 128
EPS = 1e-6

IN_SIZES = (MOBA_W, MOBA_W, MOBA_W, DSA_W, KV_LORA, IDX_HEADS * IDX_DIM, IDX_DIM, IDX_HEADS)
IN_COLS = sum(IN_SIZES)
IN_SPLITS = [int(v) for v in np.cumsum(IN_SIZES)[:-1]]

kernel_name = "hybrid_moba_dsa_moe_layer"


def rmsnorm(x, g):
    xf = x.astype(jnp.float32)
    y = xf * lax.rsqrt(jnp.mean(xf * xf, axis=-1, keepdims=True) + EPS)
    return (y * g.astype(jnp.float32)).astype(x.dtype)


def rope_tables(seq, rot_dim):
    inv = jnp.float32(ROPE_THETA) ** (-(jnp.arange(0, rot_dim, 2, dtype=jnp.float32) / rot_dim))
    ang = jnp.arange(seq, dtype=jnp.float32)[:, None] * inv[None, :]
    return jnp.cos(ang), jnp.sin(ang)


def partial_rope(x, cos, sin):
    half = cos.shape[-1]
    xf = x.astype(jnp.float32)
    x1 = xf[..., :half]
    x2 = xf[..., half:2 * half]
    c = cos[None, :, None, :]
    s = sin[None, :, None, :]
    out = jnp.concatenate([x1 * c - x2 * s, x2 * c + x1 * s, xf[..., 2 * half:]], axis=-1)
    return out.astype(x.dtype)


def moba_attention(q, k, v):
    B, S, H, dh = q.shape
    nb = -(-S // MOBA_BLOCK)
    pad = nb * MOBA_BLOCK - S

    def to_blocks(t):
        t = jnp.pad(t, ((0, 0), (0, pad), (0, 0), (0, 0)))
        return t.reshape(B, nb, MOBA_BLOCK, H, dh).transpose(0, 3, 1, 2, 4)

    kb = to_blocks(k)
    vb = to_blocks(v)
    kmean = jnp.mean(kb.astype(jnp.float32), axis=3)
    n_sel = min(MOBA_TOPK, nb)
    scale = dh ** -0.5
    bi = jnp.arange(B)[:, None, None]
    hi = jnp.arange(H)[None, :, None]
    blk_ids = jnp.arange(nb)
    key_off = jnp.arange(MOBA_BLOCK)

    def chunk(ci):
        start = ci * MOBA_Q_CHUNK
        qc = lax.dynamic_slice_in_dim(q, start, MOBA_Q_CHUNK, axis=1).transpose(0, 2, 1, 3)
        qpos = start + jnp.arange(MOBA_Q_CHUNK)
        own = start // MOBA_BLOCK
        gate = jnp.einsum('bhqd,bhnd->bhqn', qc.astype(jnp.float32), kmean)
        gate = jnp.where(blk_ids < own, gate, -jnp.inf)
        _, sel = lax.top_k(gate, n_sel)
        sel_ok = sel < own
        scores = []
        for j in range(n_sel):
            k_j = kb[bi, hi, sel[..., j]]
            s_j = jnp.einsum('bhqd,bhqkd->bhqk', qc, k_j).astype(jnp.float32) * scale
            scores.append(jnp.where(sel_ok[..., j, None], s_j, -jnp.inf))
        k_own = lax.dynamic_index_in_dim(kb, own, axis=2, keepdims=False)
        v_own = lax.dynamic_index_in_dim(vb, own, axis=2, keepdims=False)
        s_own = jnp.einsum('bhqd,bhkd->bhqk', qc, k_own).astype(jnp.float32) * scale
        own_pos = own * MOBA_BLOCK + key_off
        s_own = jnp.where(own_pos[None, :] <= qpos[:, None], s_own, -jnp.inf)
        p = jax.nn.softmax(jnp.concatenate(scores + [s_own], axis=-1), axis=-1).astype(v.dtype)
        p_parts = jnp.split(p, n_sel + 1, axis=-1)
        out = jnp.einsum('bhqk,bhkd->bhqd', p_parts[-1], v_own)
        for j in range(n_sel):
            v_j = vb[bi, hi, sel[..., j]]
            out = out + jnp.einsum('bhqk,bhqkd->bhqd', p_parts[j], v_j)
        return out.transpose(0, 2, 1, 3)

    outs = lax.map(chunk, jnp.arange(S // MOBA_Q_CHUNK))
    return outs.transpose(1, 0, 2, 3, 4).reshape(B, S, H, dh)


def dsa_attention(q, k, v, q_idx, k_idx, w_idx):
    B, S, H, dh = q.shape
    topk = min(DSA_MAX_TOPK, S // 4)
    scale = dh ** -0.5
    idx_scale = IDX_DIM ** -0.5
    key_pos = jnp.arange(S)
    bi = jnp.arange(B)[:, None, None]
    kif = k_idx.astype(jnp.float32)

    def chunk(ci):
        start = ci * DSA_Q_CHUNK
        qc = lax.dynamic_slice_in_dim(q, start, DSA_Q_CHUNK, axis=1)
        qic = lax.dynamic_slice_in_dim(q_idx, start, DSA_Q_CHUNK, axis=1)
        wic = lax.dynamic_slice_in_dim(w_idx, start, DSA_Q_CHUNK, axis=1)
        qpos = start + jnp.arange(DSA_Q_CHUNK)
        logits = jnp.einsum('bqhd,bsd->bqhs', qic.astype(jnp.float32), kif) * idx_scale
        iscore = jnp.einsum('bqhs,bqh->bqs', jax.nn.relu(logits), wic.astype(jnp.float32))
        causal = key_pos[None, :] <= qpos[:, None]
        iscore = jnp.where(causal[None], iscore, -jnp.inf)
        _, sel = lax.top_k(iscore, topk)
        ok = sel <= qpos[None, :, None]
        kg = k[bi, sel]
        vg = v[bi, sel]
        s = jnp.einsum('bqhd,bqkd->bqhk', qc, kg).astype(jnp.float32) * scale
        s = jnp.where(ok[:, :, None, :], s, -jnp.inf)
        p = jax.nn.softmax(s, axis=-1).astype(v.dtype)
        return jnp.einsum('bqhk,bqkd->bqhd', p, vg)

    outs = lax.map(chunk, jnp.arange(S // DSA_Q_CHUNK))
    return outs.transpose(1, 0, 2, 3, 4).reshape(B, S, H, dh)


def moe_ffn(h, w_router, router_bias, w_gate_e, w_up_e, w_down_e, w_gate_s, w_up_s, w_down_s):
    B, S, D = h.shape
    T = B * S
    hf = h.reshape(T, D)
    scores = jax.nn.sigmoid((hf @ w_router).astype(jnp.float32))
    biased = scores + router_bias.astype(jnp.float32)
    grp = biased.reshape(T, N_GROUPS, N_EXPERTS // N_GROUPS)
    grp_score = jnp.sum(lax.top_k(grp, 2)[0], axis=-1)
    _, grp_sel = lax.top_k(grp_score, TOPK_GROUPS)
    grp_mask = jnp.sum(jax.nn.one_hot(grp_sel, N_GROUPS, dtype=jnp.float32), axis=-2) > 0
    exp_mask = jnp.repeat(grp_mask, N_EXPERTS // N_GROUPS, axis=-1)
    _, expert_idx = lax.top_k(jnp.where(exp_mask, biased, -jnp.inf), EXPERT_TOPK)
    wts = jnp.take_along_axis(scores, expert_idx, axis=-1)
    wts = wts / jnp.sum(wts, axis=-1, keepdims=True) * ROUTED_SCALE
    A = T * EXPERT_TOPK
    e_flat = expert_idx.reshape(A).astype(jnp.int32)
    g_flat = wts.reshape(A)
    order = jnp.argsort(e_flat)
    e_sorted = e_flat[order]
    counts = jnp.bincount(e_flat, length=N_EXPERTS)
    starts = jnp.cumsum(counts) - counts
    padded = (counts + MOE_BLOCK - 1) // MOE_BLOCK * MOE_BLOCK
    pad_ends = jnp.cumsum(padded)
    pad_starts = pad_ends - padded
    dest = pad_starts[e_sorted] + (jnp.arange(A) - starts[e_sorted])
    n_blocks = A // MOE_BLOCK + N_EXPERTS
    P = n_blocks * MOE_BLOCK
    tok_buf = jnp.full((P,), T, jnp.int32).at[dest].set((order // EXPERT_TOPK).astype(jnp.int32))
    gate_buf = jnp.zeros((P,), jnp.float32).at[dest].set(g_flat[order])
    block_expert = jnp.minimum(
        jnp.searchsorted(pad_ends, jnp.arange(n_blocks) * MOE_BLOCK, side='right'), N_EXPERTS - 1)
    h_pad = jnp.concatenate([hf, jnp.zeros((1, D), hf.dtype)], axis=0)

    def expert_block(args):
        e, tok, g = args
        xb = h_pad[tok]
        a = xb @ w_gate_e[e]
        u = xb @ w_up_e[e]
        ob = (jax.nn.silu(a) * u) @ w_down_e[e]
        return ob * g[:, None].astype(ob.dtype)

    out = lax.map(expert_block, (block_expert, tok_buf.reshape(n_blocks, MOE_BLOCK),
                                 gate_buf.reshape(n_blocks, MOE_BLOCK)))
    routed = jax.ops.segment_sum(out.reshape(P, D), tok_buf, num_segments=T + 1)[:T]
    shared = (jax.nn.silu(hf @ w_gate_s) * (hf @ w_up_s)) @ w_down_s
    return (routed + shared).reshape(B, S, D)


def setup_inputs(seed: int = 0) -> dict:
    key = jax.random.key(seed)
    ks = jax.random.split(key, 24)
    L = DEPTH

    def nrm(k, shape, s):
        return jax.random.normal(k, shape, jnp.float32) * s

    return {
        "x": nrm(ks[0], (BATCH, SEQ, D_MODEL), 1.0),
        "c": nrm(ks[1], (BATCH, D_MODEL), 1.0),
        "w_ada": nrm(ks[2], (L, D_MODEL, 6 * D_MODEL), 0.5 * D_MODEL ** -0.5),
        "b_ada": nrm(ks[3], (L, 6 * D_MODEL), 0.01),
        "g_mix": 1.0 + nrm(ks[4], (L, D_MODEL), 0.05),
        "w_in": nrm(ks[5], (L, D_MODEL, IN_COLS), D_MODEL ** -0.5),
        "g_kv": 1.0 + nrm(ks[6], (L, KV_LORA), 0.05),
        "w_kv_up": nrm(ks[7], (L, KV_LORA, 2 * HEAD_DIM), KV_LORA ** -0.5),
        "g_moba_out": 1.0 + nrm(ks[8], (L, MOBA_W), 0.05),
        "g_dsa_out": 1.0 + nrm(ks[9], (L, DSA_W), 0.05),
        "w_out": nrm(ks[10], (L, D_MIX, D_MODEL), D_MIX ** -0.5),
        "g_ffn": 1.0 + nrm(ks[11], (L, D_MODEL), 0.05),
        "w_router": nrm(ks[12], (L, D_MODEL, N_EXPERTS), D_MODEL ** -0.5),
        "router_bias": nrm(ks[13], (L, N_EXPERTS), 0.01),
        "w_gate_e": nrm(ks[14], (L, N_EXPERTS, D_MODEL, D_EXPERT), D_MODEL ** -0.5),
        "w_up_e": nrm(ks[15], (L, N_EXPERTS, D_MODEL, D_EXPERT), D_MODEL ** -0.5),
        "w_down_e": nrm(ks[16], (L, N_EXPERTS, D_EXPERT, D_MODEL), D_EXPERT ** -0.5),
        "w_gate_s": nrm(ks[17], (L, D_MODEL, D_SHARED), D_MODEL ** -0.5),
        "w_up_s": nrm(ks[18], (L, D_MODEL, D_SHARED), D_MODEL ** -0.5),
        "w_down_s": nrm(ks[19], (L, D_SHARED, D_MODEL), D_SHARED ** -0.5),
        "g_final": 1.0 + nrm(ks[20], (D_MODEL,), 0.05),
    }


def reference(x, c, w_ada, b_ada, g_mix, w_in, g_kv, w_kv_up, g_moba_out, g_dsa_out, w_out,
              g_ffn, w_router, router_bias, w_gate_e, w_up_e, w_down_e, w_gate_s, w_up_s,
              w_down_s, g_final):
    B, S, _ = x.shape
    cos_h, sin_h = rope_tables(S, HEAD_DIM // ROPE_FRACTION_DIV)
    cos_i, sin_i = rope_tables(S, IDX_DIM // ROPE_FRACTION_DIV)
    c_act = jax.nn.silu(c)
    for l in range(DEPTH):
        mod = c_act @ w_ada[l] + b_ada[l]
        sh1, sc1, gt1, sh2, sc2, gt2 = [m[:, None, :] for m in jnp.split(mod, 6, axis=-1)]

        h = rmsnorm(x, g_mix[l]) * (1.0 + sc1) + sh1
        proj = h @ w_in[l]
        q_m, k_m, v_m, q_d, ckv, q_i, k_i, w_i = jnp.split(proj, IN_SPLITS, axis=-1)
        q_m = partial_rope(q_m.reshape(B, S, MOBA_HEADS, HEAD_DIM), cos_h, sin_h)
        k_m = partial_rope(k_m.reshape(B, S, MOBA_HEADS, HEAD_DIM), cos_h, sin_h)
        v_m = v_m.reshape(B, S, MOBA_HEADS, HEAD_DIM)
        o_m = moba_attention(q_m, k_m, v_m).reshape(B, S, MOBA_W)
        q_d = partial_rope(q_d.reshape(B, S, DSA_HEADS, HEAD_DIM), cos_h, sin_h)
        kv = rmsnorm(ckv, g_kv[l]) @ w_kv_up[l]
        k_d, v_d = jnp.split(kv, 2, axis=-1)
        k_d = partial_rope(k_d[:, :, None, :], cos_h, sin_h)[:, :, 0, :]
        q_i = partial_rope(q_i.reshape(B, S, IDX_HEADS, IDX_DIM), cos_i, sin_i)
        k_i = partial_rope(k_i[:, :, None, :], cos_i, sin_i)[:, :, 0, :]
        w_i = w_i * (IDX_HEADS ** -0.5)
        o_d = dsa_attention(q_d, k_d, v_d, q_i, k_i, w_i).reshape(B, S, DSA_W)
        mixed = jnp.concatenate([rmsnorm(o_m, g_moba_out[l]), rmsnorm(o_d, g_dsa_out[l])], axis=-1)
        x = x + gt1 * (mixed @ w_out[l])

        h = rmsnorm(x, g_ffn[l]) * (1.0 + sc2) + sh2
        y = moe_ffn(h, w_router[l], router_bias[l], w_gate_e[l], w_up_e[l], w_down_e[l],
                    w_gate_s[l], w_up_s[l], w_down_s[l])
        x = x + gt2 * y
    return rmsnorm(x, g_final)
```

```python
import functools

import jax
import jax.numpy as jnp
from jax import lax
from jax.experimental import pallas as pl
from jax.experimental.pallas import tpu as pltpu

HEAD_DIM = 128
MOBA_HEADS = 4
DSA_HEADS = 4
MOBA_W = MOBA_HEADS * HEAD_DIM
DSA_W = DSA_HEADS * HEAD_DIM
MOBA_BLOCK = 256
MOBA_TOPK = 3
DSA_MAX_TOPK = 256
KV_LORA = 256
IDX_HEADS = 8
IDX_DIM = 64
ROPE_THETA = 500000.0
ROPE_FRACTION_DIV = 4
N_EXPERTS = 256
EXPERT_TOPK = 8
N_GROUPS = 8
TOPK_GROUPS = 4
GROUP_SIZE = N_EXPERTS // N_GROUPS
D_EXPERT = 256
D_SHARED = 256
ROUTED_SCALE = 2.5
EPS = 1e-6

LANES = 128
SUBLANES = 8
VMEM_LIMIT = 56 * 1024 * 1024

PROJ_TM = 512
DSA_TQ = 128
DSA_KC = 512
POST_TM = 256
DISP_TM = 256
EXP_BM = 256
COMB_TM = 128
NEG = -1e30
INT_MIN = -2147483648

F32 = jnp.float32
BF16 = jnp.bfloat16
I32 = jnp.int32


def _cparams(sem, **kw):
    return pltpu.CompilerParams(dimension_semantics=sem, vmem_limit_bytes=VMEM_LIMIT, **kw)


def _dot(a, b):
    return jnp.dot(a, b, preferred_element_type=F32)


def _dot_nt(a, b):
    return lax.dot_general(a, b, (((1,), (1,)), ((), ())), preferred_element_type=F32)


def _silu(x):
    return x * (1.0 / (1.0 + jnp.exp(-x)))


def _rms(x, g):
    return x * lax.rsqrt(jnp.mean(x * x, axis=-1, keepdims=True) + EPS) * g


def _ada_kernel(c_ref, w_ref, b_ref, o_ref):
    ca = _silu(c_ref[...])
    o_ref[...] = jnp.dot(ca, w_ref[...], preferred_element_type=F32,
                         precision=lax.Precision.HIGHEST) + b_ref[...]


def _ada_mod(c, w_ada, b_ada):
    B, D = c.shape
    N = w_ada.shape[1]
    tn = 1024
    return pl.pallas_call(
        _ada_kernel,
        out_shape=jax.ShapeDtypeStruct((B, N), F32),
        grid=(N // tn,),
        in_specs=[pl.BlockSpec((B, D), lambda j: (0, 0)),
                  pl.BlockSpec((D, tn), lambda j: (0, j)),
                  pl.BlockSpec((1, tn), lambda j: (0, j))],
        out_specs=pl.BlockSpec((B, tn), lambda j: (0, j)),
        compiler_params=_cparams(("arbitrary",)),
        name="ada_mod",
    )(c, w_ada, b_ada.reshape(1, N))


def _rope_tables(seq, head_dim, heads_per_vreg):
    rot = head_dim // ROPE_FRACTION_DIV
    half = rot // 2
    inv = jnp.float32(ROPE_THETA) ** (-(jnp.arange(0, rot, 2, dtype=F32) / rot))
    ang = jnp.arange(seq, dtype=F32)[:, None] * inv[None, :]
    cos, sin = jnp.cos(ang), jnp.sin(ang)
    ones = jnp.ones((seq, head_dim - rot), F32)
    zeros_h = jnp.zeros((seq, half), F32)
    zeros_r = jnp.zeros((seq, head_dim - rot), F32)
    c = jnp.concatenate([cos, cos, ones], axis=1)
    sp = jnp.concatenate([zeros_h, sin, zeros_r], axis=1)
    sm = jnp.concatenate([-sin, zeros_h, zeros_r], axis=1)
    rep = lambda t: jnp.tile(t, (1, heads_per_vreg))
    return jnp.stack([rep(c), rep(sp), rep(sm)], axis=0), half


def _rope(x, tab_ref, half):
    return (x * tab_ref[0] + pltpu.roll(x, half, 1) * tab_ref[1]
            + pltpu.roll(x, LANES - half, 1) * tab_ref[2])


_C_QM, _C_KM, _C_VM, _C_QD = 0, MOBA_W, 2 * MOBA_W, 3 * MOBA_W
_C_CKV = 3 * MOBA_W + DSA_W
_C_QI = _C_CKV + KV_LORA
_C_KI = _C_QI + IDX_HEADS * IDX_DIM
_C_END = _C_KI + LANES


def _in_proj_kernel(x_ref, mod_ref, gmix_ref, w_ref, gkv_ref, wkv_ref, tabh_ref, tabi_ref,
                    qm_ref, km_ref, vm_ref, kmean_ref, qd_ref, kd_ref, vd_ref, qi_ref, ki_ref, wi_ref,
                    *, half_h, half_i):
    tm = x_ref.shape[0]
    x = x_ref[...]
    sh1 = mod_ref[0, 0:1, :]
    sc1 = mod_ref[0, 1:2, :]
    h = (_rms(x, gmix_ref[...]) * (1.0 + sc1) + sh1).astype(BF16)

    def proj(c0, width):
        return _dot(h, w_ref[:, c0:c0 + width])

    q_scale = HEAD_DIM ** -0.5
    qm = proj(_C_QM, MOBA_W)
    km = proj(_C_KM, MOBA_W)
    for hd in range(MOBA_HEADS):
        sl = slice(hd * HEAD_DIM, (hd + 1) * HEAD_DIM)
        qm_ref[:, sl] = (_rope(qm[:, sl], tabh_ref, half_h) * q_scale).astype(BF16)
        kr = _rope(km[:, sl], tabh_ref, half_h)
        km_ref[:, sl] = kr.astype(BF16)
        for blk in range(tm // MOBA_BLOCK):
            kmean_ref[blk:blk + 1, sl] = jnp.mean(kr[blk * MOBA_BLOCK:(blk + 1) * MOBA_BLOCK], axis=0,
                                                  keepdims=True)
    vm_ref[...] = proj(_C_VM, MOBA_W).astype(BF16)
    qd = proj(_C_QD, DSA_W)
    for hd in range(DSA_HEADS):
        sl = slice(hd * HEAD_DIM, (hd + 1) * HEAD_DIM)
        qd_ref[:, sl] = (_rope(qd[:, sl], tabh_ref, half_h) * q_scale).astype(BF16)
    ckv = proj(_C_CKV, KV_LORA)
    kv = _dot(_rms(ckv, gkv_ref[...]).astype(BF16), wkv_ref[...])
    kd_ref[...] = _rope(kv[:, :HEAD_DIM], tabh_ref, half_h).astype(BF16)
    vd_ref[...] = kv[:, HEAD_DIM:].astype(BF16)
    qi = proj(_C_QI, IDX_HEADS * IDX_DIM)
    for j in range(IDX_HEADS * IDX_DIM // LANES):
        sl = slice(j * LANES, (j + 1) * LANES)
        qi_ref[:, sl] = _rope(qi[:, sl], tabi_ref, half_i).astype(BF16)
    kw = proj(_C_KI, LANES)
    ki_ref[...] = _rope(kw, tabi_ref, half_i)[:, :IDX_DIM].astype(BF16)
    wi_ref[...] = kw[:, IDX_DIM:IDX_DIM + IDX_HEADS] * (IDX_HEADS ** -0.5 * IDX_DIM ** -0.5)


def _in_proj(x2, mod3, g_mix, w_in_p, g_kv, w_kv_up, tab_h, half_h, tab_i, half_i, S):
    T, D = x2.shape
    tm = PROJ_TM
    nt_per_seq = S // tm
    row = lambda i: (i, 0)
    outs = [
        jax.ShapeDtypeStruct((T, MOBA_W), BF16),
        jax.ShapeDtypeStruct((T, MOBA_W), BF16),
        jax.ShapeDtypeStruct((T, MOBA_W), BF16),
        jax.ShapeDtypeStruct((T // MOBA_BLOCK, MOBA_W), F32),
        jax.ShapeDtypeStruct((T, DSA_W), BF16),
        jax.ShapeDtypeStruct((T, HEAD_DIM), BF16),
        jax.ShapeDtypeStruct((T, HEAD_DIM), BF16),
        jax.ShapeDtypeStruct((T, IDX_HEADS * IDX_DIM), BF16),
        jax.ShapeDtypeStruct((T, IDX_DIM), BF16),
        jax.ShapeDtypeStruct((T, IDX_HEADS), F32),
    ]
    nb = tm // MOBA_BLOCK
    out_specs = [
        pl.BlockSpec((tm, MOBA_W), row), pl.BlockSpec((tm, MOBA_W), row), pl.BlockSpec((tm, MOBA_W), row),
        pl.BlockSpec((None, nb, MOBA_W), lambda i: (i, 0, 0)),
        pl.BlockSpec((tm, DSA_W), row), pl.BlockSpec((tm, HEAD_DIM), row), pl.BlockSpec((tm, HEAD_DIM), row),
        pl.BlockSpec((tm, IDX_HEADS * IDX_DIM), row), pl.BlockSpec((tm, IDX_DIM), row),
        pl.BlockSpec((tm, IDX_HEADS), row),
    ]
    outs[3] = jax.ShapeDtypeStruct((T // tm, nb, MOBA_W), F32)
    res = pl.pallas_call(
        functools.partial(_in_proj_kernel, half_h=half_h, half_i=half_i),
        out_shape=outs,
        grid=(T // tm,),
        in_specs=[
            pl.BlockSpec((tm, D), row),
            pl.BlockSpec((1, 6, D), lambda i: (i // nt_per_seq, 0, 0)),
            pl.BlockSpec((1, D), lambda i: (0, 0)),
            pl.BlockSpec((D, _C_END), lambda i: (0, 0)),
            pl.BlockSpec((1, KV_LORA), lambda i: (0, 0)),
            pl.BlockSpec((KV_LORA, 2 * HEAD_DIM), lambda i: (0, 0)),
            pl.BlockSpec((3, tm, LANES), lambda i: (0, i % nt_per_seq, 0)),
            pl.BlockSpec((3, tm, LANES), lambda i: (0, i % nt_per_seq, 0)),
        ],
        out_specs=out_specs,
        compiler_params=_cparams(("parallel",)),
        name="in_proj",
    )(x2, mod3, g_mix, w_in_p, g_kv, w_kv_up, tab_h, tab_i)
    res = list(res)
    res[3] = res[3].reshape(T // MOBA_BLOCK, MOBA_W)
    return res


def _moba_kernel(q_ref, k_ref, v_ref, kmean_ref, o_ref, m_sc, l_sc, acc_sc):
    qi = pl.program_id(2)
    blk = MOBA_BLOCK
    nb = kmean_ref.shape[0]
    q = q_ref[...]

    gate = _dot_nt(q.astype(F32), kmean_ref[...])
    col = lax.broadcasted_iota(I32, (blk, nb), 1)
    past = col < qi
    gate = jnp.where(past, gate, -jnp.inf)
    rank = jnp.zeros((blk, nb), I32)
    for m in range(nb):
        gm = gate[:, m:m + 1]
        beats = (gm > gate) | ((gm == gate) & (m < col))
        rank = rank + jnp.where(beats & (m < qi), 1, 0)
    sel = jnp.where(past & (rank < MOBA_TOPK), 1.0, 0.0)

    start = pl.multiple_of(qi * blk, blk)
    k_own = k_ref[pl.ds(start, blk), :]
    v_own = v_ref[pl.ds(start, blk), :]
    s = _dot_nt(q, k_own)
    r_io = lax.broadcasted_iota(I32, (blk, blk), 0)
    c_io = lax.broadcasted_iota(I32, (blk, blk), 1)
    s = jnp.where(c_io <= r_io, s, NEG)
    m0 = jnp.max(s, axis=1, keepdims=True)
    p = jnp.exp(s - m0)
    m_sc[...] = m0
    l_sc[...] = jnp.sum(p, axis=1, keepdims=True)
    acc_sc[...] = _dot(p.astype(BF16), v_own)

    def body(n, carry):
        st = pl.multiple_of(n * blk, blk)
        kb = k_ref[pl.ds(st, blk), :]
        vb = v_ref[pl.ds(st, blk), :]
        seln = jnp.sum(jnp.where(col == n, sel, 0.0), axis=1, keepdims=True)
        on = seln > 0.5
        sb = jnp.where(on, _dot_nt(q, kb), NEG)
        m_old = m_sc[...]
        m_new = jnp.maximum(m_old, jnp.max(sb, axis=1, keepdims=True))
        alpha = jnp.exp(m_old - m_new)
        pb = jnp.where(on, jnp.exp(sb - m_new), 0.0)
        l_sc[...] = alpha * l_sc[...] + jnp.sum(pb, axis=1, keepdims=True)
        acc_sc[...] = alpha * acc_sc[...] + _dot(pb.astype(BF16), vb)
        m_sc[...] = m_new
        return carry

    lax.fori_loop(0, qi, body, 0)
    o_ref[...] = acc_sc[...] / l_sc[...]


def _moba_attn(q_m, k_m, v_m, kmean, B, S):
    T = B * S
    blk = MOBA_BLOCK
    nq = S // blk
    return pl.pallas_call(
        _moba_kernel,
        out_shape=jax.ShapeDtypeStruct((T, MOBA_W), F32),
        grid=(B, MOBA_HEADS, nq),
        in_specs=[
            pl.BlockSpec((blk, HEAD_DIM), lambda b, h, i: (b * nq + i, h)),
            pl.BlockSpec((S, HEAD_DIM), lambda b, h, i: (b, h)),
            pl.BlockSpec((S, HEAD_DIM), lambda b, h, i: (b, h)),
            pl.BlockSpec((nq, HEAD_DIM), lambda b, h, i: (b, h)),
        ],
        out_specs=pl.BlockSpec((blk, HEAD_DIM), lambda b, h, i: (b * nq + i, h)),
        scratch_shapes=[pltpu.VMEM((blk, 1), F32), pltpu.VMEM((blk, 1), F32),
                        pltpu.VMEM((blk, HEAD_DIM), F32)],
        compiler_params=_cparams(("parallel", "parallel", "arbitrary")),
        name="moba_attn",
    )(q_m, k_m, v_m, kmean)


def _sortable_key(x):
    b = pltpu.bitcast(x, I32)
    return jnp.where(b >= 0, b, b ^ jnp.int32(0x7FFFFFFF))


def _dsa_kernel(qi_ref, wi_ref, qd_ref, ki_ref, kd_ref, vd_ref, o_ref, key_sc, m_sc, l_sc, acc_sc, *, topk):
    t = pl.program_id(1)
    tq, kc = DSA_TQ, DSA_KC
    q0 = t * tq
    n_chunks = (q0 + tq + kc - 1) // kc
    row_pos = q0 + lax.broadcasted_iota(I32, (tq, kc), 0)
    lane_io = lax.broadcasted_iota(I32, (tq, kc), 1)
    w = wi_ref[...]

    def score_chunk(c, carry):
        k0 = pl.multiple_of(c * kc, kc)
        kic = ki_ref[pl.ds(k0, kc), :]
        acc = jnp.zeros((tq, kc), F32)
        for hd in range(IDX_HEADS):
            lg = _dot_nt(qi_ref[:, hd * IDX_DIM:(hd + 1) * IDX_DIM], kic)
            acc = acc + jnp.maximum(lg, 0.0) * w[:, hd:hd + 1]
        acc = jnp.where(acc == 0.0, 0.0, acc)
        key = _sortable_key(acc)
        key = jnp.where(k0 + lane_io <= row_pos, key, INT_MIN)
        key_sc[:, pl.ds(k0, kc)] = key
        return carry

    lax.fori_loop(0, n_chunks, score_chunk, 0)

    def count(pred_fn):
        def cbody(c, acc):
            k0 = pl.multiple_of(c * kc, kc)
            kk = key_sc[:, pl.ds(k0, kc)]
            hit = jnp.where(pred_fn(kk, k0), 1, 0)
            part = hit[:, 0:LANES]
            for j in range(1, kc // LANES):
                part = part + hit[:, j * LANES:(j + 1) * LANES]
            return acc + part
        acc = lax.fori_loop(0, n_chunks, cbody, jnp.zeros((tq, LANES), I32))
        return jnp.sum(acc, axis=1, keepdims=True)

    c_nonneg = count(lambda kk, k0: kk >= 0)
    thr0 = jnp.where(c_nonneg >= topk, 0, INT_MIN).astype(I32)

    def bit_step(i, thr):
        cand = thr | lax.shift_left(jnp.int32(1), 30 - i)
        cnt = count(lambda kk, k0: kk >= cand)
        return jnp.where(cnt >= topk, cand, thr)

    thr = lax.fori_loop(0, 31, bit_step, thr0)

    n_gt = count(lambda kk, k0: kk > thr)
    n_ge = count(lambda kk, k0: kk >= thr)
    need = topk - n_gt
    overflow = (n_ge > topk) & (thr != INT_MIN)
    S = key_sc.shape[1]
    nbits = max(1, (S - 1).bit_length())

    def cut_search():
        def step(i, lo):
            bit = lax.shift_left(jnp.int32(1), nbits - 1 - i)
            cand = lo | bit
            cnt = count(lambda kk, k0: (kk == thr) & (k0 + lane_io < cand))
            return jnp.where(cnt >= need, lo, cand)
        return lax.fori_loop(0, nbits, step, jnp.zeros((tq, 1), I32))

    any_overflow = jnp.max(jnp.where(overflow, 1, 0)) > 0
    jcut = lax.cond(any_overflow, cut_search, lambda: jnp.zeros((tq, 1), I32))
    jcut = jnp.where(overflow, jcut, S)

    m_sc[...] = jnp.full(m_sc.shape, NEG, F32)
    l_sc[...] = jnp.zeros(l_sc.shape, F32)
    acc_sc[...] = jnp.zeros(acc_sc.shape, F32)

    def attn_chunk(c, carry):
        k0 = pl.multiple_of(c * kc, kc)
        kk = key_sc[:, pl.ds(k0, kc)]
        pos = k0 + lane_io
        on = ((kk > thr) | ((kk == thr) & (pos <= jcut))) & (kk != INT_MIN)
        kdc = kd_ref[pl.ds(k0, kc), :]
        vdc = vd_ref[pl.ds(k0, kc), :]
        for hd in range(DSA_HEADS):
            s = _dot_nt(qd_ref[:, hd * HEAD_DIM:(hd + 1) * HEAD_DIM], kdc)
            s = jnp.where(on, s, NEG)
            m_old = m_sc[hd]
            m_new = jnp.maximum(m_old, jnp.max(s, axis=1, keepdims=True))
            alpha = jnp.exp(m_old - m_new)
            p = jnp.where(on, jnp.exp(s - m_new), 0.0)
            l_sc[hd] = alpha * l_sc[hd] + jnp.sum(p, axis=1, keepdims=True)
            acc_sc[hd] = alpha * acc_sc[hd] + _dot(p.astype(BF16), vdc)
            m_sc[hd] = m_new
        return carry

    lax.fori_loop(0, n_chunks, attn_chunk, 0)
    for hd in range(DSA_HEADS):
        o_ref[:, hd * HEAD_DIM:(hd + 1) * HEAD_DIM] = acc_sc[hd] / l_sc[hd]


def _dsa_attn(q_i, w_i, q_d, k_i, k_d, v_d, B, S):
    T = B * S
    tq = DSA_TQ
    nq = S // tq
    topk = min(DSA_MAX_TOPK, S // 4)
    qrow = lambda b, t: (b * nq + t, 0)
    seq = lambda b, t: (b, 0)
    return pl.pallas_call(
        functools.partial(_dsa_kernel, topk=topk),
        out_shape=jax.ShapeDtypeStruct((T, DSA_W), F32),
        grid=(B, nq),
        in_specs=[
            pl.BlockSpec((tq, IDX_HEADS * IDX_DIM), qrow),
            pl.BlockSpec((tq, IDX_HEADS), qrow),
            pl.BlockSpec((tq, DSA_W), qrow),
            pl.BlockSpec((S, IDX_DIM), seq),
            pl.BlockSpec((S, HEAD_DIM), seq),
            pl.BlockSpec((S, HEAD_DIM), seq),
        ],
        out_specs=pl.BlockSpec((tq, DSA_W), qrow),
        scratch_shapes=[pltpu.VMEM((tq, S), I32),
                        pltpu.VMEM((DSA_HEADS, tq, 1), F32), pltpu.VMEM((DSA_HEADS, tq, 1), F32),
                        pltpu.VMEM((DSA_HEADS, tq, HEAD_DIM), F32)],
        compiler_params=_cparams(("parallel", "arbitrary")),
        name="dsa_attn",
    )(q_i, w_i, q_d, k_i, k_d, v_d)


def _first_index_of_max(v, row_io, n_rows):
    m = jnp.max(v, axis=0, keepdims=True)
    idx = jnp.min(jnp.where(v == m, row_io, n_rows), axis=0, keepdims=True)
    return m, idx


def _post_kernel(x_ref, om_ref, od_ref, mod_ref, gm_ref, gd_ref, wout_ref, gffn_ref, wgu_ref, wds_ref,
                 wrh_ref, wrl_ref, rb_ref, tri_ref,
                 xpart_ref, h2r_ref, eidx_ref, rank_ref, gate_ref, cnt_ref, base_sc):
    i = pl.program_id(0)
    tm = x_ref.shape[0]
    gt1 = mod_ref[0, 2:3, :]
    sh2 = mod_ref[0, 3:4, :]
    sc2 = mod_ref[0, 4:5, :]
    gt2 = mod_ref[0, 5:6, :]

    mixed = jnp.concatenate([_rms(om_ref[...], gm_ref[...]), _rms(od_ref[...], gd_ref[...])], axis=1)
    x1 = x_ref[...] + gt1 * _dot(mixed.astype(BF16), wout_ref[...])
    h2 = _rms(x1, gffn_ref[...]) * (1.0 + sc2) + sh2
    h2b = h2.astype(BF16)

    au = _dot(h2b, wgu_ref[...])
    hs = (_silu(au[:, :D_SHARED]) * au[:, D_SHARED:]).astype(BF16)
    xpart_ref[...] = x1 + gt2 * _dot(hs, wds_ref[...])

    for c in range(h2.shape[1] // LANES):
        h2r_ref[pl.ds(c, tm, stride=SUBLANES), :] = h2[:, c * LANES:(c + 1) * LANES]

    h2lo = (h2 - h2b.astype(F32)).astype(BF16)
    logits = _dot_nt(wrh_ref[...], h2b) + _dot_nt(wrl_ref[...], h2b) + _dot_nt(wrh_ref[...], h2lo)
    scores = 1.0 / (1.0 + jnp.exp(-logits))
    biased = scores + rb_ref[...]

    g_io = lax.broadcasted_iota(I32, (GROUP_SIZE, tm), 0)
    gs_rows = []
    for g in range(N_GROUPS):
        blk = biased[g * GROUP_SIZE:(g + 1) * GROUP_SIZE, :]
        m1, i1 = _first_index_of_max(blk, g_io, GROUP_SIZE)
        m2 = jnp.max(jnp.where(g_io == i1, -jnp.inf, blk), axis=0, keepdims=True)
        gs_rows.append(m1 + m2)
    gs = jnp.concatenate(gs_rows, axis=0)
    gi = lax.broadcasted_iota(I32, (N_GROUPS, tm), 0)
    grank = jnp.zeros((N_GROUPS, tm), I32)
    for m in range(N_GROUPS):
        gm = gs[m:m + 1, :]
        grank = grank + jnp.where((gm > gs) | ((gm == gs) & (m < gi)), 1, 0)
    gsel = grank < TOPK_GROUPS
    masked = jnp.concatenate(
        [jnp.where(gsel[g:g + 1, :], biased[g * GROUP_SIZE:(g + 1) * GROUP_SIZE, :], -jnp.inf)
         for g in range(N_GROUPS)], axis=0)

    e_io = lax.broadcasted_iota(I32, (N_EXPERTS, tm), 0)
    e_rows, s_rows = [], []
    for _ in range(EXPERT_TOPK):
        _, idx = _first_index_of_max(masked, e_io, N_EXPERTS)
        hit = e_io == idx
        e_rows.append(idx)
        s_rows.append(jnp.sum(jnp.where(hit, scores, 0.0), axis=0, keepdims=True))
        masked = jnp.where(hit, -jnp.inf, masked)
    eidx = jnp.concatenate(e_rows, axis=0)
    sk = jnp.concatenate(s_rows, axis=0)
    gate_ref[...] = sk / jnp.sum(sk, axis=0, keepdims=True) * ROUTED_SCALE
    eidx_ref[...] = eidx

    @pl.when(i == 0)
    def _():
        base_sc[...] = jnp.zeros(base_sc.shape, F32)

    chosen = jnp.zeros((N_EXPERTS, tm), F32)
    for k in range(EXPERT_TOPK):
        chosen = chosen + jnp.where(e_io == e_rows[k], 1.0, 0.0)
    incl = _dot(chosen.astype(BF16), tri_ref[...])
    pos = base_sc[...] + incl - 1.0
    rank_ref[...] = jnp.concatenate(
        [jnp.sum(jnp.where(e_io == e_rows[k], pos, 0.0), axis=0, keepdims=True)
         for k in range(EXPERT_TOPK)], axis=0).astype(I32)
    base_sc[...] = base_sc[...] + incl[:, tm - 1:tm]
    cnt_ref[...] = jnp.broadcast_to(base_sc[...], cnt_ref.shape)


def _post_attn(x2, o_m, o_d, mod3, g_moba, g_dsa, w_out, g_ffn, w_gu_s, w_down_s, wr_hi, wr_lo, rbias, S):
    T, D = x2.shape
    tm = POST_TM
    nt_per_seq = S // tm
    row = lambda i: (i, 0)
    full = lambda i: (0, 0)
    tri = (jnp.arange(tm)[:, None] <= jnp.arange(tm)[None, :]).astype(BF16)
    n_chunk = D // LANES
    return pl.pallas_call(
        _post_kernel,
        out_shape=[
            jax.ShapeDtypeStruct((T, D), F32),
            jax.ShapeDtypeStruct((T * n_chunk, LANES), F32),
            jax.ShapeDtypeStruct((EXPERT_TOPK, T), I32),
            jax.ShapeDtypeStruct((EXPERT_TOPK, T), I32),
            jax.ShapeDtypeStruct((EXPERT_TOPK, T), F32),
            jax.ShapeDtypeStruct((N_EXPERTS, LANES), F32),
        ],
        grid=(T // tm,),
        in_specs=[
            pl.BlockSpec((tm, D), row),
            pl.BlockSpec((tm, MOBA_W), row),
            pl.BlockSpec((tm, DSA_W), row),
            pl.BlockSpec((1, 6, D), lambda i: (i // nt_per_seq, 0, 0)),
            pl.BlockSpec((1, MOBA_W), full),
            pl.BlockSpec((1, DSA_W), full),
            pl.BlockSpec(w_out.shape, full),
            pl.BlockSpec((1, D), full),
            pl.BlockSpec(w_gu_s.shape, full),
            pl.BlockSpec(w_down_s.shape, full),
            pl.BlockSpec(wr_hi.shape, full),
            pl.BlockSpec(wr_lo.shape, full),
            pl.BlockSpec((N_EXPERTS, 1), full),
            pl.BlockSpec((tm, tm), full),
        ],
        out_specs=[
            pl.BlockSpec((tm, D), row),
            pl.BlockSpec((tm * n_chunk, LANES), row),
            pl.BlockSpec((EXPERT_TOPK, tm), lambda i: (0, i)),
            pl.BlockSpec((EXPERT_TOPK, tm), lambda i: (0, i)),
            pl.BlockSpec((EXPERT_TOPK, tm), lambda i: (0, i)),
            pl.BlockSpec((N_EXPERTS, LANES), full),
        ],
        scratch_shapes=[pltpu.VMEM((N_EXPERTS, 1), F32)],
        compiler_params=_cparams(("arbitrary",)),
        name="post_attn",
    )(x2, o_m, o_d, mod3, g_moba, g_dsa, w_out, g_ffn, w_gu_s, w_down_s, wr_hi, wr_lo, rbias, tri)


def _row_copy_wait(ref_hbm, n_rows, sem):
    blk = ref_hbm.at[pl.ds(0, n_rows * SUBLANES), :]
    pltpu.make_async_copy(blk, blk, sem).wait()


def _dispatch_kernel(ps_ref, e_ref, r_ref, h2r_ref, xs_ref, sem):
    tm = e_ref.shape[1]

    def body(t, carry):
        src = h2r_ref.at[pl.ds(pl.multiple_of(t * SUBLANES, SUBLANES), SUBLANES), :]
        for k in range(EXPERT_TOPK):
            d = ps_ref[e_ref[k, t]] + r_ref[k, t]
            dst = xs_ref.at[pl.ds(pl.multiple_of(d * SUBLANES, SUBLANES), SUBLANES), :]
            pltpu.make_async_copy(src, dst, sem).start()
        return carry

    lax.fori_loop(0, tm, body, 0)
    _row_copy_wait(xs_ref, tm * EXPERT_TOPK, sem)


def _dispatch(pad_starts, eidx, rank, h2r, n_rows_padded):
    T = eidx.shape[1]
    tm = DISP_TM
    n_chunk = h2r.shape[0] // T
    return pl.pallas_call(
        _dispatch_kernel,
        out_shape=jax.ShapeDtypeStruct((n_rows_padded * n_chunk, LANES), F32),
        grid_spec=pltpu.PrefetchScalarGridSpec(
            num_scalar_prefetch=1,
            grid=(T // tm,),
            in_specs=[
                pl.BlockSpec((EXPERT_TOPK, tm), lambda i, ps: (0, i), memory_space=pltpu.SMEM),
                pl.BlockSpec((EXPERT_TOPK, tm), lambda i, ps: (0, i), memory_space=pltpu.SMEM),
                pl.BlockSpec((tm * n_chunk, LANES), lambda i, ps: (i, 0)),
            ],
            out_specs=pl.BlockSpec(memory_space=pl.ANY),
            scratch_shapes=[pltpu.SemaphoreType.DMA],
        ),
        compiler_params=_cparams(("arbitrary",), disable_bounds_checks=True),
        name="dispatch",
    )(pad_starts, eidx, rank, h2r)


def _expert_kernel(be_ref, nv_ref, xs_ref, wg_ref, wu_ref, wd_ref, ys_ref):
    j = pl.program_id(0)
    bm = xs_ref.shape[0] // SUBLANES
    nv = nv_ref[j]

    @pl.when(nv > 0)
    def _():
        n_chunk = wg_ref.shape[0] // LANES
        x = jnp.concatenate([xs_ref[pl.ds(c, bm, stride=SUBLANES), :] for c in range(n_chunk)], axis=1)
        valid = lax.broadcasted_iota(I32, (bm, 1), 0) < nv
        xb = jnp.where(valid, x, 0.0).astype(BF16)
        a = _dot(xb, wg_ref[...].astype(BF16))
        u = _dot(xb, wu_ref[...].astype(BF16))
        hmid = (_silu(a) * u).astype(BF16)
        ob = _dot(hmid, wd_ref[...].astype(BF16))
        for c in range(n_chunk):
            ys_ref[pl.ds(c, bm, stride=SUBLANES), :] = ob[:, c * LANES:(c + 1) * LANES]

    @pl.when(nv == 0)
    def _():
        ys_ref[...] = jnp.zeros(ys_ref.shape, F32)


def _experts(block_expert, n_valid, xs, w_gate_e, w_up_e, w_down_e):
    n_blocks = block_expert.shape[0]
    E, D, DE = w_gate_e.shape
    bm = EXP_BM
    n_chunk = D // LANES
    return pl.pallas_call(
        _expert_kernel,
        out_shape=jax.ShapeDtypeStruct(xs.shape, F32),
        grid_spec=pltpu.PrefetchScalarGridSpec(
            num_scalar_prefetch=2,
            grid=(n_blocks,),
            in_specs=[
                pl.BlockSpec((bm * n_chunk, LANES), lambda j, be, nv: (j, 0)),
                pl.BlockSpec((None, D, DE), lambda j, be, nv: (be[j], 0, 0)),
                pl.BlockSpec((None, D, DE), lambda j, be, nv: (be[j], 0, 0)),
                pl.BlockSpec((None, DE, D), lambda j, be, nv: (be[j], 0, 0)),
            ],
            out_specs=pl.BlockSpec((bm * n_chunk, LANES), lambda j, be, nv: (j, 0)),
        ),
        compiler_params=_cparams(("arbitrary",)),
        name="experts",
    )(block_expert, n_valid, xs, w_gate_e, w_up_e, w_down_e)


def _combine_kernel(ps_ref, e_ref, r_ref, g_ref, xpart_ref, mod_ref, gfin_ref, ys_ref, o_ref, buf, sem):
    tm = xpart_ref.shape[0]
    n_chunk = xpart_ref.shape[1] // LANES

    def body(t, carry):
        for k in range(EXPERT_TOPK):
            d = ps_ref[e_ref[k, t]] + r_ref[k, t]
            src = ys_ref.at[pl.ds(pl.multiple_of(d * SUBLANES, SUBLANES), SUBLANES), :]
            dst = buf.at[pl.ds(pl.multiple_of((k * tm + t) * SUBLANES, SUBLANES), SUBLANES), :]
            pltpu.make_async_copy(src, dst, sem).start()
        return carry

    lax.fori_loop(0, tm, body, 0)
    _row_copy_wait(ys_ref, tm * EXPERT_TOPK, sem)

    gt2 = mod_ref[0, 5:6, :]
    g = g_ref[...]
    cols = []
    for c in range(n_chunk):
        acc = jnp.zeros((tm, LANES), F32)
        for k in range(EXPERT_TOPK):
            acc = acc + g[:, k:k + 1] * buf[pl.ds(k * tm * SUBLANES + c, tm, stride=SUBLANES), :]
        cols.append(acc)
    routed = jnp.concatenate(cols, axis=1)
    o_ref[...] = _rms(xpart_ref[...] + gt2 * routed, gfin_ref[...])


def _combine(pad_starts, eidx, rank, gates_t, xpart, mod3, g_final, ys, S):
    T, D = xpart.shape
    tm = COMB_TM
    nt_per_seq = S // tm
    n_chunk = D // LANES
    return pl.pallas_call(
        _combine_kernel,
        out_shape=jax.ShapeDtypeStruct((T, D), F32),
        grid_spec=pltpu.PrefetchScalarGridSpec(
            num_scalar_prefetch=1,
            grid=(T // tm,),
            in_specs=[
                pl.BlockSpec((EXPERT_TOPK, tm), lambda i, ps: (0, i), memory_space=pltpu.SMEM),
                pl.BlockSpec((EXPERT_TOPK, tm), lambda i, ps: (0, i), memory_space=pltpu.SMEM),
                pl.BlockSpec((tm, EXPERT_TOPK), lambda i, ps: (i, 0)),
                pl.BlockSpec((tm, D), lambda i, ps: (i, 0)),
                pl.BlockSpec((1, 6, D), lambda i, ps: (i // nt_per_seq, 0, 0)),
                pl.BlockSpec((1, D), lambda i, ps: (0, 0)),
                pl.BlockSpec(memory_space=pl.ANY),
            ],
            out_specs=pl.BlockSpec((tm, D), lambda i, ps: (i, 0)),
            scratch_shapes=[pltpu.VMEM((EXPERT_TOPK * tm * n_chunk, LANES), F32), pltpu.SemaphoreType.DMA],
        ),
        compiler_params=_cparams(("arbitrary",), disable_bounds_checks=True),
        name="combine",
    )(pad_starts, eidx, rank, gates_t, xpart, mod3, g_final, ys)


def _layer(x2, mod3, S, g_mix, w_in, g_kv, w_kv_up, g_moba_out, g_dsa_out, w_out, g_ffn, w_router,
           router_bias, w_gate_e, w_up_e, w_down_e, w_gate_s, w_up_s, w_down_s, g_final, tab_h, half_h,
           tab_i, half_i):
    T, D = x2.shape
    B = T // S
    w_in_p = jnp.pad(w_in, ((0, 0), (0, _C_END - w_in.shape[1]))).astype(BF16)
    (q_m, k_m, v_m, kmean, q_d, k_d, v_d, q_i, k_i, w_i) = _in_proj(
        x2, mod3, g_mix.reshape(1, D), w_in_p, g_kv.reshape(1, KV_LORA), w_kv_up.astype(BF16),
        tab_h, half_h, tab_i, half_i, S)
    o_m = _moba_attn(q_m, k_m, v_m, kmean, B, S)
    o_d = _dsa_attn(q_i, w_i, q_d, k_i, k_d, v_d, B, S)

    wr_t = w_router.T
    wr_hi = wr_t.astype(BF16)
    wr_lo = (wr_t - wr_hi.astype(F32)).astype(BF16)
    w_gu_s = jnp.concatenate([w_gate_s, w_up_s], axis=1).astype(BF16)
    xpart, h2r, eidx, rank, gates, cnt = _post_attn(
        x2, o_m, o_d, mod3, g_moba_out.reshape(1, MOBA_W), g_dsa_out.reshape(1, DSA_W), w_out.astype(BF16),
        g_ffn.reshape(1, D), w_gu_s, w_down_s.astype(BF16), wr_hi, wr_lo,
        router_bias.reshape(N_EXPERTS, 1), S)

    bm = EXP_BM
    n_blocks = T * EXPERT_TOPK // bm + N_EXPERTS
    counts = cnt[:, 0].astype(I32)
    padded = (counts + bm - 1) // bm * bm
    pad_ends = jnp.cumsum(padded)
    pad_starts = (pad_ends - padded).astype(I32)
    blk_start = jnp.arange(n_blocks, dtype=I32) * bm
    block_expert = jnp.minimum(jnp.searchsorted(pad_ends, blk_start, side='right'), N_EXPERTS - 1).astype(I32)
    n_valid = jnp.clip(counts[block_expert] - (blk_start - pad_starts[block_expert]), 0, bm).astype(I32)
    n_valid = jnp.where(blk_start < pad_ends[-1], n_valid, 0)

    xs = _dispatch(pad_starts, eidx, rank, h2r, n_blocks * bm)
    ys = _experts(block_expert, n_valid, xs, w_gate_e, w_up_e, w_down_e)
    return _combine(pad_starts, eidx, rank, gates.T, xpart, mod3, g_final.reshape(1, D), ys, S)


def kernel(x, c, w_ada, b_ada, g_mix, w_in, g_kv, w_kv_up, g_moba_out, g_dsa_out, w_out, g_ffn, w_router,
           router_bias, w_gate_e, w_up_e, w_down_e, w_gate_s, w_up_s, w_down_s, g_final):
    B, S, D = x.shape
    depth = w_ada.shape[0]
    assert depth == 1, "the final norm is fused into the single layer"
    assert S % PROJ_TM == 0 and S % DSA_KC == 0 and S % POST_TM == 0 and S >= 4 * DSA_MAX_TOPK
    tab_h, half_h = _rope_tables(S, HEAD_DIM, 1)
    tab_i, half_i = _rope_tables(S, IDX_DIM, LANES // IDX_DIM)
    x2 = x.reshape(B * S, D)
    sq = lambda a: a.reshape(a.shape[1:])
    mod3 = _ada_mod(c, sq(w_ada), sq(b_ada)).reshape(B, 6, D)
    out = _layer(x2, mod3, S, sq(g_mix), sq(w_in), sq(g_kv), sq(w_kv_up), sq(g_moba_out), sq(g_dsa_out),
                 sq(w_out), sq(g_ffn), sq(w_router), sq(router_bias), sq(w_gate_e), sq(w_up_e), sq(w_down_e),
                 sq(w_gate_s), sq(w_up_s), sq(w_down_s), g_final, tab_h, half_h, tab_i, half_i)
    return out.reshape(B, S, D)
```

```python
import functools

import jax
import jax.numpy as jnp
from jax import lax
from jax.experimental import pallas as pl
from jax.experimental.pallas import tpu as pltpu

HEAD_DIM = 128
MOBA_HEADS = 4
DSA_HEADS = 4
MOBA_W = MOBA_HEADS * HEAD_DIM
DSA_W = DSA_HEADS * HEAD_DIM
MOBA_BLOCK = 256
MOBA_TOPK = 3
DSA_MAX_TOPK = 256
KV_LORA = 256
IDX_HEADS = 8
IDX_DIM = 64
ROPE_THETA = 500000.0
ROPE_FRACTION_DIV = 4
N_EXPERTS = 256
EXPERT_TOPK = 8
N_GROUPS = 8
TOPK_GROUPS = 4
GROUP_SIZE = N_EXPERTS // N_GROUPS
D_EXPERT = 256
D_SHARED = 256
ROUTED_SCALE = 2.5
EPS = 1e-6

LANES = 128
SUBLANES = 8
VMEM_LIMIT = 56 * 1024 * 1024

PROJ_TM = 512
DSA_TQ = 256
DSA_KC = 512
DSA_KA = 256
POST_TM = 256
DISP_TM = 256
EXP_BM = 256
COMB_TM = 128
NEG = -1e30
INT_MIN = -2147483648
LOG2E = 1.4426950408889634

F32 = jnp.float32
BF16 = jnp.bfloat16
I32 = jnp.int32


def _cparams(sem, **kw):
    return pltpu.CompilerParams(dimension_semantics=sem, vmem_limit_bytes=VMEM_LIMIT, **kw)


def _dot(a, b):
    return jnp.dot(a, b, preferred_element_type=F32)


def _dot_nt(a, b):
    return lax.dot_general(a, b, (((1,), (1,)), ((), ())), preferred_element_type=F32)


def _silu(x):
    return x * (1.0 / (1.0 + jnp.exp(-x)))


def _rms(x, g):
    return x * lax.rsqrt(jnp.mean(x * x, axis=-1, keepdims=True) + EPS) * g


def _ada_kernel(c_ref, w_ref, b_ref, o_ref):
    ca = _silu(c_ref[...])
    o_ref[...] = jnp.dot(ca, w_ref[...], preferred_element_type=F32,
                         precision=lax.Precision.HIGHEST) + b_ref[...]


def _ada_mod(c, w_ada, b_ada):
    B, D = c.shape
    N = w_ada.shape[1]
    tn = 1024
    return pl.pallas_call(
        _ada_kernel,
        out_shape=jax.ShapeDtypeStruct((B, N), F32),
        grid=(N // tn,),
        in_specs=[pl.BlockSpec((B, D), lambda j: (0, 0)),
                  pl.BlockSpec((D, tn), lambda j: (0, j)),
                  pl.BlockSpec((1, tn), lambda j: (0, j))],
        out_specs=pl.BlockSpec((B, tn), lambda j: (0, j)),
        compiler_params=_cparams(("arbitrary",)),
        name="ada_mod",
    )(c, w_ada, b_ada.reshape(1, N))


def _rope_tables(seq, head_dim, heads_per_vreg):
    rot = head_dim // ROPE_FRACTION_DIV
    half = rot // 2
    inv = jnp.float32(ROPE_THETA) ** (-(jnp.arange(0, rot, 2, dtype=F32) / rot))
    ang = jnp.arange(seq, dtype=F32)[:, None] * inv[None, :]
    cos, sin = jnp.cos(ang), jnp.sin(ang)
    ones = jnp.ones((seq, head_dim - rot), F32)
    zeros_h = jnp.zeros((seq, half), F32)
    zeros_r = jnp.zeros((seq, head_dim - rot), F32)
    c = jnp.concatenate([cos, cos, ones], axis=1)
    sp = jnp.concatenate([zeros_h, sin, zeros_r], axis=1)
    sm = jnp.concatenate([-sin, zeros_h, zeros_r], axis=1)
    rep = lambda t: jnp.tile(t, (1, heads_per_vreg))
    return jnp.stack([rep(c), rep(sp), rep(sm)], axis=0), half


def _rope(x, tab_ref, half):
    return (x * tab_ref[0] + pltpu.roll(x, half, 1) * tab_ref[1]
            + pltpu.roll(x, LANES - half, 1) * tab_ref[2])


_C_QM, _C_KM, _C_VM, _C_QD = 0, MOBA_W, 2 * MOBA_W, 3 * MOBA_W
_C_CKV = 3 * MOBA_W + DSA_W
_C_QI = _C_CKV + KV_LORA
_C_KI = _C_QI + IDX_HEADS * IDX_DIM
_C_END = _C_KI + LANES


def _in_proj_kernel(x_ref, mod_ref, gmix_ref, w_ref, gkv_ref, wkv_ref, tabh_ref, tabi_ref,
                    qmt_ref, km_ref, vmt_ref, kmean_ref, qdt_ref, kd_ref, vdt_ref, qit_ref, ki_ref, wit_ref,
                    *, half_h, half_i):
    tm = x_ref.shape[0]
    x = x_ref[...]
    sh1 = mod_ref[0, 0:1, :]
    sc1 = mod_ref[0, 1:2, :]
    h = (_rms(x, gmix_ref[...]) * (1.0 + sc1) + sh1).astype(BF16)

    def proj(c0, width):
        return _dot(h, w_ref[:, c0:c0 + width])

    q_scale = HEAD_DIM ** -0.5 * LOG2E
    nblk = tm // MOBA_BLOCK
    qm = proj(_C_QM, MOBA_W)
    km = proj(_C_KM, MOBA_W)
    vm = proj(_C_VM, MOBA_W)
    for hd in range(MOBA_HEADS):
        sl = slice(hd * HEAD_DIM, (hd + 1) * HEAD_DIM)
        qmt_ref[sl, :] = (_rope(qm[:, sl], tabh_ref, half_h) * q_scale).T.astype(BF16)
        kr = _rope(km[:, sl], tabh_ref, half_h)
        km_ref[:, sl] = kr.astype(BF16)
        for blk in range(nblk):
            rows = slice(blk * MOBA_BLOCK, (blk + 1) * MOBA_BLOCK)
            kmean_ref[blk:blk + 1, sl] = jnp.mean(kr[rows], axis=0, keepdims=True)
            vmt_ref[blk, sl, :] = vm[rows, sl].T.astype(BF16)
    qd = proj(_C_QD, DSA_W)
    for hd in range(DSA_HEADS):
        sl = slice(hd * HEAD_DIM, (hd + 1) * HEAD_DIM)
        qdt_ref[sl, :] = (_rope(qd[:, sl], tabh_ref, half_h) * q_scale).T.astype(BF16)
    ckv = proj(_C_CKV, KV_LORA)
    kv = _dot(_rms(ckv, gkv_ref[...]).astype(BF16), wkv_ref[...])
    kd_ref[...] = _rope(kv[:, :HEAD_DIM], tabh_ref, half_h).astype(BF16)
    for ch in range(tm // DSA_KA):
        vdt_ref[ch] = kv[ch * DSA_KA:(ch + 1) * DSA_KA, HEAD_DIM:].T.astype(BF16)
    qi = proj(_C_QI, IDX_HEADS * IDX_DIM)
    for j in range(IDX_HEADS * IDX_DIM // LANES):
        sl = slice(j * LANES, (j + 1) * LANES)
        qit_ref[sl, :] = _rope(qi[:, sl], tabi_ref, half_i).T.astype(BF16)
    kw = proj(_C_KI, LANES)
    ki_ref[...] = _rope(kw, tabi_ref, half_i)[:, :IDX_DIM].astype(BF16)
    wit_ref[...] = kw.T[IDX_DIM:IDX_DIM + IDX_HEADS, :] * (IDX_HEADS ** -0.5 * IDX_DIM ** -0.5)


def _in_proj(x2, mod3, g_mix, w_in_p, g_kv, w_kv_up, tab_h, half_h, tab_i, half_i, S):
    T, D = x2.shape
    tm = PROJ_TM
    nt_per_seq = S // tm
    row = lambda i: (i, 0)
    col = lambda i: (0, i)
    nb = tm // MOBA_BLOCK
    nc = tm // DSA_KA
    outs = [
        jax.ShapeDtypeStruct((MOBA_W, T), BF16),
        jax.ShapeDtypeStruct((T, MOBA_W), BF16),
        jax.ShapeDtypeStruct((T // MOBA_BLOCK, MOBA_W, MOBA_BLOCK), BF16),
        jax.ShapeDtypeStruct((T // tm, nb, MOBA_W), F32),
        jax.ShapeDtypeStruct((DSA_W, T), BF16),
        jax.ShapeDtypeStruct((T, HEAD_DIM), BF16),
        jax.ShapeDtypeStruct((T // DSA_KA, HEAD_DIM, DSA_KA), BF16),
        jax.ShapeDtypeStruct((IDX_HEADS * IDX_DIM, T), BF16),
        jax.ShapeDtypeStruct((T, IDX_DIM), BF16),
        jax.ShapeDtypeStruct((IDX_HEADS, T), F32),
    ]
    out_specs = [
        pl.BlockSpec((MOBA_W, tm), col), pl.BlockSpec((tm, MOBA_W), row),
        pl.BlockSpec((nb, MOBA_W, MOBA_BLOCK), lambda i: (i, 0, 0)),
        pl.BlockSpec((None, nb, MOBA_W), lambda i: (i, 0, 0)),
        pl.BlockSpec((DSA_W, tm), col), pl.BlockSpec((tm, HEAD_DIM), row),
        pl.BlockSpec((nc, HEAD_DIM, DSA_KA), lambda i: (i, 0, 0)),
        pl.BlockSpec((IDX_HEADS * IDX_DIM, tm), col), pl.BlockSpec((tm, IDX_DIM), row),
        pl.BlockSpec((IDX_HEADS, tm), col),
    ]
    res = pl.pallas_call(
        functools.partial(_in_proj_kernel, half_h=half_h, half_i=half_i),
        out_shape=outs,
        grid=(T // tm,),
        in_specs=[
            pl.BlockSpec((tm, D), row),
            pl.BlockSpec((1, 6, D), lambda i: (i // nt_per_seq, 0, 0)),
            pl.BlockSpec((1, D), lambda i: (0, 0)),
            pl.BlockSpec((D, _C_END), lambda i: (0, 0)),
            pl.BlockSpec((1, KV_LORA), lambda i: (0, 0)),
            pl.BlockSpec((KV_LORA, 2 * HEAD_DIM), lambda i: (0, 0)),
            pl.BlockSpec((3, tm, LANES), lambda i: (0, i % nt_per_seq, 0)),
            pl.BlockSpec((3, tm, LANES), lambda i: (0, i % nt_per_seq, 0)),
        ],
        out_specs=out_specs,
        compiler_params=_cparams(("parallel",)),
        name="in_proj",
    )(x2, mod3, g_mix, w_in_p, g_kv, w_kv_up, tab_h, tab_i)
    res = list(res)
    res[3] = res[3].reshape(T // MOBA_BLOCK, MOBA_W)
    return res


def _moba_kernel(qt_ref, k_ref, vt_ref, kmean_ref, o_ref, bias_sc, *accs):
    qi = pl.program_id(1)
    blk = MOBA_BLOCK
    nb = kmean_ref.shape[0]
    heads = range(MOBA_HEADS)
    hsl = [slice(hd * HEAD_DIM, (hd + 1) * HEAD_DIM) for hd in heads]
    qts = [qt_ref[hsl[hd], :] for hd in heads]
    row = lax.broadcasted_iota(I32, (nb, blk), 0)
    past = row < qi
    start = pl.multiple_of(qi * blk, blk)
    k_io = lax.broadcasted_iota(I32, (blk, blk), 0)
    q_io = lax.broadcasted_iota(I32, (blk, blk), 1)

    init = []
    for hd in heads:
        km = kmean_ref[:, hsl[hd]]
        km_hi = km.astype(BF16)
        km_lo = (km - km_hi.astype(F32)).astype(BF16)
        gate = _dot(km_hi, qts[hd]) + _dot(km_lo, qts[hd])
        gate = jnp.where(past, gate, -jnp.inf)
        bias = jnp.full((nb, blk), NEG, F32)
        for _ in range(MOBA_TOPK):
            _, idx = _first_index_of_max(gate, row, nb)
            hit = row == idx
            bias = jnp.where(hit, 0.0, bias)
            gate = jnp.where(hit, -jnp.inf, gate)
        bias_sc[hd] = jnp.where(past, bias, NEG)

        s = _dot(k_ref[pl.ds(start, blk), hsl[hd]], qts[hd])
        s = jnp.where(k_io <= q_io, s, NEG)
        m0 = jnp.max(s, axis=0, keepdims=True)
        p = jnp.exp2(s - m0)
        accs[hd][...] = _dot(vt_ref[qi, hsl[hd], :], p.astype(BF16))
        init += [m0, jnp.sum(p, axis=0, keepdims=True)]

    def body(n, carry):
        st = pl.multiple_of(n * blk, blk)
        kb = k_ref[pl.ds(st, blk), :]
        sbs = [_dot(kb[:, hsl[hd]], qts[hd]) + bias_sc[hd, pl.ds(n, 1), :] for hd in heads]
        old = [accs[hd][...] for hd in heads]
        out, new = [], []
        for hd in heads:
            m_old, l_old = carry[2 * hd], carry[2 * hd + 1]
            m_new = jnp.maximum(m_old, jnp.max(sbs[hd], axis=0, keepdims=True))
            alpha = jnp.exp2(m_old - m_new)
            pb = jnp.exp2(sbs[hd] - m_new)
            new.append(alpha * old[hd] + _dot(vt_ref[n, hsl[hd], :], pb.astype(BF16)))
            out += [m_new, alpha * l_old + jnp.sum(pb, axis=0, keepdims=True)]
        for hd in heads:
            accs[hd][...] = new[hd]
        return tuple(out)

    fin = lax.fori_loop(0, qi, body, tuple(init))
    for hd in heads:
        o_ref[:, hsl[hd]] = (accs[hd][...] * (1.0 / fin[2 * hd + 1])).T


def _moba_attn(q_mt, k_m, v_mt, kmean, B, S):
    T = B * S
    blk = MOBA_BLOCK
    nq = S // blk
    return pl.pallas_call(
        _moba_kernel,
        out_shape=jax.ShapeDtypeStruct((T, MOBA_W), F32),
        grid=(B, nq),
        in_specs=[
            pl.BlockSpec((MOBA_W, blk), lambda b, i: (0, b * nq + i)),
            pl.BlockSpec((S, MOBA_W), lambda b, i: (b, 0)),
            pl.BlockSpec((nq, MOBA_W, blk), lambda b, i: (b, 0, 0)),
            pl.BlockSpec((nq, MOBA_W), lambda b, i: (b, 0)),
        ],
        out_specs=pl.BlockSpec((blk, MOBA_W), lambda b, i: (b * nq + i, 0)),
        scratch_shapes=[pltpu.VMEM((MOBA_HEADS, nq, blk), F32)]
        + [pltpu.VMEM((HEAD_DIM, blk), F32) for _ in range(MOBA_HEADS)],
        compiler_params=_cparams(("parallel", "arbitrary")),
        name="moba_attn",
    )(q_mt, k_m, v_mt, kmean)


def _sortable_key(x):
    b = pltpu.bitcast(x, I32)
    return jnp.where(b >= 0, b, b ^ jnp.int32(0x7FFFFFFF))


def _dsa_kernel(qit_ref, wit_ref, qdt_ref, ki_ref, kd_ref, vdt_ref, o_ref, key_sc, *accs, topk):
    t = pl.program_id(1)
    tq, kc = DSA_TQ, DSA_KC
    S = key_sc.shape[0]
    q0 = t * tq
    n_chunks = (q0 + tq + kc - 1) // kc
    key_io = lax.broadcasted_iota(I32, (kc, tq), 0)
    q_pos = q0 + lax.broadcasted_iota(I32, (kc, tq), 1)
    w = wit_ref[...]

    def score_chunk(c, carry):
        k0 = pl.multiple_of(c * kc, kc)
        kic = ki_ref[pl.ds(k0, kc), :]
        acc = jnp.zeros((kc, tq), F32)
        for hd in range(IDX_HEADS):
            lg = _dot(kic, qit_ref[hd * IDX_DIM:(hd + 1) * IDX_DIM, :])
            acc = acc + jnp.maximum(lg, 0.0) * w[hd:hd + 1, :]
        acc = jnp.where(acc == 0.0, 0.0, acc)
        key = _sortable_key(acc)
        key_sc[pl.ds(k0, kc), :] = jnp.where(k0 + key_io <= q_pos, key, INT_MIN)
        return carry

    lax.fori_loop(0, n_chunks, score_chunk, 0)

    def count(pred_fn):
        def cbody(c, acc):
            k0 = pl.multiple_of(c * kc, kc)
            hit = jnp.where(pred_fn(key_sc[pl.ds(k0, kc), :], k0), 1, 0)
            return acc + jnp.sum(hit.reshape(kc // SUBLANES, SUBLANES, tq), axis=0)
        acc = lax.fori_loop(0, n_chunks, cbody, jnp.zeros((SUBLANES, tq), I32))
        return jnp.sum(acc, axis=0, keepdims=True)

    c_nonneg = count(lambda kk, k0: kk >= 0)
    thr0 = jnp.where(c_nonneg >= topk, 0, INT_MIN).astype(I32)

    def bit_step(i, thr):
        cand = thr | lax.shift_left(jnp.int32(1), 30 - i)
        cnt = count(lambda kk, k0: kk >= cand)
        return jnp.where(cnt >= topk, cand, thr)

    thr = lax.fori_loop(0, 31, bit_step, thr0)

    n_gt = count(lambda kk, k0: kk > thr)
    n_ge = count(lambda kk, k0: kk >= thr)
    need = topk - n_gt
    overflow = (n_ge > topk) & (thr != INT_MIN)
    any_overflow = jnp.max(jnp.where(overflow, 1, 0)) > 0
    nbits = max(1, (S - 1).bit_length())

    def cut_search():
        def step(i, lo):
            cand = lo | lax.shift_left(jnp.int32(1), nbits - 1 - i)
            cnt = count(lambda kk, k0: (kk == thr) & (k0 + key_io < cand))
            return jnp.where(cnt >= need, lo, cand)
        return lax.fori_loop(0, nbits, step, jnp.zeros((1, tq), I32))

    jcut = lax.cond(any_overflow, cut_search, lambda: jnp.zeros((1, tq), I32))
    jcut = jnp.where(overflow, jcut, S)
    thr_ge = jnp.where(thr == INT_MIN, INT_MIN + 1, thr)

    pairs = range(DSA_HEADS // 2)
    qst = [jnp.concatenate([qdt_ref[(2 * g + j) * HEAD_DIM:(2 * g + j + 1) * HEAD_DIM, :] for j in range(2)],
                           axis=1) for g in pairs]
    for g in pairs:
        accs[g][...] = jnp.zeros(accs[g].shape, F32)

    ka = DSA_KA
    pos_io = lax.broadcasted_iota(I32, (ka, tq), 0)

    def attn_chunk(c, carry):
        k0 = pl.multiple_of(c * ka, ka)
        kk = key_sc[pl.ds(k0, ka), :]
        bias = lax.cond(
            any_overflow,
            lambda: jnp.where((kk > thr_ge) | ((kk == thr_ge) & (k0 + pos_io <= jcut)), 0.0, NEG),
            lambda: jnp.where(kk >= thr_ge, 0.0, NEG))
        bias2 = jnp.concatenate([bias, bias], axis=1)
        kdc = kd_ref[pl.ds(k0, ka), :]
        vtc = vdt_ref[c]
        ss = [_dot(kdc, qst[g]) + bias2 for g in pairs]
        old = [accs[g][...] for g in pairs]
        out, new = [], []
        for g in pairs:
            m_old, l_old = carry[2 * g], carry[2 * g + 1]
            m_new = jnp.maximum(m_old, jnp.max(ss[g], axis=0, keepdims=True))
            alpha = jnp.exp2(m_old - m_new)
            p = jnp.exp2(ss[g] - m_new)
            new.append(alpha * old[g] + _dot(vtc, p.astype(BF16)))
            out += [m_new, alpha * l_old + jnp.sum(p, axis=0, keepdims=True)]
        for g in pairs:
            accs[g][...] = new[g]
        return tuple(out)

    init = (jnp.full((1, 2 * tq), NEG, F32), jnp.zeros((1, 2 * tq), F32)) * len(pairs)
    fin = lax.fori_loop(0, (q0 + tq + ka - 1) // ka, attn_chunk, init)
    for g in pairs:
        out_t = accs[g][...] * (1.0 / fin[2 * g + 1])
        for j in range(2):
            hd = 2 * g + j
            o_ref[:, hd * HEAD_DIM:(hd + 1) * HEAD_DIM] = out_t[:, j * tq:(j + 1) * tq].T


def _dsa_attn(q_it, w_it, q_dt, k_i, k_d, v_dt, B, S):
    T = B * S
    tq = DSA_TQ
    nq = S // tq
    topk = min(DSA_MAX_TOPK, S // 4)
    qcol = lambda b, t: (0, b * nq + t)
    seq = lambda b, t: (b, 0)
    return pl.pallas_call(
        functools.partial(_dsa_kernel, topk=topk),
        out_shape=jax.ShapeDtypeStruct((T, DSA_W), F32),
        grid=(B, nq),
        in_specs=[
            pl.BlockSpec((IDX_HEADS * IDX_DIM, tq), qcol),
            pl.BlockSpec((IDX_HEADS, tq), qcol),
            pl.BlockSpec((DSA_W, tq), qcol),
            pl.BlockSpec((S, IDX_DIM), seq),
            pl.BlockSpec((S, HEAD_DIM), seq),
            pl.BlockSpec((S // DSA_KA, HEAD_DIM, DSA_KA), lambda b, t: (b, 0, 0)),
        ],
        out_specs=pl.BlockSpec((tq, DSA_W), lambda b, t: (b * nq + t, 0)),
        scratch_shapes=[pltpu.VMEM((S, tq), I32)]
        + [pltpu.VMEM((HEAD_DIM, 2 * tq), F32) for _ in range(DSA_HEADS // 2)],
        compiler_params=_cparams(("parallel", "arbitrary")),
        name="dsa_attn",
    )(q_it, w_it, q_dt, k_i, k_d, v_dt)


def _first_index_of_max(v, row_io, n_rows):
    m = jnp.max(v, axis=0, keepdims=True)
    idx = jnp.min(jnp.where(v == m, row_io, n_rows), axis=0, keepdims=True)
    return m, idx


def _post_kernel(x_ref, om_ref, od_ref, mod_ref, gm_ref, gd_ref, wout_ref, gffn_ref, wgu_ref, wds_ref,
                 wrh_ref, wrl_ref, rb_ref, tri_ref,
                 xpart_ref, h2r_ref, eidx_ref, rank_ref, gate_ref, cnt_ref, base_sc):
    i = pl.program_id(0)
    tm = x_ref.shape[0]
    gt1 = mod_ref[0, 2:3, :]
    sh2 = mod_ref[0, 3:4, :]
    sc2 = mod_ref[0, 4:5, :]
    gt2 = mod_ref[0, 5:6, :]

    mixed = jnp.concatenate([_rms(om_ref[...], gm_ref[...]), _rms(od_ref[...], gd_ref[...])], axis=1)
    x1 = x_ref[...] + gt1 * _dot(mixed.astype(BF16), wout_ref[...])
    h2 = _rms(x1, gffn_ref[...]) * (1.0 + sc2) + sh2
    h2b = h2.astype(BF16)

    au = _dot(h2b, wgu_ref[...])
    hs = (_silu(au[:, :D_SHARED]) * au[:, D_SHARED:]).astype(BF16)
    xpart_ref[...] = x1 + gt2 * _dot(hs, wds_ref[...])

    for c in range(h2.shape[1] // LANES):
        h2r_ref[pl.ds(c, tm, stride=SUBLANES), :] = h2[:, c * LANES:(c + 1) * LANES]

    h2lo = (h2 - h2b.astype(F32)).astype(BF16)
    logits = _dot_nt(wrh_ref[...], h2b) + _dot_nt(wrl_ref[...], h2b) + _dot_nt(wrh_ref[...], h2lo)
    scores = 1.0 / (1.0 + jnp.exp(-logits))
    biased = scores + rb_ref[...]

    g_io = lax.broadcasted_iota(I32, (GROUP_SIZE, tm), 0)
    gs_rows = []
    for g in range(N_GROUPS):
        blk = biased[g * GROUP_SIZE:(g + 1) * GROUP_SIZE, :]
        m1, i1 = _first_index_of_max(blk, g_io, GROUP_SIZE)
        m2 = jnp.max(jnp.where(g_io == i1, -jnp.inf, blk), axis=0, keepdims=True)
        gs_rows.append(m1 + m2)
    gs = jnp.concatenate(gs_rows, axis=0)
    gi = lax.broadcasted_iota(I32, (N_GROUPS, tm), 0)
    grank = jnp.zeros((N_GROUPS, tm), I32)
    for m in range(N_GROUPS):
        gm = gs[m:m + 1, :]
        grank = grank + jnp.where((gm > gs) | ((gm == gs) & (m < gi)), 1, 0)
    gsel = grank < TOPK_GROUPS
    masked = jnp.concatenate(
        [jnp.where(gsel[g:g + 1, :], biased[g * GROUP_SIZE:(g + 1) * GROUP_SIZE, :], -jnp.inf)
         for g in range(N_GROUPS)], axis=0)

    e_io = lax.broadcasted_iota(I32, (N_EXPERTS, tm), 0)
    e_rows, s_rows = [], []
    for _ in range(EXPERT_TOPK):
        _, idx = _first_index_of_max(masked, e_io, N_EXPERTS)
        hit = e_io == idx
        e_rows.append(idx)
        s_rows.append(jnp.sum(jnp.where(hit, scores, 0.0), axis=0, keepdims=True))
        masked = jnp.where(hit, -jnp.inf, masked)
    eidx = jnp.concatenate(e_rows, axis=0)
    sk = jnp.concatenate(s_rows, axis=0)
    gate_ref[...] = sk / jnp.sum(sk, axis=0, keepdims=True) * ROUTED_SCALE
    eidx_ref[...] = eidx

    @pl.when(i == 0)
    def _():
        base_sc[...] = jnp.zeros(base_sc.shape, F32)

    chosen = jnp.zeros((N_EXPERTS, tm), F32)
    for k in range(EXPERT_TOPK):
        chosen = chosen + jnp.where(e_io == e_rows[k], 1.0, 0.0)
    incl = _dot(chosen.astype(BF16), tri_ref[...])
    pos = base_sc[...] + incl - 1.0
    rank_ref[...] = jnp.concatenate(
        [jnp.sum(jnp.where(e_io == e_rows[k], pos, 0.0), axis=0, keepdims=True)
         for k in range(EXPERT_TOPK)], axis=0).astype(I32)
    base_sc[...] = base_sc[...] + incl[:, tm - 1:tm]
    cnt_ref[...] = jnp.broadcast_to(base_sc[...], cnt_ref.shape)


def _post_attn(x2, o_m, o_d, mod3, g_moba, g_dsa, w_out, g_ffn, w_gu_s, w_down_s, wr_hi, wr_lo, rbias, S):
    T, D = x2.shape
    tm = POST_TM
    nt_per_seq = S // tm
    row = lambda i: (i, 0)
    full = lambda i: (0, 0)
    tri = (jnp.arange(tm)[:, None] <= jnp.arange(tm)[None, :]).astype(BF16)
    n_chunk = D // LANES
    return pl.pallas_call(
        _post_kernel,
        out_shape=[
            jax.ShapeDtypeStruct((T, D), F32),
            jax.ShapeDtypeStruct((T * n_chunk, LANES), F32),
            jax.ShapeDtypeStruct((EXPERT_TOPK, T), I32),
            jax.ShapeDtypeStruct((EXPERT_TOPK, T), I32),
            jax.ShapeDtypeStruct((EXPERT_TOPK, T), F32),
            jax.ShapeDtypeStruct((N_EXPERTS, LANES), F32),
        ],
        grid=(T // tm,),
        in_specs=[
            pl.BlockSpec((tm, D), row),
            pl.BlockSpec((tm, MOBA_W), row),
            pl.BlockSpec((tm, DSA_W), row),
            pl.BlockSpec((1, 6, D), lambda i: (i // nt_per_seq, 0, 0)),
            pl.BlockSpec((1, MOBA_W), full),
            pl.BlockSpec((1, DSA_W), full),
            pl.BlockSpec(w_out.shape, full),
            pl.BlockSpec((1, D), full),
            pl.BlockSpec(w_gu_s.shape, full),
            pl.BlockSpec(w_down_s.shape, full),
            pl.BlockSpec(wr_hi.shape, full),
            pl.BlockSpec(wr_lo.shape, full),
            pl.BlockSpec((N_EXPERTS, 1), full),
            pl.BlockSpec((tm, tm), full),
        ],
        out_specs=[
            pl.BlockSpec((tm, D), row),
            pl.BlockSpec((tm * n_chunk, LANES), row),
            pl.BlockSpec((EXPERT_TOPK, tm), lambda i: (0, i)),
            pl.BlockSpec((EXPERT_TOPK, tm), lambda i: (0, i)),
            pl.BlockSpec((EXPERT_TOPK, tm), lambda i: (0, i)),
            pl.BlockSpec((N_EXPERTS, LANES), full),
        ],
        scratch_shapes=[pltpu.VMEM((N_EXPERTS, 1), F32)],
        compiler_params=_cparams(("arbitrary",)),
        name="post_attn",
    )(x2, o_m, o_d, mod3, g_moba, g_dsa, w_out, g_ffn, w_gu_s, w_down_s, wr_hi, wr_lo, rbias, tri)


def _row_copy_wait(ref_hbm, n_rows, sem):
    blk = ref_hbm.at[pl.ds(0, n_rows * SUBLANES), :]
    pltpu.make_async_copy(blk, blk, sem).wait()


def _dispatch_kernel(ps_ref, e_ref, r_ref, h2r_ref, xs_ref, sem):
    tm = e_ref.shape[1]

    def body(t, carry):
        src = h2r_ref.at[pl.ds(pl.multiple_of(t * SUBLANES, SUBLANES), SUBLANES), :]
        for k in range(EXPERT_TOPK):
            d = ps_ref[e_ref[k, t]] + r_ref[k, t]
            dst = xs_ref.at[pl.ds(pl.multiple_of(d * SUBLANES, SUBLANES), SUBLANES), :]
            pltpu.make_async_copy(src, dst, sem).start()
        return carry

    lax.fori_loop(0, tm, body, 0)
    _row_copy_wait(xs_ref, tm * EXPERT_TOPK, sem)


def _dispatch(pad_starts, eidx, rank, h2r, n_rows_padded):
    T = eidx.shape[1]
    tm = DISP_TM
    n_chunk = h2r.shape[0] // T
    return pl.pallas_call(
        _dispatch_kernel,
        out_shape=jax.ShapeDtypeStruct((n_rows_padded * n_chunk, LANES), F32),
        grid_spec=pltpu.PrefetchScalarGridSpec(
            num_scalar_prefetch=1,
            grid=(T // tm,),
            in_specs=[
                pl.BlockSpec((EXPERT_TOPK, tm), lambda i, ps: (0, i), memory_space=pltpu.SMEM),
                pl.BlockSpec((EXPERT_TOPK, tm), lambda i, ps: (0, i), memory_space=pltpu.SMEM),
                pl.BlockSpec((tm * n_chunk, LANES), lambda i, ps: (i, 0)),
            ],
            out_specs=pl.BlockSpec(memory_space=pl.ANY),
            scratch_shapes=[pltpu.SemaphoreType.DMA],
        ),
        compiler_params=_cparams(("arbitrary",), disable_bounds_checks=True),
        name="dispatch",
    )(pad_starts, eidx, rank, h2r)


def _expert_kernel(be_ref, nv_ref, xs_ref, wg_ref, wu_ref, wd_ref, ys_ref):
    j = pl.program_id(0)
    bm = xs_ref.shape[0] // SUBLANES
    nv = nv_ref[j]

    @pl.when(nv > 0)
    def _():
        n_chunk = wg_ref.shape[0] // LANES
        x = jnp.concatenate([xs_ref[pl.ds(c, bm, stride=SUBLANES), :] for c in range(n_chunk)], axis=1)
        valid = lax.broadcasted_iota(I32, (bm, 1), 0) < nv
        xb = jnp.where(valid, x, 0.0).astype(BF16)
        a = _dot(xb, wg_ref[...].astype(BF16))
        u = _dot(xb, wu_ref[...].astype(BF16))
        hmid = (_silu(a) * u).astype(BF16)
        ob = _dot(hmid, wd_ref[...].astype(BF16))
        for c in range(n_chunk):
            ys_ref[pl.ds(c, bm, stride=SUBLANES), :] = ob[:, c * LANES:(c + 1) * LANES]

    @pl.when(nv == 0)
    def _():
        ys_ref[...] = jnp.zeros(ys_ref.shape, F32)


def _experts(block_expert, n_valid, xs, w_gate_e, w_up_e, w_down_e):
    n_blocks = block_expert.shape[0]
    E, D, DE = w_gate_e.shape
    bm = EXP_BM
    n_chunk = D // LANES
    return pl.pallas_call(
        _expert_kernel,
        out_shape=jax.ShapeDtypeStruct(xs.shape, F32),
        grid_spec=pltpu.PrefetchScalarGridSpec(
            num_scalar_prefetch=2,
            grid=(n_blocks,),
            in_specs=[
                pl.BlockSpec((bm * n_chunk, LANES), lambda j, be, nv: (j, 0)),
                pl.BlockSpec((None, D, DE), lambda j, be, nv: (be[j], 0, 0)),
                pl.BlockSpec((None, D, DE), lambda j, be, nv: (be[j], 0, 0)),
                pl.BlockSpec((None, DE, D), lambda j, be, nv: (be[j], 0, 0)),
            ],
            out_specs=pl.BlockSpec((bm * n_chunk, LANES), lambda j, be, nv: (j, 0)),
        ),
        compiler_params=_cparams(("arbitrary",)),
        name="experts",
    )(block_expert, n_valid, xs, w_gate_e, w_up_e, w_down_e)


def _combine_kernel(ps_ref, e_ref, r_ref, g_ref, xpart_ref, mod_ref, gfin_ref, ys_ref, o_ref, buf, sem):
    tm = xpart_ref.shape[0]
    n_chunk = xpart_ref.shape[1] // LANES

    def body(t, carry):
        for k in range(EXPERT_TOPK):
            d = ps_ref[e_ref[k, t]] + r_ref[k, t]
            src = ys_ref.at[pl.ds(pl.multiple_of(d * SUBLANES, SUBLANES), SUBLANES), :]
            dst = buf.at[pl.ds(pl.multiple_of((k * tm + t) * SUBLANES, SUBLANES), SUBLANES), :]
            pltpu.make_async_copy(src, dst, sem).start()
        return carry

    lax.fori_loop(0, tm, body, 0)
    _row_copy_wait(ys_ref, tm * EXPERT_TOPK, sem)

    gt2 = mod_ref[0, 5:6, :]
    g = g_ref[...]
    cols = []
    for c in range(n_chunk):
        acc = jnp.zeros((tm, LANES), F32)
        for k in range(EXPERT_TOPK):
            acc = acc + g[:, k:k + 1] * buf[pl.ds(k * tm * SUBLANES + c, tm, stride=SUBLANES), :]
        cols.append(acc)
    routed = jnp.concatenate(cols, axis=1)
    o_ref[...] = _rms(xpart_ref[...] + gt2 * routed, gfin_ref[...])


def _combine(pad_starts, eidx, rank, gates_t, xpart, mod3, g_final, ys, S):
    T, D = xpart.shape
    tm = COMB_TM
    nt_per_seq = S // tm
    n_chunk = D // LANES
    return pl.pallas_call(
        _combine_kernel,
        out_shape=jax.ShapeDtypeStruct((T, D), F32),
        grid_spec=pltpu.PrefetchScalarGridSpec(
            num_scalar_prefetch=1,
            grid=(T // tm,),
            in_specs=[
                pl.BlockSpec((EXPERT_TOPK, tm), lambda i, ps: (0, i), memory_space=pltpu.SMEM),
                pl.BlockSpec((EXPERT_TOPK, tm), lambda i, ps: (0, i), memory_space=pltpu.SMEM),
                pl.BlockSpec((tm, EXPERT_TOPK), lambda i, ps: (i, 0)),
                pl.BlockSpec((tm, D), lambda i, ps: (i, 0)),
                pl.BlockSpec((1, 6, D), lambda i, ps: (i // nt_per_seq, 0, 0)),
                pl.BlockSpec((1, D), lambda i, ps: (0, 0)),
                pl.BlockSpec(memory_space=pl.ANY),
            ],
            out_specs=pl.BlockSpec((tm, D), lambda i, ps: (i, 0)),
            scratch_shapes=[pltpu.VMEM((EXPERT_TOPK * tm * n_chunk, LANES), F32), pltpu.SemaphoreType.DMA],
        ),
        compiler_params=_cparams(("arbitrary",), disable_bounds_checks=True),
        name="combine",
    )(pad_starts, eidx, rank, gates_t, xpart, mod3, g_final, ys)


def _layer(x2, mod3, S, g_mix, w_in, g_kv, w_kv_up, g_moba_out, g_dsa_out, w_out, g_ffn, w_router,
           router_bias, w_gate_e, w_up_e, w_down_e, w_gate_s, w_up_s, w_down_s, g_final, tab_h, half_h,
           tab_i, half_i):
    T, D = x2.shape
    B = T // S
    w_in_p = jnp.pad(w_in, ((0, 0), (0, _C_END - w_in.shape[1]))).astype(BF16)
    (q_mt, k_m, v_mt, kmean, q_dt, k_d, v_dt, q_it, k_i, w_it) = _in_proj(
        x2, mod3, g_mix.reshape(1, D), w_in_p, g_kv.reshape(1, KV_LORA), w_kv_up.astype(BF16),
        tab_h, half_h, tab_i, half_i, S)
    o_m = _moba_attn(q_mt, k_m, v_mt, kmean, B, S)
    o_d = _dsa_attn(q_it, w_it, q_dt, k_i, k_d, v_dt, B, S)

    wr_t = w_router.T
    wr_hi = wr_t.astype(BF16)
    wr_lo = (wr_t - wr_hi.astype(F32)).astype(BF16)
    w_gu_s = jnp.concatenate([w_gate_s, w_up_s], axis=1).astype(BF16)
    xpart, h2r, eidx, rank, gates, cnt = _post_attn(
        x2, o_m, o_d, mod3, g_moba_out.reshape(1, MOBA_W), g_dsa_out.reshape(1, DSA_W), w_out.astype(BF16),
        g_ffn.reshape(1, D), w_gu_s, w_down_s.astype(BF16), wr_hi, wr_lo,
        router_bias.reshape(N_EXPERTS, 1), S)

    bm = EXP_BM
    n_blocks = T * EXPERT_TOPK // bm + N_EXPERTS
    counts = cnt[:, 0].astype(I32)
    padded = (counts + bm - 1) // bm * bm
    pad_ends = jnp.cumsum(padded)
    pad_starts = (pad_ends - padded).astype(I32)
    blk_start = jnp.arange(n_blocks, dtype=I32) * bm
    block_expert = jnp.minimum(jnp.searchsorted(pad_ends, blk_start, side='right'), N_EXPERTS - 1).astype(I32)
    n_valid = jnp.clip(counts[block_expert] - (blk_start - pad_starts[block_expert]), 0, bm).astype(I32)
    n_valid = jnp.where(blk_start < pad_ends[-1], n_valid, 0)

    xs = _dispatch(pad_starts, eidx, rank, h2r, n_blocks * bm)
    ys = _experts(block_expert, n_valid, xs, w_gate_e, w_up_e, w_down_e)
    return _combine(pad_starts, eidx, rank, gates.T, xpart, mod3, g_final.reshape(1, D), ys, S)


def kernel(x, c, w_ada, b_ada, g_mix, w_in, g_kv, w_kv_up, g_moba_out, g_dsa_out, w_out, g_ffn, w_router,
           router_bias, w_gate_e, w_up_e, w_down_e, w_gate_s, w_up_s, w_down_s, g_final):
    B, S, D = x.shape
    depth = w_ada.shape[0]
    assert depth == 1, "the final norm is fused into the single layer"
    assert S % PROJ_TM == 0 and S % DSA_KC == 0 and S % POST_TM == 0 and S >= 4 * DSA_MAX_TOPK
    tab_h, half_h = _rope_tables(S, HEAD_DIM, 1)
    tab_i, half_i = _rope_tables(S, IDX_DIM, LANES // IDX_DIM)
    x2 = x.reshape(B * S, D)
    sq = lambda a: a.reshape(a.shape[1:])
    mod3 = _ada_mod(c, sq(w_ada), sq(b_ada)).reshape(B, 6, D)
    out = _layer(x2, mod3, S, sq(g_mix), sq(w_in), sq(g_kv), sq(w_kv_up), sq(g_moba_out), sq(g_dsa_out),
                 sq(w_out), sq(g_ffn), sq(w_router), sq(router_bias), sq(w_gate_e), sq(w_up_e), sq(w_down_e),
                 sq(w_gate_s), sq(w_up_s), sq(w_down_s), g_final, tab_h, half_h, tab_i, half_i)
    return out.reshape(B, S, D)
```

```python
import functools

import jax
import jax.numpy as jnp
from jax import lax
from jax.experimental import pallas as pl
from jax.experimental.pallas import tpu as pltpu

HEAD_DIM = 128
MOBA_HEADS = 4
DSA_HEADS = 4
MOBA_W = MOBA_HEADS * HEAD_DIM
DSA_W = DSA_HEADS * HEAD_DIM
MOBA_BLOCK = 256
MOBA_TOPK = 3
DSA_MAX_TOPK = 256
KV_LORA = 256
IDX_HEADS = 8
IDX_DIM = 64
ROPE_THETA = 500000.0
ROPE_FRACTION_DIV = 4
N_EXPERTS = 256
EXPERT_TOPK = 8
N_GROUPS = 8
TOPK_GROUPS = 4
GROUP_SIZE = N_EXPERTS // N_GROUPS
D_EXPERT = 256
D_SHARED = 256
ROUTED_SCALE = 2.5
EPS = 1e-6

LANES = 128
SUBLANES = 8
VMEM_LIMIT = 56 * 1024 * 1024

PROJ_TM = 512
DSA_TQ = 256
DSA_KC = 512
DSA_KA = 256
POST_TM = 256
DISP_TM = 256
EXP_BM = 256
COMB_TM = 128
NEG = -1e30
INT_MIN = -2147483648
LOG2E = 1.4426950408889634

F32 = jnp.float32
BF16 = jnp.bfloat16
I32 = jnp.int32


def _cparams(sem, **kw):
    return pltpu.CompilerParams(dimension_semantics=sem, vmem_limit_bytes=VMEM_LIMIT, **kw)


def _dot(a, b):
    return jnp.dot(a, b, preferred_element_type=F32)


def _dot_nt(a, b):
    return lax.dot_general(a, b, (((1,), (1,)), ((), ())), preferred_element_type=F32)


def _silu(x):
    return x * (1.0 / (1.0 + jnp.exp(-x)))


def _rms(x, g):
    return x * lax.rsqrt(jnp.mean(x * x, axis=-1, keepdims=True) + EPS) * g


ROW_WORDS = 2 * LANES


def _pack_row_words(x):
    slabs = []
    for c in range(x.shape[1] // ROW_WORDS):
        lo = pltpu.bitcast(x[:, c * ROW_WORDS:c * ROW_WORDS + LANES].astype(BF16).astype(F32), I32)
        hi = pltpu.bitcast(x[:, c * ROW_WORDS + LANES:(c + 1) * ROW_WORDS].astype(BF16).astype(F32), I32)
        slabs.append(lax.shift_right_logical(lo, 16) | hi)
    return slabs


def _word_slab_index(rows_ref, row0, n_rows, c):
    n_words = rows_ref.shape[1]
    flat = rows_ref.reshape(rows_ref.shape[0] * n_words, LANES)
    return flat, pl.ds(row0 * n_words + c, n_rows, stride=n_words)


def _load_word_slab(rows_ref, row0, n_rows, c):
    flat, idx = _word_slab_index(rows_ref, row0, n_rows, c)
    return flat[idx, :]


def _store_word_slab(rows_ref, row0, n_rows, c, value):
    flat, idx = _word_slab_index(rows_ref, row0, n_rows, c)
    flat[idx, :] = value


def _unpack_row_words(u):
    return pltpu.bitcast(lax.shift_left(u, 16), F32), pltpu.bitcast(u & jnp.int32(-65536), F32)


def _ada_kernel(c_ref, w_ref, b_ref, o_ref):
    ca = _silu(c_ref[...])
    o_ref[...] = jnp.dot(ca, w_ref[...], preferred_element_type=F32,
                         precision=lax.Precision.HIGHEST) + b_ref[...]


def _ada_mod(c, w_ada, b_ada):
    B, D = c.shape
    N = w_ada.shape[1]
    tn = 1024
    return pl.pallas_call(
        _ada_kernel,
        out_shape=jax.ShapeDtypeStruct((B, N), F32),
        grid=(N // tn,),
        in_specs=[pl.BlockSpec((B, D), lambda j: (0, 0)),
                  pl.BlockSpec((D, tn), lambda j: (0, j)),
                  pl.BlockSpec((1, tn), lambda j: (0, j))],
        out_specs=pl.BlockSpec((B, tn), lambda j: (0, j)),
        compiler_params=_cparams(("arbitrary",)),
        name="ada_mod",
    )(c, w_ada, b_ada.reshape(1, N))


def _rope_tables(seq, head_dim, heads_per_vreg):
    rot = head_dim // ROPE_FRACTION_DIV
    half = rot // 2
    inv = jnp.float32(ROPE_THETA) ** (-(jnp.arange(0, rot, 2, dtype=F32) / rot))
    ang = jnp.arange(seq, dtype=F32)[:, None] * inv[None, :]
    cos, sin = jnp.cos(ang), jnp.sin(ang)
    ones = jnp.ones((seq, head_dim - rot), F32)
    zeros_h = jnp.zeros((seq, half), F32)
    zeros_r = jnp.zeros((seq, head_dim - rot), F32)
    c = jnp.concatenate([cos, cos, ones], axis=1)
    sp = jnp.concatenate([zeros_h, sin, zeros_r], axis=1)
    sm = jnp.concatenate([-sin, zeros_h, zeros_r], axis=1)
    rep = lambda t: jnp.tile(t, (1, heads_per_vreg))
    return jnp.stack([rep(c), rep(sp), rep(sm)], axis=0), half


def _rope(x, tab_ref, half):
    return (x * tab_ref[0] + pltpu.roll(x, half, 1) * tab_ref[1]
            + pltpu.roll(x, LANES - half, 1) * tab_ref[2])


_C_QM, _C_KM, _C_VM, _C_QD = 0, MOBA_W, 2 * MOBA_W, 3 * MOBA_W
_C_CKV = 3 * MOBA_W + DSA_W
_C_QI = _C_CKV + KV_LORA
_C_KI = _C_QI + IDX_HEADS * IDX_DIM
_C_END = _C_KI + LANES


def _in_proj_kernel(x_ref, mod_ref, gmix_ref, w_ref, gkv_ref, wkv_ref, tabh_ref, tabi_ref,
                    qmt_ref, km_ref, vmt_ref, kmean_ref, qdt_ref, kd_ref, vdt_ref, qit_ref, ki_ref, wit_ref,
                    *, half_h, half_i):
    tm = x_ref.shape[0]
    x = x_ref[...]
    sh1 = mod_ref[0, 0:1, :]
    sc1 = mod_ref[0, 1:2, :]
    h = (_rms(x, gmix_ref[...]) * (1.0 + sc1) + sh1).astype(BF16)

    def proj(c0, width):
        return _dot(h, w_ref[:, c0:c0 + width])

    q_scale = HEAD_DIM ** -0.5 * LOG2E
    nblk = tm // MOBA_BLOCK
    qm = proj(_C_QM, MOBA_W)
    km = proj(_C_KM, MOBA_W)
    vm = proj(_C_VM, MOBA_W)
    for hd in range(MOBA_HEADS):
        sl = slice(hd * HEAD_DIM, (hd + 1) * HEAD_DIM)
        qmt_ref[sl, :] = (_rope(qm[:, sl], tabh_ref, half_h) * q_scale).T.astype(BF16)
        kr = _rope(km[:, sl], tabh_ref, half_h)
        km_ref[:, sl] = kr.astype(BF16)
        for blk in range(nblk):
            rows = slice(blk * MOBA_BLOCK, (blk + 1) * MOBA_BLOCK)
            kmean_ref[blk:blk + 1, sl] = jnp.mean(kr[rows], axis=0, keepdims=True)
            vmt_ref[blk, sl, :] = vm[rows, sl].T.astype(BF16)
    qd = proj(_C_QD, DSA_W)
    for hd in range(DSA_HEADS):
        sl = slice(hd * HEAD_DIM, (hd + 1) * HEAD_DIM)
        qdt_ref[sl, :] = (_rope(qd[:, sl], tabh_ref, half_h) * q_scale).T.astype(BF16)
    ckv = proj(_C_CKV, KV_LORA)
    kv = _dot(_rms(ckv, gkv_ref[...]).astype(BF16), wkv_ref[...])
    kd_ref[...] = _rope(kv[:, :HEAD_DIM], tabh_ref, half_h).astype(BF16)
    for ch in range(tm // DSA_KA):
        vdt_ref[ch] = kv[ch * DSA_KA:(ch + 1) * DSA_KA, HEAD_DIM:].T.astype(BF16)
    qi = proj(_C_QI, IDX_HEADS * IDX_DIM)
    for j in range(IDX_HEADS * IDX_DIM // LANES):
        sl = slice(j * LANES, (j + 1) * LANES)
        qit_ref[sl, :] = _rope(qi[:, sl], tabi_ref, half_i).T.astype(BF16)
    kw = proj(_C_KI, LANES)
    ki_ref[...] = _rope(kw, tabi_ref, half_i)[:, :IDX_DIM].astype(BF16)
    wit_ref[...] = kw.T[IDX_DIM:IDX_DIM + IDX_HEADS, :] * (IDX_HEADS ** -0.5 * IDX_DIM ** -0.5)


def _in_proj(x2, mod3, g_mix, w_in_p, g_kv, w_kv_up, tab_h, half_h, tab_i, half_i, S):
    T, D = x2.shape
    tm = PROJ_TM
    nt_per_seq = S // tm
    row = lambda i: (i, 0)
    col = lambda i: (0, i)
    nb = tm // MOBA_BLOCK
    nc = tm // DSA_KA
    outs = [
        jax.ShapeDtypeStruct((MOBA_W, T), BF16),
        jax.ShapeDtypeStruct((T, MOBA_W), BF16),
        jax.ShapeDtypeStruct((T // MOBA_BLOCK, MOBA_W, MOBA_BLOCK), BF16),
        jax.ShapeDtypeStruct((T // tm, nb, MOBA_W), F32),
        jax.ShapeDtypeStruct((DSA_W, T), BF16),
        jax.ShapeDtypeStruct((T, HEAD_DIM), BF16),
        jax.ShapeDtypeStruct((T // DSA_KA, HEAD_DIM, DSA_KA), BF16),
        jax.ShapeDtypeStruct((IDX_HEADS * IDX_DIM, T), BF16),
        jax.ShapeDtypeStruct((T, IDX_DIM), BF16),
        jax.ShapeDtypeStruct((IDX_HEADS, T), F32),
    ]
    out_specs = [
        pl.BlockSpec((MOBA_W, tm), col), pl.BlockSpec((tm, MOBA_W), row),
        pl.BlockSpec((nb, MOBA_W, MOBA_BLOCK), lambda i: (i, 0, 0)),
        pl.BlockSpec((None, nb, MOBA_W), lambda i: (i, 0, 0)),
        pl.BlockSpec((DSA_W, tm), col), pl.BlockSpec((tm, HEAD_DIM), row),
        pl.BlockSpec((nc, HEAD_DIM, DSA_KA), lambda i: (i, 0, 0)),
        pl.BlockSpec((IDX_HEADS * IDX_DIM, tm), col), pl.BlockSpec((tm, IDX_DIM), row),
        pl.BlockSpec((IDX_HEADS, tm), col),
    ]
    res = pl.pallas_call(
        functools.partial(_in_proj_kernel, half_h=half_h, half_i=half_i),
        out_shape=outs,
        grid=(T // tm,),
        in_specs=[
            pl.BlockSpec((tm, D), row),
            pl.BlockSpec((1, 6, D), lambda i: (i // nt_per_seq, 0, 0)),
            pl.BlockSpec((1, D), lambda i: (0, 0)),
            pl.BlockSpec((D, _C_END), lambda i: (0, 0)),
            pl.BlockSpec((1, KV_LORA), lambda i: (0, 0)),
            pl.BlockSpec((KV_LORA, 2 * HEAD_DIM), lambda i: (0, 0)),
            pl.BlockSpec((3, tm, LANES), lambda i: (0, i % nt_per_seq, 0)),
            pl.BlockSpec((3, tm, LANES), lambda i: (0, i % nt_per_seq, 0)),
        ],
        out_specs=out_specs,
        compiler_params=_cparams(("parallel",)),
        name="in_proj",
    )(x2, mod3, g_mix, w_in_p, g_kv, w_kv_up, tab_h, tab_i)
    res = list(res)
    res[3] = res[3].reshape(T // MOBA_BLOCK, MOBA_W)
    return res


def _moba_kernel(qt_ref, k_ref, vt_ref, kmean_ref, o_ref, bias_sc, *accs):
    qi = pl.program_id(1)
    blk = MOBA_BLOCK
    nb = kmean_ref.shape[0]
    heads = range(MOBA_HEADS)
    hsl = [slice(hd * HEAD_DIM, (hd + 1) * HEAD_DIM) for hd in heads]
    qts = [qt_ref[hsl[hd], :] for hd in heads]
    row = lax.broadcasted_iota(I32, (nb, blk), 0)
    past = row < qi
    start = pl.multiple_of(qi * blk, blk)
    k_io = lax.broadcasted_iota(I32, (blk, blk), 0)
    q_io = lax.broadcasted_iota(I32, (blk, blk), 1)

    init = []
    for hd in heads:
        km = kmean_ref[:, hsl[hd]]
        km_hi = km.astype(BF16)
        km_lo = (km - km_hi.astype(F32)).astype(BF16)
        gate = _dot(km_hi, qts[hd]) + _dot(km_lo, qts[hd])
        gate = jnp.where(past, gate, -jnp.inf)
        bias = jnp.full((nb, blk), NEG, F32)
        for _ in range(MOBA_TOPK):
            _, idx = _first_index_of_max(gate, row, nb)
            hit = row == idx
            bias = jnp.where(hit, 0.0, bias)
            gate = jnp.where(hit, -jnp.inf, gate)
        bias_sc[hd] = jnp.where(past, bias, NEG)

        s = _dot(k_ref[pl.ds(start, blk), hsl[hd]], qts[hd])
        s = jnp.where(k_io <= q_io, s, NEG)
        m0 = jnp.max(s, axis=0, keepdims=True)
        p = jnp.exp2(s - m0)
        accs[hd][...] = _dot(vt_ref[qi, hsl[hd], :], p.astype(BF16))
        init += [m0, jnp.sum(p, axis=0, keepdims=True)]

    def body(n, carry):
        st = pl.multiple_of(n * blk, blk)
        kb = k_ref[pl.ds(st, blk), :]
        sbs = [_dot(kb[:, hsl[hd]], qts[hd]) + bias_sc[hd, pl.ds(n, 1), :] for hd in heads]
        old = [accs[hd][...] for hd in heads]
        out, new = [], []
        for hd in heads:
            m_old, l_old = carry[2 * hd], carry[2 * hd + 1]
            m_new = jnp.maximum(m_old, jnp.max(sbs[hd], axis=0, keepdims=True))
            alpha = jnp.exp2(m_old - m_new)
            pb = jnp.exp2(sbs[hd] - m_new)
            new.append(alpha * old[hd] + _dot(vt_ref[n, hsl[hd], :], pb.astype(BF16)))
            out += [m_new, alpha * l_old + jnp.sum(pb, axis=0, keepdims=True)]
        for hd in heads:
            accs[hd][...] = new[hd]
        return tuple(out)

    fin = lax.fori_loop(0, qi, body, tuple(init))
    for hd in heads:
        o_ref[:, hsl[hd]] = (accs[hd][...] * (1.0 / fin[2 * hd + 1])).T


def _moba_attn(q_mt, k_m, v_mt, kmean, B, S):
    T = B * S
    blk = MOBA_BLOCK
    nq = S // blk
    return pl.pallas_call(
        _moba_kernel,
        out_shape=jax.ShapeDtypeStruct((T, MOBA_W), F32),
        grid=(B, nq),
        in_specs=[
            pl.BlockSpec((MOBA_W, blk), lambda b, i: (0, b * nq + i)),
            pl.BlockSpec((S, MOBA_W), lambda b, i: (b, 0)),
            pl.BlockSpec((nq, MOBA_W, blk), lambda b, i: (b, 0, 0)),
            pl.BlockSpec((nq, MOBA_W), lambda b, i: (b, 0)),
        ],
        out_specs=pl.BlockSpec((blk, MOBA_W), lambda b, i: (b * nq + i, 0)),
        scratch_shapes=[pltpu.VMEM((MOBA_HEADS, nq, blk), F32)]
        + [pltpu.VMEM((HEAD_DIM, blk), F32) for _ in range(MOBA_HEADS)],
        compiler_params=_cparams(("parallel", "arbitrary")),
        name="moba_attn",
    )(q_mt, k_m, v_mt, kmean)


def _sortable_key(x):
    b = pltpu.bitcast(x, I32)
    return jnp.where(b >= 0, b, b ^ jnp.int32(0x7FFFFFFF))


def _dsa_kernel(qit_ref, wit_ref, qdt_ref, ki_ref, kd_ref, vdt_ref, o_ref, key_sc, *accs, topk):
    t = pl.program_id(1)
    tq, kc = DSA_TQ, DSA_KC
    S = key_sc.shape[0]
    q0 = t * tq
    n_chunks = (q0 + tq + kc - 1) // kc
    key_io = lax.broadcasted_iota(I32, (kc, tq), 0)
    q_pos = q0 + lax.broadcasted_iota(I32, (kc, tq), 1)
    w = wit_ref[...]

    def score_chunk(c, carry):
        k0 = pl.multiple_of(c * kc, kc)
        kic = ki_ref[pl.ds(k0, kc), :]
        acc = jnp.zeros((kc, tq), F32)
        for hd in range(IDX_HEADS):
            lg = _dot(kic, qit_ref[hd * IDX_DIM:(hd + 1) * IDX_DIM, :])
            acc = acc + jnp.maximum(lg, 0.0) * w[hd:hd + 1, :]
        acc = jnp.where(acc == 0.0, 0.0, acc)
        key = _sortable_key(acc)
        key_sc[pl.ds(k0, kc), :] = jnp.where(k0 + key_io <= q_pos, key, INT_MIN)
        return carry

    lax.fori_loop(0, n_chunks, score_chunk, 0)

    def count(pred_fn):
        def cbody(c, acc):
            k0 = pl.multiple_of(c * kc, kc)
            hit = jnp.where(pred_fn(key_sc[pl.ds(k0, kc), :], k0), 1, 0)
            return acc + jnp.sum(hit.reshape(kc // SUBLANES, SUBLANES, tq), axis=0)
        acc = lax.fori_loop(0, n_chunks, cbody, jnp.zeros((SUBLANES, tq), I32))
        return jnp.sum(acc, axis=0, keepdims=True)

    c_nonneg = count(lambda kk, k0: kk >= 0)
    thr0 = jnp.where(c_nonneg >= topk, 0, INT_MIN).astype(I32)

    def bit_step(i, thr):
        cand = thr | lax.shift_left(jnp.int32(1), 30 - i)
        cnt = count(lambda kk, k0: kk >= cand)
        return jnp.where(cnt >= topk, cand, thr)

    thr = lax.fori_loop(0, 31, bit_step, thr0)

    n_gt = count(lambda kk, k0: kk > thr)
    n_ge = count(lambda kk, k0: kk >= thr)
    need = topk - n_gt
    overflow = (n_ge > topk) & (thr != INT_MIN)
    any_overflow = jnp.max(jnp.where(overflow, 1, 0)) > 0
    nbits = max(1, (S - 1).bit_length())

    def cut_search():
        def step(i, lo):
            cand = lo | lax.shift_left(jnp.int32(1), nbits - 1 - i)
            cnt = count(lambda kk, k0: (kk == thr) & (k0 + key_io < cand))
            return jnp.where(cnt >= need, lo, cand)
        return lax.fori_loop(0, nbits, step, jnp.zeros((1, tq), I32))

    jcut = lax.cond(any_overflow, cut_search, lambda: jnp.zeros((1, tq), I32))
    jcut = jnp.where(overflow, jcut, S)
    thr_ge = jnp.where(thr == INT_MIN, INT_MIN + 1, thr)

    pairs = range(DSA_HEADS // 2)
    qst = [jnp.concatenate([qdt_ref[(2 * g + j) * HEAD_DIM:(2 * g + j + 1) * HEAD_DIM, :] for j in range(2)],
                           axis=1) for g in pairs]
    for g in pairs:
        accs[g][...] = jnp.zeros(accs[g].shape, F32)

    ka = DSA_KA
    pos_io = lax.broadcasted_iota(I32, (ka, tq), 0)

    def attn_chunk(c, carry):
        k0 = pl.multiple_of(c * ka, ka)
        kk = key_sc[pl.ds(k0, ka), :]
        bias = lax.cond(
            any_overflow,
            lambda: jnp.where((kk > thr_ge) | ((kk == thr_ge) & (k0 + pos_io <= jcut)), 0.0, NEG),
            lambda: jnp.where(kk >= thr_ge, 0.0, NEG))
        bias2 = jnp.concatenate([bias, bias], axis=1)
        kdc = kd_ref[pl.ds(k0, ka), :]
        vtc = vdt_ref[c]
        ss = [_dot(kdc, qst[g]) + bias2 for g in pairs]
        old = [accs[g][...] for g in pairs]
        out, new = [], []
        for g in pairs:
            m_old, l_old = carry[2 * g], carry[2 * g + 1]
            m_new = jnp.maximum(m_old, jnp.max(ss[g], axis=0, keepdims=True))
            alpha = jnp.exp2(m_old - m_new)
            p = jnp.exp2(ss[g] - m_new)
            new.append(alpha * old[g] + _dot(vtc, p.astype(BF16)))
            out += [m_new, alpha * l_old + jnp.sum(p, axis=0, keepdims=True)]
        for g in pairs:
            accs[g][...] = new[g]
        return tuple(out)

    init = (jnp.full((1, 2 * tq), NEG, F32), jnp.zeros((1, 2 * tq), F32)) * len(pairs)
    fin = lax.fori_loop(0, (q0 + tq + ka - 1) // ka, attn_chunk, init)
    for g in pairs:
        out_t = accs[g][...] * (1.0 / fin[2 * g + 1])
        for j in range(2):
            hd = 2 * g + j
            o_ref[:, hd * HEAD_DIM:(hd + 1) * HEAD_DIM] = out_t[:, j * tq:(j + 1) * tq].T


def _dsa_attn(q_it, w_it, q_dt, k_i, k_d, v_dt, B, S):
    T = B * S
    tq = DSA_TQ
    nq = S // tq
    topk = min(DSA_MAX_TOPK, S // 4)
    qcol = lambda b, t: (0, b * nq + t)
    seq = lambda b, t: (b, 0)
    return pl.pallas_call(
        functools.partial(_dsa_kernel, topk=topk),
        out_shape=jax.ShapeDtypeStruct((T, DSA_W), F32),
        grid=(B, nq),
        in_specs=[
            pl.BlockSpec((IDX_HEADS * IDX_DIM, tq), qcol),
            pl.BlockSpec((IDX_HEADS, tq), qcol),
            pl.BlockSpec((DSA_W, tq), qcol),
            pl.BlockSpec((S, IDX_DIM), seq),
            pl.BlockSpec((S, HEAD_DIM), seq),
            pl.BlockSpec((S // DSA_KA, HEAD_DIM, DSA_KA), lambda b, t: (b, 0, 0)),
        ],
        out_specs=pl.BlockSpec((tq, DSA_W), lambda b, t: (b * nq + t, 0)),
        scratch_shapes=[pltpu.VMEM((S, tq), I32)]
        + [pltpu.VMEM((HEAD_DIM, 2 * tq), F32) for _ in range(DSA_HEADS // 2)],
        compiler_params=_cparams(("parallel", "arbitrary")),
        name="dsa_attn",
    )(q_it, w_it, q_dt, k_i, k_d, v_dt)


def _first_index_of_max(v, row_io, n_rows):
    m = jnp.max(v, axis=0, keepdims=True)
    idx = jnp.min(jnp.where(v == m, row_io, n_rows), axis=0, keepdims=True)
    return m, idx


def _post_kernel(x_ref, om_ref, od_ref, mod_ref, gm_ref, gd_ref, wout_ref, gffn_ref, wgu_ref, wds_ref,
                 wrh_ref, wrl_ref, rb_ref, tri_ref,
                 xpart_ref, h2r_ref, eidx_ref, rank_ref, gate_ref, cnt_ref, base_sc):
    i = pl.program_id(0)
    tm = x_ref.shape[0]
    gt1 = mod_ref[0, 2:3, :]
    sh2 = mod_ref[0, 3:4, :]
    sc2 = mod_ref[0, 4:5, :]
    gt2 = mod_ref[0, 5:6, :]

    mixed = jnp.concatenate([_rms(om_ref[...], gm_ref[...]), _rms(od_ref[...], gd_ref[...])], axis=1)
    x1 = x_ref[...] + gt1 * _dot(mixed.astype(BF16), wout_ref[...])
    h2 = _rms(x1, gffn_ref[...]) * (1.0 + sc2) + sh2
    h2b = h2.astype(BF16)

    au = _dot(h2b, wgu_ref[...])
    hs = (_silu(au[:, :D_SHARED]) * au[:, D_SHARED:]).astype(BF16)
    xpart_ref[...] = x1 + gt2 * _dot(hs, wds_ref[...])

    for c, slab in enumerate(_pack_row_words(h2)):
        _store_word_slab(h2r_ref, 0, tm, c, slab)

    h2lo = (h2 - h2b.astype(F32)).astype(BF16)
    logits = _dot_nt(wrh_ref[...], h2b) + _dot_nt(wrl_ref[...], h2b) + _dot_nt(wrh_ref[...], h2lo)
    scores = 1.0 / (1.0 + jnp.exp(-logits))
    biased = scores + rb_ref[...]

    g_io = lax.broadcasted_iota(I32, (GROUP_SIZE, tm), 0)
    gs_rows = []
    for g in range(N_GROUPS):
        blk = biased[g * GROUP_SIZE:(g + 1) * GROUP_SIZE, :]
        m1, i1 = _first_index_of_max(blk, g_io, GROUP_SIZE)
        m2 = jnp.max(jnp.where(g_io == i1, -jnp.inf, blk), axis=0, keepdims=True)
        gs_rows.append(m1 + m2)
    gs = jnp.concatenate(gs_rows, axis=0)
    gi = lax.broadcasted_iota(I32, (N_GROUPS, tm), 0)
    grank = jnp.zeros((N_GROUPS, tm), I32)
    for m in range(N_GROUPS):
        gm = gs[m:m + 1, :]
        grank = grank + jnp.where((gm > gs) | ((gm == gs) & (m < gi)), 1, 0)
    gsel = grank < TOPK_GROUPS
    masked = jnp.concatenate(
        [jnp.where(gsel[g:g + 1, :], biased[g * GROUP_SIZE:(g + 1) * GROUP_SIZE, :], -jnp.inf)
         for g in range(N_GROUPS)], axis=0)

    e_io = lax.broadcasted_iota(I32, (N_EXPERTS, tm), 0)
    e_rows, s_rows = [], []
    for _ in range(EXPERT_TOPK):
        _, idx = _first_index_of_max(masked, e_io, N_EXPERTS)
        hit = e_io == idx
        e_rows.append(idx)
        s_rows.append(jnp.sum(jnp.where(hit, scores, 0.0), axis=0, keepdims=True))
        masked = jnp.where(hit, -jnp.inf, masked)
    eidx = jnp.concatenate(e_rows, axis=0)
    sk = jnp.concatenate(s_rows, axis=0)
    gate_ref[...] = sk / jnp.sum(sk, axis=0, keepdims=True) * ROUTED_SCALE
    eidx_ref[...] = eidx

    @pl.when(i == 0)
    def _():
        base_sc[...] = jnp.zeros(base_sc.shape, F32)

    chosen = jnp.zeros((N_EXPERTS, tm), F32)
    for k in range(EXPERT_TOPK):
        chosen = chosen + jnp.where(e_io == e_rows[k], 1.0, 0.0)
    incl = _dot(chosen.astype(BF16), tri_ref[...])
    pos = base_sc[...] + incl - 1.0
    rank_ref[...] = jnp.concatenate(
        [jnp.sum(jnp.where(e_io == e_rows[k], pos, 0.0), axis=0, keepdims=True)
         for k in range(EXPERT_TOPK)], axis=0).astype(I32)
    base_sc[...] = base_sc[...] + incl[:, tm - 1:tm]
    cnt_ref[...] = jnp.broadcast_to(base_sc[...], cnt_ref.shape)


def _post_attn(x2, o_m, o_d, mod3, g_moba, g_dsa, w_out, g_ffn, w_gu_s, w_down_s, wr_hi, wr_lo, rbias, S):
    T, D = x2.shape
    tm = POST_TM
    nt_per_seq = S // tm
    row = lambda i: (i, 0)
    full = lambda i: (0, 0)
    tri = (jnp.arange(tm)[:, None] <= jnp.arange(tm)[None, :]).astype(BF16)
    n_words = D // ROW_WORDS
    return pl.pallas_call(
        _post_kernel,
        out_shape=[
            jax.ShapeDtypeStruct((T, D), F32),
            jax.ShapeDtypeStruct((T, n_words, LANES), I32),
            jax.ShapeDtypeStruct((EXPERT_TOPK, T), I32),
            jax.ShapeDtypeStruct((EXPERT_TOPK, T), I32),
            jax.ShapeDtypeStruct((EXPERT_TOPK, T), F32),
            jax.ShapeDtypeStruct((N_EXPERTS, LANES), F32),
        ],
        grid=(T // tm,),
        in_specs=[
            pl.BlockSpec((tm, D), row),
            pl.BlockSpec((tm, MOBA_W), row),
            pl.BlockSpec((tm, DSA_W), row),
            pl.BlockSpec((1, 6, D), lambda i: (i // nt_per_seq, 0, 0)),
            pl.BlockSpec((1, MOBA_W), full),
            pl.BlockSpec((1, DSA_W), full),
            pl.BlockSpec(w_out.shape, full),
            pl.BlockSpec((1, D), full),
            pl.BlockSpec(w_gu_s.shape, full),
            pl.BlockSpec(w_down_s.shape, full),
            pl.BlockSpec(wr_hi.shape, full),
            pl.BlockSpec(wr_lo.shape, full),
            pl.BlockSpec((N_EXPERTS, 1), full),
            pl.BlockSpec((tm, tm), full),
        ],
        out_specs=[
            pl.BlockSpec((tm, D), row),
            pl.BlockSpec((tm, n_words, LANES), lambda i: (i, 0, 0)),
            pl.BlockSpec((EXPERT_TOPK, tm), lambda i: (0, i)),
            pl.BlockSpec((EXPERT_TOPK, tm), lambda i: (0, i)),
            pl.BlockSpec((EXPERT_TOPK, tm), lambda i: (0, i)),
            pl.BlockSpec((N_EXPERTS, LANES), full),
        ],
        scratch_shapes=[pltpu.VMEM((N_EXPERTS, 1), F32)],
        compiler_params=_cparams(("arbitrary",)),
        name="post_attn",
    )(x2, o_m, o_d, mod3, g_moba, g_dsa, w_out, g_ffn, w_gu_s, w_down_s, wr_hi, wr_lo, rbias, tri)


def _row_copy_wait(rows_hbm, n_rows, sem):
    blk = rows_hbm.at[pl.ds(0, n_rows)]
    pltpu.make_async_copy(blk, blk, sem).wait()


def _dispatch_kernel(ps_ref, e_ref, r_ref, h2r_ref, dest_ref, xs_ref, sem):
    tm = e_ref.shape[1]

    def body(t, carry):
        for k in range(EXPERT_TOPK):
            d = ps_ref[e_ref[k, t]] + r_ref[k, t]
            dest_ref[k, t] = d
            pltpu.make_async_copy(h2r_ref.at[t], xs_ref.at[d], sem).start(priority=k % 2)
        return carry

    lax.fori_loop(0, tm, body, 0)
    _row_copy_wait(xs_ref, tm * EXPERT_TOPK, sem)


def _dispatch(pad_starts, eidx, rank, h2r, n_rows_padded):
    T, n_words, _ = h2r.shape
    tm = DISP_TM
    smem_blk = lambda: pl.BlockSpec((EXPERT_TOPK, tm), lambda i, ps: (0, i), memory_space=pltpu.SMEM)
    return pl.pallas_call(
        _dispatch_kernel,
        out_shape=[jax.ShapeDtypeStruct((EXPERT_TOPK, T), I32),
                   jax.ShapeDtypeStruct((n_rows_padded, n_words, LANES), I32)],
        grid_spec=pltpu.PrefetchScalarGridSpec(
            num_scalar_prefetch=1,
            grid=(T // tm,),
            in_specs=[smem_blk(), smem_blk(), pl.BlockSpec((tm, n_words, LANES), lambda i, ps: (i, 0, 0))],
            out_specs=[smem_blk(), pl.BlockSpec(memory_space=pl.ANY)],
            scratch_shapes=[pltpu.SemaphoreType.DMA],
        ),
        compiler_params=_cparams(("arbitrary",), disable_bounds_checks=True),
        name="dispatch",
    )(pad_starts, eidx, rank, h2r)


def _expert_kernel(be_ref, nv_ref, xs_ref, wg_ref, wu_ref, wd_ref, ys_ref, wg_sc, wu_sc, wd_sc):
    j = pl.program_id(0)
    bm, n_words, _ = xs_ref.shape
    nv = nv_ref[j]

    @pl.when((j == 0) | (be_ref[j] != be_ref[jnp.maximum(j - 1, 0)]))
    def _():
        wg_sc[...] = wg_ref[...].astype(BF16)
        wu_sc[...] = wu_ref[...].astype(BF16)
        wd_sc[...] = wd_ref[...].astype(BF16)

    @pl.when(nv > 0)
    def _():
        feats = []
        for c in range(n_words):
            feats += _unpack_row_words(_load_word_slab(xs_ref, 0, bm, c))
        x = jnp.concatenate(feats, axis=1)
        valid = lax.broadcasted_iota(I32, (bm, 1), 0) < nv
        xb = jnp.where(valid, x, 0.0).astype(BF16)
        a = _dot(xb, wg_sc[...])
        u = _dot(xb, wu_sc[...])
        hmid = (_silu(a) * u).astype(BF16)
        ob = _dot(hmid, wd_sc[...])
        for c, slab in enumerate(_pack_row_words(ob)):
            _store_word_slab(ys_ref, 0, bm, c, slab)

    @pl.when(nv == 0)
    def _():
        ys_ref[...] = jnp.zeros(ys_ref.shape, I32)


def _experts(block_expert, n_valid, xs, w_gate_e, w_up_e, w_down_e):
    n_blocks = block_expert.shape[0]
    E, D, DE = w_gate_e.shape
    n_words = xs.shape[1]
    bm = EXP_BM
    rows = lambda j, be, nv: (j, 0, 0)
    wsel = lambda j, be, nv: (be[j], 0, 0)
    return pl.pallas_call(
        _expert_kernel,
        out_shape=jax.ShapeDtypeStruct(xs.shape, I32),
        grid_spec=pltpu.PrefetchScalarGridSpec(
            num_scalar_prefetch=2,
            grid=(n_blocks,),
            in_specs=[
                pl.BlockSpec((bm, n_words, LANES), rows),
                pl.BlockSpec((None, D, DE), wsel),
                pl.BlockSpec((None, D, DE), wsel),
                pl.BlockSpec((None, DE, D), wsel),
            ],
            out_specs=pl.BlockSpec((bm, n_words, LANES), rows),
            scratch_shapes=[pltpu.VMEM((D, DE), BF16), pltpu.VMEM((D, DE), BF16), pltpu.VMEM((DE, D), BF16)],
        ),
        compiler_params=_cparams(("arbitrary",)),
        name="experts",
    )(block_expert, n_valid, xs, w_gate_e, w_up_e, w_down_e)


def _combine_kernel(dcur_ref, dnxt_ref, g_ref, xpart_ref, mod_ref, gfin_ref, ys_ref, o_ref,
                    buf0, buf1, sem0, sem1):
    i = pl.program_id(0)
    n_steps = pl.num_programs(0)
    tm = xpart_ref.shape[0]
    n_words = buf0.shape[1]

    def issue(dest_ref, buf, sem):
        def body(t, carry):
            for k in range(EXPERT_TOPK):
                pltpu.make_async_copy(ys_ref.at[dest_ref[k, t]], buf.at[k * tm + t], sem).start(priority=k % 2)
            return carry
        lax.fori_loop(0, tm, body, 0)

    def reduce_tile(buf, sem):
        _row_copy_wait(ys_ref, tm * EXPERT_TOPK, sem)
        gt2 = mod_ref[0, 5:6, :]
        g = g_ref[...]
        cols = []
        for c in range(n_words):
            lo = jnp.zeros((tm, LANES), F32)
            hi = jnp.zeros((tm, LANES), F32)
            for k in range(EXPERT_TOPK):
                a, b = _unpack_row_words(_load_word_slab(buf, k * tm, tm, c))
                gk = g[:, k:k + 1]
                lo = lo + gk * a
                hi = hi + gk * b
            cols += [lo, hi]
        routed = jnp.concatenate(cols, axis=1)
        o_ref[...] = _rms(xpart_ref[...] + gt2 * routed, gfin_ref[...])

    @pl.when(i == 0)
    def _():
        issue(dcur_ref, buf0, sem0)

    for parity, (cur, nxt) in enumerate((((buf0, sem0), (buf1, sem1)), ((buf1, sem1), (buf0, sem0)))):
        @pl.when(i % 2 == parity)
        def _():
            @pl.when(i + 1 < n_steps)
            def _():
                issue(dnxt_ref, *nxt)
            reduce_tile(*cur)


def _combine(dest, gates_t, xpart, mod3, g_final, ys, S):
    T, D = xpart.shape
    tm = COMB_TM
    n_steps = T // tm
    nt_per_seq = S // tm
    n_words = ys.shape[1]
    return pl.pallas_call(
        _combine_kernel,
        out_shape=jax.ShapeDtypeStruct((T, D), F32),
        grid=(n_steps,),
        in_specs=[
            pl.BlockSpec((EXPERT_TOPK, tm), lambda i: (0, i), memory_space=pltpu.SMEM),
            pl.BlockSpec((EXPERT_TOPK, tm), lambda i: (0, jnp.minimum(i + 1, n_steps - 1)),
                         memory_space=pltpu.SMEM),
            pl.BlockSpec((tm, EXPERT_TOPK), lambda i: (i, 0)),
            pl.BlockSpec((tm, D), lambda i: (i, 0)),
            pl.BlockSpec((1, 6, D), lambda i: (i // nt_per_seq, 0, 0)),
            pl.BlockSpec((1, D), lambda i: (0, 0)),
            pl.BlockSpec(memory_space=pl.ANY),
        ],
        out_specs=pl.BlockSpec((tm, D), lambda i: (i, 0)),
        scratch_shapes=[pltpu.VMEM((EXPERT_TOPK * tm, n_words, LANES), I32),
                        pltpu.VMEM((EXPERT_TOPK * tm, n_words, LANES), I32),
                        pltpu.SemaphoreType.DMA, pltpu.SemaphoreType.DMA],
        compiler_params=_cparams(("arbitrary",), disable_bounds_checks=True),
        name="combine",
    )(dest, dest, gates_t, xpart, mod3, g_final, ys)


def _layer(x2, mod3, S, g_mix, w_in, g_kv, w_kv_up, g_moba_out, g_dsa_out, w_out, g_ffn, w_router,
           router_bias, w_gate_e, w_up_e, w_down_e, w_gate_s, w_up_s, w_down_s, g_final, tab_h, half_h,
           tab_i, half_i):
    T, D = x2.shape
    B = T // S
    w_in_p = jnp.pad(w_in, ((0, 0), (0, _C_END - w_in.shape[1]))).astype(BF16)
    (q_mt, k_m, v_mt, kmean, q_dt, k_d, v_dt, q_it, k_i, w_it) = _in_proj(
        x2, mod3, g_mix.reshape(1, D), w_in_p, g_kv.reshape(1, KV_LORA), w_kv_up.astype(BF16),
        tab_h, half_h, tab_i, half_i, S)
    o_m = _moba_attn(q_mt, k_m, v_mt, kmean, B, S)
    o_d = _dsa_attn(q_it, w_it, q_dt, k_i, k_d, v_dt, B, S)

    wr_t = w_router.T
    wr_hi = wr_t.astype(BF16)
    wr_lo = (wr_t - wr_hi.astype(F32)).astype(BF16)
    w_gu_s = jnp.concatenate([w_gate_s, w_up_s], axis=1).astype(BF16)
    xpart, h2r, eidx, rank, gates, cnt = _post_attn(
        x2, o_m, o_d, mod3, g_moba_out.reshape(1, MOBA_W), g_dsa_out.reshape(1, DSA_W), w_out.astype(BF16),
        g_ffn.reshape(1, D), w_gu_s, w_down_s.astype(BF16), wr_hi, wr_lo,
        router_bias.reshape(N_EXPERTS, 1), S)

    bm = EXP_BM
    n_blocks = T * EXPERT_TOPK // bm + N_EXPERTS
    counts = cnt[:, 0].astype(I32)
    padded = (counts + bm - 1) // bm * bm
    pad_ends = jnp.cumsum(padded)
    pad_starts = (pad_ends - padded).astype(I32)
    blk_start = jnp.arange(n_blocks, dtype=I32) * bm
    block_expert = jnp.minimum(jnp.searchsorted(pad_ends, blk_start, side='right'), N_EXPERTS - 1).astype(I32)
    n_valid = jnp.clip(counts[block_expert] - (blk_start - pad_starts[block_expert]), 0, bm).astype(I32)
    n_valid = jnp.where(blk_start < pad_ends[-1], n_valid, 0)

    dest, xs = _dispatch(pad_starts, eidx, rank, h2r, n_blocks * bm)
    ys = _experts(block_expert, n_valid, xs, w_gate_e, w_up_e, w_down_e)
    return _combine(dest, gates.T, xpart, mod3, g_final.reshape(1, D), ys, S)


def kernel(x, c, w_ada, b_ada, g_mix, w_in, g_kv, w_kv_up, g_moba_out, g_dsa_out, w_out, g_ffn, w_router,
           router_bias, w_gate_e, w_up_e, w_down_e, w_gate_s, w_up_s, w_down_s, g_final):
    B, S, D = x.shape
    depth = w_ada.shape[0]
    assert depth == 1, "the final norm is fused into the single layer"
    assert S % PROJ_TM == 0 and S % DSA_KC == 0 and S % POST_TM == 0 and S >= 4 * DSA_MAX_TOPK
    tab_h, half_h = _rope_tables(S, HEAD_DIM, 1)
    tab_i, half_i = _rope_tables(S, IDX_DIM, LANES // IDX_DIM)
    x2 = x.reshape(B * S, D)
    sq = lambda a: a.reshape(a.shape[1:])
    mod3 = _ada_mod(c, sq(w_ada), sq(b_ada)).reshape(B, 6, D)
    out = _layer(x2, mod3, S, sq(g_mix), sq(w_in), sq(g_kv), sq(w_kv_up), sq(g_moba_out), sq(g_dsa_out),
                 sq(w_out), sq(g_ffn), sq(w_router), sq(router_bias), sq(w_gate_e), sq(w_up_e), sq(w_down_e),
                 sq(w_gate_s), sq(w_up_s), sq(w_down_s), g_final, tab_h, half_h, tab_i, half_i)
    return out.reshape(B, S, D)
```

```python
import functools

import jax
import jax.numpy as jnp
from jax import lax
from jax.experimental import pallas as pl
from jax.experimental.pallas import tpu as pltpu

HEAD_DIM = 128
MOBA_HEADS = 4
DSA_HEADS = 4
MOBA_W = MOBA_HEADS * HEAD_DIM
DSA_W = DSA_HEADS * HEAD_DIM
MOBA_BLOCK = 256
MOBA_TOPK = 3
DSA_MAX_TOPK = 256
KV_LORA = 256
IDX_HEADS = 8
IDX_DIM = 64
ROPE_THETA = 500000.0
ROPE_FRACTION_DIV = 4
N_EXPERTS = 256
EXPERT_TOPK = 8
N_GROUPS = 8
TOPK_GROUPS = 4
GROUP_SIZE = N_EXPERTS // N_GROUPS
D_EXPERT = 256
D_SHARED = 256
ROUTED_SCALE = 2.5
EPS = 1e-6

LANES = 128
SUBLANES = 8
VMEM_LIMIT = 56 * 1024 * 1024

PROJ_TM = 512
DSA_TQ = 256
DSA_KC = 512
DSA_KA = 256
POST_TM = 256
DISP_TM = 256
EXP_BM = 256
COMB_TM = 128
NEG = -1e30
INT_MIN = -2147483648
LOG2E = 1.4426950408889634

F32 = jnp.float32
BF16 = jnp.bfloat16
I32 = jnp.int32


def _cparams(sem, **kw):
    return pltpu.CompilerParams(dimension_semantics=sem, vmem_limit_bytes=VMEM_LIMIT, **kw)


def _dot(a, b):
    return jnp.dot(a, b, preferred_element_type=F32)


def _dot_nt(a, b):
    return lax.dot_general(a, b, (((1,), (1,)), ((), ())), preferred_element_type=F32)


def _silu(x):
    return x * (1.0 / (1.0 + jnp.exp(-x)))


def _rms(x, g):
    return x * lax.rsqrt(jnp.mean(x * x, axis=-1, keepdims=True) + EPS) * g


ROW_WORDS = 2 * LANES


def _pack_row_words(x):
    slabs = []
    for c in range(x.shape[1] // ROW_WORDS):
        lo = pltpu.bitcast(x[:, c * ROW_WORDS:c * ROW_WORDS + LANES].astype(BF16).astype(F32), I32)
        hi = pltpu.bitcast(x[:, c * ROW_WORDS + LANES:(c + 1) * ROW_WORDS].astype(BF16).astype(F32), I32)
        slabs.append(lax.shift_right_logical(lo, 16) | hi)
    return slabs


def _word_slab_index(rows_ref, row0, n_rows, c):
    n_words = rows_ref.shape[1]
    flat = rows_ref.reshape(rows_ref.shape[0] * n_words, LANES)
    return flat, pl.ds(row0 * n_words + c, n_rows, stride=n_words)


def _load_word_slab(rows_ref, row0, n_rows, c):
    flat, idx = _word_slab_index(rows_ref, row0, n_rows, c)
    return flat[idx, :]


def _store_word_slab(rows_ref, row0, n_rows, c, value):
    flat, idx = _word_slab_index(rows_ref, row0, n_rows, c)
    flat[idx, :] = value


def _unpack_row_words(u):
    return pltpu.bitcast(lax.shift_left(u, 16), F32), pltpu.bitcast(u & jnp.int32(-65536), F32)


def _ada_kernel(c_ref, w_ref, b_ref, o_ref):
    ca = _silu(c_ref[...])
    o_ref[...] = jnp.dot(ca, w_ref[...], preferred_element_type=F32,
                         precision=lax.Precision.HIGHEST) + b_ref[...]


def _ada_mod(c, w_ada, b_ada):
    B, D = c.shape
    N = w_ada.shape[1]
    tn = 1024
    return pl.pallas_call(
        _ada_kernel,
        out_shape=jax.ShapeDtypeStruct((B, N), F32),
        grid=(N // tn,),
        in_specs=[pl.BlockSpec((B, D), lambda j: (0, 0)),
                  pl.BlockSpec((D, tn), lambda j: (0, j)),
                  pl.BlockSpec((1, tn), lambda j: (0, j))],
        out_specs=pl.BlockSpec((B, tn), lambda j: (0, j)),
        compiler_params=_cparams(("arbitrary",)),
        name="ada_mod",
    )(c, w_ada, b_ada.reshape(1, N))


def _rope_tables(seq, head_dim, heads_per_vreg):
    rot = head_dim // ROPE_FRACTION_DIV
    half = rot // 2
    inv = jnp.float32(ROPE_THETA) ** (-(jnp.arange(0, rot, 2, dtype=F32) / rot))
    ang = jnp.arange(seq, dtype=F32)[:, None] * inv[None, :]
    cos, sin = jnp.cos(ang), jnp.sin(ang)
    ones = jnp.ones((seq, head_dim - rot), F32)
    zeros_h = jnp.zeros((seq, half), F32)
    zeros_r = jnp.zeros((seq, head_dim - rot), F32)
    c = jnp.concatenate([cos, cos, ones], axis=1)
    sp = jnp.concatenate([zeros_h, sin, zeros_r], axis=1)
    sm = jnp.concatenate([-sin, zeros_h, zeros_r], axis=1)
    rep = lambda t: jnp.tile(t, (1, heads_per_vreg))
    return jnp.stack([rep(c), rep(sp), rep(sm)], axis=0), half


def _rope(x, tab_ref, half):
    return (x * tab_ref[0] + pltpu.roll(x, half, 1) * tab_ref[1]
            + pltpu.roll(x, LANES - half, 1) * tab_ref[2])


_C_QM, _C_KM, _C_VM, _C_QD = 0, MOBA_W, 2 * MOBA_W, 3 * MOBA_W
_C_CKV = 3 * MOBA_W + DSA_W
_C_QI = _C_CKV + KV_LORA
_C_KI = _C_QI + IDX_HEADS * IDX_DIM
_C_END = _C_KI + LANES


def _in_proj_kernel(x_ref, mod_ref, gmix_ref, w_ref, gkv_ref, wkv_ref, tabh_ref, tabi_ref,
                    qmt_ref, km_ref, vmt_ref, kmean_ref, qdt_ref, kd_ref, vdt_ref, qit_ref, ki_ref, wit_ref,
                    *, half_h, half_i):
    tm = x_ref.shape[0]
    x = x_ref[...]
    sh1 = mod_ref[0, 0:1, :]
    sc1 = mod_ref[0, 1:2, :]
    h = (_rms(x, gmix_ref[...]) * (1.0 + sc1) + sh1).astype(BF16)

    def proj(c0, width):
        return _dot(h, w_ref[:, c0:c0 + width])

    q_scale = HEAD_DIM ** -0.5 * LOG2E
    nblk = tm // MOBA_BLOCK
    qm = proj(_C_QM, MOBA_W)
    km = proj(_C_KM, MOBA_W)
    vm = proj(_C_VM, MOBA_W)
    for hd in range(MOBA_HEADS):
        sl = slice(hd * HEAD_DIM, (hd + 1) * HEAD_DIM)
        qmt_ref[sl, :] = (_rope(qm[:, sl], tabh_ref, half_h) * q_scale).T.astype(BF16)
        kr = _rope(km[:, sl], tabh_ref, half_h)
        km_ref[:, sl] = kr.astype(BF16)
        for blk in range(nblk):
            rows = slice(blk * MOBA_BLOCK, (blk + 1) * MOBA_BLOCK)
            kmean_ref[blk:blk + 1, sl] = jnp.mean(kr[rows], axis=0, keepdims=True)
            vmt_ref[blk, sl, :] = vm[rows, sl].T.astype(BF16)
    qd = proj(_C_QD, DSA_W)
    for hd in range(DSA_HEADS):
        sl = slice(hd * HEAD_DIM, (hd + 1) * HEAD_DIM)
        qdt_ref[sl, :] = (_rope(qd[:, sl], tabh_ref, half_h) * q_scale).T.astype(BF16)
    ckv = proj(_C_CKV, KV_LORA)
    kv = _dot(_rms(ckv, gkv_ref[...]).astype(BF16), wkv_ref[...])
    kd_ref[...] = _rope(kv[:, :HEAD_DIM], tabh_ref, half_h).astype(BF16)
    for ch in range(tm // DSA_KA):
        vdt_ref[ch] = kv[ch * DSA_KA:(ch + 1) * DSA_KA, HEAD_DIM:].T.astype(BF16)
    qi = proj(_C_QI, IDX_HEADS * IDX_DIM)
    for j in range(IDX_HEADS * IDX_DIM // LANES):
        sl = slice(j * LANES, (j + 1) * LANES)
        qit_ref[sl, :] = _rope(qi[:, sl], tabi_ref, half_i).T.astype(BF16)
    kw = proj(_C_KI, LANES)
    ki_ref[...] = _rope(kw, tabi_ref, half_i)[:, :IDX_DIM].astype(BF16)
    wit_ref[...] = kw.T[IDX_DIM:IDX_DIM + IDX_HEADS, :] * (IDX_HEADS ** -0.5 * IDX_DIM ** -0.5)


def _in_proj(x2, mod3, g_mix, w_in_p, g_kv, w_kv_up, tab_h, half_h, tab_i, half_i, S):
    T, D = x2.shape
    tm = PROJ_TM
    nt_per_seq = S // tm
    row = lambda i: (i, 0)
    col = lambda i: (0, i)
    nb = tm // MOBA_BLOCK
    nc = tm // DSA_KA
    outs = [
        jax.ShapeDtypeStruct((MOBA_W, T), BF16),
        jax.ShapeDtypeStruct((T, MOBA_W), BF16),
        jax.ShapeDtypeStruct((T // MOBA_BLOCK, MOBA_W, MOBA_BLOCK), BF16),
        jax.ShapeDtypeStruct((T // tm, nb, MOBA_W), F32),
        jax.ShapeDtypeStruct((DSA_W, T), BF16),
        jax.ShapeDtypeStruct((T, HEAD_DIM), BF16),
        jax.ShapeDtypeStruct((T // DSA_KA, HEAD_DIM, DSA_KA), BF16),
        jax.ShapeDtypeStruct((IDX_HEADS * IDX_DIM, T), BF16),
        jax.ShapeDtypeStruct((T, IDX_DIM), BF16),
        jax.ShapeDtypeStruct((IDX_HEADS, T), F32),
    ]
    out_specs = [
        pl.BlockSpec((MOBA_W, tm), col), pl.BlockSpec((tm, MOBA_W), row),
        pl.BlockSpec((nb, MOBA_W, MOBA_BLOCK), lambda i: (i, 0, 0)),
        pl.BlockSpec((None, nb, MOBA_W), lambda i: (i, 0, 0)),
        pl.BlockSpec((DSA_W, tm), col), pl.BlockSpec((tm, HEAD_DIM), row),
        pl.BlockSpec((nc, HEAD_DIM, DSA_KA), lambda i: (i, 0, 0)),
        pl.BlockSpec((IDX_HEADS * IDX_DIM, tm), col), pl.BlockSpec((tm, IDX_DIM), row),
        pl.BlockSpec((IDX_HEADS, tm), col),
    ]
    res = pl.pallas_call(
        functools.partial(_in_proj_kernel, half_h=half_h, half_i=half_i),
        out_shape=outs,
        grid=(T // tm,),
        in_specs=[
            pl.BlockSpec((tm, D), row),
            pl.BlockSpec((1, 6, D), lambda i: (i // nt_per_seq, 0, 0)),
            pl.BlockSpec((1, D), lambda i: (0, 0)),
            pl.BlockSpec((D, _C_END), lambda i: (0, 0)),
            pl.BlockSpec((1, KV_LORA), lambda i: (0, 0)),
            pl.BlockSpec((KV_LORA, 2 * HEAD_DIM), lambda i: (0, 0)),
            pl.BlockSpec((3, tm, LANES), lambda i: (0, i % nt_per_seq, 0)),
            pl.BlockSpec((3, tm, LANES), lambda i: (0, i % nt_per_seq, 0)),
        ],
        out_specs=out_specs,
        compiler_params=_cparams(("parallel",)),
        name="in_proj",
    )(x2, mod3, g_mix, w_in_p, g_kv, w_kv_up, tab_h, tab_i)
    res = list(res)
    res[3] = res[3].reshape(T // MOBA_BLOCK, MOBA_W)
    return res


def _moba_kernel(qt_ref, k_ref, vt_ref, kmean_ref, o_ref, bias_sc, *accs):
    qi = pl.program_id(1)
    blk = MOBA_BLOCK
    nb = kmean_ref.shape[0]
    heads = range(MOBA_HEADS)
    hsl = [slice(hd * HEAD_DIM, (hd + 1) * HEAD_DIM) for hd in heads]
    qts = [qt_ref[hsl[hd], :] for hd in heads]
    row = lax.broadcasted_iota(I32, (nb, blk), 0)
    past = row < qi
    start = pl.multiple_of(qi * blk, blk)
    k_io = lax.broadcasted_iota(I32, (blk, blk), 0)
    q_io = lax.broadcasted_iota(I32, (blk, blk), 1)

    init = []
    for hd in heads:
        km = kmean_ref[:, hsl[hd]]
        km_hi = km.astype(BF16)
        km_lo = (km - km_hi.astype(F32)).astype(BF16)
        gate = _dot(km_hi, qts[hd]) + _dot(km_lo, qts[hd])
        gate = jnp.where(past, gate, -jnp.inf)
        bias = jnp.full((nb, blk), NEG, F32)
        for _ in range(MOBA_TOPK):
            _, idx = _first_index_of_max(gate, row, nb)
            hit = row == idx
            bias = jnp.where(hit, 0.0, bias)
            gate = jnp.where(hit, -jnp.inf, gate)
        bias_sc[hd] = jnp.where(past, bias, NEG)

        s = _dot(k_ref[pl.ds(start, blk), hsl[hd]], qts[hd])
        s = jnp.where(k_io <= q_io, s, NEG)
        m0 = jnp.max(s, axis=0, keepdims=True)
        p = jnp.exp2(s - m0)
        accs[hd][...] = _dot(vt_ref[qi, hsl[hd], :], p.astype(BF16))
        init += [m0, jnp.sum(p, axis=0, keepdims=True)]

    def body(n, carry):
        st = pl.multiple_of(n * blk, blk)
        kb = k_ref[pl.ds(st, blk), :]
        sbs = [_dot(kb[:, hsl[hd]], qts[hd]) + bias_sc[hd, pl.ds(n, 1), :] for hd in heads]
        old = [accs[hd][...] for hd in heads]
        out, new = [], []
        for hd in heads:
            m_old, l_old = carry[2 * hd], carry[2 * hd + 1]
            m_new = jnp.maximum(m_old, jnp.max(sbs[hd], axis=0, keepdims=True))
            alpha = jnp.exp2(m_old - m_new)
            pb = jnp.exp2(sbs[hd] - m_new)
            new.append(alpha * old[hd] + _dot(vt_ref[n, hsl[hd], :], pb.astype(BF16)))
            out += [m_new, alpha * l_old + jnp.sum(pb, axis=0, keepdims=True)]
        for hd in heads:
            accs[hd][...] = new[hd]
        return tuple(out)

    fin = lax.fori_loop(0, qi, body, tuple(init))
    for hd in heads:
        o_ref[:, hsl[hd]] = (accs[hd][...] * (1.0 / fin[2 * hd + 1])).T


def _moba_attn(q_mt, k_m, v_mt, kmean, B, S):
    T = B * S
    blk = MOBA_BLOCK
    nq = S // blk
    return pl.pallas_call(
        _moba_kernel,
        out_shape=jax.ShapeDtypeStruct((T, MOBA_W), F32),
        grid=(B, nq),
        in_specs=[
            pl.BlockSpec((MOBA_W, blk), lambda b, i: (0, b * nq + i)),
            pl.BlockSpec((S, MOBA_W), lambda b, i: (b, 0)),
            pl.BlockSpec((nq, MOBA_W, blk), lambda b, i: (b, 0, 0)),
            pl.BlockSpec((nq, MOBA_W), lambda b, i: (b, 0)),
        ],
        out_specs=pl.BlockSpec((blk, MOBA_W), lambda b, i: (b * nq + i, 0)),
        scratch_shapes=[pltpu.VMEM((MOBA_HEADS, nq, blk), F32)]
        + [pltpu.VMEM((HEAD_DIM, blk), F32) for _ in range(MOBA_HEADS)],
        compiler_params=_cparams(("parallel", "arbitrary")),
        name="moba_attn",
    )(q_mt, k_m, v_mt, kmean)


def _sortable_key(x):
    b = pltpu.bitcast(x, I32)
    return jnp.where(b >= 0, b, b ^ jnp.int32(0x7FFFFFFF))


def _dsa_kernel(qit_ref, wit_ref, qdt_ref, ki_ref, kd_ref, vdt_ref, o_ref, key_sc, *accs, topk):
    t = pl.program_id(1)
    tq, kc = DSA_TQ, DSA_KC
    S = key_sc.shape[0]
    q0 = t * tq
    n_chunks = (q0 + tq + kc - 1) // kc
    key_io = lax.broadcasted_iota(I32, (kc, tq), 0)
    q_pos = q0 + lax.broadcasted_iota(I32, (kc, tq), 1)
    w = wit_ref[...]

    def score_chunk(c, carry):
        k0 = pl.multiple_of(c * kc, kc)
        kic = ki_ref[pl.ds(k0, kc), :]
        acc = jnp.zeros((kc, tq), F32)
        for hd in range(IDX_HEADS):
            lg = _dot(kic, qit_ref[hd * IDX_DIM:(hd + 1) * IDX_DIM, :])
            acc = acc + jnp.maximum(lg, 0.0) * w[hd:hd + 1, :]
        acc = jnp.where(acc == 0.0, 0.0, acc)
        key = _sortable_key(acc)
        key_sc[pl.ds(k0, kc), :] = jnp.where(k0 + key_io <= q_pos, key, INT_MIN)
        return carry

    lax.fori_loop(0, n_chunks, score_chunk, 0)

    def count(pred_fn):
        def cbody(c, acc):
            k0 = pl.multiple_of(c * kc, kc)
            hit = jnp.where(pred_fn(key_sc[pl.ds(k0, kc), :], k0), 1, 0)
            return acc + jnp.sum(hit.reshape(kc // SUBLANES, SUBLANES, tq), axis=0)
        acc = lax.fori_loop(0, n_chunks, cbody, jnp.zeros((SUBLANES, tq), I32))
        return jnp.sum(acc, axis=0, keepdims=True)

    c_nonneg = count(lambda kk, k0: kk >= 0)
    thr0 = jnp.where(c_nonneg >= topk, 0, INT_MIN).astype(I32)

    def bit_step(i, thr):
        cand = thr | lax.shift_left(jnp.int32(1), 30 - i)
        cnt = count(lambda kk, k0: kk >= cand)
        return jnp.where(cnt >= topk, cand, thr)

    thr = lax.fori_loop(0, 31, bit_step, thr0)

    n_gt = count(lambda kk, k0: kk > thr)
    n_ge = count(lambda kk, k0: kk >= thr)
    need = topk - n_gt
    overflow = (n_ge > topk) & (thr != INT_MIN)
    any_overflow = jnp.max(jnp.where(overflow, 1, 0)) > 0
    nbits = max(1, (S - 1).bit_length())

    def cut_search():
        def step(i, lo):
            cand = lo | lax.shift_left(jnp.int32(1), nbits - 1 - i)
            cnt = count(lambda kk, k0: (kk == thr) & (k0 + key_io < cand))
            return jnp.where(cnt >= need, lo, cand)
        return lax.fori_loop(0, nbits, step, jnp.zeros((1, tq), I32))

    jcut = lax.cond(any_overflow, cut_search, lambda: jnp.zeros((1, tq), I32))
    jcut = jnp.where(overflow, jcut, S)
    thr_ge = jnp.where(thr == INT_MIN, INT_MIN + 1, thr)

    pairs = range(DSA_HEADS // 2)
    qst = [jnp.concatenate([qdt_ref[(2 * g + j) * HEAD_DIM:(2 * g + j + 1) * HEAD_DIM, :] for j in range(2)],
                           axis=1) for g in pairs]
    for g in pairs:
        accs[g][...] = jnp.zeros(accs[g].shape, F32)

    ka = DSA_KA
    pos_io = lax.broadcasted_iota(I32, (ka, tq), 0)

    def attn_chunk(c, carry):
        k0 = pl.multiple_of(c * ka, ka)
        kk = key_sc[pl.ds(k0, ka), :]
        bias = lax.cond(
            any_overflow,
            lambda: jnp.where((kk > thr_ge) | ((kk == thr_ge) & (k0 + pos_io <= jcut)), 0.0, NEG),
            lambda: jnp.where(kk >= thr_ge, 0.0, NEG))
        bias2 = jnp.concatenate([bias, bias], axis=1)
        kdc = kd_ref[pl.ds(k0, ka), :]
        vtc = vdt_ref[c]
        ss = [_dot(kdc, qst[g]) + bias2 for g in pairs]
        old = [accs[g][...] for g in pairs]
        out, new = [], []
        for g in pairs:
            m_old, l_old = carry[2 * g], carry[2 * g + 1]
            m_new = jnp.maximum(m_old, jnp.max(ss[g], axis=0, keepdims=True))
            alpha = jnp.exp2(m_old - m_new)
            p = jnp.exp2(ss[g] - m_new)
            new.append(alpha * old[g] + _dot(vtc, p.astype(BF16)))
            out += [m_new, alpha * l_old + jnp.sum(p, axis=0, keepdims=True)]
        for g in pairs:
            accs[g][...] = new[g]
        return tuple(out)

    init = (jnp.full((1, 2 * tq), NEG, F32), jnp.zeros((1, 2 * tq), F32)) * len(pairs)
    fin = lax.fori_loop(0, (q0 + tq + ka - 1) // ka, attn_chunk, init)
    for g in pairs:
        out_t = accs[g][...] * (1.0 / fin[2 * g + 1])
        for j in range(2):
            hd = 2 * g + j
            o_ref[:, hd * HEAD_DIM:(hd + 1) * HEAD_DIM] = out_t[:, j * tq:(j + 1) * tq].T


def _dsa_attn(q_it, w_it, q_dt, k_i, k_d, v_dt, B, S):
    T = B * S
    tq = DSA_TQ
    nq = S // tq
    topk = min(DSA_MAX_TOPK, S // 4)
    qcol = lambda b, t: (0, b * nq + t)
    seq = lambda b, t: (b, 0)
    return pl.pallas_call(
        functools.partial(_dsa_kernel, topk=topk),
        out_shape=jax.ShapeDtypeStruct((T, DSA_W), F32),
        grid=(B, nq),
        in_specs=[
            pl.BlockSpec((IDX_HEADS * IDX_DIM, tq), qcol),
            pl.BlockSpec((IDX_HEADS, tq), qcol),
            pl.BlockSpec((DSA_W, tq), qcol),
            pl.BlockSpec((S, IDX_DIM), seq),
            pl.BlockSpec((S, HEAD_DIM), seq),
            pl.BlockSpec((S // DSA_KA, HEAD_DIM, DSA_KA), lambda b, t: (b, 0, 0)),
        ],
        out_specs=pl.BlockSpec((tq, DSA_W), lambda b, t: (b * nq + t, 0)),
        scratch_shapes=[pltpu.VMEM((S, tq), I32)]
        + [pltpu.VMEM((HEAD_DIM, 2 * tq), F32) for _ in range(DSA_HEADS // 2)],
        compiler_params=_cparams(("parallel", "arbitrary")),
        name="dsa_attn",
    )(q_it, w_it, q_dt, k_i, k_d, v_dt)


def _first_index_of_max(v, row_io, n_rows):
    m = jnp.max(v, axis=0, keepdims=True)
    idx = jnp.min(jnp.where(v == m, row_io, n_rows), axis=0, keepdims=True)
    return m, idx


def _post_kernel(x_ref, om_ref, od_ref, mod_ref, gm_ref, gd_ref, wout_ref, gffn_ref, wgu_ref, wds_ref,
                 wrh_ref, wrl_ref, rb_ref, tri_ref,
                 xpart_ref, h2r_ref, eidx_ref, rank_ref, gate_ref, cnt_ref, base_sc):
    i = pl.program_id(0)
    tm = x_ref.shape[0]
    gt1 = mod_ref[0, 2:3, :]
    sh2 = mod_ref[0, 3:4, :]
    sc2 = mod_ref[0, 4:5, :]
    gt2 = mod_ref[0, 5:6, :]

    mixed = jnp.concatenate([_rms(om_ref[...], gm_ref[...]), _rms(od_ref[...], gd_ref[...])], axis=1)
    x1 = x_ref[...] + gt1 * _dot(mixed.astype(BF16), wout_ref[...])
    h2 = _rms(x1, gffn_ref[...]) * (1.0 + sc2) + sh2
    h2b = h2.astype(BF16)

    au = _dot(h2b, wgu_ref[...])
    hs = (_silu(au[:, :D_SHARED]) * au[:, D_SHARED:]).astype(BF16)
    xpart_ref[...] = x1 + gt2 * _dot(hs, wds_ref[...])

    for c, slab in enumerate(_pack_row_words(h2)):
        _store_word_slab(h2r_ref, 0, tm, c, slab)

    h2lo = (h2 - h2b.astype(F32)).astype(BF16)
    logits = _dot_nt(wrh_ref[...], h2b) + _dot_nt(wrl_ref[...], h2b) + _dot_nt(wrh_ref[...], h2lo)
    scores = 1.0 / (1.0 + jnp.exp(-logits))
    biased = scores + rb_ref[...]

    g_io = lax.broadcasted_iota(I32, (GROUP_SIZE, tm), 0)
    gs_rows = []
    for g in range(N_GROUPS):
        blk = biased[g * GROUP_SIZE:(g + 1) * GROUP_SIZE, :]
        m1, i1 = _first_index_of_max(blk, g_io, GROUP_SIZE)
        m2 = jnp.max(jnp.where(g_io == i1, -jnp.inf, blk), axis=0, keepdims=True)
        gs_rows.append(m1 + m2)
    gs = jnp.concatenate(gs_rows, axis=0)
    gi = lax.broadcasted_iota(I32, (N_GROUPS, tm), 0)
    grank = jnp.zeros((N_GROUPS, tm), I32)
    for m in range(N_GROUPS):
        gm = gs[m:m + 1, :]
        grank = grank + jnp.where((gm > gs) | ((gm == gs) & (m < gi)), 1, 0)
    gsel = grank < TOPK_GROUPS
    masked = jnp.concatenate(
        [jnp.where(gsel[g:g + 1, :], biased[g * GROUP_SIZE:(g + 1) * GROUP_SIZE, :], -jnp.inf)
         for g in range(N_GROUPS)], axis=0)

    e_io = lax.broadcasted_iota(I32, (N_EXPERTS, tm), 0)
    e_rows, s_rows = [], []
    for _ in range(EXPERT_TOPK):
        _, idx = _first_index_of_max(masked, e_io, N_EXPERTS)
        hit = e_io == idx
        e_rows.append(idx)
        s_rows.append(jnp.sum(jnp.where(hit, scores, 0.0), axis=0, keepdims=True))
        masked = jnp.where(hit, -jnp.inf, masked)
    eidx = jnp.concatenate(e_rows, axis=0)
    sk = jnp.concatenate(s_rows, axis=0)
    gate_ref[...] = sk / jnp.sum(sk, axis=0, keepdims=True) * ROUTED_SCALE
    eidx_ref[...] = eidx

    @pl.when(i == 0)
    def _():
        base_sc[...] = jnp.zeros(base_sc.shape, F32)

    chosen = jnp.zeros((N_EXPERTS, tm), F32)
    for k in range(EXPERT_TOPK):
        chosen = chosen + jnp.where(e_io == e_rows[k], 1.0, 0.0)
    incl = _dot(chosen.astype(BF16), tri_ref[...])
    pos = base_sc[...] + incl - 1.0
    rank_ref[...] = jnp.concatenate(
        [jnp.sum(jnp.where(e_io == e_rows[k], pos, 0.0), axis=0, keepdims=True)
         for k in range(EXPERT_TOPK)], axis=0).astype(I32)
    base_sc[...] = base_sc[...] + incl[:, tm - 1:tm]
    cnt_ref[...] = jnp.broadcast_to(base_sc[...], cnt_ref.shape)


def _post_attn(x2, o_m, o_d, mod3, g_moba, g_dsa, w_out, g_ffn, w_gu_s, w_down_s, wr_hi, wr_lo, rbias, S):
    T, D = x2.shape
    tm = POST_TM
    nt_per_seq = S // tm
    row = lambda i: (i, 0)
    full = lambda i: (0, 0)
    tri = (jnp.arange(tm)[:, None] <= jnp.arange(tm)[None, :]).astype(BF16)
    n_words = D // ROW_WORDS
    return pl.pallas_call(
        _post_kernel,
        out_shape=[
            jax.ShapeDtypeStruct((T, D), F32),
            jax.ShapeDtypeStruct((T, n_words, LANES), I32),
            jax.ShapeDtypeStruct((EXPERT_TOPK, T), I32),
            jax.ShapeDtypeStruct((EXPERT_TOPK, T), I32),
            jax.ShapeDtypeStruct((EXPERT_TOPK, T), F32),
            jax.ShapeDtypeStruct((N_EXPERTS, LANES), F32),
        ],
        grid=(T // tm,),
        in_specs=[
            pl.BlockSpec((tm, D), row),
            pl.BlockSpec((tm, MOBA_W), row),
            pl.BlockSpec((tm, DSA_W), row),
            pl.BlockSpec((1, 6, D), lambda i: (i // nt_per_seq, 0, 0)),
            pl.BlockSpec((1, MOBA_W), full),
            pl.BlockSpec((1, DSA_W), full),
            pl.BlockSpec(w_out.shape, full),
            pl.BlockSpec((1, D), full),
            pl.BlockSpec(w_gu_s.shape, full),
            pl.BlockSpec(w_down_s.shape, full),
            pl.BlockSpec(wr_hi.shape, full),
            pl.BlockSpec(wr_lo.shape, full),
            pl.BlockSpec((N_EXPERTS, 1), full),
            pl.BlockSpec((tm, tm), full),
        ],
        out_specs=[
            pl.BlockSpec((tm, D), row),
            pl.BlockSpec((tm, n_words, LANES), lambda i: (i, 0, 0)),
            pl.BlockSpec((EXPERT_TOPK, tm), lambda i: (0, i)),
            pl.BlockSpec((EXPERT_TOPK, tm), lambda i: (0, i)),
            pl.BlockSpec((EXPERT_TOPK, tm), lambda i: (0, i)),
            pl.BlockSpec((N_EXPERTS, LANES), full),
        ],
        scratch_shapes=[pltpu.VMEM((N_EXPERTS, 1), F32)],
        compiler_params=_cparams(("arbitrary",)),
        name="post_attn",
    )(x2, o_m, o_d, mod3, g_moba, g_dsa, w_out, g_ffn, w_gu_s, w_down_s, wr_hi, wr_lo, rbias, tri)


def _row_copy_wait(rows_hbm, n_rows, sem):
    blk = rows_hbm.at[pl.ds(0, n_rows)]
    pltpu.make_async_copy(blk, blk, sem).wait()


def _slots_kernel(ps_ref, e_ref, r_ref, d_ref):
    e = e_ref[...]

    def body(x, acc):
        return jnp.where(e == x, ps_ref[x], acc)

    d_ref[...] = lax.fori_loop(0, N_EXPERTS, body, jnp.zeros(e.shape, I32)) + r_ref[...]


def _slots(pad_starts, eidx, rank):
    K, T = eidx.shape
    tm = min(T, 4096)
    blk = lambda: pl.BlockSpec((K, tm), lambda i, ps: (0, i))
    return pl.pallas_call(
        _slots_kernel,
        out_shape=jax.ShapeDtypeStruct((K, T), I32),
        grid_spec=pltpu.PrefetchScalarGridSpec(num_scalar_prefetch=1, grid=(T // tm,),
                                               in_specs=[blk(), blk()], out_specs=blk()),
        compiler_params=_cparams(("parallel",)),
        name="slots",
    )(pad_starts, eidx, rank)


def _dispatch_kernel(d_ref, h2r_ref, xs_ref, sem):
    tm = h2r_ref.shape[0]

    def body(t, carry):
        for k in range(EXPERT_TOPK):
            pltpu.make_async_copy(h2r_ref.at[t], xs_ref.at[d_ref[t * EXPERT_TOPK + k]],
                                  sem).start(priority=k % 2)
        return carry

    lax.fori_loop(0, tm, body, 0)
    _row_copy_wait(xs_ref, tm * EXPERT_TOPK, sem)


def _dispatch(dest_flat, h2r, n_rows_padded):
    T, n_words, _ = h2r.shape
    tm = DISP_TM
    return pl.pallas_call(
        _dispatch_kernel,
        out_shape=jax.ShapeDtypeStruct((n_rows_padded, n_words, LANES), I32),
        grid=(T // tm,),
        in_specs=[pl.BlockSpec((tm * EXPERT_TOPK,), lambda i: (i,), memory_space=pltpu.SMEM),
                  pl.BlockSpec((tm, n_words, LANES), lambda i: (i, 0, 0))],
        out_specs=pl.BlockSpec(memory_space=pl.ANY),
        scratch_shapes=[pltpu.SemaphoreType.DMA],
        compiler_params=_cparams(("arbitrary",), disable_bounds_checks=True),
        name="dispatch",
    )(dest_flat, h2r)


def _expert_kernel(be_ref, nv_ref, xs_ref, wg_ref, wu_ref, wd_ref, ys_ref, wg_sc, wu_sc, wd_sc):
    j = pl.program_id(0)
    bm, n_words, _ = xs_ref.shape
    nv = nv_ref[j]

    @pl.when((j == 0) | (be_ref[j] != be_ref[jnp.maximum(j - 1, 0)]))
    def _():
        wg_sc[...] = wg_ref[...].astype(BF16)
        wu_sc[...] = wu_ref[...].astype(BF16)
        wd_sc[...] = wd_ref[...].astype(BF16)

    @pl.when(nv > 0)
    def _():
        feats = []
        for c in range(n_words):
            feats += _unpack_row_words(_load_word_slab(xs_ref, 0, bm, c))
        x = jnp.concatenate(feats, axis=1)
        valid = lax.broadcasted_iota(I32, (bm, 1), 0) < nv
        xb = jnp.where(valid, x, 0.0).astype(BF16)
        a = _dot(xb, wg_sc[...])
        u = _dot(xb, wu_sc[...])
        hmid = (_silu(a) * u).astype(BF16)
        ob = _dot(hmid, wd_sc[...])
        for c, slab in enumerate(_pack_row_words(ob)):
            _store_word_slab(ys_ref, 0, bm, c, slab)

    @pl.when(nv == 0)
    def _():
        ys_ref[...] = jnp.zeros(ys_ref.shape, I32)


def _experts(block_expert, n_valid, xs, w_gate_e, w_up_e, w_down_e):
    n_blocks = block_expert.shape[0]
    E, D, DE = w_gate_e.shape
    n_words = xs.shape[1]
    bm = EXP_BM
    rows = lambda j, be, nv: (j, 0, 0)
    wsel = lambda j, be, nv: (be[j], 0, 0)
    return pl.pallas_call(
        _expert_kernel,
        out_shape=jax.ShapeDtypeStruct(xs.shape, I32),
        grid_spec=pltpu.PrefetchScalarGridSpec(
            num_scalar_prefetch=2,
            grid=(n_blocks,),
            in_specs=[
                pl.BlockSpec((bm, n_words, LANES), rows),
                pl.BlockSpec((None, D, DE), wsel),
                pl.BlockSpec((None, D, DE), wsel),
                pl.BlockSpec((None, DE, D), wsel),
            ],
            out_specs=pl.BlockSpec((bm, n_words, LANES), rows),
            scratch_shapes=[pltpu.VMEM((D, DE), BF16), pltpu.VMEM((D, DE), BF16), pltpu.VMEM((DE, D), BF16)],
        ),
        compiler_params=_cparams(("arbitrary",)),
        name="experts",
    )(block_expert, n_valid, xs, w_gate_e, w_up_e, w_down_e)


def _combine_kernel(dcur_ref, dnxt_ref, g_ref, xpart_ref, mod_ref, gfin_ref, ys_ref, o_ref,
                    buf0, buf1, sem0, sem1):
    i = pl.program_id(0)
    n_steps = pl.num_programs(0)
    tm = xpart_ref.shape[0]
    n_words = buf0.shape[1]

    def issue(dest_ref, buf, sem):
        def body(t, carry):
            for k in range(EXPERT_TOPK):
                pltpu.make_async_copy(ys_ref.at[dest_ref[t * EXPERT_TOPK + k]], buf.at[k * tm + t],
                                      sem).start(priority=k % 2)
            return carry
        lax.fori_loop(0, tm, body, 0)

    def reduce_tile(buf, sem):
        _row_copy_wait(ys_ref, tm * EXPERT_TOPK, sem)
        gt2 = mod_ref[0, 5:6, :]
        g = g_ref[...]
        cols = []
        for c in range(n_words):
            lo = jnp.zeros((tm, LANES), F32)
            hi = jnp.zeros((tm, LANES), F32)
            for k in range(EXPERT_TOPK):
                a, b = _unpack_row_words(_load_word_slab(buf, k * tm, tm, c))
                gk = g[:, k:k + 1]
                lo = lo + gk * a
                hi = hi + gk * b
            cols += [lo, hi]
        routed = jnp.concatenate(cols, axis=1)
        o_ref[...] = _rms(xpart_ref[...] + gt2 * routed, gfin_ref[...])

    @pl.when(i == 0)
    def _():
        issue(dcur_ref, buf0, sem0)

    for parity, (cur, nxt) in enumerate((((buf0, sem0), (buf1, sem1)), ((buf1, sem1), (buf0, sem0)))):
        @pl.when(i % 2 == parity)
        def _():
            @pl.when(i + 1 < n_steps)
            def _():
                issue(dnxt_ref, *nxt)
            reduce_tile(*cur)


def _combine(dest, gates_t, xpart, mod3, g_final, ys, S):
    T, D = xpart.shape
    tm = COMB_TM
    n_steps = T // tm
    nt_per_seq = S // tm
    n_words = ys.shape[1]
    return pl.pallas_call(
        _combine_kernel,
        out_shape=jax.ShapeDtypeStruct((T, D), F32),
        grid=(n_steps,),
        in_specs=[
            pl.BlockSpec((tm * EXPERT_TOPK,), lambda i: (i,), memory_space=pltpu.SMEM),
            pl.BlockSpec((tm * EXPERT_TOPK,), lambda i: (jnp.minimum(i + 1, n_steps - 1),),
                         memory_space=pltpu.SMEM),
            pl.BlockSpec((tm, EXPERT_TOPK), lambda i: (i, 0)),
            pl.BlockSpec((tm, D), lambda i: (i, 0)),
            pl.BlockSpec((1, 6, D), lambda i: (i // nt_per_seq, 0, 0)),
            pl.BlockSpec((1, D), lambda i: (0, 0)),
            pl.BlockSpec(memory_space=pl.ANY),
        ],
        out_specs=pl.BlockSpec((tm, D), lambda i: (i, 0)),
        scratch_shapes=[pltpu.VMEM((EXPERT_TOPK * tm, n_words, LANES), I32),
                        pltpu.VMEM((EXPERT_TOPK * tm, n_words, LANES), I32),
                        pltpu.SemaphoreType.DMA, pltpu.SemaphoreType.DMA],
        compiler_params=_cparams(("arbitrary",), disable_bounds_checks=True),
        name="combine",
    )(dest, dest, gates_t, xpart, mod3, g_final, ys)


def _layer(x2, mod3, S, g_mix, w_in, g_kv, w_kv_up, g_moba_out, g_dsa_out, w_out, g_ffn, w_router,
           router_bias, w_gate_e, w_up_e, w_down_e, w_gate_s, w_up_s, w_down_s, g_final, tab_h, half_h,
           tab_i, half_i):
    T, D = x2.shape
    B = T // S
    w_in_p = jnp.pad(w_in, ((0, 0), (0, _C_END - w_in.shape[1]))).astype(BF16)
    (q_mt, k_m, v_mt, kmean, q_dt, k_d, v_dt, q_it, k_i, w_it) = _in_proj(
        x2, mod3, g_mix.reshape(1, D), w_in_p, g_kv.reshape(1, KV_LORA), w_kv_up.astype(BF16),
        tab_h, half_h, tab_i, half_i, S)
    o_m = _moba_attn(q_mt, k_m, v_mt, kmean, B, S)
    o_d = _dsa_attn(q_it, w_it, q_dt, k_i, k_d, v_dt, B, S)

    wr_t = w_router.T
    wr_hi = wr_t.astype(BF16)
    wr_lo = (wr_t - wr_hi.astype(F32)).astype(BF16)
    w_gu_s = jnp.concatenate([w_gate_s, w_up_s], axis=1).astype(BF16)
    xpart, h2r, eidx, rank, gates, cnt = _post_attn(
        x2, o_m, o_d, mod3, g_moba_out.reshape(1, MOBA_W), g_dsa_out.reshape(1, DSA_W), w_out.astype(BF16),
        g_ffn.reshape(1, D), w_gu_s, w_down_s.astype(BF16), wr_hi, wr_lo,
        router_bias.reshape(N_EXPERTS, 1), S)

    bm = EXP_BM
    n_blocks = T * EXPERT_TOPK // bm + N_EXPERTS
    counts = cnt[:, 0].astype(I32)
    padded = (counts + bm - 1) // bm * bm
    pad_ends = jnp.cumsum(padded)
    pad_starts = (pad_ends - padded).astype(I32)
    blk_start = jnp.arange(n_blocks, dtype=I32) * bm
    block_expert = jnp.minimum(jnp.sum(blk_start[:, None] >= pad_ends[None, :], axis=1), N_EXPERTS - 1).astype(I32)
    mine = block_expert[:, None] == jnp.arange(N_EXPERTS, dtype=I32)[None, :]
    row_end = jnp.sum(jnp.where(mine, (pad_starts + counts)[None, :], 0), axis=1)
    n_valid = jnp.where(blk_start < pad_ends[-1], jnp.clip(row_end - blk_start, 0, bm), 0).astype(I32)

    dest_flat = _slots(pad_starts, eidx, rank).T.reshape(-1)
    xs = _dispatch(dest_flat, h2r, n_blocks * bm)
    ys = _experts(block_expert, n_valid, xs, w_gate_e, w_up_e, w_down_e)
    return _combine(dest_flat, gates.T, xpart, mod3, g_final.reshape(1, D), ys, S)


def kernel(x, c, w_ada, b_ada, g_mix, w_in, g_kv, w_kv_up, g_moba_out, g_dsa_out, w_out, g_ffn, w_router,
           router_bias, w_gate_e, w_up_e, w_down_e, w_gate_s, w_up_s, w_down_s, g_final):
    B, S, D = x.shape
    depth = w_ada.shape[0]
    assert depth == 1, "the final norm is fused into the single layer"
    assert S % PROJ_TM == 0 and S % DSA_KC == 0 and S % POST_TM == 0 and S >= 4 * DSA_MAX_TOPK
    tab_h, half_h = _rope_tables(S, HEAD_DIM, 1)
    tab_i, half_i = _rope_tables(S, IDX_DIM, LANES // IDX_DIM)
    x2 = x.reshape(B * S, D)
    sq = lambda a: a.reshape(a.shape[1:])
    mod3 = _ada_mod(c, sq(w_ada), sq(b_ada)).reshape(B, 6, D)
    out = _layer(x2, mod3, S, sq(g_mix), sq(w_in), sq(g_kv), sq(w_kv_up), sq(g_moba_out), sq(g_dsa_out),
                 sq(w_out), sq(g_ffn), sq(w_router), sq(router_bias), sq(w_gate_e), sq(w_up_e), sq(w_down_e),
                 sq(w_gate_s), sq(w_up_s), sq(w_down_s), g_final, tab_h, half_h, tab_i, half_i)
    return out.reshape(B, S, D)
```

```python
import functools

import jax
import jax.numpy as jnp
from jax import lax
from jax.experimental import pallas as pl
from jax.experimental.pallas import tpu as pltpu

HEAD_DIM = 128
MOBA_HEADS = 4
DSA_HEADS = 4
MOBA_W = MOBA_HEADS * HEAD_DIM
DSA_W = DSA_HEADS * HEAD_DIM
MOBA_BLOCK = 256
MOBA_TOPK = 3
DSA_MAX_TOPK = 256
KV_LORA = 256
IDX_HEADS = 8
IDX_DIM = 64
ROPE_THETA = 500000.0
ROPE_FRACTION_DIV = 4
N_EXPERTS = 256
EXPERT_TOPK = 8
N_GROUPS = 8
TOPK_GROUPS = 4
GROUP_SIZE = N_EXPERTS // N_GROUPS
D_EXPERT = 256
D_SHARED = 256
ROUTED_SCALE = 2.5
EPS = 1e-6

LANES = 128
SUBLANES = 8
VMEM_LIMIT = 56 * 1024 * 1024

PROJ_TM = 512
DSA_TQ = 256
DSA_KC = 512
DSA_KA = 256
POST_TM = 256
DISP_TM = 256
EXP_BM = 256
COMB_TM = 128
NEG = -1e30
INT_MIN = -2147483648
LOG2E = 1.4426950408889634

F32 = jnp.float32
BF16 = jnp.bfloat16
I32 = jnp.int32


def _cparams(sem, **kw):
    return pltpu.CompilerParams(dimension_semantics=sem, vmem_limit_bytes=VMEM_LIMIT, **kw)


def _dot(a, b):
    return jnp.dot(a, b, preferred_element_type=F32)


def _dot_nt(a, b):
    return lax.dot_general(a, b, (((1,), (1,)), ((), ())), preferred_element_type=F32)


def _silu(x):
    return x * (1.0 / (1.0 + jnp.exp(-x)))


def _rms(x, g):
    return x * lax.rsqrt(jnp.mean(x * x, axis=-1, keepdims=True) + EPS) * g


ROW_WORDS = 2 * LANES


def _pack_row_words(x):
    slabs = []
    for c in range(x.shape[1] // ROW_WORDS):
        lo = pltpu.bitcast(x[:, c * ROW_WORDS:c * ROW_WORDS + LANES].astype(BF16).astype(F32), I32)
        hi = pltpu.bitcast(x[:, c * ROW_WORDS + LANES:(c + 1) * ROW_WORDS].astype(BF16).astype(F32), I32)
        slabs.append(lax.shift_right_logical(lo, 16) | hi)
    return slabs


def _word_slab_index(rows_ref, row0, n_rows, c):
    n_words = rows_ref.shape[1]
    flat = rows_ref.reshape(rows_ref.shape[0] * n_words, LANES)
    return flat, pl.ds(row0 * n_words + c, n_rows, stride=n_words)


def _load_word_slab(rows_ref, row0, n_rows, c):
    flat, idx = _word_slab_index(rows_ref, row0, n_rows, c)
    return flat[idx, :]


def _store_word_slab(rows_ref, row0, n_rows, c, value):
    flat, idx = _word_slab_index(rows_ref, row0, n_rows, c)
    flat[idx, :] = value


def _unpack_row_words(u):
    return pltpu.bitcast(lax.shift_left(u, 16), F32), pltpu.bitcast(u & jnp.int32(-65536), F32)


def _ada_kernel(c_ref, w_ref, b_ref, o_ref):
    ca = _silu(c_ref[...])
    o_ref[...] = jnp.dot(ca, w_ref[...], preferred_element_type=F32,
                         precision=lax.Precision.HIGHEST) + b_ref[...]


def _ada_mod(c, w_ada, b_ada):
    B, D = c.shape
    N = w_ada.shape[1]
    tn = 1024
    return pl.pallas_call(
        _ada_kernel,
        out_shape=jax.ShapeDtypeStruct((B, N), F32),
        grid=(N // tn,),
        in_specs=[pl.BlockSpec((B, D), lambda j: (0, 0)),
                  pl.BlockSpec((D, tn), lambda j: (0, j)),
                  pl.BlockSpec((1, tn), lambda j: (0, j))],
        out_specs=pl.BlockSpec((B, tn), lambda j: (0, j)),
        compiler_params=_cparams(("arbitrary",)),
        name="ada_mod",
    )(c, w_ada, b_ada.reshape(1, N))


def _rope_tables(seq, head_dim, heads_per_vreg):
    rot = head_dim // ROPE_FRACTION_DIV
    half = rot // 2
    inv = jnp.float32(ROPE_THETA) ** (-(jnp.arange(0, rot, 2, dtype=F32) / rot))
    ang = jnp.arange(seq, dtype=F32)[:, None] * inv[None, :]
    cos, sin = jnp.cos(ang), jnp.sin(ang)
    ones = jnp.ones((seq, head_dim - rot), F32)
    zeros_h = jnp.zeros((seq, half), F32)
    zeros_r = jnp.zeros((seq, head_dim - rot), F32)
    c = jnp.concatenate([cos, cos, ones], axis=1)
    sp = jnp.concatenate([zeros_h, sin, zeros_r], axis=1)
    sm = jnp.concatenate([-sin, zeros_h, zeros_r], axis=1)
    rep = lambda t: jnp.tile(t, (1, heads_per_vreg))
    return jnp.stack([rep(c), rep(sp), rep(sm)], axis=0), half


def _rope(x, tab_ref, half):
    return (x * tab_ref[0] + pltpu.roll(x, half, 1) * tab_ref[1]
            + pltpu.roll(x, LANES - half, 1) * tab_ref[2])


_C_QM, _C_KM, _C_VM, _C_QD = 0, MOBA_W, 2 * MOBA_W, 3 * MOBA_W
_C_CKV = 3 * MOBA_W + DSA_W
_C_QI = _C_CKV + KV_LORA
_C_KI = _C_QI + IDX_HEADS * IDX_DIM
_C_END = _C_KI + LANES


def _in_proj_kernel(x_ref, mod_ref, gmix_ref, w_ref, gkv_ref, wkv_ref, tabh_ref, tabi_ref,
                    qmt_ref, km_ref, vmt_ref, kmean_ref, qdt_ref, kd_ref, vdt_ref, qit_ref, ki_ref, wit_ref,
                    *, half_h, half_i):
    tm = x_ref.shape[0]
    x = x_ref[...]
    sh1 = mod_ref[0, 0:1, :]
    sc1 = mod_ref[0, 1:2, :]
    h = (_rms(x, gmix_ref[...]) * (1.0 + sc1) + sh1).astype(BF16)

    def proj(c0, width):
        return _dot(h, w_ref[:, c0:c0 + width])

    q_scale = HEAD_DIM ** -0.5 * LOG2E
    nblk = tm // MOBA_BLOCK
    qm = proj(_C_QM, MOBA_W)
    km = proj(_C_KM, MOBA_W)
    vm = proj(_C_VM, MOBA_W)
    for hd in range(MOBA_HEADS):
        sl = slice(hd * HEAD_DIM, (hd + 1) * HEAD_DIM)
        qmt_ref[sl, :] = (_rope(qm[:, sl], tabh_ref, half_h) * q_scale).T.astype(BF16)
        kr = _rope(km[:, sl], tabh_ref, half_h)
        km_ref[:, sl] = kr.astype(BF16)
        for blk in range(nblk):
            rows = slice(blk * MOBA_BLOCK, (blk + 1) * MOBA_BLOCK)
            kmean_ref[blk:blk + 1, sl] = jnp.mean(kr[rows], axis=0, keepdims=True)
            vmt_ref[blk, sl, :] = vm[rows, sl].T.astype(BF16)
    qd = proj(_C_QD, DSA_W)
    for hd in range(DSA_HEADS):
        sl = slice(hd * HEAD_DIM, (hd + 1) * HEAD_DIM)
        qdt_ref[sl, :] = (_rope(qd[:, sl], tabh_ref, half_h) * q_scale).T.astype(BF16)
    ckv = proj(_C_CKV, KV_LORA)
    kv = _dot(_rms(ckv, gkv_ref[...]).astype(BF16), wkv_ref[...])
    kd_ref[...] = _rope(kv[:, :HEAD_DIM], tabh_ref, half_h).astype(BF16)
    for ch in range(tm // DSA_KA):
        vdt_ref[ch] = kv[ch * DSA_KA:(ch + 1) * DSA_KA, HEAD_DIM:].T.astype(BF16)
    qi = proj(_C_QI, IDX_HEADS * IDX_DIM)
    for j in range(IDX_HEADS * IDX_DIM // LANES):
        sl = slice(j * LANES, (j + 1) * LANES)
        qit_ref[sl, :] = _rope(qi[:, sl], tabi_ref, half_i).T.astype(BF16)
    kw = proj(_C_KI, LANES)
    ki_ref[...] = _rope(kw, tabi_ref, half_i)[:, :IDX_DIM].astype(BF16)
    wit_ref[...] = kw.T[IDX_DIM:IDX_DIM + IDX_HEADS, :] * (IDX_HEADS ** -0.5 * IDX_DIM ** -0.5)


def _in_proj(x2, mod3, g_mix, w_in_p, g_kv, w_kv_up, tab_h, half_h, tab_i, half_i, S):
    T, D = x2.shape
    tm = PROJ_TM
    nt_per_seq = S // tm
    row = lambda i: (i, 0)
    col = lambda i: (0, i)
    nb = tm // MOBA_BLOCK
    nc = tm // DSA_KA
    outs = [
        jax.ShapeDtypeStruct((MOBA_W, T), BF16),
        jax.ShapeDtypeStruct((T, MOBA_W), BF16),
        jax.ShapeDtypeStruct((T // MOBA_BLOCK, MOBA_W, MOBA_BLOCK), BF16),
        jax.ShapeDtypeStruct((T // tm, nb, MOBA_W), F32),
        jax.ShapeDtypeStruct((DSA_W, T), BF16),
        jax.ShapeDtypeStruct((T, HEAD_DIM), BF16),
        jax.ShapeDtypeStruct((T // DSA_KA, HEAD_DIM, DSA_KA), BF16),
        jax.ShapeDtypeStruct((IDX_HEADS * IDX_DIM, T), BF16),
        jax.ShapeDtypeStruct((T, IDX_DIM), BF16),
        jax.ShapeDtypeStruct((IDX_HEADS, T), F32),
    ]
    out_specs = [
        pl.BlockSpec((MOBA_W, tm), col), pl.BlockSpec((tm, MOBA_W), row),
        pl.BlockSpec((nb, MOBA_W, MOBA_BLOCK), lambda i: (i, 0, 0)),
        pl.BlockSpec((None, nb, MOBA_W), lambda i: (i, 0, 0)),
        pl.BlockSpec((DSA_W, tm), col), pl.BlockSpec((tm, HEAD_DIM), row),
        pl.BlockSpec((nc, HEAD_DIM, DSA_KA), lambda i: (i, 0, 0)),
        pl.BlockSpec((IDX_HEADS * IDX_DIM, tm), col), pl.BlockSpec((tm, IDX_DIM), row),
        pl.BlockSpec((IDX_HEADS, tm), col),
    ]
    res = pl.pallas_call(
        functools.partial(_in_proj_kernel, half_h=half_h, half_i=half_i),
        out_shape=outs,
        grid=(T // tm,),
        in_specs=[
            pl.BlockSpec((tm, D), row),
            pl.BlockSpec((1, 6, D), lambda i: (i // nt_per_seq, 0, 0)),
            pl.BlockSpec((1, D), lambda i: (0, 0)),
            pl.BlockSpec((D, _C_END), lambda i: (0, 0)),
            pl.BlockSpec((1, KV_LORA), lambda i: (0, 0)),
            pl.BlockSpec((KV_LORA, 2 * HEAD_DIM), lambda i: (0, 0)),
            pl.BlockSpec((3, tm, LANES), lambda i: (0, i % nt_per_seq, 0)),
            pl.BlockSpec((3, tm, LANES), lambda i: (0, i % nt_per_seq, 0)),
        ],
        out_specs=out_specs,
        compiler_params=_cparams(("parallel",)),
        name="in_proj",
    )(x2, mod3, g_mix, w_in_p, g_kv, w_kv_up, tab_h, tab_i)
    res = list(res)
    res[3] = res[3].reshape(T // MOBA_BLOCK, MOBA_W)
    return res


def _moba_kernel(qt_ref, k_ref, vt_ref, kmean_ref, o_ref, bias_sc, *accs):
    qi = pl.program_id(1)
    blk = MOBA_BLOCK
    nb = kmean_ref.shape[0]
    heads = range(MOBA_HEADS)
    hsl = [slice(hd * HEAD_DIM, (hd + 1) * HEAD_DIM) for hd in heads]
    qts = [qt_ref[hsl[hd], :] for hd in heads]
    row = lax.broadcasted_iota(I32, (nb, blk), 0)
    past = row < qi
    start = pl.multiple_of(qi * blk, blk)
    k_io = lax.broadcasted_iota(I32, (blk, blk), 0)
    q_io = lax.broadcasted_iota(I32, (blk, blk), 1)

    init = []
    for hd in heads:
        km = kmean_ref[:, hsl[hd]]
        km_hi = km.astype(BF16)
        km_lo = (km - km_hi.astype(F32)).astype(BF16)
        gate = _dot(km_hi, qts[hd]) + _dot(km_lo, qts[hd])
        gate = jnp.where(past, gate, -jnp.inf)
        bias = jnp.full((nb, blk), NEG, F32)
        for _ in range(MOBA_TOPK):
            _, idx = _first_index_of_max(gate, row, nb)
            hit = row == idx
            bias = jnp.where(hit, 0.0, bias)
            gate = jnp.where(hit, -jnp.inf, gate)
        bias_sc[hd] = jnp.where(past, bias, NEG)

        s = _dot(k_ref[pl.ds(start, blk), hsl[hd]], qts[hd])
        s = jnp.where(k_io <= q_io, s, NEG)
        m0 = jnp.max(s, axis=0, keepdims=True)
        p = jnp.exp2(s - m0)
        accs[hd][...] = _dot(vt_ref[qi, hsl[hd], :], p.astype(BF16))
        init += [m0, jnp.sum(p, axis=0, keepdims=True)]

    def body(n, carry):
        st = pl.multiple_of(n * blk, blk)
        kb = k_ref[pl.ds(st, blk), :]
        sbs = [_dot(kb[:, hsl[hd]], qts[hd]) + bias_sc[hd, pl.ds(n, 1), :] for hd in heads]
        old = [accs[hd][...] for hd in heads]
        out, new = [], []
        for hd in heads:
            m_old, l_old = carry[2 * hd], carry[2 * hd + 1]
            m_new = jnp.maximum(m_old, jnp.max(sbs[hd], axis=0, keepdims=True))
            alpha = jnp.exp2(m_old - m_new)
            pb = jnp.exp2(sbs[hd] - m_new)
            new.append(alpha * old[hd] + _dot(vt_ref[n, hsl[hd], :], pb.astype(BF16)))
            out += [m_new, alpha * l_old + jnp.sum(pb, axis=0, keepdims=True)]
        for hd in heads:
            accs[hd][...] = new[hd]
        return tuple(out)

    fin = lax.fori_loop(0, qi, body, tuple(init))
    for hd in heads:
        o_ref[:, hsl[hd]] = (accs[hd][...] * (1.0 / fin[2 * hd + 1])).T


def _moba_attn(q_mt, k_m, v_mt, kmean, B, S):
    T = B * S
    blk = MOBA_BLOCK
    nq = S // blk
    return pl.pallas_call(
        _moba_kernel,
        out_shape=jax.ShapeDtypeStruct((T, MOBA_W), F32),
        grid=(B, nq),
        in_specs=[
            pl.BlockSpec((MOBA_W, blk), lambda b, i: (0, b * nq + i)),
            pl.BlockSpec((S, MOBA_W), lambda b, i: (b, 0)),
            pl.BlockSpec((nq, MOBA_W, blk), lambda b, i: (b, 0, 0)),
            pl.BlockSpec((nq, MOBA_W), lambda b, i: (b, 0)),
        ],
        out_specs=pl.BlockSpec((blk, MOBA_W), lambda b, i: (b * nq + i, 0)),
        scratch_shapes=[pltpu.VMEM((MOBA_HEADS, nq, blk), F32)]
        + [pltpu.VMEM((HEAD_DIM, blk), F32) for _ in range(MOBA_HEADS)],
        compiler_params=_cparams(("parallel", "arbitrary")),
        name="moba_attn",
    )(q_mt, k_m, v_mt, kmean)


def _sortable_key(x):
    b = pltpu.bitcast(x, I32)
    return jnp.where(b >= 0, b, b ^ jnp.int32(0x7FFFFFFF))


def _dsa_kernel(qit_ref, wit_ref, qdt_ref, ki_ref, kd_ref, vdt_ref, o_ref, key_sc, *accs, topk):
    t = pl.program_id(1)
    tq, kc = DSA_TQ, DSA_KC
    S = key_sc.shape[0]
    q0 = t * tq
    n_chunks = (q0 + tq + kc - 1) // kc
    key_io = lax.broadcasted_iota(I32, (kc, tq), 0)
    q_pos = q0 + lax.broadcasted_iota(I32, (kc, tq), 1)
    w = wit_ref[...]

    def score_chunk(c, carry):
        k0 = pl.multiple_of(c * kc, kc)
        kic = ki_ref[pl.ds(k0, kc), :]
        acc = jnp.zeros((kc, tq), F32)
        for hd in range(IDX_HEADS):
            lg = _dot(kic, qit_ref[hd * IDX_DIM:(hd + 1) * IDX_DIM, :])
            acc = acc + jnp.maximum(lg, 0.0) * w[hd:hd + 1, :]
        acc = jnp.where(acc == 0.0, 0.0, acc)
        key = _sortable_key(acc)
        key_sc[pl.ds(k0, kc), :] = jnp.where(k0 + key_io <= q_pos, key, INT_MIN)
        return carry

    lax.fori_loop(0, n_chunks, score_chunk, 0)

    def count(pred_fn):
        def cbody(c, acc):
            k0 = pl.multiple_of(c * kc, kc)
            hit = jnp.where(pred_fn(key_sc[pl.ds(k0, kc), :], k0), 1, 0)
            return acc + jnp.sum(hit.reshape(kc // SUBLANES, SUBLANES, tq), axis=0)
        acc = lax.fori_loop(0, n_chunks, cbody, jnp.zeros((SUBLANES, tq), I32))
        return jnp.sum(acc, axis=0, keepdims=True)

    c_nonneg = count(lambda kk, k0: kk >= 0)
    thr0 = jnp.where(c_nonneg >= topk, 0, INT_MIN).astype(I32)

    def bit_step(i, thr):
        cand = thr | lax.shift_left(jnp.int32(1), 30 - i)
        cnt = count(lambda kk, k0: kk >= cand)
        return jnp.where(cnt >= topk, cand, thr)

    thr = lax.fori_loop(0, 31, bit_step, thr0)

    n_gt = count(lambda kk, k0: kk > thr)
    n_ge = count(lambda kk, k0: kk >= thr)
    need = topk - n_gt
    overflow = (n_ge > topk) & (thr != INT_MIN)
    any_overflow = jnp.max(jnp.where(overflow, 1, 0)) > 0
    nbits = max(1, (S - 1).bit_length())

    def cut_search():
        def step(i, lo):
            cand = lo | lax.shift_left(jnp.int32(1), nbits - 1 - i)
            cnt = count(lambda kk, k0: (kk == thr) & (k0 + key_io < cand))
            return jnp.where(cnt >= need, lo, cand)
        return lax.fori_loop(0, nbits, step, jnp.zeros((1, tq), I32))

    jcut = lax.cond(any_overflow, cut_search, lambda: jnp.zeros((1, tq), I32))
    jcut = jnp.where(overflow, jcut, S)
    thr_ge = jnp.where(thr == INT_MIN, INT_MIN + 1, thr)

    pairs = range(DSA_HEADS // 2)
    qst = [jnp.concatenate([qdt_ref[(2 * g + j) * HEAD_DIM:(2 * g + j + 1) * HEAD_DIM, :] for j in range(2)],
                           axis=1) for g in pairs]
    for g in pairs:
        accs[g][...] = jnp.zeros(accs[g].shape, F32)

    ka = DSA_KA
    pos_io = lax.broadcasted_iota(I32, (ka, tq), 0)

    def attn_chunk(c, carry):
        k0 = pl.multiple_of(c * ka, ka)
        kk = key_sc[pl.ds(k0, ka), :]
        bias = lax.cond(
            any_overflow,
            lambda: jnp.where((kk > thr_ge) | ((kk == thr_ge) & (k0 + pos_io <= jcut)), 0.0, NEG),
            lambda: jnp.where(kk >= thr_ge, 0.0, NEG))
        bias2 = jnp.concatenate([bias, bias], axis=1)
        kdc = kd_ref[pl.ds(k0, ka), :]
        vtc = vdt_ref[c]
        ss = [_dot(kdc, qst[g]) + bias2 for g in pairs]
        old = [accs[g][...] for g in pairs]
        out, new = [], []
        for g in pairs:
            m_old, l_old = carry[2 * g], carry[2 * g + 1]
            m_new = jnp.maximum(m_old, jnp.max(ss[g], axis=0, keepdims=True))
            alpha = jnp.exp2(m_old - m_new)
            p = jnp.exp2(ss[g] - m_new)
            new.append(alpha * old[g] + _dot(vtc, p.astype(BF16)))
            out += [m_new, alpha * l_old + jnp.sum(p, axis=0, keepdims=True)]
        for g in pairs:
            accs[g][...] = new[g]
        return tuple(out)

    init = (jnp.full((1, 2 * tq), NEG, F32), jnp.zeros((1, 2 * tq), F32)) * len(pairs)
    fin = lax.fori_loop(0, (q0 + tq + ka - 1) // ka, attn_chunk, init)
    for g in pairs:
        out_t = accs[g][...] * (1.0 / fin[2 * g + 1])
        for j in range(2):
            hd = 2 * g + j
            o_ref[:, hd * HEAD_DIM:(hd + 1) * HEAD_DIM] = out_t[:, j * tq:(j + 1) * tq].T


def _dsa_attn(q_it, w_it, q_dt, k_i, k_d, v_dt, B, S):
    T = B * S
    tq = DSA_TQ
    nq = S // tq
    topk = min(DSA_MAX_TOPK, S // 4)
    qcol = lambda b, t: (0, b * nq + t)
    seq = lambda b, t: (b, 0)
    return pl.pallas_call(
        functools.partial(_dsa_kernel, topk=topk),
        out_shape=jax.ShapeDtypeStruct((T, DSA_W), F32),
        grid=(B, nq),
        in_specs=[
            pl.BlockSpec((IDX_HEADS * IDX_DIM, tq), qcol),
            pl.BlockSpec((IDX_HEADS, tq), qcol),
            pl.BlockSpec((DSA_W, tq), qcol),
            pl.BlockSpec((S, IDX_DIM), seq),
            pl.BlockSpec((S, HEAD_DIM), seq),
            pl.BlockSpec((S // DSA_KA, HEAD_DIM, DSA_KA), lambda b, t: (b, 0, 0)),
        ],
        out_specs=pl.BlockSpec((tq, DSA_W), lambda b, t: (b * nq + t, 0)),
        scratch_shapes=[pltpu.VMEM((S, tq), I32)]
        + [pltpu.VMEM((HEAD_DIM, 2 * tq), F32) for _ in range(DSA_HEADS // 2)],
        compiler_params=_cparams(("parallel", "arbitrary")),
        name="dsa_attn",
    )(q_it, w_it, q_dt, k_i, k_d, v_dt)


def _first_index_of_max(v, row_io, n_rows):
    m = jnp.max(v, axis=0, keepdims=True)
    idx = jnp.min(jnp.where(v == m, row_io, n_rows), axis=0, keepdims=True)
    return m, idx


def _post_kernel(x_ref, om_ref, od_ref, mod_ref, gm_ref, gd_ref, wout_ref, gffn_ref, wgu_ref, wds_ref,
                 wrh_ref, wrl_ref, rb_ref, tri_ref,
                 xpart_ref, h2r_ref, eidx_ref, rank_ref, gate_ref, cnt_ref, base_sc):
    i = pl.program_id(0)
    tm = x_ref.shape[0]
    gt1 = mod_ref[0, 2:3, :]
    sh2 = mod_ref[0, 3:4, :]
    sc2 = mod_ref[0, 4:5, :]
    gt2 = mod_ref[0, 5:6, :]

    mixed = jnp.concatenate([_rms(om_ref[...], gm_ref[...]), _rms(od_ref[...], gd_ref[...])], axis=1)
    x1 = x_ref[...] + gt1 * _dot(mixed.astype(BF16), wout_ref[...])
    h2 = _rms(x1, gffn_ref[...]) * (1.0 + sc2) + sh2
    h2b = h2.astype(BF16)

    au = _dot(h2b, wgu_ref[...])
    hs = (_silu(au[:, :D_SHARED]) * au[:, D_SHARED:]).astype(BF16)
    xpart_ref[...] = x1 + gt2 * _dot(hs, wds_ref[...])

    for c, slab in enumerate(_pack_row_words(h2)):
        _store_word_slab(h2r_ref, 0, tm, c, slab)

    h2lo = (h2 - h2b.astype(F32)).astype(BF16)
    logits = _dot_nt(wrh_ref[...], h2b) + _dot_nt(wrl_ref[...], h2b) + _dot_nt(wrh_ref[...], h2lo)
    scores = 1.0 / (1.0 + jnp.exp(-logits))
    biased = scores + rb_ref[...]

    g_io = lax.broadcasted_iota(I32, (GROUP_SIZE, tm), 0)
    gs_rows = []
    for g in range(N_GROUPS):
        blk = biased[g * GROUP_SIZE:(g + 1) * GROUP_SIZE, :]
        m1, i1 = _first_index_of_max(blk, g_io, GROUP_SIZE)
        m2 = jnp.max(jnp.where(g_io == i1, -jnp.inf, blk), axis=0, keepdims=True)
        gs_rows.append(m1 + m2)
    gs = jnp.concatenate(gs_rows, axis=0)
    gi = lax.broadcasted_iota(I32, (N_GROUPS, tm), 0)
    grank = jnp.zeros((N_GROUPS, tm), I32)
    for m in range(N_GROUPS):
        gm = gs[m:m + 1, :]
        grank = grank + jnp.where((gm > gs) | ((gm == gs) & (m < gi)), 1, 0)
    gsel = grank < TOPK_GROUPS
    masked = jnp.concatenate(
        [jnp.where(gsel[g:g + 1, :], biased[g * GROUP_SIZE:(g + 1) * GROUP_SIZE, :], -jnp.inf)
         for g in range(N_GROUPS)], axis=0)

    e_io = lax.broadcasted_iota(I32, (N_EXPERTS, tm), 0)
    e_rows, s_rows = [], []
    for _ in range(EXPERT_TOPK):
        _, idx = _first_index_of_max(masked, e_io, N_EXPERTS)
        hit = e_io == idx
        e_rows.append(idx)
        s_rows.append(jnp.sum(jnp.where(hit, scores, 0.0), axis=0, keepdims=True))
        masked = jnp.where(hit, -jnp.inf, masked)
    eidx = jnp.concatenate(e_rows, axis=0)
    sk = jnp.concatenate(s_rows, axis=0)
    gate_ref[...] = sk / jnp.sum(sk, axis=0, keepdims=True) * ROUTED_SCALE
    eidx_ref[...] = eidx

    @pl.when(i == 0)
    def _():
        base_sc[...] = jnp.zeros(base_sc.shape, F32)

    chosen = jnp.zeros((N_EXPERTS, tm), F32)
    for k in range(EXPERT_TOPK):
        chosen = chosen + jnp.where(e_io == e_rows[k], 1.0, 0.0)
    incl = _dot(chosen.astype(BF16), tri_ref[...])
    pos = base_sc[...] + incl - 1.0
    rank_ref[...] = jnp.concatenate(
        [jnp.sum(jnp.where(e_io == e_rows[k], pos, 0.0), axis=0, keepdims=True)
         for k in range(EXPERT_TOPK)], axis=0).astype(I32)
    base_sc[...] = base_sc[...] + incl[:, tm - 1:tm]
    cnt_ref[...] = jnp.broadcast_to(base_sc[...], cnt_ref.shape)


def _post_attn(x2, o_m, o_d, mod3, g_moba, g_dsa, w_out, g_ffn, w_gu_s, w_down_s, wr_hi, wr_lo, rbias, S):
    T, D = x2.shape
    tm = POST_TM
    nt_per_seq = S // tm
    row = lambda i: (i, 0)
    full = lambda i: (0, 0)
    tri = (jnp.arange(tm)[:, None] <= jnp.arange(tm)[None, :]).astype(BF16)
    n_words = D // ROW_WORDS
    return pl.pallas_call(
        _post_kernel,
        out_shape=[
            jax.ShapeDtypeStruct((T, D), F32),
            jax.ShapeDtypeStruct((T, n_words, LANES), I32),
            jax.ShapeDtypeStruct((EXPERT_TOPK, T), I32),
            jax.ShapeDtypeStruct((EXPERT_TOPK, T), I32),
            jax.ShapeDtypeStruct((EXPERT_TOPK, T), F32),
            jax.ShapeDtypeStruct((N_EXPERTS, LANES), F32),
        ],
        grid=(T // tm,),
        in_specs=[
            pl.BlockSpec((tm, D), row),
            pl.BlockSpec((tm, MOBA_W), row),
            pl.BlockSpec((tm, DSA_W), row),
            pl.BlockSpec((1, 6, D), lambda i: (i // nt_per_seq, 0, 0)),
            pl.BlockSpec((1, MOBA_W), full),
            pl.BlockSpec((1, DSA_W), full),
            pl.BlockSpec(w_out.shape, full),
            pl.BlockSpec((1, D), full),
            pl.BlockSpec(w_gu_s.shape, full),
            pl.BlockSpec(w_down_s.shape, full),
            pl.BlockSpec(wr_hi.shape, full),
            pl.BlockSpec(wr_lo.shape, full),
            pl.BlockSpec((N_EXPERTS, 1), full),
            pl.BlockSpec((tm, tm), full),
        ],
        out_specs=[
            pl.BlockSpec((tm, D), row),
            pl.BlockSpec((tm, n_words, LANES), lambda i: (i, 0, 0)),
            pl.BlockSpec((EXPERT_TOPK, tm), lambda i: (0, i)),
            pl.BlockSpec((EXPERT_TOPK, tm), lambda i: (0, i)),
            pl.BlockSpec((EXPERT_TOPK, tm), lambda i: (0, i)),
            pl.BlockSpec((N_EXPERTS, LANES), full),
        ],
        scratch_shapes=[pltpu.VMEM((N_EXPERTS, 1), F32)],
        compiler_params=_cparams(("arbitrary",)),
        name="post_attn",
    )(x2, o_m, o_d, mod3, g_moba, g_dsa, w_out, g_ffn, w_gu_s, w_down_s, wr_hi, wr_lo, rbias, tri)


def _row_copy_wait(rows_hbm, n_rows, sem):
    blk = rows_hbm.at[pl.ds(0, n_rows)]
    pltpu.make_async_copy(blk, blk, sem).wait()


def _slots_kernel(ps_ref, e_ref, r_ref, d_ref):
    e = e_ref[...]

    def body(x, acc):
        return jnp.where(e == x, ps_ref[x], acc)

    d_ref[...] = lax.fori_loop(0, N_EXPERTS, body, jnp.zeros(e.shape, I32)) + r_ref[...]


def _slots(pad_starts, eidx, rank):
    K, T = eidx.shape
    tm = min(T, 4096)
    blk = lambda: pl.BlockSpec((K, tm), lambda i, ps: (0, i))
    return pl.pallas_call(
        _slots_kernel,
        out_shape=jax.ShapeDtypeStruct((K, T), I32),
        grid_spec=pltpu.PrefetchScalarGridSpec(num_scalar_prefetch=1, grid=(T // tm,),
                                               in_specs=[blk(), blk()], out_specs=blk()),
        compiler_params=_cparams(("parallel",)),
        name="slots",
    )(pad_starts, eidx, rank)


def _dispatch_kernel(d_ref, h2r_ref, xs_ref, sem):
    tm = h2r_ref.shape[0]

    def body(t, carry):
        for k in range(EXPERT_TOPK):
            pltpu.make_async_copy(h2r_ref.at[t], xs_ref.at[d_ref[t * EXPERT_TOPK + k]],
                                  sem).start(priority=k % 2)
        return carry

    lax.fori_loop(0, tm, body, 0)
    _row_copy_wait(xs_ref, tm * EXPERT_TOPK, sem)


def _dispatch(dest_flat, h2r, n_rows_padded):
    T, n_words, _ = h2r.shape
    tm = DISP_TM
    return pl.pallas_call(
        _dispatch_kernel,
        out_shape=jax.ShapeDtypeStruct((n_rows_padded, n_words, LANES), I32),
        grid=(T // tm,),
        in_specs=[pl.BlockSpec((tm * EXPERT_TOPK,), lambda i: (i,), memory_space=pltpu.SMEM),
                  pl.BlockSpec((tm, n_words, LANES), lambda i: (i, 0, 0))],
        out_specs=pl.BlockSpec(memory_space=pl.ANY),
        scratch_shapes=[pltpu.SemaphoreType.DMA],
        compiler_params=_cparams(("arbitrary",), disable_bounds_checks=True),
        name="dispatch",
    )(dest_flat, h2r)


def _expert_kernel(ps_ref, cnt_ref, wg_ref, wu_ref, wd_ref, xs_ref, ys_ref,
                   wg_sc, wu_sc, wd_sc, xbuf, ybuf, sem_in, sem_out):
    e = pl.program_id(0)
    bm = EXP_BM
    n_words = xbuf.shape[1]
    cnt = cnt_ref[e]
    row0 = ps_ref[e]
    nb = (cnt + bm - 1) // bm

    def in_copy(b, slot):
        return pltpu.make_async_copy(xs_ref.at[pl.ds(row0 + b * bm, bm)], xbuf.at[pl.ds(slot * bm, bm)],
                                     sem_in.at[slot])

    def out_copy(b, slot):
        return pltpu.make_async_copy(ybuf.at[pl.ds(slot * bm, bm)], ys_ref.at[pl.ds(row0 + b * bm, bm)],
                                     sem_out.at[slot])

    @pl.when(nb > 0)
    def _():
        in_copy(0, 0).start()
        wg_sc[...] = wg_ref[...].astype(BF16)
        wu_sc[...] = wu_ref[...].astype(BF16)
        wd_sc[...] = wd_ref[...].astype(BF16)

        def body(b, carry):
            slot = lax.rem(b, 2)

            @pl.when(b + 1 < nb)
            def _():
                in_copy(b + 1, 1 - slot).start()

            in_copy(b, slot).wait()

            @pl.when(b >= 2)
            def _():
                out_copy(b - 2, slot).wait()

            feats = []
            for c in range(n_words):
                feats += _unpack_row_words(_load_word_slab(xbuf, slot * bm, bm, c))
            x = jnp.concatenate(feats, axis=1)
            valid = lax.broadcasted_iota(I32, (bm, 1), 0) < cnt - b * bm
            xb = jnp.where(valid, x, 0.0).astype(BF16)
            a = _dot(xb, wg_sc[...])
            u = _dot(xb, wu_sc[...])
            hmid = (_silu(a) * u).astype(BF16)
            ob = _dot(hmid, wd_sc[...])
            for c, slab in enumerate(_pack_row_words(ob)):
                _store_word_slab(ybuf, slot * bm, bm, c, slab)
            out_copy(b, slot).start()
            return carry

        lax.fori_loop(0, nb, body, 0)

        @pl.when(nb >= 2)
        def _():
            out_copy(nb - 2, lax.rem(nb, 2)).wait()
        out_copy(nb - 1, lax.rem(nb - 1, 2)).wait()


def _experts(pad_starts, counts, xs, w_gate_e, w_up_e, w_down_e):
    E, D, DE = w_gate_e.shape
    n_words = xs.shape[1]
    bm = EXP_BM
    wsel = lambda e, ps, cnt: (e, 0, 0)
    anyspec = pl.BlockSpec(memory_space=pl.ANY)
    return pl.pallas_call(
        _expert_kernel,
        out_shape=jax.ShapeDtypeStruct(xs.shape, I32),
        grid_spec=pltpu.PrefetchScalarGridSpec(
            num_scalar_prefetch=2,
            grid=(E,),
            in_specs=[pl.BlockSpec((None, D, DE), wsel), pl.BlockSpec((None, D, DE), wsel),
                      pl.BlockSpec((None, DE, D), wsel), anyspec],
            out_specs=anyspec,
            scratch_shapes=[pltpu.VMEM((D, DE), BF16), pltpu.VMEM((D, DE), BF16), pltpu.VMEM((DE, D), BF16),
                            pltpu.VMEM((2 * bm, n_words, LANES), I32), pltpu.VMEM((2 * bm, n_words, LANES), I32),
                            pltpu.SemaphoreType.DMA((2,)), pltpu.SemaphoreType.DMA((2,))],
        ),
        compiler_params=_cparams(("arbitrary",)),
        name="experts",
    )(pad_starts, counts, w_gate_e, w_up_e, w_down_e, xs)


def _combine_kernel(dcur_ref, dnxt_ref, g_ref, xpart_ref, mod_ref, gfin_ref, ys_ref, o_ref,
                    buf0, buf1, sem0, sem1):
    i = pl.program_id(0)
    n_steps = pl.num_programs(0)
    tm = xpart_ref.shape[0]
    n_words = buf0.shape[1]

    def issue(dest_ref, buf, sem):
        def body(t, carry):
            for k in range(EXPERT_TOPK):
                pltpu.make_async_copy(ys_ref.at[dest_ref[t * EXPERT_TOPK + k]], buf.at[k * tm + t],
                                      sem).start(priority=k % 2)
            return carry
        lax.fori_loop(0, tm, body, 0)

    def reduce_tile(buf, sem):
        _row_copy_wait(ys_ref, tm * EXPERT_TOPK, sem)
        gt2 = mod_ref[0, 5:6, :]
        g = g_ref[...]
        cols = []
        for c in range(n_words):
            lo = jnp.zeros((tm, LANES), F32)
            hi = jnp.zeros((tm, LANES), F32)
            for k in range(EXPERT_TOPK):
                a, b = _unpack_row_words(_load_word_slab(buf, k * tm, tm, c))
                gk = g[:, k:k + 1]
                lo = lo + gk * a
                hi = hi + gk * b
            cols += [lo, hi]
        routed = jnp.concatenate(cols, axis=1)
        o_ref[...] = _rms(xpart_ref[...] + gt2 * routed, gfin_ref[...])

    @pl.when(i == 0)
    def _():
        issue(dcur_ref, buf0, sem0)

    for parity, (cur, nxt) in enumerate((((buf0, sem0), (buf1, sem1)), ((buf1, sem1), (buf0, sem0)))):
        @pl.when(i % 2 == parity)
        def _():
            @pl.when(i + 1 < n_steps)
            def _():
                issue(dnxt_ref, *nxt)
            reduce_tile(*cur)


def _combine(dest, gates_t, xpart, mod3, g_final, ys, S):
    T, D = xpart.shape
    tm = COMB_TM
    n_steps = T // tm
    nt_per_seq = S // tm
    n_words = ys.shape[1]
    return pl.pallas_call(
        _combine_kernel,
        out_shape=jax.ShapeDtypeStruct((T, D), F32),
        grid=(n_steps,),
        in_specs=[
            pl.BlockSpec((tm * EXPERT_TOPK,), lambda i: (i,), memory_space=pltpu.SMEM),
            pl.BlockSpec((tm * EXPERT_TOPK,), lambda i: (jnp.minimum(i + 1, n_steps - 1),),
                         memory_space=pltpu.SMEM),
            pl.BlockSpec((tm, EXPERT_TOPK), lambda i: (i, 0)),
            pl.BlockSpec((tm, D), lambda i: (i, 0)),
            pl.BlockSpec((1, 6, D), lambda i: (i // nt_per_seq, 0, 0)),
            pl.BlockSpec((1, D), lambda i: (0, 0)),
            pl.BlockSpec(memory_space=pl.ANY),
        ],
        out_specs=pl.BlockSpec((tm, D), lambda i: (i, 0)),
        scratch_shapes=[pltpu.VMEM((EXPERT_TOPK * tm, n_words, LANES), I32),
                        pltpu.VMEM((EXPERT_TOPK * tm, n_words, LANES), I32),
                        pltpu.SemaphoreType.DMA, pltpu.SemaphoreType.DMA],
        compiler_params=_cparams(("arbitrary",), disable_bounds_checks=True),
        name="combine",
    )(dest, dest, gates_t, xpart, mod3, g_final, ys)


def _layer(x2, mod3, S, g_mix, w_in, g_kv, w_kv_up, g_moba_out, g_dsa_out, w_out, g_ffn, w_router,
           router_bias, w_gate_e, w_up_e, w_down_e, w_gate_s, w_up_s, w_down_s, g_final, tab_h, half_h,
           tab_i, half_i):
    T, D = x2.shape
    B = T // S
    w_in_p = jnp.pad(w_in, ((0, 0), (0, _C_END - w_in.shape[1]))).astype(BF16)
    (q_mt, k_m, v_mt, kmean, q_dt, k_d, v_dt, q_it, k_i, w_it) = _in_proj(
        x2, mod3, g_mix.reshape(1, D), w_in_p, g_kv.reshape(1, KV_LORA), w_kv_up.astype(BF16),
        tab_h, half_h, tab_i, half_i, S)
    o_m = _moba_attn(q_mt, k_m, v_mt, kmean, B, S)
    o_d = _dsa_attn(q_it, w_it, q_dt, k_i, k_d, v_dt, B, S)

    wr_t = w_router.T
    wr_hi = wr_t.astype(BF16)
    wr_lo = (wr_t - wr_hi.astype(F32)).astype(BF16)
    w_gu_s = jnp.concatenate([w_gate_s, w_up_s], axis=1).astype(BF16)
    xpart, h2r, eidx, rank, gates, cnt = _post_attn(
        x2, o_m, o_d, mod3, g_moba_out.reshape(1, MOBA_W), g_dsa_out.reshape(1, DSA_W), w_out.astype(BF16),
        g_ffn.reshape(1, D), w_gu_s, w_down_s.astype(BF16), wr_hi, wr_lo,
        router_bias.reshape(N_EXPERTS, 1), S)

    bm = EXP_BM
    n_blocks = T * EXPERT_TOPK // bm + N_EXPERTS
    counts = cnt[:, 0].astype(I32)
    padded = (counts + bm - 1) // bm * bm
    pad_ends = jnp.cumsum(padded)
    pad_starts = (pad_ends - padded).astype(I32)
    dest_flat = _slots(pad_starts, eidx, rank).T.reshape(-1)
    xs = _dispatch(dest_flat, h2r, n_blocks * bm)
    ys = _experts(pad_starts, counts, xs, w_gate_e, w_up_e, w_down_e)
    return _combine(dest_flat, gates.T, xpart, mod3, g_final.reshape(1, D), ys, S)


def kernel(x, c, w_ada, b_ada, g_mix, w_in, g_kv, w_kv_up, g_moba_out, g_dsa_out, w_out, g_ffn, w_router,
           router_bias, w_gate_e, w_up_e, w_down_e, w_gate_s, w_up_s, w_down_s, g_final):
    B, S, D = x.shape
    depth = w_ada.shape[0]
    assert depth == 1, "the final norm is fused into the single layer"
    assert S % PROJ_TM == 0 and S % DSA_KC == 0 and S % POST_TM == 0 and S >= 4 * DSA_MAX_TOPK
    tab_h, half_h = _rope_tables(S, HEAD_DIM, 1)
    tab_i, half_i = _rope_tables(S, IDX_DIM, LANES // IDX_DIM)
    x2 = x.reshape(B * S, D)
    sq = lambda a: a.reshape(a.shape[1:])
    mod3 = _ada_mod(c, sq(w_ada), sq(b_ada)).reshape(B, 6, D)
    out = _layer(x2, mod3, S, sq(g_mix), sq(w_in), sq(g_kv), sq(w_kv_up), sq(g_moba_out), sq(g_dsa_out),
                 sq(w_out), sq(g_ffn), sq(w_router), sq(router_bias), sq(w_gate_e), sq(w_up_e), sq(w_down_e),
                 sq(w_gate_s), sq(w_up_s), sq(w_down_s), g_final, tab_h, half_h, tab_i, half_i)
    return out.reshape(B, S, D)
```

```python
import functools

import jax
import jax.numpy as jnp
from jax import lax
from jax.experimental import pallas as pl
from jax.experimental.pallas import tpu as pltpu

HEAD_DIM = 128
MOBA_HEADS = 4
DSA_HEADS = 4
MOBA_W = MOBA_HEADS * HEAD_DIM
DSA_W = DSA_HEADS * HEAD_DIM
MOBA_BLOCK = 256
MOBA_TOPK = 3
DSA_MAX_TOPK = 256
KV_LORA = 256
IDX_HEADS = 8
IDX_DIM = 64
ROPE_THETA = 500000.0
ROPE_FRACTION_DIV = 4
N_EXPERTS = 256
EXPERT_TOPK = 8
N_GROUPS = 8
TOPK_GROUPS = 4
GROUP_SIZE = N_EXPERTS // N_GROUPS
D_EXPERT = 256
D_SHARED = 256
ROUTED_SCALE = 2.5
EPS = 1e-6

LANES = 128
SUBLANES = 8
VMEM_LIMIT = 56 * 1024 * 1024

PROJ_TM = 512
DSA_TQ = 256
DSA_KC = 512
DSA_KA = 256
POST_TM = 256
DISP_TM = 256
EXP_BM = 256
EXP_AHEAD = 3
EXP_IN_SLOTS = EXP_AHEAD + 1
EXP_OUT_SLOTS = 2
COMB_TM = 128
NEG = -1e30
INT_MIN = -2147483648
LOG2E = 1.4426950408889634

F32 = jnp.float32
BF16 = jnp.bfloat16
I32 = jnp.int32


def _cparams(sem, **kw):
    return pltpu.CompilerParams(dimension_semantics=sem, vmem_limit_bytes=VMEM_LIMIT, **kw)


def _dot(a, b):
    return jnp.dot(a, b, preferred_element_type=F32)


def _dot_nt(a, b):
    return lax.dot_general(a, b, (((1,), (1,)), ((), ())), preferred_element_type=F32)


def _silu(x):
    return x * (1.0 / (1.0 + jnp.exp(-x)))


def _rms(x, g):
    return x * lax.rsqrt(jnp.mean(x * x, axis=-1, keepdims=True) + EPS) * g


ROW_WORDS = 2 * LANES


def _pack_row_words(x):
    slabs = []
    for c in range(x.shape[1] // ROW_WORDS):
        lo = pltpu.bitcast(x[:, c * ROW_WORDS:c * ROW_WORDS + LANES].astype(BF16).astype(F32), I32)
        hi = pltpu.bitcast(x[:, c * ROW_WORDS + LANES:(c + 1) * ROW_WORDS].astype(BF16).astype(F32), I32)
        slabs.append(lax.shift_right_logical(lo, 16) | hi)
    return slabs


def _word_slab_index(rows_ref, row0, n_rows, c):
    n_words = rows_ref.shape[1]
    flat = rows_ref.reshape(rows_ref.shape[0] * n_words, LANES)
    return flat, pl.ds(row0 * n_words + c, n_rows, stride=n_words)


def _load_word_slab(rows_ref, row0, n_rows, c):
    flat, idx = _word_slab_index(rows_ref, row0, n_rows, c)
    return flat[idx, :]


def _store_word_slab(rows_ref, row0, n_rows, c, value):
    flat, idx = _word_slab_index(rows_ref, row0, n_rows, c)
    flat[idx, :] = value


def _unpack_row_words(u):
    return pltpu.bitcast(lax.shift_left(u, 16), F32), pltpu.bitcast(u & jnp.int32(-65536), F32)


def _ada_kernel(c_ref, w_ref, b_ref, o_ref):
    ca = _silu(c_ref[...])
    o_ref[...] = jnp.dot(ca, w_ref[...], preferred_element_type=F32,
                         precision=lax.Precision.HIGHEST) + b_ref[...]


def _ada_mod(c, w_ada, b_ada):
    B, D = c.shape
    N = w_ada.shape[1]
    tn = 1024
    return pl.pallas_call(
        _ada_kernel,
        out_shape=jax.ShapeDtypeStruct((B, N), F32),
        grid=(N // tn,),
        in_specs=[pl.BlockSpec((B, D), lambda j: (0, 0)),
                  pl.BlockSpec((D, tn), lambda j: (0, j)),
                  pl.BlockSpec((1, tn), lambda j: (0, j))],
        out_specs=pl.BlockSpec((B, tn), lambda j: (0, j)),
        compiler_params=_cparams(("arbitrary",)),
        name="ada_mod",
    )(c, w_ada, b_ada.reshape(1, N))


def _rope_tables(seq, head_dim, heads_per_vreg):
    rot = head_dim // ROPE_FRACTION_DIV
    half = rot // 2
    inv = jnp.float32(ROPE_THETA) ** (-(jnp.arange(0, rot, 2, dtype=F32) / rot))
    ang = jnp.arange(seq, dtype=F32)[:, None] * inv[None, :]
    cos, sin = jnp.cos(ang), jnp.sin(ang)
    ones = jnp.ones((seq, head_dim - rot), F32)
    zeros_h = jnp.zeros((seq, half), F32)
    zeros_r = jnp.zeros((seq, head_dim - rot), F32)
    c = jnp.concatenate([cos, cos, ones], axis=1)
    sp = jnp.concatenate([zeros_h, sin, zeros_r], axis=1)
    sm = jnp.concatenate([-sin, zeros_h, zeros_r], axis=1)
    rep = lambda t: jnp.tile(t, (1, heads_per_vreg))
    return jnp.stack([rep(c), rep(sp), rep(sm)], axis=0), half


def _rope(x, tab_ref, half):
    return (x * tab_ref[0] + pltpu.roll(x, half, 1) * tab_ref[1]
            + pltpu.roll(x, LANES - half, 1) * tab_ref[2])


_C_QM, _C_KM, _C_VM, _C_QD = 0, MOBA_W, 2 * MOBA_W, 3 * MOBA_W
_C_CKV = 3 * MOBA_W + DSA_W
_C_QI = _C_CKV + KV_LORA
_C_KI = _C_QI + IDX_HEADS * IDX_DIM
_C_END = _C_KI + LANES


def _in_proj_kernel(x_ref, mod_ref, gmix_ref, w_ref, gkv_ref, wkv_ref, tabh_ref, tabi_ref,
                    qmt_ref, km_ref, vmt_ref, kmean_ref, qdt_ref, kd_ref, vdt_ref, qit_ref, ki_ref, wit_ref,
                    *, half_h, half_i):
    tm = x_ref.shape[0]
    x = x_ref[...]
    sh1 = mod_ref[0, 0:1, :]
    sc1 = mod_ref[0, 1:2, :]
    h = (_rms(x, gmix_ref[...]) * (1.0 + sc1) + sh1).astype(BF16)

    def proj(c0, width):
        return _dot(h, w_ref[:, c0:c0 + width])

    q_scale = HEAD_DIM ** -0.5 * LOG2E
    nblk = tm // MOBA_BLOCK
    qm = proj(_C_QM, MOBA_W)
    km = proj(_C_KM, MOBA_W)
    vm = proj(_C_VM, MOBA_W)
    for hd in range(MOBA_HEADS):
        sl = slice(hd * HEAD_DIM, (hd + 1) * HEAD_DIM)
        qmt_ref[sl, :] = (_rope(qm[:, sl], tabh_ref, half_h) * q_scale).T.astype(BF16)
        kr = _rope(km[:, sl], tabh_ref, half_h)
        km_ref[:, sl] = kr.astype(BF16)
        for blk in range(nblk):
            rows = slice(blk * MOBA_BLOCK, (blk + 1) * MOBA_BLOCK)
            kmean_ref[blk:blk + 1, sl] = jnp.mean(kr[rows], axis=0, keepdims=True)
            vmt_ref[blk, sl, :] = vm[rows, sl].T.astype(BF16)
    qd = proj(_C_QD, DSA_W)
    for hd in range(DSA_HEADS):
        sl = slice(hd * HEAD_DIM, (hd + 1) * HEAD_DIM)
        qdt_ref[sl, :] = (_rope(qd[:, sl], tabh_ref, half_h) * q_scale).T.astype(BF16)
    ckv = proj(_C_CKV, KV_LORA)
    kv = _dot(_rms(ckv, gkv_ref[...]).astype(BF16), wkv_ref[...])
    kd_ref[...] = _rope(kv[:, :HEAD_DIM], tabh_ref, half_h).astype(BF16)
    for ch in range(tm // DSA_KA):
        vdt_ref[ch] = kv[ch * DSA_KA:(ch + 1) * DSA_KA, HEAD_DIM:].T.astype(BF16)
    qi = proj(_C_QI, IDX_HEADS * IDX_DIM)
    for j in range(IDX_HEADS * IDX_DIM // LANES):
        sl = slice(j * LANES, (j + 1) * LANES)
        qit_ref[sl, :] = _rope(qi[:, sl], tabi_ref, half_i).T.astype(BF16)
    kw = proj(_C_KI, LANES)
    ki_ref[...] = _rope(kw, tabi_ref, half_i)[:, :IDX_DIM].astype(BF16)
    wit_ref[...] = kw.T[IDX_DIM:IDX_DIM + IDX_HEADS, :] * (IDX_HEADS ** -0.5 * IDX_DIM ** -0.5)


def _in_proj(x2, mod3, g_mix, w_in_p, g_kv, w_kv_up, tab_h, half_h, tab_i, half_i, S):
    T, D = x2.shape
    tm = PROJ_TM
    nt_per_seq = S // tm
    row = lambda i: (i, 0)
    col = lambda i: (0, i)
    nb = tm // MOBA_BLOCK
    nc = tm // DSA_KA
    outs = [
        jax.ShapeDtypeStruct((MOBA_W, T), BF16),
        jax.ShapeDtypeStruct((T, MOBA_W), BF16),
        jax.ShapeDtypeStruct((T // MOBA_BLOCK, MOBA_W, MOBA_BLOCK), BF16),
        jax.ShapeDtypeStruct((T // tm, nb, MOBA_W), F32),
        jax.ShapeDtypeStruct((DSA_W, T), BF16),
        jax.ShapeDtypeStruct((T, HEAD_DIM), BF16),
        jax.ShapeDtypeStruct((T // DSA_KA, HEAD_DIM, DSA_KA), BF16),
        jax.ShapeDtypeStruct((IDX_HEADS * IDX_DIM, T), BF16),
        jax.ShapeDtypeStruct((T, IDX_DIM), BF16),
        jax.ShapeDtypeStruct((IDX_HEADS, T), F32),
    ]
    out_specs = [
        pl.BlockSpec((MOBA_W, tm), col), pl.BlockSpec((tm, MOBA_W), row),
        pl.BlockSpec((nb, MOBA_W, MOBA_BLOCK), lambda i: (i, 0, 0)),
        pl.BlockSpec((None, nb, MOBA_W), lambda i: (i, 0, 0)),
        pl.BlockSpec((DSA_W, tm), col), pl.BlockSpec((tm, HEAD_DIM), row),
        pl.BlockSpec((nc, HEAD_DIM, DSA_KA), lambda i: (i, 0, 0)),
        pl.BlockSpec((IDX_HEADS * IDX_DIM, tm), col), pl.BlockSpec((tm, IDX_DIM), row),
        pl.BlockSpec((IDX_HEADS, tm), col),
    ]
    res = pl.pallas_call(
        functools.partial(_in_proj_kernel, half_h=half_h, half_i=half_i),
        out_shape=outs,
        grid=(T // tm,),
        in_specs=[
            pl.BlockSpec((tm, D), row),
            pl.BlockSpec((1, 6, D), lambda i: (i // nt_per_seq, 0, 0)),
            pl.BlockSpec((1, D), lambda i: (0, 0)),
            pl.BlockSpec((D, _C_END), lambda i: (0, 0)),
            pl.BlockSpec((1, KV_LORA), lambda i: (0, 0)),
            pl.BlockSpec((KV_LORA, 2 * HEAD_DIM), lambda i: (0, 0)),
            pl.BlockSpec((3, tm, LANES), lambda i: (0, i % nt_per_seq, 0)),
            pl.BlockSpec((3, tm, LANES), lambda i: (0, i % nt_per_seq, 0)),
        ],
        out_specs=out_specs,
        compiler_params=_cparams(("parallel",)),
        name="in_proj",
    )(x2, mod3, g_mix, w_in_p, g_kv, w_kv_up, tab_h, tab_i)
    res = list(res)
    res[3] = res[3].reshape(T // MOBA_BLOCK, MOBA_W)
    return res


def _moba_kernel(qt_ref, k_ref, vt_ref, kmean_ref, o_ref, bias_sc, *accs):
    qi = pl.program_id(1)
    blk = MOBA_BLOCK
    nb = kmean_ref.shape[0]
    heads = range(MOBA_HEADS)
    hsl = [slice(hd * HEAD_DIM, (hd + 1) * HEAD_DIM) for hd in heads]
    qts = [qt_ref[hsl[hd], :] for hd in heads]
    row = lax.broadcasted_iota(I32, (nb, blk), 0)
    past = row < qi
    start = pl.multiple_of(qi * blk, blk)
    k_io = lax.broadcasted_iota(I32, (blk, blk), 0)
    q_io = lax.broadcasted_iota(I32, (blk, blk), 1)

    init = []
    for hd in heads:
        km = kmean_ref[:, hsl[hd]]
        km_hi = km.astype(BF16)
        km_lo = (km - km_hi.astype(F32)).astype(BF16)
        gate = _dot(km_hi, qts[hd]) + _dot(km_lo, qts[hd])
        gate = jnp.where(past, gate, -jnp.inf)
        bias = jnp.full((nb, blk), NEG, F32)
        for _ in range(MOBA_TOPK):
            _, idx = _first_index_of_max(gate, row, nb)
            hit = row == idx
            bias = jnp.where(hit, 0.0, bias)
            gate = jnp.where(hit, -jnp.inf, gate)
        bias_sc[hd] = jnp.where(past, bias, NEG)

        s = _dot(k_ref[pl.ds(start, blk), hsl[hd]], qts[hd])
        s = jnp.where(k_io <= q_io, s, NEG)
        m0 = jnp.max(s, axis=0, keepdims=True)
        p = jnp.exp2(s - m0)
        accs[hd][...] = _dot(vt_ref[qi, hsl[hd], :], p.astype(BF16))
        init += [m0, jnp.sum(p, axis=0, keepdims=True)]

    def body(n, carry):
        st = pl.multiple_of(n * blk, blk)
        kb = k_ref[pl.ds(st, blk), :]
        sbs = [_dot(kb[:, hsl[hd]], qts[hd]) + bias_sc[hd, pl.ds(n, 1), :] for hd in heads]
        old = [accs[hd][...] for hd in heads]
        out, new = [], []
        for hd in heads:
            m_old, l_old = carry[2 * hd], carry[2 * hd + 1]
            m_new = jnp.maximum(m_old, jnp.max(sbs[hd], axis=0, keepdims=True))
            alpha = jnp.exp2(m_old - m_new)
            pb = jnp.exp2(sbs[hd] - m_new)
            new.append(alpha * old[hd] + _dot(vt_ref[n, hsl[hd], :], pb.astype(BF16)))
            out += [m_new, alpha * l_old + jnp.sum(pb, axis=0, keepdims=True)]
        for hd in heads:
            accs[hd][...] = new[hd]
        return tuple(out)

    fin = lax.fori_loop(0, qi, body, tuple(init))
    for hd in heads:
        o_ref[:, hsl[hd]] = (accs[hd][...] * (1.0 / fin[2 * hd + 1])).T


def _moba_attn(q_mt, k_m, v_mt, kmean, B, S):
    T = B * S
    blk = MOBA_BLOCK
    nq = S // blk
    return pl.pallas_call(
        _moba_kernel,
        out_shape=jax.ShapeDtypeStruct((T, MOBA_W), F32),
        grid=(B, nq),
        in_specs=[
            pl.BlockSpec((MOBA_W, blk), lambda b, i: (0, b * nq + i)),
            pl.BlockSpec((S, MOBA_W), lambda b, i: (b, 0)),
            pl.BlockSpec((nq, MOBA_W, blk), lambda b, i: (b, 0, 0)),
            pl.BlockSpec((nq, MOBA_W), lambda b, i: (b, 0)),
        ],
        out_specs=pl.BlockSpec((blk, MOBA_W), lambda b, i: (b * nq + i, 0)),
        scratch_shapes=[pltpu.VMEM((MOBA_HEADS, nq, blk), F32)]
        + [pltpu.VMEM((HEAD_DIM, blk), F32) for _ in range(MOBA_HEADS)],
        compiler_params=_cparams(("parallel", "arbitrary")),
        name="moba_attn",
    )(q_mt, k_m, v_mt, kmean)


def _sortable_key(x):
    b = pltpu.bitcast(x, I32)
    return jnp.where(b >= 0, b, b ^ jnp.int32(0x7FFFFFFF))


def _dsa_kernel(qit_ref, wit_ref, qdt_ref, ki_ref, kd_ref, vdt_ref, o_ref, key_sc, *accs, topk):
    t = pl.program_id(1)
    tq, kc = DSA_TQ, DSA_KC
    S = key_sc.shape[0]
    q0 = t * tq
    n_chunks = (q0 + tq + kc - 1) // kc
    key_io = lax.broadcasted_iota(I32, (kc, tq), 0)
    q_pos = q0 + lax.broadcasted_iota(I32, (kc, tq), 1)
    w = wit_ref[...]

    def score_chunk(c, carry):
        k0 = pl.multiple_of(c * kc, kc)
        kic = ki_ref[pl.ds(k0, kc), :]
        acc = jnp.zeros((kc, tq), F32)
        for hd in range(IDX_HEADS):
            lg = _dot(kic, qit_ref[hd * IDX_DIM:(hd + 1) * IDX_DIM, :])
            acc = acc + jnp.maximum(lg, 0.0) * w[hd:hd + 1, :]
        acc = jnp.where(acc == 0.0, 0.0, acc)
        key = _sortable_key(acc)
        key_sc[pl.ds(k0, kc), :] = jnp.where(k0 + key_io <= q_pos, key, INT_MIN)
        return carry

    lax.fori_loop(0, n_chunks, score_chunk, 0)

    def count(pred_fn):
        def cbody(c, acc):
            k0 = pl.multiple_of(c * kc, kc)
            hit = jnp.where(pred_fn(key_sc[pl.ds(k0, kc), :], k0), 1, 0)
            return acc + jnp.sum(hit.reshape(kc // SUBLANES, SUBLANES, tq), axis=0)
        acc = lax.fori_loop(0, n_chunks, cbody, jnp.zeros((SUBLANES, tq), I32))
        return jnp.sum(acc, axis=0, keepdims=True)

    c_nonneg = count(lambda kk, k0: kk >= 0)
    thr0 = jnp.where(c_nonneg >= topk, 0, INT_MIN).astype(I32)

    def bit_step(i, thr):
        cand = thr | lax.shift_left(jnp.int32(1), 30 - i)
        cnt = count(lambda kk, k0: kk >= cand)
        return jnp.where(cnt >= topk, cand, thr)

    thr = lax.fori_loop(0, 31, bit_step, thr0)

    n_gt = count(lambda kk, k0: kk > thr)
    n_ge = count(lambda kk, k0: kk >= thr)
    need = topk - n_gt
    overflow = (n_ge > topk) & (thr != INT_MIN)
    any_overflow = jnp.max(jnp.where(overflow, 1, 0)) > 0
    nbits = max(1, (S - 1).bit_length())

    def cut_search():
        def step(i, lo):
            cand = lo | lax.shift_left(jnp.int32(1), nbits - 1 - i)
            cnt = count(lambda kk, k0: (kk == thr) & (k0 + key_io < cand))
            return jnp.where(cnt >= need, lo, cand)
        return lax.fori_loop(0, nbits, step, jnp.zeros((1, tq), I32))

    jcut = lax.cond(any_overflow, cut_search, lambda: jnp.zeros((1, tq), I32))
    jcut = jnp.where(overflow, jcut, S)
    thr_ge = jnp.where(thr == INT_MIN, INT_MIN + 1, thr)

    pairs = range(DSA_HEADS // 2)
    qst = [jnp.concatenate([qdt_ref[(2 * g + j) * HEAD_DIM:(2 * g + j + 1) * HEAD_DIM, :] for j in range(2)],
                           axis=1) for g in pairs]
    for g in pairs:
        accs[g][...] = jnp.zeros(accs[g].shape, F32)

    ka = DSA_KA
    pos_io = lax.broadcasted_iota(I32, (ka, tq), 0)

    def attn_chunk(c, carry):
        k0 = pl.multiple_of(c * ka, ka)
        kk = key_sc[pl.ds(k0, ka), :]
        bias = lax.cond(
            any_overflow,
            lambda: jnp.where((kk > thr_ge) | ((kk == thr_ge) & (k0 + pos_io <= jcut)), 0.0, NEG),
            lambda: jnp.where(kk >= thr_ge, 0.0, NEG))
        bias2 = jnp.concatenate([bias, bias], axis=1)
        kdc = kd_ref[pl.ds(k0, ka), :]
        vtc = vdt_ref[c]
        ss = [_dot(kdc, qst[g]) + bias2 for g in pairs]
        old = [accs[g][...] for g in pairs]
        out, new = [], []
        for g in pairs:
            m_old, l_old = carry[2 * g], carry[2 * g + 1]
            m_new = jnp.maximum(m_old, jnp.max(ss[g], axis=0, keepdims=True))
            alpha = jnp.exp2(m_old - m_new)
            p = jnp.exp2(ss[g] - m_new)
            new.append(alpha * old[g] + _dot(vtc, p.astype(BF16)))
            out += [m_new, alpha * l_old + jnp.sum(p, axis=0, keepdims=True)]
        for g in pairs:
            accs[g][...] = new[g]
        return tuple(out)

    init = (jnp.full((1, 2 * tq), NEG, F32), jnp.zeros((1, 2 * tq), F32)) * len(pairs)
    fin = lax.fori_loop(0, (q0 + tq + ka - 1) // ka, attn_chunk, init)
    for g in pairs:
        out_t = accs[g][...] * (1.0 / fin[2 * g + 1])
        for j in range(2):
            hd = 2 * g + j
            o_ref[:, hd * HEAD_DIM:(hd + 1) * HEAD_DIM] = out_t[:, j * tq:(j + 1) * tq].T


def _dsa_attn(q_it, w_it, q_dt, k_i, k_d, v_dt, B, S):
    T = B * S
    tq = DSA_TQ
    nq = S // tq
    topk = min(DSA_MAX_TOPK, S // 4)
    qcol = lambda b, t: (0, b * nq + t)
    seq = lambda b, t: (b, 0)
    return pl.pallas_call(
        functools.partial(_dsa_kernel, topk=topk),
        out_shape=jax.ShapeDtypeStruct((T, DSA_W), F32),
        grid=(B, nq),
        in_specs=[
            pl.BlockSpec((IDX_HEADS * IDX_DIM, tq), qcol),
            pl.BlockSpec((IDX_HEADS, tq), qcol),
            pl.BlockSpec((DSA_W, tq), qcol),
            pl.BlockSpec((S, IDX_DIM), seq),
            pl.BlockSpec((S, HEAD_DIM), seq),
            pl.BlockSpec((S // DSA_KA, HEAD_DIM, DSA_KA), lambda b, t: (b, 0, 0)),
        ],
        out_specs=pl.BlockSpec((tq, DSA_W), lambda b, t: (b * nq + t, 0)),
        scratch_shapes=[pltpu.VMEM((S, tq), I32)]
        + [pltpu.VMEM((HEAD_DIM, 2 * tq), F32) for _ in range(DSA_HEADS // 2)],
        compiler_params=_cparams(("parallel", "arbitrary")),
        name="dsa_attn",
    )(q_it, w_it, q_dt, k_i, k_d, v_dt)


def _first_index_of_max(v, row_io, n_rows):
    m = jnp.max(v, axis=0, keepdims=True)
    idx = jnp.min(jnp.where(v == m, row_io, n_rows), axis=0, keepdims=True)
    return m, idx


def _post_kernel(x_ref, om_ref, od_ref, mod_ref, gm_ref, gd_ref, wout_ref, gffn_ref, wgu_ref, wds_ref,
                 wrh_ref, wrl_ref, rb_ref, tri_ref,
                 xpart_ref, h2r_ref, eidx_ref, rank_ref, gate_ref, cnt_ref, base_sc):
    i = pl.program_id(0)
    tm = x_ref.shape[0]
    gt1 = mod_ref[0, 2:3, :]
    sh2 = mod_ref[0, 3:4, :]
    sc2 = mod_ref[0, 4:5, :]
    gt2 = mod_ref[0, 5:6, :]

    mixed = jnp.concatenate([_rms(om_ref[...], gm_ref[...]), _rms(od_ref[...], gd_ref[...])], axis=1)
    x1 = x_ref[...] + gt1 * _dot(mixed.astype(BF16), wout_ref[...])
    h2 = _rms(x1, gffn_ref[...]) * (1.0 + sc2) + sh2
    h2b = h2.astype(BF16)

    au = _dot(h2b, wgu_ref[...])
    hs = (_silu(au[:, :D_SHARED]) * au[:, D_SHARED:]).astype(BF16)
    xpart_ref[...] = x1 + gt2 * _dot(hs, wds_ref[...])

    for c, slab in enumerate(_pack_row_words(h2)):
        _store_word_slab(h2r_ref, 0, tm, c, slab)

    h2lo = (h2 - h2b.astype(F32)).astype(BF16)
    logits = _dot_nt(wrh_ref[...], h2b) + _dot_nt(wrl_ref[...], h2b) + _dot_nt(wrh_ref[...], h2lo)
    scores = 1.0 / (1.0 + jnp.exp(-logits))
    biased = scores + rb_ref[...]

    g_io = lax.broadcasted_iota(I32, (GROUP_SIZE, tm), 0)
    gs_rows = []
    for g in range(N_GROUPS):
        blk = biased[g * GROUP_SIZE:(g + 1) * GROUP_SIZE, :]
        m1, i1 = _first_index_of_max(blk, g_io, GROUP_SIZE)
        m2 = jnp.max(jnp.where(g_io == i1, -jnp.inf, blk), axis=0, keepdims=True)
        gs_rows.append(m1 + m2)
    gs = jnp.concatenate(gs_rows, axis=0)
    gi = lax.broadcasted_iota(I32, (N_GROUPS, tm), 0)
    grank = jnp.zeros((N_GROUPS, tm), I32)
    for m in range(N_GROUPS):
        gm = gs[m:m + 1, :]
        grank = grank + jnp.where((gm > gs) | ((gm == gs) & (m < gi)), 1, 0)
    gsel = grank < TOPK_GROUPS
    masked = jnp.concatenate(
        [jnp.where(gsel[g:g + 1, :], biased[g * GROUP_SIZE:(g + 1) * GROUP_SIZE, :], -jnp.inf)
         for g in range(N_GROUPS)], axis=0)

    e_io = lax.broadcasted_iota(I32, (N_EXPERTS, tm), 0)
    e_rows, s_rows = [], []
    for _ in range(EXPERT_TOPK):
        _, idx = _first_index_of_max(masked, e_io, N_EXPERTS)
        hit = e_io == idx
        e_rows.append(idx)
        s_rows.append(jnp.sum(jnp.where(hit, scores, 0.0), axis=0, keepdims=True))
        masked = jnp.where(hit, -jnp.inf, masked)
    eidx = jnp.concatenate(e_rows, axis=0)
    sk = jnp.concatenate(s_rows, axis=0)
    gate_ref[...] = sk / jnp.sum(sk, axis=0, keepdims=True) * ROUTED_SCALE
    eidx_ref[...] = eidx

    @pl.when(i == 0)
    def _():
        base_sc[...] = jnp.zeros(base_sc.shape, F32)

    chosen = jnp.zeros((N_EXPERTS, tm), F32)
    for k in range(EXPERT_TOPK):
        chosen = chosen + jnp.where(e_io == e_rows[k], 1.0, 0.0)
    incl = _dot(chosen.astype(BF16), tri_ref[...])
    pos = base_sc[...] + incl - 1.0
    rank_ref[...] = jnp.concatenate(
        [jnp.sum(jnp.where(e_io == e_rows[k], pos, 0.0), axis=0, keepdims=True)
         for k in range(EXPERT_TOPK)], axis=0).astype(I32)
    base_sc[...] = base_sc[...] + incl[:, tm - 1:tm]
    cnt_ref[...] = jnp.broadcast_to(base_sc[...], cnt_ref.shape)


def _post_attn(x2, o_m, o_d, mod3, g_moba, g_dsa, w_out, g_ffn, w_gu_s, w_down_s, wr_hi, wr_lo, rbias, S):
    T, D = x2.shape
    tm = POST_TM
    nt_per_seq = S // tm
    row = lambda i: (i, 0)
    full = lambda i: (0, 0)
    tri = (jnp.arange(tm)[:, None] <= jnp.arange(tm)[None, :]).astype(BF16)
    n_words = D // ROW_WORDS
    return pl.pallas_call(
        _post_kernel,
        out_shape=[
            jax.ShapeDtypeStruct((T, D), F32),
            jax.ShapeDtypeStruct((T, n_words, LANES), I32),
            jax.ShapeDtypeStruct((EXPERT_TOPK, T), I32),
            jax.ShapeDtypeStruct((EXPERT_TOPK, T), I32),
            jax.ShapeDtypeStruct((EXPERT_TOPK, T), F32),
            jax.ShapeDtypeStruct((N_EXPERTS, LANES), F32),
        ],
        grid=(T // tm,),
        in_specs=[
            pl.BlockSpec((tm, D), row),
            pl.BlockSpec((tm, MOBA_W), row),
            pl.BlockSpec((tm, DSA_W), row),
            pl.BlockSpec((1, 6, D), lambda i: (i // nt_per_seq, 0, 0)),
            pl.BlockSpec((1, MOBA_W), full),
            pl.BlockSpec((1, DSA_W), full),
            pl.BlockSpec(w_out.shape, full),
            pl.BlockSpec((1, D), full),
            pl.BlockSpec(w_gu_s.shape, full),
            pl.BlockSpec(w_down_s.shape, full),
            pl.BlockSpec(wr_hi.shape, full),
            pl.BlockSpec(wr_lo.shape, full),
            pl.BlockSpec((N_EXPERTS, 1), full),
            pl.BlockSpec((tm, tm), full),
        ],
        out_specs=[
            pl.BlockSpec((tm, D), row),
            pl.BlockSpec((tm, n_words, LANES), lambda i: (i, 0, 0)),
            pl.BlockSpec((EXPERT_TOPK, tm), lambda i: (0, i)),
            pl.BlockSpec((EXPERT_TOPK, tm), lambda i: (0, i)),
            pl.BlockSpec((EXPERT_TOPK, tm), lambda i: (0, i)),
            pl.BlockSpec((N_EXPERTS, LANES), full),
        ],
        scratch_shapes=[pltpu.VMEM((N_EXPERTS, 1), F32)],
        compiler_params=_cparams(("arbitrary",)),
        name="post_attn",
    )(x2, o_m, o_d, mod3, g_moba, g_dsa, w_out, g_ffn, w_gu_s, w_down_s, wr_hi, wr_lo, rbias, tri)


def _row_copy_wait(rows_hbm, n_rows, sem):
    blk = rows_hbm.at[pl.ds(0, n_rows)]
    pltpu.make_async_copy(blk, blk, sem).wait()


def _slots_kernel(ps_ref, e_ref, r_ref, d_ref):
    e = e_ref[...]

    def body(x, acc):
        return jnp.where(e == x, ps_ref[x], acc)

    d_ref[...] = lax.fori_loop(0, N_EXPERTS, body, jnp.zeros(e.shape, I32)) + r_ref[...]


def _slots(pad_starts, eidx, rank):
    K, T = eidx.shape
    tm = min(T, 4096)
    blk = lambda: pl.BlockSpec((K, tm), lambda i, ps: (0, i))
    return pl.pallas_call(
        _slots_kernel,
        out_shape=jax.ShapeDtypeStruct((K, T), I32),
        grid_spec=pltpu.PrefetchScalarGridSpec(num_scalar_prefetch=1, grid=(T // tm,),
                                               in_specs=[blk(), blk()], out_specs=blk()),
        compiler_params=_cparams(("parallel",)),
        name="slots",
    )(pad_starts, eidx, rank)


def _dispatch_kernel(d_ref, h2r_ref, xs_ref, sem):
    tm = h2r_ref.shape[0]

    def body(t, carry):
        for k in range(EXPERT_TOPK):
            pltpu.make_async_copy(h2r_ref.at[t], xs_ref.at[d_ref[t * EXPERT_TOPK + k]],
                                  sem).start(priority=k % 2)
        return carry

    lax.fori_loop(0, tm, body, 0)
    _row_copy_wait(xs_ref, tm * EXPERT_TOPK, sem)


def _dispatch(dest_flat, h2r, n_rows_padded):
    T, n_words, _ = h2r.shape
    tm = DISP_TM
    return pl.pallas_call(
        _dispatch_kernel,
        out_shape=jax.ShapeDtypeStruct((n_rows_padded, n_words, LANES), I32),
        grid=(T // tm,),
        in_specs=[pl.BlockSpec((tm * EXPERT_TOPK,), lambda i: (i,), memory_space=pltpu.SMEM),
                  pl.BlockSpec((tm, n_words, LANES), lambda i: (i, 0, 0))],
        out_specs=pl.BlockSpec(memory_space=pl.ANY),
        scratch_shapes=[pltpu.SemaphoreType.DMA],
        compiler_params=_cparams(("arbitrary",), disable_bounds_checks=True),
        name="dispatch",
    )(dest_flat, h2r)


def _expert_kernel(ps_ref, cnt_ref, wg_ref, wu_ref, wd_ref, xs_ref, ys_ref,
                   wg_sc, wu_sc, wd_sc, xbuf, ybuf, sem_in, sem_out):
    e = pl.program_id(0)
    bm = EXP_BM
    n_words = xbuf.shape[1]
    cnt = cnt_ref[e]
    g0 = ps_ref[e] // bm
    nb = (cnt + bm - 1) // bm
    n_used = (ps_ref[N_EXPERTS - 1] + cnt_ref[N_EXPERTS - 1] + bm - 1) // bm

    def in_copy(g):
        slot = lax.rem(g, EXP_IN_SLOTS)
        return pltpu.make_async_copy(xs_ref.at[pl.ds(g * bm, bm)], xbuf.at[pl.ds(slot * bm, bm)],
                                     sem_in.at[slot])

    def out_copy(g):
        slot = lax.rem(g, EXP_OUT_SLOTS)
        return pltpu.make_async_copy(ybuf.at[pl.ds(slot * bm, bm)], ys_ref.at[pl.ds(g * bm, bm)],
                                     sem_out.at[slot])

    @pl.when(e == 0)
    def _():
        for g in range(EXP_AHEAD):
            @pl.when(g < n_used)
            def _():
                in_copy(g).start()

    @pl.when(nb > 0)
    def _():
        wg_sc[...] = wg_ref[...].astype(BF16)
        wu_sc[...] = wu_ref[...].astype(BF16)
        wd_sc[...] = wd_ref[...].astype(BF16)

        def body(b, carry):
            g = g0 + b

            @pl.when(g + EXP_AHEAD < n_used)
            def _():
                in_copy(g + EXP_AHEAD).start()

            in_copy(g).wait()

            @pl.when(g >= EXP_OUT_SLOTS)
            def _():
                out_copy(g - EXP_OUT_SLOTS).wait()

            xrow = lax.rem(g, EXP_IN_SLOTS) * bm
            yrow = lax.rem(g, EXP_OUT_SLOTS) * bm
            feats = []
            for c in range(n_words):
                feats += _unpack_row_words(_load_word_slab(xbuf, xrow, bm, c))
            x = jnp.concatenate(feats, axis=1)
            valid = lax.broadcasted_iota(I32, (bm, 1), 0) < cnt - b * bm
            xb = jnp.where(valid, x, 0.0).astype(BF16)
            a = _dot(xb, wg_sc[...])
            u = _dot(xb, wu_sc[...])
            hmid = (_silu(a) * u).astype(BF16)
            ob = _dot(hmid, wd_sc[...])
            for c, slab in enumerate(_pack_row_words(ob)):
                _store_word_slab(ybuf, yrow, bm, c, slab)
            out_copy(g).start()
            return carry

        lax.fori_loop(0, nb, body, 0)

    @pl.when(e == N_EXPERTS - 1)
    def _():
        for back in range(EXP_OUT_SLOTS, 0, -1):
            @pl.when(n_used >= back)
            def _():
                out_copy(n_used - back).wait()


def _experts(pad_starts, counts, xs, w_gate_e, w_up_e, w_down_e):
    E, D, DE = w_gate_e.shape
    n_words = xs.shape[1]
    bm = EXP_BM
    wsel = lambda e, ps, cnt: (e, 0, 0)
    anyspec = pl.BlockSpec(memory_space=pl.ANY)
    return pl.pallas_call(
        _expert_kernel,
        out_shape=jax.ShapeDtypeStruct(xs.shape, I32),
        grid_spec=pltpu.PrefetchScalarGridSpec(
            num_scalar_prefetch=2,
            grid=(E,),
            in_specs=[pl.BlockSpec((None, D, DE), wsel), pl.BlockSpec((None, D, DE), wsel),
                      pl.BlockSpec((None, DE, D), wsel), anyspec],
            out_specs=anyspec,
            scratch_shapes=[pltpu.VMEM((D, DE), BF16), pltpu.VMEM((D, DE), BF16), pltpu.VMEM((DE, D), BF16),
                            pltpu.VMEM((EXP_IN_SLOTS * bm, n_words, LANES), I32),
                            pltpu.VMEM((EXP_OUT_SLOTS * bm, n_words, LANES), I32),
                            pltpu.SemaphoreType.DMA((EXP_IN_SLOTS,)), pltpu.SemaphoreType.DMA((EXP_OUT_SLOTS,))],
        ),
        compiler_params=_cparams(("arbitrary",)),
        name="experts",
    )(pad_starts, counts, w_gate_e, w_up_e, w_down_e, xs)


def _combine_kernel(dcur_ref, dnxt_ref, g_ref, xpart_ref, mod_ref, gfin_ref, ys_ref, o_ref,
                    buf0, buf1, sem0, sem1):
    i = pl.program_id(0)
    n_steps = pl.num_programs(0)
    tm = xpart_ref.shape[0]
    n_words = buf0.shape[1]

    def issue(dest_ref, buf, sem):
        def body(t, carry):
            for k in range(EXPERT_TOPK):
                pltpu.make_async_copy(ys_ref.at[dest_ref[t * EXPERT_TOPK + k]], buf.at[k * tm + t],
                                      sem).start(priority=k % 2)
            return carry
        lax.fori_loop(0, tm, body, 0)

    def reduce_tile(buf, sem):
        _row_copy_wait(ys_ref, tm * EXPERT_TOPK, sem)
        gt2 = mod_ref[0, 5:6, :]
        g = g_ref[...]
        cols = []
        for c in range(n_words):
            lo = jnp.zeros((tm, LANES), F32)
            hi = jnp.zeros((tm, LANES), F32)
            for k in range(EXPERT_TOPK):
                a, b = _unpack_row_words(_load_word_slab(buf, k * tm, tm, c))
                gk = g[:, k:k + 1]
                lo = lo + gk * a
                hi = hi + gk * b
            cols += [lo, hi]
        routed = jnp.concatenate(cols, axis=1)
        o_ref[...] = _rms(xpart_ref[...] + gt2 * routed, gfin_ref[...])

    @pl.when(i == 0)
    def _():
        issue(dcur_ref, buf0, sem0)

    for parity, (cur, nxt) in enumerate((((buf0, sem0), (buf1, sem1)), ((buf1, sem1), (buf0, sem0)))):
        @pl.when(i % 2 == parity)
        def _():
            @pl.when(i + 1 < n_steps)
            def _():
                issue(dnxt_ref, *nxt)
            reduce_tile(*cur)


def _combine(dest, gates_t, xpart, mod3, g_final, ys, S):
    T, D = xpart.shape
    tm = COMB_TM
    n_steps = T // tm
    nt_per_seq = S // tm
    n_words = ys.shape[1]
    return pl.pallas_call(
        _combine_kernel,
        out_shape=jax.ShapeDtypeStruct((T, D), F32),
        grid=(n_steps,),
        in_specs=[
            pl.BlockSpec((tm * EXPERT_TOPK,), lambda i: (i,), memory_space=pltpu.SMEM),
            pl.BlockSpec((tm * EXPERT_TOPK,), lambda i: (jnp.minimum(i + 1, n_steps - 1),),
                         memory_space=pltpu.SMEM),
            pl.BlockSpec((tm, EXPERT_TOPK), lambda i: (i, 0)),
            pl.BlockSpec((tm, D), lambda i: (i, 0)),
            pl.BlockSpec((1, 6, D), lambda i: (i // nt_per_seq, 0, 0)),
            pl.BlockSpec((1, D), lambda i: (0, 0)),
            pl.BlockSpec(memory_space=pl.ANY),
        ],
        out_specs=pl.BlockSpec((tm, D), lambda i: (i, 0)),
        scratch_shapes=[pltpu.VMEM((EXPERT_TOPK * tm, n_words, LANES), I32),
                        pltpu.VMEM((EXPERT_TOPK * tm, n_words, LANES), I32),
                        pltpu.SemaphoreType.DMA, pltpu.SemaphoreType.DMA],
        compiler_params=_cparams(("arbitrary",), disable_bounds_checks=True),
        name="combine",
    )(dest, dest, gates_t, xpart, mod3, g_final, ys)


def _layer(x2, mod3, S, g_mix, w_in, g_kv, w_kv_up, g_moba_out, g_dsa_out, w_out, g_ffn, w_router,
           router_bias, w_gate_e, w_up_e, w_down_e, w_gate_s, w_up_s, w_down_s, g_final, tab_h, half_h,
           tab_i, half_i):
    T, D = x2.shape
    B = T // S
    w_in_p = jnp.pad(w_in, ((0, 0), (0, _C_END - w_in.shape[1]))).astype(BF16)
    (q_mt, k_m, v_mt, kmean, q_dt, k_d, v_dt, q_it, k_i, w_it) = _in_proj(
        x2, mod3, g_mix.reshape(1, D), w_in_p, g_kv.reshape(1, KV_LORA), w_kv_up.astype(BF16),
        tab_h, half_h, tab_i, half_i, S)
    o_m = _moba_attn(q_mt, k_m, v_mt, kmean, B, S)
    o_d = _dsa_attn(q_it, w_it, q_dt, k_i, k_d, v_dt, B, S)

    wr_t = w_router.T
    wr_hi = wr_t.astype(BF16)
    wr_lo = (wr_t - wr_hi.astype(F32)).astype(BF16)
    w_gu_s = jnp.concatenate([w_gate_s, w_up_s], axis=1).astype(BF16)
    xpart, h2r, eidx, rank, gates, cnt = _post_attn(
        x2, o_m, o_d, mod3, g_moba_out.reshape(1, MOBA_W), g_dsa_out.reshape(1, DSA_W), w_out.astype(BF16),
        g_ffn.reshape(1, D), w_gu_s, w_down_s.astype(BF16), wr_hi, wr_lo,
        router_bias.reshape(N_EXPERTS, 1), S)

    bm = EXP_BM
    n_blocks = T * EXPERT_TOPK // bm + N_EXPERTS
    counts = cnt[:, 0].astype(I32)
    padded = (counts + bm - 1) // bm * bm
    pad_ends = jnp.cumsum(padded)
    pad_starts = (pad_ends - padded).astype(I32)
    dest_flat = _slots(pad_starts, eidx, rank).T.reshape(-1)
    xs = _dispatch(dest_flat, h2r, n_blocks * bm)
    ys = _experts(pad_starts, counts, xs, w_gate_e, w_up_e, w_down_e)
    return _combine(dest_flat, gates.T, xpart, mod3, g_final.reshape(1, D), ys, S)


def kernel(x, c, w_ada, b_ada, g_mix, w_in, g_kv, w_kv_up, g_moba_out, g_dsa_out, w_out, g_ffn, w_router,
           router_bias, w_gate_e, w_up_e, w_down_e, w_gate_s, w_up_s, w_down_s, g_final):
    B, S, D = x.shape
    depth = w_ada.shape[0]
    assert depth == 1, "the final norm is fused into the single layer"
    assert S % PROJ_TM == 0 and S % DSA_KC == 0 and S % POST_TM == 0 and S >= 4 * DSA_MAX_TOPK
    tab_h, half_h = _rope_tables(S, HEAD_DIM, 1)
    tab_i, half_i = _rope_tables(S, IDX_DIM, LANES // IDX_DIM)
    x2 = x.reshape(B * S, D)
    sq = lambda a: a.reshape(a.shape[1:])
    mod3 = _ada_mod(c, sq(w_ada), sq(b_ada)).reshape(B, 6, D)
    out = _layer(x2, mod3, S, sq(g_mix), sq(w_in), sq(g_kv), sq(w_kv_up), sq(g_moba_out), sq(g_dsa_out),
                 sq(w_out), sq(g_ffn), sq(w_router), sq(router_bias), sq(w_gate_e), sq(w_up_e), sq(w_down_e),
                 sq(w_gate_s), sq(w_up_s), sq(w_down_s), g_final, tab_h, half_h, tab_i, half_i)
    return out.reshape(B, S, D)
```

```python
import functools

import jax
import jax.numpy as jnp
from jax import lax
from jax.experimental import pallas as pl
from jax.experimental.pallas import tpu as pltpu

HEAD_DIM = 128
MOBA_HEADS = 4
DSA_HEADS = 4
MOBA_W = MOBA_HEADS * HEAD_DIM
DSA_W = DSA_HEADS * HEAD_DIM
MOBA_BLOCK = 256
MOBA_TOPK = 3
DSA_MAX_TOPK = 256
KV_LORA = 256
IDX_HEADS = 8
IDX_DIM = 64
ROPE_THETA = 500000.0
ROPE_FRACTION_DIV = 4
N_EXPERTS = 256
EXPERT_TOPK = 8
N_GROUPS = 8
TOPK_GROUPS = 4
GROUP_SIZE = N_EXPERTS // N_GROUPS
D_EXPERT = 256
D_SHARED = 256
ROUTED_SCALE = 2.5
EPS = 1e-6

LANES = 128
SUBLANES = 8
VMEM_LIMIT = 56 * 1024 * 1024

PROJ_TM = 512
DSA_TQ = 256
DSA_KC = 512
DSA_KA = 256
DSA_KS = 128
POST_TM = 256
DISP_TM = 256
EXP_BM = 256
EXP_AHEAD = 3
EXP_IN_SLOTS = EXP_AHEAD + 1
EXP_OUT_SLOTS = 2
COMB_TM = 128
REDUCE_CHAINS = 4
NEG = -1e30
INT_MIN = -2147483648
LOG2E = 1.4426950408889634

F32 = jnp.float32
BF16 = jnp.bfloat16
I32 = jnp.int32


def _cparams(sem, **kw):
    return pltpu.CompilerParams(dimension_semantics=sem, vmem_limit_bytes=VMEM_LIMIT, **kw)


def _dot(a, b):
    return jnp.dot(a, b, preferred_element_type=F32)


def _dot_nt(a, b):
    return lax.dot_general(a, b, (((1,), (1,)), ((), ())), preferred_element_type=F32)


def _silu(x):
    return x * (1.0 / (1.0 + jnp.exp(-x)))


def _rms(x, g):
    return x * lax.rsqrt(jnp.mean(x * x, axis=-1, keepdims=True) + EPS) * g


def _rows_to_tile(op, x):
    parts = [x[i:i + SUBLANES] for i in range(0, x.shape[0], SUBLANES)]
    n_chains = min(REDUCE_CHAINS, len(parts))
    accs = parts[:n_chains]
    for i in range(n_chains, len(parts)):
        accs[i % n_chains] = op(accs[i % n_chains], parts[i])
    while len(accs) > 1:
        accs = [op(accs[i], accs[i + 1]) for i in range(0, len(accs) - 1, 2)] + ([accs[-1]] if len(accs) % 2 else [])
    return accs[0]


def _col_max(x):
    return jnp.max(_rows_to_tile(jnp.maximum, x), axis=0, keepdims=True)


def _col_sum(x):
    return jnp.sum(_rows_to_tile(jnp.add, x), axis=0, keepdims=True)


ROW_WORDS = 2 * LANES


def _pack_row_words(x):
    slabs = []
    for c in range(x.shape[1] // ROW_WORDS):
        lo = pltpu.bitcast(x[:, c * ROW_WORDS:c * ROW_WORDS + LANES].astype(BF16).astype(F32), I32)
        hi = pltpu.bitcast(x[:, c * ROW_WORDS + LANES:(c + 1) * ROW_WORDS].astype(BF16).astype(F32), I32)
        slabs.append(lax.shift_right_logical(lo, 16) | hi)
    return slabs


def _word_slab_index(rows_ref, row0, n_rows, c):
    n_words = rows_ref.shape[1]
    flat = rows_ref.reshape(rows_ref.shape[0] * n_words, LANES)
    return flat, pl.ds(row0 * n_words + c, n_rows, stride=n_words)


def _load_word_slab(rows_ref, row0, n_rows, c):
    flat, idx = _word_slab_index(rows_ref, row0, n_rows, c)
    return flat[idx, :]


def _store_word_slab(rows_ref, row0, n_rows, c, value):
    flat, idx = _word_slab_index(rows_ref, row0, n_rows, c)
    flat[idx, :] = value


def _unpack_row_words(u):
    return pltpu.bitcast(lax.shift_left(u, 16), F32), pltpu.bitcast(u & jnp.int32(-65536), F32)


def _ada_kernel(c_ref, w_ref, b_ref, o_ref):
    ca = _silu(c_ref[...])
    o_ref[...] = jnp.dot(ca, w_ref[...], preferred_element_type=F32,
                         precision=lax.Precision.HIGHEST) + b_ref[...]


def _ada_mod(c, w_ada, b_ada):
    B, D = c.shape
    N = w_ada.shape[1]
    tn = 1024
    return pl.pallas_call(
        _ada_kernel,
        out_shape=jax.ShapeDtypeStruct((B, N), F32),
        grid=(N // tn,),
        in_specs=[pl.BlockSpec((B, D), lambda j: (0, 0)),
                  pl.BlockSpec((D, tn), lambda j: (0, j)),
                  pl.BlockSpec((1, tn), lambda j: (0, j))],
        out_specs=pl.BlockSpec((B, tn), lambda j: (0, j)),
        compiler_params=_cparams(("arbitrary",)),
        name="ada_mod",
    )(c, w_ada, b_ada.reshape(1, N))


def _rope_tables(seq, head_dim, heads_per_vreg):
    rot = head_dim // ROPE_FRACTION_DIV
    half = rot // 2
    inv = jnp.float32(ROPE_THETA) ** (-(jnp.arange(0, rot, 2, dtype=F32) / rot))
    ang = jnp.arange(seq, dtype=F32)[:, None] * inv[None, :]
    cos, sin = jnp.cos(ang), jnp.sin(ang)
    ones = jnp.ones((seq, head_dim - rot), F32)
    zeros_h = jnp.zeros((seq, half), F32)
    zeros_r = jnp.zeros((seq, head_dim - rot), F32)
    c = jnp.concatenate([cos, cos, ones], axis=1)
    sp = jnp.concatenate([zeros_h, sin, zeros_r], axis=1)
    sm = jnp.concatenate([-sin, zeros_h, zeros_r], axis=1)
    rep = lambda t: jnp.tile(t, (1, heads_per_vreg))
    return jnp.stack([rep(c), rep(sp), rep(sm)], axis=0), half


def _rope(x, tab_ref, half):
    return (x * tab_ref[0] + pltpu.roll(x, half, 1) * tab_ref[1]
            + pltpu.roll(x, LANES - half, 1) * tab_ref[2])


_C_QM, _C_KM, _C_VM, _C_QD = 0, MOBA_W, 2 * MOBA_W, 3 * MOBA_W
_C_CKV = 3 * MOBA_W + DSA_W
_C_QI = _C_CKV + KV_LORA
_C_KI = _C_QI + IDX_HEADS * IDX_DIM
_C_END = _C_KI + LANES


def _in_proj_kernel(x_ref, mod_ref, gmix_ref, w_ref, gkv_ref, wkv_ref, tabh_ref, tabi_ref,
                    qmt_ref, km_ref, vmt_ref, kmean_ref, qdt_ref, kd_ref, vdt_ref, qit_ref, ki_ref, wit_ref,
                    *, half_h, half_i):
    tm = x_ref.shape[0]
    x = x_ref[...]
    sh1 = mod_ref[0, 0:1, :]
    sc1 = mod_ref[0, 1:2, :]
    h = (_rms(x, gmix_ref[...]) * (1.0 + sc1) + sh1).astype(BF16)

    def proj(c0, width):
        return _dot(h, w_ref[:, c0:c0 + width])

    q_scale = HEAD_DIM ** -0.5 * LOG2E
    nblk = tm // MOBA_BLOCK
    qm = proj(_C_QM, MOBA_W)
    km = proj(_C_KM, MOBA_W)
    vm = proj(_C_VM, MOBA_W)
    for hd in range(MOBA_HEADS):
        sl = slice(hd * HEAD_DIM, (hd + 1) * HEAD_DIM)
        qmt_ref[sl, :] = (_rope(qm[:, sl], tabh_ref, half_h) * q_scale).T.astype(BF16)
        kr = _rope(km[:, sl], tabh_ref, half_h)
        km_ref[:, sl] = kr.astype(BF16)
        for blk in range(nblk):
            rows = slice(blk * MOBA_BLOCK, (blk + 1) * MOBA_BLOCK)
            kmean_ref[blk:blk + 1, sl] = jnp.mean(kr[rows], axis=0, keepdims=True)
            vmt_ref[blk, sl, :] = vm[rows, sl].T.astype(BF16)
    qd = proj(_C_QD, DSA_W)
    for hd in range(DSA_HEADS):
        sl = slice(hd * HEAD_DIM, (hd + 1) * HEAD_DIM)
        qdt_ref[sl, :] = (_rope(qd[:, sl], tabh_ref, half_h) * q_scale).T.astype(BF16)
    ckv = proj(_C_CKV, KV_LORA)
    kv = _dot(_rms(ckv, gkv_ref[...]).astype(BF16), wkv_ref[...])
    kd_ref[...] = _rope(kv[:, :HEAD_DIM], tabh_ref, half_h).astype(BF16)
    for ch in range(tm // DSA_KA):
        vdt_ref[ch] = kv[ch * DSA_KA:(ch + 1) * DSA_KA, HEAD_DIM:].T.astype(BF16)
    qi = proj(_C_QI, IDX_HEADS * IDX_DIM)
    for j in range(IDX_HEADS * IDX_DIM // LANES):
        sl = slice(j * LANES, (j + 1) * LANES)
        qit_ref[sl, :] = _rope(qi[:, sl], tabi_ref, half_i).T.astype(BF16)
    kw = proj(_C_KI, LANES)
    ki_ref[...] = _rope(kw, tabi_ref, half_i)[:, :IDX_DIM].astype(BF16)
    wit_ref[...] = kw.T[IDX_DIM:IDX_DIM + IDX_HEADS, :] * (IDX_HEADS ** -0.5 * IDX_DIM ** -0.5)


def _in_proj(x2, mod3, g_mix, w_in_p, g_kv, w_kv_up, tab_h, half_h, tab_i, half_i, S):
    T, D = x2.shape
    tm = PROJ_TM
    nt_per_seq = S // tm
    row = lambda i: (i, 0)
    col = lambda i: (0, i)
    nb = tm // MOBA_BLOCK
    nc = tm // DSA_KA
    outs = [
        jax.ShapeDtypeStruct((MOBA_W, T), BF16),
        jax.ShapeDtypeStruct((T, MOBA_W), BF16),
        jax.ShapeDtypeStruct((T // MOBA_BLOCK, MOBA_W, MOBA_BLOCK), BF16),
        jax.ShapeDtypeStruct((T // tm, nb, MOBA_W), F32),
        jax.ShapeDtypeStruct((DSA_W, T), BF16),
        jax.ShapeDtypeStruct((T, HEAD_DIM), BF16),
        jax.ShapeDtypeStruct((T // DSA_KA, HEAD_DIM, DSA_KA), BF16),
        jax.ShapeDtypeStruct((IDX_HEADS * IDX_DIM, T), BF16),
        jax.ShapeDtypeStruct((T, IDX_DIM), BF16),
        jax.ShapeDtypeStruct((IDX_HEADS, T), F32),
    ]
    out_specs = [
        pl.BlockSpec((MOBA_W, tm), col), pl.BlockSpec((tm, MOBA_W), row),
        pl.BlockSpec((nb, MOBA_W, MOBA_BLOCK), lambda i: (i, 0, 0)),
        pl.BlockSpec((None, nb, MOBA_W), lambda i: (i, 0, 0)),
        pl.BlockSpec((DSA_W, tm), col), pl.BlockSpec((tm, HEAD_DIM), row),
        pl.BlockSpec((nc, HEAD_DIM, DSA_KA), lambda i: (i, 0, 0)),
        pl.BlockSpec((IDX_HEADS * IDX_DIM, tm), col), pl.BlockSpec((tm, IDX_DIM), row),
        pl.BlockSpec((IDX_HEADS, tm), col),
    ]
    res = pl.pallas_call(
        functools.partial(_in_proj_kernel, half_h=half_h, half_i=half_i),
        out_shape=outs,
        grid=(T // tm,),
        in_specs=[
            pl.BlockSpec((tm, D), row),
            pl.BlockSpec((1, 6, D), lambda i: (i // nt_per_seq, 0, 0)),
            pl.BlockSpec((1, D), lambda i: (0, 0)),
            pl.BlockSpec((D, _C_END), lambda i: (0, 0)),
            pl.BlockSpec((1, KV_LORA), lambda i: (0, 0)),
            pl.BlockSpec((KV_LORA, 2 * HEAD_DIM), lambda i: (0, 0)),
            pl.BlockSpec((3, tm, LANES), lambda i: (0, i % nt_per_seq, 0)),
            pl.BlockSpec((3, tm, LANES), lambda i: (0, i % nt_per_seq, 0)),
        ],
        out_specs=out_specs,
        compiler_params=_cparams(("parallel",)),
        name="in_proj",
    )(x2, mod3, g_mix, w_in_p, g_kv, w_kv_up, tab_h, tab_i)
    res = list(res)
    res[3] = res[3].reshape(T // MOBA_BLOCK, MOBA_W)
    return res


def _moba_kernel(qt_ref, k_ref, vt_ref, kmean_ref, o_ref, bias_sc, *accs):
    qi = pl.program_id(1)
    blk = MOBA_BLOCK
    nb = kmean_ref.shape[0]
    heads = range(MOBA_HEADS)
    hsl = [slice(hd * HEAD_DIM, (hd + 1) * HEAD_DIM) for hd in heads]
    qts = [qt_ref[hsl[hd], :] for hd in heads]
    row = lax.broadcasted_iota(I32, (nb, blk), 0)
    past = row < qi
    start = pl.multiple_of(qi * blk, blk)
    k_io = lax.broadcasted_iota(I32, (blk, blk), 0)
    q_io = lax.broadcasted_iota(I32, (blk, blk), 1)

    init = []
    for hd in heads:
        km = kmean_ref[:, hsl[hd]]
        km_hi = km.astype(BF16)
        km_lo = (km - km_hi.astype(F32)).astype(BF16)
        gate = _dot(km_hi, qts[hd]) + _dot(km_lo, qts[hd])
        gate = jnp.where(past, gate, -jnp.inf)
        bias = jnp.full((nb, blk), NEG, F32)
        for _ in range(MOBA_TOPK):
            _, idx = _first_index_of_max(gate, row, nb)
            hit = row == idx
            bias = jnp.where(hit, 0.0, bias)
            gate = jnp.where(hit, -jnp.inf, gate)
        bias_sc[hd] = jnp.where(past, bias, NEG)

        s = _dot(k_ref[pl.ds(start, blk), hsl[hd]], qts[hd])
        s = jnp.where(k_io <= q_io, s, NEG)
        m0 = _col_max(s)
        p = jnp.exp2(s - m0)
        accs[hd][...] = _dot(vt_ref[qi, hsl[hd], :], p.astype(BF16))
        init += [m0, _col_sum(p)]

    def body(n, carry):
        st = pl.multiple_of(n * blk, blk)
        kb = k_ref[pl.ds(st, blk), :]
        sbs = [_dot(kb[:, hsl[hd]], qts[hd]) + bias_sc[hd, pl.ds(n, 1), :] for hd in heads]
        old = [accs[hd][...] for hd in heads]
        out, new = [], []
        for hd in heads:
            m_old, l_old = carry[2 * hd], carry[2 * hd + 1]
            m_new = jnp.maximum(m_old, _col_max(sbs[hd]))
            alpha = jnp.exp2(m_old - m_new)
            pb = jnp.exp2(sbs[hd] - m_new)
            new.append(alpha * old[hd] + _dot(vt_ref[n, hsl[hd], :], pb.astype(BF16)))
            out += [m_new, alpha * l_old + _col_sum(pb)]
        for hd in heads:
            accs[hd][...] = new[hd]
        return tuple(out)

    fin = lax.fori_loop(0, qi, body, tuple(init))
    for hd in heads:
        o_ref[:, hsl[hd]] = (accs[hd][...] * (1.0 / fin[2 * hd + 1])).T


def _moba_attn(q_mt, k_m, v_mt, kmean, B, S):
    T = B * S
    blk = MOBA_BLOCK
    nq = S // blk
    return pl.pallas_call(
        _moba_kernel,
        out_shape=jax.ShapeDtypeStruct((T, MOBA_W), F32),
        grid=(B, nq),
        in_specs=[
            pl.BlockSpec((MOBA_W, blk), lambda b, i: (0, b * nq + i)),
            pl.BlockSpec((S, MOBA_W), lambda b, i: (b, 0)),
            pl.BlockSpec((nq, MOBA_W, blk), lambda b, i: (b, 0, 0)),
            pl.BlockSpec((nq, MOBA_W), lambda b, i: (b, 0)),
        ],
        out_specs=pl.BlockSpec((blk, MOBA_W), lambda b, i: (b * nq + i, 0)),
        scratch_shapes=[pltpu.VMEM((MOBA_HEADS, nq, blk), F32)]
        + [pltpu.VMEM((HEAD_DIM, blk), F32) for _ in range(MOBA_HEADS)],
        compiler_params=_cparams(("parallel", "arbitrary")),
        name="moba_attn",
    )(q_mt, k_m, v_mt, kmean)


def _sortable_key(x):
    b = pltpu.bitcast(x, I32)
    return jnp.where(b >= 0, b, b ^ jnp.int32(0x7FFFFFFF))


def _dsa_kernel(qit_ref, wit_ref, qdt_ref, ki_ref, kd_ref, vdt_ref, o_ref, key_sc, high_sc, *accs, topk):
    t = pl.program_id(1)
    tq, kc = DSA_TQ, DSA_KC
    S = key_sc.shape[0]
    q0 = t * tq
    n_chunks = (q0 + tq + kc - 1) // kc
    key_io = lax.broadcasted_iota(I32, (kc, tq), 0)
    q_pos = q0 + lax.broadcasted_iota(I32, (kc, tq), 1)
    w = wit_ref[...]

    def score_chunk(c, carry):
        k0 = pl.multiple_of(c * kc, kc)
        kic = ki_ref[pl.ds(k0, kc), :]
        acc = jnp.zeros((kc, tq), F32)
        for hd in range(IDX_HEADS):
            lg = _dot(kic, qit_ref[hd * IDX_DIM:(hd + 1) * IDX_DIM, :])
            acc = acc + jnp.maximum(lg, 0.0) * w[hd:hd + 1, :]
        acc = jnp.where(acc == 0.0, 0.0, acc)
        key = _sortable_key(acc)
        key = jnp.where(k0 + key_io <= q_pos, key, INT_MIN)
        key_sc[pl.ds(k0, kc), :] = key
        high_sc[pl.ds(k0, kc), :] = lax.shift_right_arithmetic(key, 16).astype(jnp.int16)
        return carry

    lax.fori_loop(0, n_chunks, score_chunk, 0)

    ks = DSA_KS
    ks_io = lax.broadcasted_iota(I32, (ks, tq), 0)

    def column_total(slab_fn, dtype):
        rows = SUBLANES * (4 // jnp.dtype(dtype).itemsize)

        def cbody(c, accs):
            vals = slab_fn(pl.multiple_of(c * ks, ks))
            accs = list(accs)
            for i in range(ks // rows):
                j = i % REDUCE_CHAINS
                accs[j] = accs[j] + vals[i * rows:(i + 1) * rows]
            return tuple(accs)
        zero = jnp.zeros((rows, tq), dtype)
        accs = lax.fori_loop(0, n_chunks * (kc // ks), cbody, (zero,) * REDUCE_CHAINS)
        total = accs[0].astype(I32)
        for a in accs[1:]:
            total = total + a.astype(I32)
        return jnp.sum(total, axis=0, keepdims=True)

    def count(pred_fn):
        return column_total(lambda k0: jnp.where(pred_fn(key_sc[pl.ds(k0, ks), :], k0), 1, 0), I32)

    def count_ge_high(cand):
        c16 = lax.shift_right_arithmetic(cand, 16).astype(jnp.int16)
        one, zero = jnp.int16(1), jnp.int16(0)
        return column_total(lambda k0: jnp.where(high_sc[pl.ds(k0, ks), :] >= c16, one, zero), jnp.int16)

    thr0 = jnp.where(count_ge_high(jnp.zeros((1, tq), I32)) >= topk, 0, INT_MIN).astype(I32)

    def high_step(i, thr):
        cand = thr | lax.shift_left(jnp.int32(1), 30 - i)
        return jnp.where(count_ge_high(cand) >= topk, cand, thr)

    def low_step(i, thr):
        cand = thr | lax.shift_left(jnp.int32(1), 15 - i)
        return jnp.where(count(lambda kk, k0: kk >= cand) >= topk, cand, thr)

    thr = lax.fori_loop(0, 15, high_step, thr0)
    thr = lax.fori_loop(0, 16, low_step, thr)

    n_gt = count(lambda kk, k0: kk > thr)
    n_ge = count(lambda kk, k0: kk >= thr)
    need = topk - n_gt
    overflow = (n_ge > topk) & (thr != INT_MIN)
    any_overflow = jnp.max(jnp.where(overflow, 1, 0)) > 0
    nbits = max(1, (S - 1).bit_length())

    def cut_search():
        def step(i, lo):
            cand = lo | lax.shift_left(jnp.int32(1), nbits - 1 - i)
            cnt = count(lambda kk, k0: (kk == thr) & (k0 + ks_io < cand))
            return jnp.where(cnt >= need, lo, cand)
        return lax.fori_loop(0, nbits, step, jnp.zeros((1, tq), I32))

    jcut = lax.cond(any_overflow, cut_search, lambda: jnp.zeros((1, tq), I32))
    jcut = jnp.where(overflow, jcut, S)
    thr_ge = jnp.where(thr == INT_MIN, INT_MIN + 1, thr)

    pairs = range(DSA_HEADS // 2)
    qst = [jnp.concatenate([qdt_ref[(2 * g + j) * HEAD_DIM:(2 * g + j + 1) * HEAD_DIM, :] for j in range(2)],
                           axis=1) for g in pairs]
    for g in pairs:
        accs[g][...] = jnp.zeros(accs[g].shape, F32)

    ka = DSA_KA
    pos_io = lax.broadcasted_iota(I32, (ka, tq), 0)

    def attn_chunk(c, carry):
        k0 = pl.multiple_of(c * ka, ka)
        kk = key_sc[pl.ds(k0, ka), :]
        bias = lax.cond(
            any_overflow,
            lambda: jnp.where((kk > thr_ge) | ((kk == thr_ge) & (k0 + pos_io <= jcut)), 0.0, NEG),
            lambda: jnp.where(kk >= thr_ge, 0.0, NEG))
        bias2 = jnp.concatenate([bias, bias], axis=1)
        kdc = kd_ref[pl.ds(k0, ka), :]
        vtc = vdt_ref[c]
        ss = [_dot(kdc, qst[g]) + bias2 for g in pairs]
        old = [accs[g][...] for g in pairs]
        out, new = [], []
        for g in pairs:
            m_old, l_old = carry[2 * g], carry[2 * g + 1]
            m_new = jnp.maximum(m_old, _col_max(ss[g]))
            alpha = jnp.exp2(m_old - m_new)
            p = jnp.exp2(ss[g] - m_new)
            new.append(alpha * old[g] + _dot(vtc, p.astype(BF16)))
            out += [m_new, alpha * l_old + _col_sum(p)]
        for g in pairs:
            accs[g][...] = new[g]
        return tuple(out)

    init = (jnp.full((1, 2 * tq), NEG, F32), jnp.zeros((1, 2 * tq), F32)) * len(pairs)
    fin = lax.fori_loop(0, (q0 + tq + ka - 1) // ka, attn_chunk, init)
    for g in pairs:
        out_t = accs[g][...] * (1.0 / fin[2 * g + 1])
        for j in range(2):
            hd = 2 * g + j
            o_ref[:, hd * HEAD_DIM:(hd + 1) * HEAD_DIM] = out_t[:, j * tq:(j + 1) * tq].T


def _dsa_attn(q_it, w_it, q_dt, k_i, k_d, v_dt, B, S):
    T = B * S
    tq = DSA_TQ
    nq = S // tq
    topk = min(DSA_MAX_TOPK, S // 4)
    qcol = lambda b, t: (0, b * nq + t)
    seq = lambda b, t: (b, 0)
    return pl.pallas_call(
        functools.partial(_dsa_kernel, topk=topk),
        out_shape=jax.ShapeDtypeStruct((T, DSA_W), F32),
        grid=(B, nq),
        in_specs=[
            pl.BlockSpec((IDX_HEADS * IDX_DIM, tq), qcol),
            pl.BlockSpec((IDX_HEADS, tq), qcol),
            pl.BlockSpec((DSA_W, tq), qcol),
            pl.BlockSpec((S, IDX_DIM), seq),
            pl.BlockSpec((S, HEAD_DIM), seq),
            pl.BlockSpec((S // DSA_KA, HEAD_DIM, DSA_KA), lambda b, t: (b, 0, 0)),
        ],
        out_specs=pl.BlockSpec((tq, DSA_W), lambda b, t: (b * nq + t, 0)),
        scratch_shapes=[pltpu.VMEM((S, tq), I32), pltpu.VMEM((S, tq), jnp.int16)]
        + [pltpu.VMEM((HEAD_DIM, 2 * tq), F32) for _ in range(DSA_HEADS // 2)],
        compiler_params=_cparams(("parallel", "arbitrary")),
        name="dsa_attn",
    )(q_it, w_it, q_dt, k_i, k_d, v_dt)


def _first_index_of_max(v, row_io, n_rows):
    m = jnp.max(v, axis=0, keepdims=True)
    idx = jnp.min(jnp.where(v == m, row_io, n_rows), axis=0, keepdims=True)
    return m, idx


def _post_kernel(x_ref, om_ref, od_ref, mod_ref, gm_ref, gd_ref, wout_ref, gffn_ref, wgu_ref, wds_ref,
                 wrh_ref, wrl_ref, rb_ref, tri_ref,
                 xpart_ref, h2r_ref, eidx_ref, rank_ref, gate_ref, cnt_ref, base_sc):
    i = pl.program_id(0)
    tm = x_ref.shape[0]
    gt1 = mod_ref[0, 2:3, :]
    sh2 = mod_ref[0, 3:4, :]
    sc2 = mod_ref[0, 4:5, :]
    gt2 = mod_ref[0, 5:6, :]

    mixed = jnp.concatenate([_rms(om_ref[...], gm_ref[...]), _rms(od_ref[...], gd_ref[...])], axis=1)
    x1 = x_ref[...] + gt1 * _dot(mixed.astype(BF16), wout_ref[...])
    h2 = _rms(x1, gffn_ref[...]) * (1.0 + sc2) + sh2
    h2b = h2.astype(BF16)

    au = _dot(h2b, wgu_ref[...])
    hs = (_silu(au[:, :D_SHARED]) * au[:, D_SHARED:]).astype(BF16)
    xpart_ref[...] = x1 + gt2 * _dot(hs, wds_ref[...])

    for c, slab in enumerate(_pack_row_words(h2)):
        _store_word_slab(h2r_ref, 0, tm, c, slab)

    h2lo = (h2 - h2b.astype(F32)).astype(BF16)
    logits = _dot_nt(wrh_ref[...], h2b) + _dot_nt(wrl_ref[...], h2b) + _dot_nt(wrh_ref[...], h2lo)
    scores = 1.0 / (1.0 + jnp.exp(-logits))
    biased = scores + rb_ref[...]

    g_io = lax.broadcasted_iota(I32, (GROUP_SIZE, tm), 0)
    gs_rows = []
    for g in range(N_GROUPS):
        blk = biased[g * GROUP_SIZE:(g + 1) * GROUP_SIZE, :]
        m1, i1 = _first_index_of_max(blk, g_io, GROUP_SIZE)
        m2 = jnp.max(jnp.where(g_io == i1, -jnp.inf, blk), axis=0, keepdims=True)
        gs_rows.append(m1 + m2)
    gs = jnp.concatenate(gs_rows, axis=0)
    gi = lax.broadcasted_iota(I32, (N_GROUPS, tm), 0)
    grank = jnp.zeros((N_GROUPS, tm), I32)
    for m in range(N_GROUPS):
        gm = gs[m:m + 1, :]
        grank = grank + jnp.where((gm > gs) | ((gm == gs) & (m < gi)), 1, 0)
    gsel = grank < TOPK_GROUPS
    masked = jnp.concatenate(
        [jnp.where(gsel[g:g + 1, :], biased[g * GROUP_SIZE:(g + 1) * GROUP_SIZE, :], -jnp.inf)
         for g in range(N_GROUPS)], axis=0)

    e_io = lax.broadcasted_iota(I32, (N_EXPERTS, tm), 0)
    e_rows, s_rows = [], []
    for _ in range(EXPERT_TOPK):
        _, idx = _first_index_of_max(masked, e_io, N_EXPERTS)
        hit = e_io == idx
        e_rows.append(idx)
        s_rows.append(jnp.sum(jnp.where(hit, scores, 0.0), axis=0, keepdims=True))
        masked = jnp.where(hit, -jnp.inf, masked)
    eidx = jnp.concatenate(e_rows, axis=0)
    sk = jnp.concatenate(s_rows, axis=0)
    gate_ref[...] = sk / jnp.sum(sk, axis=0, keepdims=True) * ROUTED_SCALE
    eidx_ref[...] = eidx

    @pl.when(i == 0)
    def _():
        base_sc[...] = jnp.zeros(base_sc.shape, F32)

    chosen = jnp.zeros((N_EXPERTS, tm), F32)
    for k in range(EXPERT_TOPK):
        chosen = chosen + jnp.where(e_io == e_rows[k], 1.0, 0.0)
    incl = _dot(chosen.astype(BF16), tri_ref[...])
    pos = base_sc[...] + incl - 1.0
    rank_ref[...] = jnp.concatenate(
        [jnp.sum(jnp.where(e_io == e_rows[k], pos, 0.0), axis=0, keepdims=True)
         for k in range(EXPERT_TOPK)], axis=0).astype(I32)
    base_sc[...] = base_sc[...] + incl[:, tm - 1:tm]
    cnt_ref[...] = jnp.broadcast_to(base_sc[...], cnt_ref.shape)


def _post_attn(x2, o_m, o_d, mod3, g_moba, g_dsa, w_out, g_ffn, w_gu_s, w_down_s, wr_hi, wr_lo, rbias, S):
    T, D = x2.shape
    tm = POST_TM
    nt_per_seq = S // tm
    row = lambda i: (i, 0)
    full = lambda i: (0, 0)
    tri = (jnp.arange(tm)[:, None] <= jnp.arange(tm)[None, :]).astype(BF16)
    n_words = D // ROW_WORDS
    return pl.pallas_call(
        _post_kernel,
        out_shape=[
            jax.ShapeDtypeStruct((T, D), F32),
            jax.ShapeDtypeStruct((T, n_words, LANES), I32),
            jax.ShapeDtypeStruct((EXPERT_TOPK, T), I32),
            jax.ShapeDtypeStruct((EXPERT_TOPK, T), I32),
            jax.ShapeDtypeStruct((EXPERT_TOPK, T), F32),
            jax.ShapeDtypeStruct((N_EXPERTS, LANES), F32),
        ],
        grid=(T // tm,),
        in_specs=[
            pl.BlockSpec((tm, D), row),
            pl.BlockSpec((tm, MOBA_W), row),
            pl.BlockSpec((tm, DSA_W), row),
            pl.BlockSpec((1, 6, D), lambda i: (i // nt_per_seq, 0, 0)),
            pl.BlockSpec((1, MOBA_W), full),
            pl.BlockSpec((1, DSA_W), full),
            pl.BlockSpec(w_out.shape, full),
            pl.BlockSpec((1, D), full),
            pl.BlockSpec(w_gu_s.shape, full),
            pl.BlockSpec(w_down_s.shape, full),
            pl.BlockSpec(wr_hi.shape, full),
            pl.BlockSpec(wr_lo.shape, full),
            pl.BlockSpec((N_EXPERTS, 1), full),
            pl.BlockSpec((tm, tm), full),
        ],
        out_specs=[
            pl.BlockSpec((tm, D), row),
            pl.BlockSpec((tm, n_words, LANES), lambda i: (i, 0, 0)),
            pl.BlockSpec((EXPERT_TOPK, tm), lambda i: (0, i)),
            pl.BlockSpec((EXPERT_TOPK, tm), lambda i: (0, i)),
            pl.BlockSpec((EXPERT_TOPK, tm), lambda i: (0, i)),
            pl.BlockSpec((N_EXPERTS, LANES), full),
        ],
        scratch_shapes=[pltpu.VMEM((N_EXPERTS, 1), F32)],
        compiler_params=_cparams(("arbitrary",)),
        name="post_attn",
    )(x2, o_m, o_d, mod3, g_moba, g_dsa, w_out, g_ffn, w_gu_s, w_down_s, wr_hi, wr_lo, rbias, tri)


def _row_copy_wait(rows_hbm, n_rows, sem):
    blk = rows_hbm.at[pl.ds(0, n_rows)]
    pltpu.make_async_copy(blk, blk, sem).wait()


def _slots_kernel(ps_ref, e_ref, r_ref, d_ref):
    e = e_ref[...]

    def body(x, acc):
        return jnp.where(e == x, ps_ref[x], acc)

    d_ref[...] = lax.fori_loop(0, N_EXPERTS, body, jnp.zeros(e.shape, I32)) + r_ref[...]


def _slots(pad_starts, eidx, rank):
    K, T = eidx.shape
    tm = min(T, 4096)
    blk = lambda: pl.BlockSpec((K, tm), lambda i, ps: (0, i))
    return pl.pallas_call(
        _slots_kernel,
        out_shape=jax.ShapeDtypeStruct((K, T), I32),
        grid_spec=pltpu.PrefetchScalarGridSpec(num_scalar_prefetch=1, grid=(T // tm,),
                                               in_specs=[blk(), blk()], out_specs=blk()),
        compiler_params=_cparams(("parallel",)),
        name="slots",
    )(pad_starts, eidx, rank)


def _dispatch_kernel(d_ref, h2r_ref, xs_ref, sem):
    tm = h2r_ref.shape[0]

    def body(t, carry):
        for k in range(EXPERT_TOPK):
            pltpu.make_async_copy(h2r_ref.at[t], xs_ref.at[d_ref[t * EXPERT_TOPK + k]],
                                  sem).start(priority=k % 2)
        return carry

    lax.fori_loop(0, tm, body, 0)
    _row_copy_wait(xs_ref, tm * EXPERT_TOPK, sem)


def _dispatch(dest_flat, h2r, n_rows_padded):
    T, n_words, _ = h2r.shape
    tm = DISP_TM
    return pl.pallas_call(
        _dispatch_kernel,
        out_shape=jax.ShapeDtypeStruct((n_rows_padded, n_words, LANES), I32),
        grid=(T // tm,),
        in_specs=[pl.BlockSpec((tm * EXPERT_TOPK,), lambda i: (i,), memory_space=pltpu.SMEM),
                  pl.BlockSpec((tm, n_words, LANES), lambda i: (i, 0, 0))],
        out_specs=pl.BlockSpec(memory_space=pl.ANY),
        scratch_shapes=[pltpu.SemaphoreType.DMA],
        compiler_params=_cparams(("arbitrary",), disable_bounds_checks=True),
        name="dispatch",
    )(dest_flat, h2r)


def _expert_kernel(ps_ref, cnt_ref, wg_ref, wu_ref, wd_ref, xs_ref, ys_ref,
                   wg_sc, wu_sc, wd_sc, xbuf, ybuf, sem_in, sem_out):
    e = pl.program_id(0)
    bm = EXP_BM
    n_words = xbuf.shape[1]
    cnt = cnt_ref[e]
    g0 = ps_ref[e] // bm
    nb = (cnt + bm - 1) // bm
    n_used = (ps_ref[N_EXPERTS - 1] + cnt_ref[N_EXPERTS - 1] + bm - 1) // bm

    def in_copy(g):
        slot = lax.rem(g, EXP_IN_SLOTS)
        return pltpu.make_async_copy(xs_ref.at[pl.ds(g * bm, bm)], xbuf.at[pl.ds(slot * bm, bm)],
                                     sem_in.at[slot])

    def out_copy(g):
        slot = lax.rem(g, EXP_OUT_SLOTS)
        return pltpu.make_async_copy(ybuf.at[pl.ds(slot * bm, bm)], ys_ref.at[pl.ds(g * bm, bm)],
                                     sem_out.at[slot])

    @pl.when(e == 0)
    def _():
        for g in range(EXP_AHEAD):
            @pl.when(g < n_used)
            def _():
                in_copy(g).start()

    @pl.when(nb > 0)
    def _():
        wg_sc[...] = wg_ref[...].astype(BF16)
        wu_sc[...] = wu_ref[...].astype(BF16)
        wd_sc[...] = wd_ref[...].astype(BF16)

        def body(b, carry):
            g = g0 + b

            @pl.when(g + EXP_AHEAD < n_used)
            def _():
                in_copy(g + EXP_AHEAD).start()

            in_copy(g).wait()

            @pl.when(g >= EXP_OUT_SLOTS)
            def _():
                out_copy(g - EXP_OUT_SLOTS).wait()

            xrow = lax.rem(g, EXP_IN_SLOTS) * bm
            yrow = lax.rem(g, EXP_OUT_SLOTS) * bm
            feats = []
            for c in range(n_words):
                feats += _unpack_row_words(_load_word_slab(xbuf, xrow, bm, c))
            x = jnp.concatenate(feats, axis=1)
            valid = lax.broadcasted_iota(I32, (bm, 1), 0) < cnt - b * bm
            xb = jnp.where(valid, x, 0.0).astype(BF16)
            a = _dot(xb, wg_sc[...])
            u = _dot(xb, wu_sc[...])
            hmid = (_silu(a) * u).astype(BF16)
            ob = _dot(hmid, wd_sc[...])
            for c, slab in enumerate(_pack_row_words(ob)):
                _store_word_slab(ybuf, yrow, bm, c, slab)
            out_copy(g).start()
            return carry

        lax.fori_loop(0, nb, body, 0)

    @pl.when(e == N_EXPERTS - 1)
    def _():
        for back in range(EXP_OUT_SLOTS, 0, -1):
            @pl.when(n_used >= back)
            def _():
                out_copy(n_used - back).wait()


def _experts(pad_starts, counts, xs, w_gate_e, w_up_e, w_down_e):
    E, D, DE = w_gate_e.shape
    n_words = xs.shape[1]
    bm = EXP_BM
    wsel = lambda e, ps, cnt: (e, 0, 0)
    anyspec = pl.BlockSpec(memory_space=pl.ANY)
    return pl.pallas_call(
        _expert_kernel,
        out_shape=jax.ShapeDtypeStruct(xs.shape, I32),
        grid_spec=pltpu.PrefetchScalarGridSpec(
            num_scalar_prefetch=2,
            grid=(E,),
            in_specs=[pl.BlockSpec((None, D, DE), wsel), pl.BlockSpec((None, D, DE), wsel),
                      pl.BlockSpec((None, DE, D), wsel), anyspec],
            out_specs=anyspec,
            scratch_shapes=[pltpu.VMEM((D, DE), BF16), pltpu.VMEM((D, DE), BF16), pltpu.VMEM((DE, D), BF16),
                            pltpu.VMEM((EXP_IN_SLOTS * bm, n_words, LANES), I32),
                            pltpu.VMEM((EXP_OUT_SLOTS * bm, n_words, LANES), I32),
                            pltpu.SemaphoreType.DMA((EXP_IN_SLOTS,)), pltpu.SemaphoreType.DMA((EXP_OUT_SLOTS,))],
        ),
        compiler_params=_cparams(("arbitrary",)),
        name="experts",
    )(pad_starts, counts, w_gate_e, w_up_e, w_down_e, xs)


def _combine_kernel(dcur_ref, dnxt_ref, g_ref, xpart_ref, mod_ref, gfin_ref, ys_ref, o_ref,
                    buf0, buf1, sem0, sem1):
    i = pl.program_id(0)
    n_steps = pl.num_programs(0)
    tm = xpart_ref.shape[0]
    n_words = buf0.shape[1]

    def issue(dest_ref, buf, sem):
        def body(t, carry):
            for k in range(EXPERT_TOPK):
                pltpu.make_async_copy(ys_ref.at[dest_ref[t * EXPERT_TOPK + k]], buf.at[k * tm + t],
                                      sem).start(priority=k % 2)
            return carry
        lax.fori_loop(0, tm, body, 0)

    def reduce_tile(buf, sem):
        _row_copy_wait(ys_ref, tm * EXPERT_TOPK, sem)
        gt2 = mod_ref[0, 5:6, :]
        g = g_ref[...]
        cols = []
        for c in range(n_words):
            lo = jnp.zeros((tm, LANES), F32)
            hi = jnp.zeros((tm, LANES), F32)
            for k in range(EXPERT_TOPK):
                a, b = _unpack_row_words(_load_word_slab(buf, k * tm, tm, c))
                gk = g[:, k:k + 1]
                lo = lo + gk * a
                hi = hi + gk * b
            cols += [lo, hi]
        routed = jnp.concatenate(cols, axis=1)
        o_ref[...] = _rms(xpart_ref[...] + gt2 * routed, gfin_ref[...])

    @pl.when(i == 0)
    def _():
        issue(dcur_ref, buf0, sem0)

    for parity, (cur, nxt) in enumerate((((buf0, sem0), (buf1, sem1)), ((buf1, sem1), (buf0, sem0)))):
        @pl.when(i % 2 == parity)
        def _():
            @pl.when(i + 1 < n_steps)
            def _():
                issue(dnxt_ref, *nxt)
            reduce_tile(*cur)


def _combine(dest, gates_t, xpart, mod3, g_final, ys, S):
    T, D = xpart.shape
    tm = COMB_TM
    n_steps = T // tm
    nt_per_seq = S // tm
    n_words = ys.shape[1]
    return pl.pallas_call(
        _combine_kernel,
        out_shape=jax.ShapeDtypeStruct((T, D), F32),
        grid=(n_steps,),
        in_specs=[
            pl.BlockSpec((tm * EXPERT_TOPK,), lambda i: (i,), memory_space=pltpu.SMEM),
            pl.BlockSpec((tm * EXPERT_TOPK,), lambda i: (jnp.minimum(i + 1, n_steps - 1),),
                         memory_space=pltpu.SMEM),
            pl.BlockSpec((tm, EXPERT_TOPK), lambda i: (i, 0)),
            pl.BlockSpec((tm, D), lambda i: (i, 0)),
            pl.BlockSpec((1, 6, D), lambda i: (i // nt_per_seq, 0, 0)),
            pl.BlockSpec((1, D), lambda i: (0, 0)),
            pl.BlockSpec(memory_space=pl.ANY),
        ],
        out_specs=pl.BlockSpec((tm, D), lambda i: (i, 0)),
        scratch_shapes=[pltpu.VMEM((EXPERT_TOPK * tm, n_words, LANES), I32),
                        pltpu.VMEM((EXPERT_TOPK * tm, n_words, LANES), I32),
                        pltpu.SemaphoreType.DMA, pltpu.SemaphoreType.DMA],
        compiler_params=_cparams(("arbitrary",), disable_bounds_checks=True),
        name="combine",
    )(dest, dest, gates_t, xpart, mod3, g_final, ys)


def _layer(x2, mod3, S, g_mix, w_in, g_kv, w_kv_up, g_moba_out, g_dsa_out, w_out, g_ffn, w_router,
           router_bias, w_gate_e, w_up_e, w_down_e, w_gate_s, w_up_s, w_down_s, g_final, tab_h, half_h,
           tab_i, half_i):
    T, D = x2.shape
    B = T // S
    w_in_p = jnp.pad(w_in, ((0, 0), (0, _C_END - w_in.shape[1]))).astype(BF16)
    (q_mt, k_m, v_mt, kmean, q_dt, k_d, v_dt, q_it, k_i, w_it) = _in_proj(
        x2, mod3, g_mix.reshape(1, D), w_in_p, g_kv.reshape(1, KV_LORA), w_kv_up.astype(BF16),
        tab_h, half_h, tab_i, half_i, S)
    o_m = _moba_attn(q_mt, k_m, v_mt, kmean, B, S)
    o_d = _dsa_attn(q_it, w_it, q_dt, k_i, k_d, v_dt, B, S)

    wr_t = w_router.T
    wr_hi = wr_t.astype(BF16)
    wr_lo = (wr_t - wr_hi.astype(F32)).astype(BF16)
    w_gu_s = jnp.concatenate([w_gate_s, w_up_s], axis=1).astype(BF16)
    xpart, h2r, eidx, rank, gates, cnt = _post_attn(
        x2, o_m, o_d, mod3, g_moba_out.reshape(1, MOBA_W), g_dsa_out.reshape(1, DSA_W), w_out.astype(BF16),
        g_ffn.reshape(1, D), w_gu_s, w_down_s.astype(BF16), wr_hi, wr_lo,
        router_bias.reshape(N_EXPERTS, 1), S)

    bm = EXP_BM
    n_blocks = T * EXPERT_TOPK // bm + N_EXPERTS
    counts = cnt[:, 0].astype(I32)
    padded = (counts + bm - 1) // bm * bm
    pad_ends = jnp.cumsum(padded)
    pad_starts = (pad_ends - padded).astype(I32)
    dest_flat = _slots(pad_starts, eidx, rank).T.reshape(-1)
    xs = _dispatch(dest_flat, h2r, n_blocks * bm)
    ys = _experts(pad_starts, counts, xs, w_gate_e, w_up_e, w_down_e)
    return _combine(dest_flat, gates.T, xpart, mod3, g_final.reshape(1, D), ys, S)


def kernel(x, c, w_ada, b_ada, g_mix, w_in, g_kv, w_kv_up, g_moba_out, g_dsa_out, w_out, g_ffn, w_router,
           router_bias, w_gate_e, w_up_e, w_down_e, w_gate_s, w_up_s, w_down_s, g_final):
    B, S, D = x.shape
    depth = w_ada.shape[0]
    assert depth == 1, "the final norm is fused into the single layer"
    assert S % PROJ_TM == 0 and S % DSA_KC == 0 and S % POST_TM == 0 and S >= 4 * DSA_MAX_TOPK
    tab_h, half_h = _rope_tables(S, HEAD_DIM, 1)
    tab_i, half_i = _rope_tables(S, IDX_DIM, LANES // IDX_DIM)
    x2 = x.reshape(B * S, D)
    sq = lambda a: a.reshape(a.shape[1:])
    mod3 = _ada_mod(c, sq(w_ada), sq(b_ada)).reshape(B, 6, D)
    out = _layer(x2, mod3, S, sq(g_mix), sq(w_in), sq(g_kv), sq(w_kv_up), sq(g_moba_out), sq(g_dsa_out),
                 sq(w_out), sq(g_ffn), sq(w_router), sq(router_bias), sq(w_gate_e), sq(w_up_e), sq(w_down_e),
                 sq(w_gate_s), sq(w_up_s), sq(w_down_s), g_final, tab_h, half_h, tab_i, half_i)
    return out.reshape(B, S, D)
```

```python
import functools

import jax
import jax.numpy as jnp
from jax import lax
from jax.experimental import pallas as pl
from jax.experimental.pallas import tpu as pltpu

HEAD_DIM = 128
MOBA_HEADS = 4
DSA_HEADS = 4
MOBA_W = MOBA_HEADS * HEAD_DIM
DSA_W = DSA_HEADS * HEAD_DIM
MOBA_BLOCK = 256
MOBA_TOPK = 3
DSA_MAX_TOPK = 256
KV_LORA = 256
IDX_HEADS = 8
IDX_DIM = 64
ROPE_THETA = 500000.0
ROPE_FRACTION_DIV = 4
N_EXPERTS = 256
EXPERT_TOPK = 8
N_GROUPS = 8
TOPK_GROUPS = 4
GROUP_SIZE = N_EXPERTS // N_GROUPS
D_EXPERT = 256
D_SHARED = 256
ROUTED_SCALE = 2.5
EPS = 1e-6

LANES = 128
SUBLANES = 8
VMEM_LIMIT = 56 * 1024 * 1024

PROJ_TM = 512
DSA_TQ = 256
DSA_KC = 512
DSA_KA = 256
DSA_KS = 128
POST_TM = 256
DISP_TM = 256
EXP_BM = 256
EXP_AHEAD = 3
EXP_IN_SLOTS = EXP_AHEAD + 1
EXP_OUT_SLOTS = 2
COMB_TM = 128
REDUCE_CHAINS = 4
NEG = -1e30
INT_MIN = -2147483648
LOG2E = 1.4426950408889634

F32 = jnp.float32
BF16 = jnp.bfloat16
I32 = jnp.int32


def _cparams(sem, **kw):
    return pltpu.CompilerParams(dimension_semantics=sem, vmem_limit_bytes=VMEM_LIMIT, **kw)


def _dot(a, b):
    return jnp.dot(a, b, preferred_element_type=F32)


def _dot_nt(a, b):
    return lax.dot_general(a, b, (((1,), (1,)), ((), ())), preferred_element_type=F32)


def _silu(x):
    return x * (1.0 / (1.0 + jnp.exp(-x)))


def _rms(x, g):
    return x * lax.rsqrt(jnp.mean(x * x, axis=-1, keepdims=True) + EPS) * g


def _rows_to_tile(op, x):
    parts = [x[i:i + SUBLANES] for i in range(0, x.shape[0], SUBLANES)]
    n_chains = min(REDUCE_CHAINS, len(parts))
    accs = parts[:n_chains]
    for i in range(n_chains, len(parts)):
        accs[i % n_chains] = op(accs[i % n_chains], parts[i])
    while len(accs) > 1:
        accs = [op(accs[i], accs[i + 1]) for i in range(0, len(accs) - 1, 2)] + ([accs[-1]] if len(accs) % 2 else [])
    return accs[0]


def _col_max(x):
    return jnp.max(_rows_to_tile(jnp.maximum, x), axis=0, keepdims=True)


def _col_sum(x):
    return jnp.sum(_rows_to_tile(jnp.add, x), axis=0, keepdims=True)


ROW_WORDS = 2 * LANES


def _pack_row_words(x):
    slabs = []
    for c in range(x.shape[1] // ROW_WORDS):
        lo = pltpu.bitcast(x[:, c * ROW_WORDS:c * ROW_WORDS + LANES].astype(BF16).astype(F32), I32)
        hi = pltpu.bitcast(x[:, c * ROW_WORDS + LANES:(c + 1) * ROW_WORDS].astype(BF16).astype(F32), I32)
        slabs.append(lax.shift_right_logical(lo, 16) | hi)
    return slabs


def _word_slab_index(rows_ref, row0, n_rows, c):
    n_words = rows_ref.shape[1]
    flat = rows_ref.reshape(rows_ref.shape[0] * n_words, LANES)
    return flat, pl.ds(row0 * n_words + c, n_rows, stride=n_words)


def _load_word_slab(rows_ref, row0, n_rows, c):
    flat, idx = _word_slab_index(rows_ref, row0, n_rows, c)
    return flat[idx, :]


def _store_word_slab(rows_ref, row0, n_rows, c, value):
    flat, idx = _word_slab_index(rows_ref, row0, n_rows, c)
    flat[idx, :] = value


def _unpack_row_words(u):
    return pltpu.bitcast(lax.shift_left(u, 16), F32), pltpu.bitcast(u & jnp.int32(-65536), F32)


def _ada_kernel(c_ref, w_ref, b_ref, o_ref):
    ca = _silu(c_ref[...])
    o_ref[...] = jnp.dot(ca, w_ref[...], preferred_element_type=F32,
                         precision=lax.Precision.HIGHEST) + b_ref[...]


def _ada_mod(c, w_ada, b_ada):
    B, D = c.shape
    N = w_ada.shape[1]
    tn = 1024
    return pl.pallas_call(
        _ada_kernel,
        out_shape=jax.ShapeDtypeStruct((B, N), F32),
        grid=(N // tn,),
        in_specs=[pl.BlockSpec((B, D), lambda j: (0, 0)),
                  pl.BlockSpec((D, tn), lambda j: (0, j)),
                  pl.BlockSpec((1, tn), lambda j: (0, j))],
        out_specs=pl.BlockSpec((B, tn), lambda j: (0, j)),
        compiler_params=_cparams(("arbitrary",)),
        name="ada_mod",
    )(c, w_ada, b_ada.reshape(1, N))


def _rope_tables(seq, head_dim, heads_per_vreg):
    rot = head_dim // ROPE_FRACTION_DIV
    half = rot // 2
    inv = jnp.float32(ROPE_THETA) ** (-(jnp.arange(0, rot, 2, dtype=F32) / rot))
    ang = jnp.arange(seq, dtype=F32)[:, None] * inv[None, :]
    cos, sin = jnp.cos(ang), jnp.sin(ang)
    ones = jnp.ones((seq, head_dim - rot), F32)
    zeros_h = jnp.zeros((seq, half), F32)
    zeros_r = jnp.zeros((seq, head_dim - rot), F32)
    c = jnp.concatenate([cos, cos, ones], axis=1)
    sp = jnp.concatenate([zeros_h, sin, zeros_r], axis=1)
    sm = jnp.concatenate([-sin, zeros_h, zeros_r], axis=1)
    rep = lambda t: jnp.tile(t, (1, heads_per_vreg))
    return jnp.stack([rep(c), rep(sp), rep(sm)], axis=0), half


def _rope(x, tab_ref, half):
    return (x * tab_ref[0] + pltpu.roll(x, half, 1) * tab_ref[1]
            + pltpu.roll(x, LANES - half, 1) * tab_ref[2])


_C_QM, _C_KM, _C_VM, _C_QD = 0, MOBA_W, 2 * MOBA_W, 3 * MOBA_W
_C_CKV = 3 * MOBA_W + DSA_W
_C_QI = _C_CKV + KV_LORA
_C_KI = _C_QI + IDX_HEADS * IDX_DIM
_C_END = _C_KI + LANES


def _in_proj_kernel(x_ref, mod_ref, gmix_ref, w_ref, gkv_ref, wkv_ref, tabh_ref, tabi_ref,
                    qmt_ref, km_ref, vmt_ref, kmean_ref, qdt_ref, kd_ref, vdt_ref, qit_ref, ki_ref, wit_ref,
                    *, half_h, half_i):
    tm = x_ref.shape[0]
    x = x_ref[...]
    sh1 = mod_ref[0, 0:1, :]
    sc1 = mod_ref[0, 1:2, :]
    h = (_rms(x, gmix_ref[...]) * (1.0 + sc1) + sh1).astype(BF16)

    def proj(c0, width):
        return _dot(h, w_ref[:, c0:c0 + width])

    q_scale = HEAD_DIM ** -0.5 * LOG2E
    nblk = tm // MOBA_BLOCK
    qm = proj(_C_QM, MOBA_W)
    km = proj(_C_KM, MOBA_W)
    vm = proj(_C_VM, MOBA_W)
    for hd in range(MOBA_HEADS):
        sl = slice(hd * HEAD_DIM, (hd + 1) * HEAD_DIM)
        qmt_ref[sl, :] = (_rope(qm[:, sl], tabh_ref, half_h) * q_scale).T.astype(BF16)
        kr = _rope(km[:, sl], tabh_ref, half_h)
        km_ref[:, sl] = kr.astype(BF16)
        for blk in range(nblk):
            rows = slice(blk * MOBA_BLOCK, (blk + 1) * MOBA_BLOCK)
            kmean_ref[blk:blk + 1, sl] = jnp.mean(kr[rows], axis=0, keepdims=True)
            vmt_ref[blk, sl, :] = vm[rows, sl].T.astype(BF16)
    qd = proj(_C_QD, DSA_W)
    for hd in range(DSA_HEADS):
        sl = slice(hd * HEAD_DIM, (hd + 1) * HEAD_DIM)
        qdt_ref[sl, :] = (_rope(qd[:, sl], tabh_ref, half_h) * q_scale).T.astype(BF16)
    ckv = proj(_C_CKV, KV_LORA)
    kv = _dot(_rms(ckv, gkv_ref[...]).astype(BF16), wkv_ref[...])
    kd_ref[...] = _rope(kv[:, :HEAD_DIM], tabh_ref, half_h).astype(BF16)
    for ch in range(tm // DSA_KA):
        vdt_ref[ch] = kv[ch * DSA_KA:(ch + 1) * DSA_KA, HEAD_DIM:].T.astype(BF16)
    qi = proj(_C_QI, IDX_HEADS * IDX_DIM)
    for j in range(IDX_HEADS * IDX_DIM // LANES):
        sl = slice(j * LANES, (j + 1) * LANES)
        qit_ref[sl, :] = _rope(qi[:, sl], tabi_ref, half_i).T.astype(BF16)
    kw = proj(_C_KI, LANES)
    ki_ref[...] = _rope(kw, tabi_ref, half_i)[:, :IDX_DIM].astype(BF16)
    wit_ref[...] = kw.T[IDX_DIM:IDX_DIM + IDX_HEADS, :] * (IDX_HEADS ** -0.5 * IDX_DIM ** -0.5)


def _in_proj(x2, mod3, g_mix, w_in_p, g_kv, w_kv_up, tab_h, half_h, tab_i, half_i, S):
    T, D = x2.shape
    tm = PROJ_TM
    nt_per_seq = S // tm
    row = lambda i: (i, 0)
    col = lambda i: (0, i)
    nb = tm // MOBA_BLOCK
    nc = tm // DSA_KA
    outs = [
        jax.ShapeDtypeStruct((MOBA_W, T), BF16),
        jax.ShapeDtypeStruct((T, MOBA_W), BF16),
        jax.ShapeDtypeStruct((T // MOBA_BLOCK, MOBA_W, MOBA_BLOCK), BF16),
        jax.ShapeDtypeStruct((T // tm, nb, MOBA_W), F32),
        jax.ShapeDtypeStruct((DSA_W, T), BF16),
        jax.ShapeDtypeStruct((T, HEAD_DIM), BF16),
        jax.ShapeDtypeStruct((T // DSA_KA, HEAD_DIM, DSA_KA), BF16),
        jax.ShapeDtypeStruct((IDX_HEADS * IDX_DIM, T), BF16),
        jax.ShapeDtypeStruct((T, IDX_DIM), BF16),
        jax.ShapeDtypeStruct((IDX_HEADS, T), F32),
    ]
    out_specs = [
        pl.BlockSpec((MOBA_W, tm), col), pl.BlockSpec((tm, MOBA_W), row),
        pl.BlockSpec((nb, MOBA_W, MOBA_BLOCK), lambda i: (i, 0, 0)),
        pl.BlockSpec((None, nb, MOBA_W), lambda i: (i, 0, 0)),
        pl.BlockSpec((DSA_W, tm), col), pl.BlockSpec((tm, HEAD_DIM), row),
        pl.BlockSpec((nc, HEAD_DIM, DSA_KA), lambda i: (i, 0, 0)),
        pl.BlockSpec((IDX_HEADS * IDX_DIM, tm), col), pl.BlockSpec((tm, IDX_DIM), row),
        pl.BlockSpec((IDX_HEADS, tm), col),
    ]
    res = pl.pallas_call(
        functools.partial(_in_proj_kernel, half_h=half_h, half_i=half_i),
        out_shape=outs,
        grid=(T // tm,),
        in_specs=[
            pl.BlockSpec((tm, D), row),
            pl.BlockSpec((1, 6, D), lambda i: (i // nt_per_seq, 0, 0)),
            pl.BlockSpec((1, D), lambda i: (0, 0)),
            pl.BlockSpec((D, _C_END), lambda i: (0, 0)),
            pl.BlockSpec((1, KV_LORA), lambda i: (0, 0)),
            pl.BlockSpec((KV_LORA, 2 * HEAD_DIM), lambda i: (0, 0)),
            pl.BlockSpec((3, tm, LANES), lambda i: (0, i % nt_per_seq, 0)),
            pl.BlockSpec((3, tm, LANES), lambda i: (0, i % nt_per_seq, 0)),
        ],
        out_specs=out_specs,
        compiler_params=_cparams(("parallel",)),
        name="in_proj",
    )(x2, mod3, g_mix, w_in_p, g_kv, w_kv_up, tab_h, tab_i)
    res = list(res)
    res[3] = res[3].reshape(T // MOBA_BLOCK, MOBA_W)
    return res


def _moba_kernel(qt_ref, k_ref, vt_ref, kmean_ref, o_ref, bias_sc, *accs):
    qi = pl.program_id(1)
    blk = MOBA_BLOCK
    nb = kmean_ref.shape[0]
    heads = range(MOBA_HEADS)
    hsl = [slice(hd * HEAD_DIM, (hd + 1) * HEAD_DIM) for hd in heads]
    qts = [qt_ref[hsl[hd], :] for hd in heads]
    row = lax.broadcasted_iota(I32, (nb, blk), 0)
    past = row < qi
    start = pl.multiple_of(qi * blk, blk)
    k_io = lax.broadcasted_iota(I32, (blk, blk), 0)
    q_io = lax.broadcasted_iota(I32, (blk, blk), 1)

    gates = []
    for hd in heads:
        km = kmean_ref[:, hsl[hd]]
        km_hi = km.astype(BF16)
        km_lo = (km - km_hi.astype(F32)).astype(BF16)
        gate = _dot(km_hi, qts[hd]) + _dot(km_lo, qts[hd])
        gates.append(jnp.where(past, gate, -jnp.inf))
    k_own = k_ref[pl.ds(start, blk), :]
    own = [jnp.where(k_io <= q_io, _dot(k_own[:, hsl[hd]], qts[hd]), NEG) for hd in heads]
    biases = [jnp.full((nb, blk), NEG, F32) for _ in heads]
    for _ in range(MOBA_TOPK):
        for hd in heads:
            _, idx = _first_index_of_max(gates[hd], row, nb)
            hit = row == idx
            biases[hd] = jnp.where(hit, 0.0, biases[hd])
            gates[hd] = jnp.where(hit, -jnp.inf, gates[hd])
    init = []
    for hd in heads:
        bias_sc[hd] = jnp.where(past, biases[hd], NEG)
        m0 = _col_max(own[hd])
        p = jnp.exp2(own[hd] - m0)
        accs[hd][...] = _dot(vt_ref[qi, hsl[hd], :], p.astype(BF16))
        init += [m0, _col_sum(p)]

    def body(n, carry):
        st = pl.multiple_of(n * blk, blk)
        kb = k_ref[pl.ds(st, blk), :]
        sbs = [_dot(kb[:, hsl[hd]], qts[hd]) + bias_sc[hd, pl.ds(n, 1), :] for hd in heads]
        old = [accs[hd][...] for hd in heads]
        out, new = [], []
        for hd in heads:
            m_old, l_old = carry[2 * hd], carry[2 * hd + 1]
            m_new = jnp.maximum(m_old, _col_max(sbs[hd]))
            alpha = jnp.exp2(m_old - m_new)
            pb = jnp.exp2(sbs[hd] - m_new)
            new.append(alpha * old[hd] + _dot(vt_ref[n, hsl[hd], :], pb.astype(BF16)))
            out += [m_new, alpha * l_old + _col_sum(pb)]
        for hd in heads:
            accs[hd][...] = new[hd]
        return tuple(out)

    fin = lax.fori_loop(0, qi, body, tuple(init))
    for hd in heads:
        o_ref[:, hsl[hd]] = (accs[hd][...] * (1.0 / fin[2 * hd + 1])).T


def _moba_attn(q_mt, k_m, v_mt, kmean, B, S):
    T = B * S
    blk = MOBA_BLOCK
    nq = S // blk
    return pl.pallas_call(
        _moba_kernel,
        out_shape=jax.ShapeDtypeStruct((T, MOBA_W), F32),
        grid=(B, nq),
        in_specs=[
            pl.BlockSpec((MOBA_W, blk), lambda b, i: (0, b * nq + i)),
            pl.BlockSpec((S, MOBA_W), lambda b, i: (b, 0)),
            pl.BlockSpec((nq, MOBA_W, blk), lambda b, i: (b, 0, 0)),
            pl.BlockSpec((nq, MOBA_W), lambda b, i: (b, 0)),
        ],
        out_specs=pl.BlockSpec((blk, MOBA_W), lambda b, i: (b * nq + i, 0)),
        scratch_shapes=[pltpu.VMEM((MOBA_HEADS, nq, blk), F32)]
        + [pltpu.VMEM((HEAD_DIM, blk), F32) for _ in range(MOBA_HEADS)],
        compiler_params=_cparams(("parallel", "arbitrary")),
        name="moba_attn",
    )(q_mt, k_m, v_mt, kmean)


def _sortable_key(x):
    b = pltpu.bitcast(x, I32)
    return jnp.where(b >= 0, b, b ^ jnp.int32(0x7FFFFFFF))


def _dsa_kernel(qit_ref, wit_ref, qdt_ref, ki_ref, kd_ref, vdt_ref, o_ref, key_sc, high_sc, *accs, topk):
    t = pl.program_id(1)
    tq, kc = DSA_TQ, DSA_KC
    S = key_sc.shape[0]
    q0 = t * tq
    n_chunks = (q0 + tq + kc - 1) // kc
    key_io = lax.broadcasted_iota(I32, (kc, tq), 0)
    q_pos = q0 + lax.broadcasted_iota(I32, (kc, tq), 1)
    w = wit_ref[...]

    def score_chunk(c, carry):
        k0 = pl.multiple_of(c * kc, kc)
        kic = ki_ref[pl.ds(k0, kc), :]
        acc = jnp.zeros((kc, tq), F32)
        for hd in range(IDX_HEADS):
            lg = _dot(kic, qit_ref[hd * IDX_DIM:(hd + 1) * IDX_DIM, :])
            acc = acc + jnp.maximum(lg, 0.0) * w[hd:hd + 1, :]
        acc = jnp.where(acc == 0.0, 0.0, acc)
        key = _sortable_key(acc)
        key = jnp.where(k0 + key_io <= q_pos, key, INT_MIN)
        key_sc[pl.ds(k0, kc), :] = key
        high_sc[pl.ds(k0, kc), :] = lax.shift_right_arithmetic(key, 16).astype(jnp.int16)
        return carry

    lax.fori_loop(0, n_chunks, score_chunk, 0)

    ks = DSA_KS
    ks_io = lax.broadcasted_iota(I32, (ks, tq), 0)

    def column_total(slab_fn, dtype):
        rows = SUBLANES * (4 // jnp.dtype(dtype).itemsize)

        def cbody(c, accs):
            vals = slab_fn(pl.multiple_of(c * ks, ks))
            accs = list(accs)
            for i in range(ks // rows):
                j = i % REDUCE_CHAINS
                accs[j] = accs[j] + vals[i * rows:(i + 1) * rows]
            return tuple(accs)
        zero = jnp.zeros((rows, tq), dtype)
        accs = lax.fori_loop(0, n_chunks * (kc // ks), cbody, (zero,) * REDUCE_CHAINS)
        total = accs[0].astype(I32)
        for a in accs[1:]:
            total = total + a.astype(I32)
        return jnp.sum(total, axis=0, keepdims=True)

    def count(pred_fn):
        return column_total(lambda k0: jnp.where(pred_fn(key_sc[pl.ds(k0, ks), :], k0), 1, 0), I32)

    def count_ge_high(cand):
        c16 = lax.shift_right_arithmetic(cand, 16).astype(jnp.int16)
        one, zero = jnp.int16(1), jnp.int16(0)
        return column_total(lambda k0: jnp.where(high_sc[pl.ds(k0, ks), :] >= c16, one, zero), jnp.int16)

    thr0 = jnp.where(count_ge_high(jnp.zeros((1, tq), I32)) >= topk, 0, INT_MIN).astype(I32)

    def high_step(i, thr):
        cand = thr | lax.shift_left(jnp.int32(1), 30 - i)
        return jnp.where(count_ge_high(cand) >= topk, cand, thr)

    thr = lax.fori_loop(0, 15, high_step, thr0)

    high = lax.shift_right_arithmetic(thr, 16)
    h16 = high.astype(jnp.int16)
    one16, zero16 = jnp.int16(1), jnp.int16(0)
    n_above = column_total(lambda k0: jnp.where(high_sc[pl.ds(k0, ks), :] > h16, one16, zero16), jnp.int16)

    def repack(c, carry):
        k0 = pl.multiple_of(c * kc, kc)
        kk = key_sc[pl.ds(k0, kc), :]
        low = (kk & 0xFFFF) - 32768
        same = lax.shift_right_arithmetic(kk, 16) == high
        high_sc[pl.ds(k0, kc), :] = jnp.where(same, low, -32768).astype(jnp.int16)
        return carry

    lax.fori_loop(0, n_chunks, repack, 0)

    def low_step(i, low_bits):
        cand = low_bits | lax.shift_left(jnp.int32(1), 15 - i)
        c16 = (cand - 32768).astype(jnp.int16)
        cnt = n_above + column_total(
            lambda k0: jnp.where(high_sc[pl.ds(k0, ks), :] >= c16, one16, zero16), jnp.int16)
        return jnp.where(cnt >= topk, cand, low_bits)

    thr = thr | lax.fori_loop(0, 16, low_step, jnp.zeros((1, tq), I32))

    n_gt = count(lambda kk, k0: kk > thr)
    n_ge = count(lambda kk, k0: kk >= thr)
    need = topk - n_gt
    overflow = (n_ge > topk) & (thr != INT_MIN)
    any_overflow = jnp.max(jnp.where(overflow, 1, 0)) > 0
    nbits = max(1, (S - 1).bit_length())

    def cut_search():
        def step(i, lo):
            cand = lo | lax.shift_left(jnp.int32(1), nbits - 1 - i)
            cnt = count(lambda kk, k0: (kk == thr) & (k0 + ks_io < cand))
            return jnp.where(cnt >= need, lo, cand)
        return lax.fori_loop(0, nbits, step, jnp.zeros((1, tq), I32))

    jcut = lax.cond(any_overflow, cut_search, lambda: jnp.zeros((1, tq), I32))
    jcut = jnp.where(overflow, jcut, S)
    thr_ge = jnp.where(thr == INT_MIN, INT_MIN + 1, thr)

    pairs = range(DSA_HEADS // 2)
    qst = [jnp.concatenate([qdt_ref[(2 * g + j) * HEAD_DIM:(2 * g + j + 1) * HEAD_DIM, :] for j in range(2)],
                           axis=1) for g in pairs]
    for g in pairs:
        accs[g][...] = jnp.zeros(accs[g].shape, F32)

    ka = DSA_KA
    pos_io = lax.broadcasted_iota(I32, (ka, tq), 0)

    def attn_chunk(c, carry):
        k0 = pl.multiple_of(c * ka, ka)
        kk = key_sc[pl.ds(k0, ka), :]
        bias = lax.cond(
            any_overflow,
            lambda: jnp.where((kk > thr_ge) | ((kk == thr_ge) & (k0 + pos_io <= jcut)), 0.0, NEG),
            lambda: jnp.where(kk >= thr_ge, 0.0, NEG))
        bias2 = jnp.concatenate([bias, bias], axis=1)
        kdc = kd_ref[pl.ds(k0, ka), :]
        vtc = vdt_ref[c]
        ss = [_dot(kdc, qst[g]) + bias2 for g in pairs]
        old = [accs[g][...] for g in pairs]
        out, new = [], []
        for g in pairs:
            m_old, l_old = carry[2 * g], carry[2 * g + 1]
            m_new = jnp.maximum(m_old, _col_max(ss[g]))
            alpha = jnp.exp2(m_old - m_new)
            p = jnp.exp2(ss[g] - m_new)
            new.append(alpha * old[g] + _dot(vtc, p.astype(BF16)))
            out += [m_new, alpha * l_old + _col_sum(p)]
        for g in pairs:
            accs[g][...] = new[g]
        return tuple(out)

    init = (jnp.full((1, 2 * tq), NEG, F32), jnp.zeros((1, 2 * tq), F32)) * len(pairs)
    fin = lax.fori_loop(0, (q0 + tq + ka - 1) // ka, attn_chunk, init)
    for g in pairs:
        out_t = accs[g][...] * (1.0 / fin[2 * g + 1])
        for j in range(2):
            hd = 2 * g + j
            o_ref[:, hd * HEAD_DIM:(hd + 1) * HEAD_DIM] = out_t[:, j * tq:(j + 1) * tq].T


def _dsa_attn(q_it, w_it, q_dt, k_i, k_d, v_dt, B, S):
    T = B * S
    tq = DSA_TQ
    nq = S // tq
    topk = min(DSA_MAX_TOPK, S // 4)
    qcol = lambda b, t: (0, b * nq + t)
    seq = lambda b, t: (b, 0)
    return pl.pallas_call(
        functools.partial(_dsa_kernel, topk=topk),
        out_shape=jax.ShapeDtypeStruct((T, DSA_W), F32),
        grid=(B, nq),
        in_specs=[
            pl.BlockSpec((IDX_HEADS * IDX_DIM, tq), qcol),
            pl.BlockSpec((IDX_HEADS, tq), qcol),
            pl.BlockSpec((DSA_W, tq), qcol),
            pl.BlockSpec((S, IDX_DIM), seq),
            pl.BlockSpec((S, HEAD_DIM), seq),
            pl.BlockSpec((S // DSA_KA, HEAD_DIM, DSA_KA), lambda b, t: (b, 0, 0)),
        ],
        out_specs=pl.BlockSpec((tq, DSA_W), lambda b, t: (b * nq + t, 0)),
        scratch_shapes=[pltpu.VMEM((S, tq), I32), pltpu.VMEM((S, tq), jnp.int16)]
        + [pltpu.VMEM((HEAD_DIM, 2 * tq), F32) for _ in range(DSA_HEADS // 2)],
        compiler_params=_cparams(("parallel", "arbitrary")),
        name="dsa_attn",
    )(q_it, w_it, q_dt, k_i, k_d, v_dt)


def _first_index_of_max(v, row_io, n_rows):
    m = jnp.max(v, axis=0, keepdims=True)
    idx = jnp.min(jnp.where(v == m, row_io, n_rows), axis=0, keepdims=True)
    return m, idx


def _post_kernel(x_ref, om_ref, od_ref, mod_ref, gm_ref, gd_ref, wout_ref, gffn_ref, wgu_ref, wds_ref,
                 wrh_ref, wrl_ref, rb_ref, tri_ref,
                 xpart_ref, h2r_ref, eidx_ref, rank_ref, gate_ref, cnt_ref, base_sc):
    i = pl.program_id(0)
    tm = x_ref.shape[0]
    gt1 = mod_ref[0, 2:3, :]
    sh2 = mod_ref[0, 3:4, :]
    sc2 = mod_ref[0, 4:5, :]
    gt2 = mod_ref[0, 5:6, :]

    mixed = jnp.concatenate([_rms(om_ref[...], gm_ref[...]), _rms(od_ref[...], gd_ref[...])], axis=1)
    x1 = x_ref[...] + gt1 * _dot(mixed.astype(BF16), wout_ref[...])
    h2 = _rms(x1, gffn_ref[...]) * (1.0 + sc2) + sh2
    h2b = h2.astype(BF16)

    au = _dot(h2b, wgu_ref[...])
    hs = (_silu(au[:, :D_SHARED]) * au[:, D_SHARED:]).astype(BF16)
    xpart_ref[...] = x1 + gt2 * _dot(hs, wds_ref[...])

    for c, slab in enumerate(_pack_row_words(h2)):
        _store_word_slab(h2r_ref, 0, tm, c, slab)

    h2lo = (h2 - h2b.astype(F32)).astype(BF16)
    logits = _dot_nt(wrh_ref[...], h2b) + _dot_nt(wrl_ref[...], h2b) + _dot_nt(wrh_ref[...], h2lo)
    scores = 1.0 / (1.0 + jnp.exp(-logits))
    biased = scores + rb_ref[...]

    g_io = lax.broadcasted_iota(I32, (GROUP_SIZE, tm), 0)
    gs_rows = []
    for g in range(N_GROUPS):
        blk = biased[g * GROUP_SIZE:(g + 1) * GROUP_SIZE, :]
        m1, i1 = _first_index_of_max(blk, g_io, GROUP_SIZE)
        m2 = jnp.max(jnp.where(g_io == i1, -jnp.inf, blk), axis=0, keepdims=True)
        gs_rows.append(m1 + m2)
    gs = jnp.concatenate(gs_rows, axis=0)
    gi = lax.broadcasted_iota(I32, (N_GROUPS, tm), 0)
    grank = jnp.zeros((N_GROUPS, tm), I32)
    for m in range(N_GROUPS):
        gm = gs[m:m + 1, :]
        grank = grank + jnp.where((gm > gs) | ((gm == gs) & (m < gi)), 1, 0)
    gsel = grank < TOPK_GROUPS
    masked = jnp.concatenate(
        [jnp.where(gsel[g:g + 1, :], biased[g * GROUP_SIZE:(g + 1) * GROUP_SIZE, :], -jnp.inf)
         for g in range(N_GROUPS)], axis=0)

    e_io = lax.broadcasted_iota(I32, (N_EXPERTS, tm), 0)
    e_rows, s_rows = [], []
    for _ in range(EXPERT_TOPK):
        _, idx = _first_index_of_max(masked, e_io, N_EXPERTS)
        hit = e_io == idx
        e_rows.append(idx)
        s_rows.append(jnp.sum(jnp.where(hit, scores, 0.0), axis=0, keepdims=True))
        masked = jnp.where(hit, -jnp.inf, masked)
    eidx = jnp.concatenate(e_rows, axis=0)
    sk = jnp.concatenate(s_rows, axis=0)
    gate_ref[...] = sk / jnp.sum(sk, axis=0, keepdims=True) * ROUTED_SCALE
    eidx_ref[...] = eidx

    @pl.when(i == 0)
    def _():
        base_sc[...] = jnp.zeros(base_sc.shape, F32)

    chosen = jnp.zeros((N_EXPERTS, tm), F32)
    for k in range(EXPERT_TOPK):
        chosen = chosen + jnp.where(e_io == e_rows[k], 1.0, 0.0)
    incl = _dot(chosen.astype(BF16), tri_ref[...])
    pos = base_sc[...] + incl - 1.0
    rank_ref[...] = jnp.concatenate(
        [jnp.sum(jnp.where(e_io == e_rows[k], pos, 0.0), axis=0, keepdims=True)
         for k in range(EXPERT_TOPK)], axis=0).astype(I32)
    base_sc[...] = base_sc[...] + incl[:, tm - 1:tm]
    cnt_ref[...] = jnp.broadcast_to(base_sc[...], cnt_ref.shape)


def _post_attn(x2, o_m, o_d, mod3, g_moba, g_dsa, w_out, g_ffn, w_gu_s, w_down_s, wr_hi, wr_lo, rbias, S):
    T, D = x2.shape
    tm = POST_TM
    nt_per_seq = S // tm
    row = lambda i: (i, 0)
    full = lambda i: (0, 0)
    tri = (jnp.arange(tm)[:, None] <= jnp.arange(tm)[None, :]).astype(BF16)
    n_words = D // ROW_WORDS
    return pl.pallas_call(
        _post_kernel,
        out_shape=[
            jax.ShapeDtypeStruct((T, D), F32),
            jax.ShapeDtypeStruct((T, n_words, LANES), I32),
            jax.ShapeDtypeStruct((EXPERT_TOPK, T), I32),
            jax.ShapeDtypeStruct((EXPERT_TOPK, T), I32),
            jax.ShapeDtypeStruct((EXPERT_TOPK, T), F32),
            jax.ShapeDtypeStruct((N_EXPERTS, LANES), F32),
        ],
        grid=(T // tm,),
        in_specs=[
            pl.BlockSpec((tm, D), row),
            pl.BlockSpec((tm, MOBA_W), row),
            pl.BlockSpec((tm, DSA_W), row),
            pl.BlockSpec((1, 6, D), lambda i: (i // nt_per_seq, 0, 0)),
            pl.BlockSpec((1, MOBA_W), full),
            pl.BlockSpec((1, DSA_W), full),
            pl.BlockSpec(w_out.shape, full),
            pl.BlockSpec((1, D), full),
            pl.BlockSpec(w_gu_s.shape, full),
            pl.BlockSpec(w_down_s.shape, full),
            pl.BlockSpec(wr_hi.shape, full),
            pl.BlockSpec(wr_lo.shape, full),
            pl.BlockSpec((N_EXPERTS, 1), full),
            pl.BlockSpec((tm, tm), full),
        ],
        out_specs=[
            pl.BlockSpec((tm, D), row),
            pl.BlockSpec((tm, n_words, LANES), lambda i: (i, 0, 0)),
            pl.BlockSpec((EXPERT_TOPK, tm), lambda i: (0, i)),
            pl.BlockSpec((EXPERT_TOPK, tm), lambda i: (0, i)),
            pl.BlockSpec((EXPERT_TOPK, tm), lambda i: (0, i)),
            pl.BlockSpec((N_EXPERTS, LANES), full),
        ],
        scratch_shapes=[pltpu.VMEM((N_EXPERTS, 1), F32)],
        compiler_params=_cparams(("arbitrary",)),
        name="post_attn",
    )(x2, o_m, o_d, mod3, g_moba, g_dsa, w_out, g_ffn, w_gu_s, w_down_s, wr_hi, wr_lo, rbias, tri)


def _row_copy_wait(rows_hbm, n_rows, sem):
    blk = rows_hbm.at[pl.ds(0, n_rows)]
    pltpu.make_async_copy(blk, blk, sem).wait()


def _slots_kernel(ps_ref, e_ref, r_ref, d_ref):
    e = e_ref[...]

    def body(x, acc):
        return jnp.where(e == x, ps_ref[x], acc)

    d_ref[...] = lax.fori_loop(0, N_EXPERTS, body, jnp.zeros(e.shape, I32)) + r_ref[...]


def _slots(pad_starts, eidx, rank):
    K, T = eidx.shape
    tm = min(T, 4096)
    blk = lambda: pl.BlockSpec((K, tm), lambda i, ps: (0, i))
    return pl.pallas_call(
        _slots_kernel,
        out_shape=jax.ShapeDtypeStruct((K, T), I32),
        grid_spec=pltpu.PrefetchScalarGridSpec(num_scalar_prefetch=1, grid=(T // tm,),
                                               in_specs=[blk(), blk()], out_specs=blk()),
        compiler_params=_cparams(("parallel",)),
        name="slots",
    )(pad_starts, eidx, rank)


def _dispatch_kernel(d_ref, h2r_ref, xs_ref, sem):
    tm = h2r_ref.shape[0]

    def body(t, carry):
        for k in range(EXPERT_TOPK):
            pltpu.make_async_copy(h2r_ref.at[t], xs_ref.at[d_ref[t * EXPERT_TOPK + k]],
                                  sem).start(priority=k % 2)
        return carry

    lax.fori_loop(0, tm, body, 0)
    _row_copy_wait(xs_ref, tm * EXPERT_TOPK, sem)


def _dispatch(dest_flat, h2r, n_rows_padded):
    T, n_words, _ = h2r.shape
    tm = DISP_TM
    return pl.pallas_call(
        _dispatch_kernel,
        out_shape=jax.ShapeDtypeStruct((n_rows_padded, n_words, LANES), I32),
        grid=(T // tm,),
        in_specs=[pl.BlockSpec((tm * EXPERT_TOPK,), lambda i: (i,), memory_space=pltpu.SMEM),
                  pl.BlockSpec((tm, n_words, LANES), lambda i: (i, 0, 0))],
        out_specs=pl.BlockSpec(memory_space=pl.ANY),
        scratch_shapes=[pltpu.SemaphoreType.DMA],
        compiler_params=_cparams(("arbitrary",), disable_bounds_checks=True),
        name="dispatch",
    )(dest_flat, h2r)


def _expert_kernel(ps_ref, cnt_ref, wg_ref, wu_ref, wd_ref, xs_ref, ys_ref,
                   wg_sc, wu_sc, wd_sc, xbuf, ybuf, sem_in, sem_out):
    e = pl.program_id(0)
    bm = EXP_BM
    n_words = xbuf.shape[1]
    cnt = cnt_ref[e]
    g0 = ps_ref[e] // bm
    nb = (cnt + bm - 1) // bm
    n_used = (ps_ref[N_EXPERTS - 1] + cnt_ref[N_EXPERTS - 1] + bm - 1) // bm

    def in_copy(g):
        slot = lax.rem(g, EXP_IN_SLOTS)
        return pltpu.make_async_copy(xs_ref.at[pl.ds(g * bm, bm)], xbuf.at[pl.ds(slot * bm, bm)],
                                     sem_in.at[slot])

    def out_copy(g):
        slot = lax.rem(g, EXP_OUT_SLOTS)
        return pltpu.make_async_copy(ybuf.at[pl.ds(slot * bm, bm)], ys_ref.at[pl.ds(g * bm, bm)],
                                     sem_out.at[slot])

    @pl.when(e == 0)
    def _():
        for g in range(EXP_AHEAD):
            @pl.when(g < n_used)
            def _():
                in_copy(g).start()

    @pl.when(nb > 0)
    def _():
        wg_sc[...] = wg_ref[...].astype(BF16)
        wu_sc[...] = wu_ref[...].astype(BF16)
        wd_sc[...] = wd_ref[...].astype(BF16)

        def body(b, carry):
            g = g0 + b

            @pl.when(g + EXP_AHEAD < n_used)
            def _():
                in_copy(g + EXP_AHEAD).start()

            in_copy(g).wait()

            @pl.when(g >= EXP_OUT_SLOTS)
            def _():
                out_copy(g - EXP_OUT_SLOTS).wait()

            xrow = lax.rem(g, EXP_IN_SLOTS) * bm
            yrow = lax.rem(g, EXP_OUT_SLOTS) * bm
            feats = []
            for c in range(n_words):
                feats += _unpack_row_words(_load_word_slab(xbuf, xrow, bm, c))
            x = jnp.concatenate(feats, axis=1)
            valid = lax.broadcasted_iota(I32, (bm, 1), 0) < cnt - b * bm
            xb = jnp.where(valid, x, 0.0).astype(BF16)
            a = _dot(xb, wg_sc[...])
            u = _dot(xb, wu_sc[...])
            hmid = (_silu(a) * u).astype(BF16)
            ob = _dot(hmid, wd_sc[...])
            for c, slab in enumerate(_pack_row_words(ob)):
                _store_word_slab(ybuf, yrow, bm, c, slab)
            out_copy(g).start()
            return carry

        lax.fori_loop(0, nb, body, 0)

    @pl.when(e == N_EXPERTS - 1)
    def _():
        for back in range(EXP_OUT_SLOTS, 0, -1):
            @pl.when(n_used >= back)
            def _():
                out_copy(n_used - back).wait()


def _experts(pad_starts, counts, xs, w_gate_e, w_up_e, w_down_e):
    E, D, DE = w_gate_e.shape
    n_words = xs.shape[1]
    bm = EXP_BM
    wsel = lambda e, ps, cnt: (e, 0, 0)
    anyspec = pl.BlockSpec(memory_space=pl.ANY)
    return pl.pallas_call(
        _expert_kernel,
        out_shape=jax.ShapeDtypeStruct(xs.shape, I32),
        grid_spec=pltpu.PrefetchScalarGridSpec(
            num_scalar_prefetch=2,
            grid=(E,),
            in_specs=[pl.BlockSpec((None, D, DE), wsel), pl.BlockSpec((None, D, DE), wsel),
                      pl.BlockSpec((None, DE, D), wsel), anyspec],
            out_specs=anyspec,
            scratch_shapes=[pltpu.VMEM((D, DE), BF16), pltpu.VMEM((D, DE), BF16), pltpu.VMEM((DE, D), BF16),
                            pltpu.VMEM((EXP_IN_SLOTS * bm, n_words, LANES), I32),
                            pltpu.VMEM((EXP_OUT_SLOTS * bm, n_words, LANES), I32),
                            pltpu.SemaphoreType.DMA((EXP_IN_SLOTS,)), pltpu.SemaphoreType.DMA((EXP_OUT_SLOTS,))],
        ),
        compiler_params=_cparams(("arbitrary",)),
        name="experts",
    )(pad_starts, counts, w_gate_e, w_up_e, w_down_e, xs)


def _combine_kernel(dcur_ref, dnxt_ref, g_ref, xpart_ref, mod_ref, gfin_ref, ys_ref, o_ref,
                    buf0, buf1, sem0, sem1):
    i = pl.program_id(0)
    n_steps = pl.num_programs(0)
    tm = xpart_ref.shape[0]
    n_words = buf0.shape[1]

    def issue(dest_ref, buf, sem):
        def body(t, carry):
            for k in range(EXPERT_TOPK):
                pltpu.make_async_copy(ys_ref.at[dest_ref[t * EXPERT_TOPK + k]], buf.at[k * tm + t],
                                      sem).start(priority=k % 2)
            return carry
        lax.fori_loop(0, tm, body, 0)

    def reduce_tile(buf, sem):
        _row_copy_wait(ys_ref, tm * EXPERT_TOPK, sem)
        gt2 = mod_ref[0, 5:6, :]
        g = g_ref[...]
        cols = []
        for c in range(n_words):
            lo = jnp.zeros((tm, LANES), F32)
            hi = jnp.zeros((tm, LANES), F32)
            for k in range(EXPERT_TOPK):
                a, b = _unpack_row_words(_load_word_slab(buf, k * tm, tm, c))
                gk = g[:, k:k + 1]
                lo = lo + gk * a
                hi = hi + gk * b
            cols += [lo, hi]
        routed = jnp.concatenate(cols, axis=1)
        o_ref[...] = _rms(xpart_ref[...] + gt2 * routed, gfin_ref[...])

    @pl.when(i == 0)
    def _():
        issue(dcur_ref, buf0, sem0)

    for parity, (cur, nxt) in enumerate((((buf0, sem0), (buf1, sem1)), ((buf1, sem1), (buf0, sem0)))):
        @pl.when(i % 2 == parity)
        def _():
            @pl.when(i + 1 < n_steps)
            def _():
                issue(dnxt_ref, *nxt)
            reduce_tile(*cur)


def _combine(dest, gates_t, xpart, mod3, g_final, ys, S):
    T, D = xpart.shape
    tm = COMB_TM
    n_steps = T // tm
    nt_per_seq = S // tm
    n_words = ys.shape[1]
    return pl.pallas_call(
        _combine_kernel,
        out_shape=jax.ShapeDtypeStruct((T, D), F32),
        grid=(n_steps,),
        in_specs=[
            pl.BlockSpec((tm * EXPERT_TOPK,), lambda i: (i,), memory_space=pltpu.SMEM),
            pl.BlockSpec((tm * EXPERT_TOPK,), lambda i: (jnp.minimum(i + 1, n_steps - 1),),
                         memory_space=pltpu.SMEM),
            pl.BlockSpec((tm, EXPERT_TOPK), lambda i: (i, 0)),
            pl.BlockSpec((tm, D), lambda i: (i, 0)),
            pl.BlockSpec((1, 6, D), lambda i: (i // nt_per_seq, 0, 0)),
            pl.BlockSpec((1, D), lambda i: (0, 0)),
            pl.BlockSpec(memory_space=pl.ANY),
        ],
        out_specs=pl.BlockSpec((tm, D), lambda i: (i, 0)),
        scratch_shapes=[pltpu.VMEM((EXPERT_TOPK * tm, n_words, LANES), I32),
                        pltpu.VMEM((EXPERT_TOPK * tm, n_words, LANES), I32),
                        pltpu.SemaphoreType.DMA, pltpu.SemaphoreType.DMA],
        compiler_params=_cparams(("arbitrary",), disable_bounds_checks=True),
        name="combine",
    )(dest, dest, gates_t, xpart, mod3, g_final, ys)


def _layer(x2, mod3, S, g_mix, w_in, g_kv, w_kv_up, g_moba_out, g_dsa_out, w_out, g_ffn, w_router,
           router_bias, w_gate_e, w_up_e, w_down_e, w_gate_s, w_up_s, w_down_s, g_final, tab_h, half_h,
           tab_i, half_i):
    T, D = x2.shape
    B = T // S
    w_in_p = jnp.pad(w_in, ((0, 0), (0, _C_END - w_in.shape[1]))).astype(BF16)
    (q_mt, k_m, v_mt, kmean, q_dt, k_d, v_dt, q_it, k_i, w_it) = _in_proj(
        x2, mod3, g_mix.reshape(1, D), w_in_p, g_kv.reshape(1, KV_LORA), w_kv_up.astype(BF16),
        tab_h, half_h, tab_i, half_i, S)
    o_m = _moba_attn(q_mt, k_m, v_mt, kmean, B, S)
    o_d = _dsa_attn(q_it, w_it, q_dt, k_i, k_d, v_dt, B, S)

    wr_t = w_router.T
    wr_hi = wr_t.astype(BF16)
    wr_lo = (wr_t - wr_hi.astype(F32)).astype(BF16)
    w_gu_s = jnp.concatenate([w_gate_s, w_up_s], axis=1).astype(BF16)
    xpart, h2r, eidx, rank, gates, cnt = _post_attn(
        x2, o_m, o_d, mod3, g_moba_out.reshape(1, MOBA_W), g_dsa_out.reshape(1, DSA_W), w_out.astype(BF16),
        g_ffn.reshape(1, D), w_gu_s, w_down_s.astype(BF16), wr_hi, wr_lo,
        router_bias.reshape(N_EXPERTS, 1), S)

    bm = EXP_BM
    n_blocks = T * EXPERT_TOPK // bm + N_EXPERTS
    counts = cnt[:, 0].astype(I32)
    padded = (counts + bm - 1) // bm * bm
    pad_ends = jnp.cumsum(padded)
    pad_starts = (pad_ends - padded).astype(I32)
    dest_flat = _slots(pad_starts, eidx, rank).T.reshape(-1)
    xs = _dispatch(dest_flat, h2r, n_blocks * bm)
    ys = _experts(pad_starts, counts, xs, w_gate_e, w_up_e, w_down_e)
    return _combine(dest_flat, gates.T, xpart, mod3, g_final.reshape(1, D), ys, S)


def kernel(x, c, w_ada, b_ada, g_mix, w_in, g_kv, w_kv_up, g_moba_out, g_dsa_out, w_out, g_ffn, w_router,
           router_bias, w_gate_e, w_up_e, w_down_e, w_gate_s, w_up_s, w_down_s, g_final):
    B, S, D = x.shape
    depth = w_ada.shape[0]
    assert depth == 1, "the final norm is fused into the single layer"
    assert S % PROJ_TM == 0 and S % DSA_KC == 0 and S % POST_TM == 0 and S >= 4 * DSA_MAX_TOPK
    tab_h, half_h = _rope_tables(S, HEAD_DIM, 1)
    tab_i, half_i = _rope_tables(S, IDX_DIM, LANES // IDX_DIM)
    x2 = x.reshape(B * S, D)
    sq = lambda a: a.reshape(a.shape[1:])
    mod3 = _ada_mod(c, sq(w_ada), sq(b_ada)).reshape(B, 6, D)
    out = _layer(x2, mod3, S, sq(g_mix), sq(w_in), sq(g_kv), sq(w_kv_up), sq(g_moba_out), sq(g_dsa_out),
                 sq(w_out), sq(g_ffn), sq(w_router), sq(router_bias), sq(w_gate_e), sq(w_up_e), sq(w_down_e),
                 sq(w_gate_s), sq(w_up_s), sq(w_down_s), g_final, tab_h, half_h, tab_i, half_i)
    return out.reshape(B, S, D)
```

```python
import functools

import jax
import jax.numpy as jnp
from jax import lax
from jax.experimental import pallas as pl
from jax.experimental.pallas import tpu as pltpu

HEAD_DIM = 128
MOBA_HEADS = 4
DSA_HEADS = 4
MOBA_W = MOBA_HEADS * HEAD_DIM
DSA_W = DSA_HEADS * HEAD_DIM
MOBA_BLOCK = 256
MOBA_TOPK = 3
DSA_MAX_TOPK = 256
KV_LORA = 256
IDX_HEADS = 8
IDX_DIM = 64
ROPE_THETA = 500000.0
ROPE_FRACTION_DIV = 4
N_EXPERTS = 256
EXPERT_TOPK = 8
N_GROUPS = 8
TOPK_GROUPS = 4
GROUP_SIZE = N_EXPERTS // N_GROUPS
D_EXPERT = 256
D_SHARED = 256
ROUTED_SCALE = 2.5
EPS = 1e-6

LANES = 128
SUBLANES = 8
VMEM_LIMIT = 56 * 1024 * 1024

PROJ_TM = 512
DSA_TQ = 256
DSA_KC = 512
DSA_KA = 256
DSA_KS = 128
POST_TM = 512
DISP_TM = 256
EXP_BM = 256
EXP_AHEAD = 3
EXP_IN_SLOTS = EXP_AHEAD + 1
EXP_OUT_SLOTS = 2
COMB_TM = 128
REDUCE_CHAINS = 4
NEG = -1e30
INT_MIN = -2147483648
LOG2E = 1.4426950408889634

F32 = jnp.float32
BF16 = jnp.bfloat16
I32 = jnp.int32


def _cparams(sem, **kw):
    return pltpu.CompilerParams(dimension_semantics=sem, vmem_limit_bytes=VMEM_LIMIT, **kw)


def _dot(a, b):
    return jnp.dot(a, b, preferred_element_type=F32)


def _dot_nt(a, b):
    return lax.dot_general(a, b, (((1,), (1,)), ((), ())), preferred_element_type=F32)


def _silu(x):
    return x * (1.0 / (1.0 + jnp.exp(-x)))


def _rms(x, g):
    return x * lax.rsqrt(jnp.mean(x * x, axis=-1, keepdims=True) + EPS) * g


def _rows_to_tile(op, x):
    parts = [x[i:i + SUBLANES] for i in range(0, x.shape[0], SUBLANES)]
    n_chains = min(REDUCE_CHAINS, len(parts))
    accs = parts[:n_chains]
    for i in range(n_chains, len(parts)):
        accs[i % n_chains] = op(accs[i % n_chains], parts[i])
    while len(accs) > 1:
        accs = [op(accs[i], accs[i + 1]) for i in range(0, len(accs) - 1, 2)] + ([accs[-1]] if len(accs) % 2 else [])
    return accs[0]


def _col_max(x):
    return jnp.max(_rows_to_tile(jnp.maximum, x), axis=0, keepdims=True)


def _col_sum(x):
    return jnp.sum(_rows_to_tile(jnp.add, x), axis=0, keepdims=True)


ROW_WORDS = 2 * LANES


def _pack_row_words(x):
    slabs = []
    for c in range(x.shape[1] // ROW_WORDS):
        lo = pltpu.bitcast(x[:, c * ROW_WORDS:c * ROW_WORDS + LANES].astype(BF16).astype(F32), I32)
        hi = pltpu.bitcast(x[:, c * ROW_WORDS + LANES:(c + 1) * ROW_WORDS].astype(BF16).astype(F32), I32)
        slabs.append(lax.shift_right_logical(lo, 16) | hi)
    return slabs


def _word_slab_index(rows_ref, row0, n_rows, c):
    n_words = rows_ref.shape[1]
    flat = rows_ref.reshape(rows_ref.shape[0] * n_words, LANES)
    return flat, pl.ds(row0 * n_words + c, n_rows, stride=n_words)


def _load_word_slab(rows_ref, row0, n_rows, c):
    flat, idx = _word_slab_index(rows_ref, row0, n_rows, c)
    return flat[idx, :]


def _store_word_slab(rows_ref, row0, n_rows, c, value):
    flat, idx = _word_slab_index(rows_ref, row0, n_rows, c)
    flat[idx, :] = value


def _unpack_row_words(u):
    return pltpu.bitcast(lax.shift_left(u, 16), F32), pltpu.bitcast(u & jnp.int32(-65536), F32)


def _ada_kernel(c_ref, w_ref, b_ref, o_ref):
    ca = _silu(c_ref[...])
    o_ref[...] = jnp.dot(ca, w_ref[...], preferred_element_type=F32,
                         precision=lax.Precision.HIGHEST) + b_ref[...]


def _ada_mod(c, w_ada, b_ada):
    B, D = c.shape
    N = w_ada.shape[1]
    tn = 1024
    return pl.pallas_call(
        _ada_kernel,
        out_shape=jax.ShapeDtypeStruct((B, N), F32),
        grid=(N // tn,),
        in_specs=[pl.BlockSpec((B, D), lambda j: (0, 0)),
                  pl.BlockSpec((D, tn), lambda j: (0, j)),
                  pl.BlockSpec((1, tn), lambda j: (0, j))],
        out_specs=pl.BlockSpec((B, tn), lambda j: (0, j)),
        compiler_params=_cparams(("arbitrary",)),
        name="ada_mod",
    )(c, w_ada, b_ada.reshape(1, N))


def _rope_tables(seq, head_dim, heads_per_vreg):
    rot = head_dim // ROPE_FRACTION_DIV
    half = rot // 2
    inv = jnp.float32(ROPE_THETA) ** (-(jnp.arange(0, rot, 2, dtype=F32) / rot))
    ang = jnp.arange(seq, dtype=F32)[:, None] * inv[None, :]
    cos, sin = jnp.cos(ang), jnp.sin(ang)
    ones = jnp.ones((seq, head_dim - rot), F32)
    zeros_h = jnp.zeros((seq, half), F32)
    zeros_r = jnp.zeros((seq, head_dim - rot), F32)
    c = jnp.concatenate([cos, cos, ones], axis=1)
    sp = jnp.concatenate([zeros_h, sin, zeros_r], axis=1)
    sm = jnp.concatenate([-sin, zeros_h, zeros_r], axis=1)
    rep = lambda t: jnp.tile(t, (1, heads_per_vreg))
    return jnp.stack([rep(c), rep(sp), rep(sm)], axis=0), half


def _rope(x, tab_ref, half):
    return (x * tab_ref[0] + pltpu.roll(x, half, 1) * tab_ref[1]
            + pltpu.roll(x, LANES - half, 1) * tab_ref[2])


_C_QM, _C_KM, _C_VM, _C_QD = 0, MOBA_W, 2 * MOBA_W, 3 * MOBA_W
_C_CKV = 3 * MOBA_W + DSA_W
_C_QI = _C_CKV + KV_LORA
_C_KI = _C_QI + IDX_HEADS * IDX_DIM
_C_END = _C_KI + LANES


def _in_proj_kernel(x_ref, mod_ref, gmix_ref, w_ref, gkv_ref, wkv_ref, tabh_ref, tabi_ref,
                    qmt_ref, km_ref, vmt_ref, kmean_ref, qdt_ref, kd_ref, vdt_ref, qit_ref, ki_ref, wit_ref,
                    *, half_h, half_i):
    tm = x_ref.shape[0]
    x = x_ref[...]
    sh1 = mod_ref[0, 0:1, :]
    sc1 = mod_ref[0, 1:2, :]
    h = (_rms(x, gmix_ref[...]) * (1.0 + sc1) + sh1).astype(BF16)

    def proj(c0, width):
        return _dot(h, w_ref[:, c0:c0 + width])

    q_scale = HEAD_DIM ** -0.5 * LOG2E
    nblk = tm // MOBA_BLOCK
    qm = proj(_C_QM, MOBA_W)
    km = proj(_C_KM, MOBA_W)
    vm = proj(_C_VM, MOBA_W)
    for hd in range(MOBA_HEADS):
        sl = slice(hd * HEAD_DIM, (hd + 1) * HEAD_DIM)
        qmt_ref[sl, :] = (_rope(qm[:, sl], tabh_ref, half_h) * q_scale).T.astype(BF16)
        kr = _rope(km[:, sl], tabh_ref, half_h)
        km_ref[:, sl] = kr.astype(BF16)
        for blk in range(nblk):
            rows = slice(blk * MOBA_BLOCK, (blk + 1) * MOBA_BLOCK)
            kmean_ref[blk:blk + 1, sl] = jnp.mean(kr[rows], axis=0, keepdims=True)
            vmt_ref[blk, sl, :] = vm[rows, sl].T.astype(BF16)
    qd = proj(_C_QD, DSA_W)
    for hd in range(DSA_HEADS):
        sl = slice(hd * HEAD_DIM, (hd + 1) * HEAD_DIM)
        qdt_ref[sl, :] = (_rope(qd[:, sl], tabh_ref, half_h) * q_scale).T.astype(BF16)
    ckv = proj(_C_CKV, KV_LORA)
    kv = _dot(_rms(ckv, gkv_ref[...]).astype(BF16), wkv_ref[...])
    kd_ref[...] = _rope(kv[:, :HEAD_DIM], tabh_ref, half_h).astype(BF16)
    for ch in range(tm // DSA_KA):
        vdt_ref[ch] = kv[ch * DSA_KA:(ch + 1) * DSA_KA, HEAD_DIM:].T.astype(BF16)
    qi = proj(_C_QI, IDX_HEADS * IDX_DIM)
    for j in range(IDX_HEADS * IDX_DIM // LANES):
        sl = slice(j * LANES, (j + 1) * LANES)
        qit_ref[sl, :] = _rope(qi[:, sl], tabi_ref, half_i).T.astype(BF16)
    kw = proj(_C_KI, LANES)
    ki_ref[...] = _rope(kw, tabi_ref, half_i)[:, :IDX_DIM].astype(BF16)
    wit_ref[...] = kw.T[IDX_DIM:IDX_DIM + IDX_HEADS, :] * (IDX_HEADS ** -0.5 * IDX_DIM ** -0.5)


def _in_proj(x2, mod3, g_mix, w_in_p, g_kv, w_kv_up, tab_h, half_h, tab_i, half_i, S):
    T, D = x2.shape
    tm = PROJ_TM
    nt_per_seq = S // tm
    row = lambda i: (i, 0)
    col = lambda i: (0, i)
    nb = tm // MOBA_BLOCK
    nc = tm // DSA_KA
    outs = [
        jax.ShapeDtypeStruct((MOBA_W, T), BF16),
        jax.ShapeDtypeStruct((T, MOBA_W), BF16),
        jax.ShapeDtypeStruct((T // MOBA_BLOCK, MOBA_W, MOBA_BLOCK), BF16),
        jax.ShapeDtypeStruct((T // tm, nb, MOBA_W), F32),
        jax.ShapeDtypeStruct((DSA_W, T), BF16),
        jax.ShapeDtypeStruct((T, HEAD_DIM), BF16),
        jax.ShapeDtypeStruct((T // DSA_KA, HEAD_DIM, DSA_KA), BF16),
        jax.ShapeDtypeStruct((IDX_HEADS * IDX_DIM, T), BF16),
        jax.ShapeDtypeStruct((T, IDX_DIM), BF16),
        jax.ShapeDtypeStruct((IDX_HEADS, T), F32),
    ]
    out_specs = [
        pl.BlockSpec((MOBA_W, tm), col), pl.BlockSpec((tm, MOBA_W), row),
        pl.BlockSpec((nb, MOBA_W, MOBA_BLOCK), lambda i: (i, 0, 0)),
        pl.BlockSpec((None, nb, MOBA_W), lambda i: (i, 0, 0)),
        pl.BlockSpec((DSA_W, tm), col), pl.BlockSpec((tm, HEAD_DIM), row),
        pl.BlockSpec((nc, HEAD_DIM, DSA_KA), lambda i: (i, 0, 0)),
        pl.BlockSpec((IDX_HEADS * IDX_DIM, tm), col), pl.BlockSpec((tm, IDX_DIM), row),
        pl.BlockSpec((IDX_HEADS, tm), col),
    ]
    res = pl.pallas_call(
        functools.partial(_in_proj_kernel, half_h=half_h, half_i=half_i),
        out_shape=outs,
        grid=(T // tm,),
        in_specs=[
            pl.BlockSpec((tm, D), row),
            pl.BlockSpec((1, 6, D), lambda i: (i // nt_per_seq, 0, 0)),
            pl.BlockSpec((1, D), lambda i: (0, 0)),
            pl.BlockSpec((D, _C_END), lambda i: (0, 0)),
            pl.BlockSpec((1, KV_LORA), lambda i: (0, 0)),
            pl.BlockSpec((KV_LORA, 2 * HEAD_DIM), lambda i: (0, 0)),
            pl.BlockSpec((3, tm, LANES), lambda i: (0, i % nt_per_seq, 0)),
            pl.BlockSpec((3, tm, LANES), lambda i: (0, i % nt_per_seq, 0)),
        ],
        out_specs=out_specs,
        compiler_params=_cparams(("parallel",)),
        name="in_proj",
    )(x2, mod3, g_mix, w_in_p, g_kv, w_kv_up, tab_h, tab_i)
    res = list(res)
    res[3] = res[3].reshape(T // MOBA_BLOCK, MOBA_W)
    return res


def _moba_kernel(qt_ref, k_ref, vt_ref, kmean_ref, o_ref, bias_sc, *accs):
    qi = pl.program_id(1)
    blk = MOBA_BLOCK
    nb = kmean_ref.shape[0]
    heads = range(MOBA_HEADS)
    hsl = [slice(hd * HEAD_DIM, (hd + 1) * HEAD_DIM) for hd in heads]
    qts = [qt_ref[hsl[hd], :] for hd in heads]
    row = lax.broadcasted_iota(I32, (nb, blk), 0)
    past = row < qi
    start = pl.multiple_of(qi * blk, blk)
    k_io = lax.broadcasted_iota(I32, (blk, blk), 0)
    q_io = lax.broadcasted_iota(I32, (blk, blk), 1)

    gates = []
    for hd in heads:
        km = kmean_ref[:, hsl[hd]]
        km_hi = km.astype(BF16)
        km_lo = (km - km_hi.astype(F32)).astype(BF16)
        gate = _dot(km_hi, qts[hd]) + _dot(km_lo, qts[hd])
        gates.append(jnp.where(past, gate, -jnp.inf))
    k_own = k_ref[pl.ds(start, blk), :]
    own = [jnp.where(k_io <= q_io, _dot(k_own[:, hsl[hd]], qts[hd]), NEG) for hd in heads]
    biases = [jnp.full((nb, blk), NEG, F32) for _ in heads]
    for _ in range(MOBA_TOPK):
        for hd in heads:
            _, idx = _first_index_of_max(gates[hd], row, nb)
            hit = row == idx
            biases[hd] = jnp.where(hit, 0.0, biases[hd])
            gates[hd] = jnp.where(hit, -jnp.inf, gates[hd])
    init = []
    for hd in heads:
        bias_sc[hd] = jnp.where(past, biases[hd], NEG)
        m0 = _col_max(own[hd])
        p = jnp.exp2(own[hd] - m0)
        accs[hd][...] = _dot(vt_ref[qi, hsl[hd], :], p.astype(BF16))
        init += [m0, _col_sum(p)]

    def body(n, carry):
        st = pl.multiple_of(n * blk, blk)
        kb = k_ref[pl.ds(st, blk), :]
        sbs = [_dot(kb[:, hsl[hd]], qts[hd]) + bias_sc[hd, pl.ds(n, 1), :] for hd in heads]
        old = [accs[hd][...] for hd in heads]
        out, new = [], []
        for hd in heads:
            m_old, l_old = carry[2 * hd], carry[2 * hd + 1]
            m_new = jnp.maximum(m_old, _col_max(sbs[hd]))
            alpha = jnp.exp2(m_old - m_new)
            pb = jnp.exp2(sbs[hd] - m_new)
            new.append(alpha * old[hd] + _dot(vt_ref[n, hsl[hd], :], pb.astype(BF16)))
            out += [m_new, alpha * l_old + _col_sum(pb)]
        for hd in heads:
            accs[hd][...] = new[hd]
        return tuple(out)

    fin = lax.fori_loop(0, qi, body, tuple(init))
    for hd in heads:
        o_ref[:, hsl[hd]] = (accs[hd][...] * (1.0 / fin[2 * hd + 1])).T


def _moba_attn(q_mt, k_m, v_mt, kmean, B, S):
    T = B * S
    blk = MOBA_BLOCK
    nq = S // blk
    return pl.pallas_call(
        _moba_kernel,
        out_shape=jax.ShapeDtypeStruct((T, MOBA_W), F32),
        grid=(B, nq),
        in_specs=[
            pl.BlockSpec((MOBA_W, blk), lambda b, i: (0, b * nq + i)),
            pl.BlockSpec((S, MOBA_W), lambda b, i: (b, 0)),
            pl.BlockSpec((nq, MOBA_W, blk), lambda b, i: (b, 0, 0)),
            pl.BlockSpec((nq, MOBA_W), lambda b, i: (b, 0)),
        ],
        out_specs=pl.BlockSpec((blk, MOBA_W), lambda b, i: (b * nq + i, 0)),
        scratch_shapes=[pltpu.VMEM((MOBA_HEADS, nq, blk), F32)]
        + [pltpu.VMEM((HEAD_DIM, blk), F32) for _ in range(MOBA_HEADS)],
        compiler_params=_cparams(("parallel", "arbitrary")),
        name="moba_attn",
    )(q_mt, k_m, v_mt, kmean)


def _sortable_key(x):
    b = pltpu.bitcast(x, I32)
    return jnp.where(b >= 0, b, b ^ jnp.int32(0x7FFFFFFF))


def _dsa_kernel(qit_ref, wit_ref, qdt_ref, ki_ref, kd_ref, vdt_ref, o_ref, key_sc, high_sc, *accs, topk):
    t = pl.program_id(1)
    tq, kc = DSA_TQ, DSA_KC
    S = key_sc.shape[0]
    q0 = t * tq
    n_chunks = (q0 + tq + kc - 1) // kc
    key_io = lax.broadcasted_iota(I32, (kc, tq), 0)
    q_pos = q0 + lax.broadcasted_iota(I32, (kc, tq), 1)
    w = wit_ref[...]

    def score_chunk(c, carry):
        k0 = pl.multiple_of(c * kc, kc)
        kic = ki_ref[pl.ds(k0, kc), :]
        acc = jnp.zeros((kc, tq), F32)
        for hd in range(IDX_HEADS):
            lg = _dot(kic, qit_ref[hd * IDX_DIM:(hd + 1) * IDX_DIM, :])
            acc = acc + jnp.maximum(lg, 0.0) * w[hd:hd + 1, :]
        acc = jnp.where(acc == 0.0, 0.0, acc)
        key = _sortable_key(acc)
        key = jnp.where(k0 + key_io <= q_pos, key, INT_MIN)
        key_sc[pl.ds(k0, kc), :] = key
        high_sc[pl.ds(k0, kc), :] = lax.shift_right_arithmetic(key, 16).astype(jnp.int16)
        return carry

    lax.fori_loop(0, n_chunks, score_chunk, 0)

    ks = DSA_KS
    ks_io = lax.broadcasted_iota(I32, (ks, tq), 0)

    def column_total(slab_fn, dtype):
        rows = SUBLANES * (4 // jnp.dtype(dtype).itemsize)

        def cbody(c, accs):
            vals = slab_fn(pl.multiple_of(c * ks, ks))
            accs = list(accs)
            for i in range(ks // rows):
                j = i % REDUCE_CHAINS
                accs[j] = accs[j] + vals[i * rows:(i + 1) * rows]
            return tuple(accs)
        zero = jnp.zeros((rows, tq), dtype)
        accs = lax.fori_loop(0, n_chunks * (kc // ks), cbody, (zero,) * REDUCE_CHAINS)
        total = accs[0].astype(I32)
        for a in accs[1:]:
            total = total + a.astype(I32)
        return jnp.sum(total, axis=0, keepdims=True)

    def count(pred_fn):
        return column_total(lambda k0: jnp.where(pred_fn(key_sc[pl.ds(k0, ks), :], k0), 1, 0), I32)

    def count_ge_high(cand):
        c16 = lax.shift_right_arithmetic(cand, 16).astype(jnp.int16)
        one, zero = jnp.int16(1), jnp.int16(0)
        return column_total(lambda k0: jnp.where(high_sc[pl.ds(k0, ks), :] >= c16, one, zero), jnp.int16)

    thr0 = jnp.where(count_ge_high(jnp.zeros((1, tq), I32)) >= topk, 0, INT_MIN).astype(I32)

    def high_step(i, thr):
        cand = thr | lax.shift_left(jnp.int32(1), 30 - i)
        return jnp.where(count_ge_high(cand) >= topk, cand, thr)

    thr = lax.fori_loop(0, 15, high_step, thr0)

    high = lax.shift_right_arithmetic(thr, 16)
    h16 = high.astype(jnp.int16)
    one16, zero16 = jnp.int16(1), jnp.int16(0)
    n_above = column_total(lambda k0: jnp.where(high_sc[pl.ds(k0, ks), :] > h16, one16, zero16), jnp.int16)

    def repack(c, carry):
        k0 = pl.multiple_of(c * kc, kc)
        kk = key_sc[pl.ds(k0, kc), :]
        low = (kk & 0xFFFF) - 32768
        same = lax.shift_right_arithmetic(kk, 16) == high
        high_sc[pl.ds(k0, kc), :] = jnp.where(same, low, -32768).astype(jnp.int16)
        return carry

    lax.fori_loop(0, n_chunks, repack, 0)

    def low_step(i, low_bits):
        cand = low_bits | lax.shift_left(jnp.int32(1), 15 - i)
        c16 = (cand - 32768).astype(jnp.int16)
        cnt = n_above + column_total(
            lambda k0: jnp.where(high_sc[pl.ds(k0, ks), :] >= c16, one16, zero16), jnp.int16)
        return jnp.where(cnt >= topk, cand, low_bits)

    thr = thr | lax.fori_loop(0, 16, low_step, jnp.zeros((1, tq), I32))

    n_gt = count(lambda kk, k0: kk > thr)
    n_ge = count(lambda kk, k0: kk >= thr)
    need = topk - n_gt
    overflow = (n_ge > topk) & (thr != INT_MIN)
    any_overflow = jnp.max(jnp.where(overflow, 1, 0)) > 0
    nbits = max(1, (S - 1).bit_length())

    def cut_search():
        def step(i, lo):
            cand = lo | lax.shift_left(jnp.int32(1), nbits - 1 - i)
            cnt = count(lambda kk, k0: (kk == thr) & (k0 + ks_io < cand))
            return jnp.where(cnt >= need, lo, cand)
        return lax.fori_loop(0, nbits, step, jnp.zeros((1, tq), I32))

    jcut = lax.cond(any_overflow, cut_search, lambda: jnp.zeros((1, tq), I32))
    jcut = jnp.where(overflow, jcut, S)
    thr_ge = jnp.where(thr == INT_MIN, INT_MIN + 1, thr)

    pairs = range(DSA_HEADS // 2)
    qst = [jnp.concatenate([qdt_ref[(2 * g + j) * HEAD_DIM:(2 * g + j + 1) * HEAD_DIM, :] for j in range(2)],
                           axis=1) for g in pairs]
    for g in pairs:
        accs[g][...] = jnp.zeros(accs[g].shape, F32)

    ka = DSA_KA
    pos_io = lax.broadcasted_iota(I32, (ka, tq), 0)

    def attn_chunk(c, carry):
        k0 = pl.multiple_of(c * ka, ka)
        kk = key_sc[pl.ds(k0, ka), :]
        bias = lax.cond(
            any_overflow,
            lambda: jnp.where((kk > thr_ge) | ((kk == thr_ge) & (k0 + pos_io <= jcut)), 0.0, NEG),
            lambda: jnp.where(kk >= thr_ge, 0.0, NEG))
        bias2 = jnp.concatenate([bias, bias], axis=1)
        kdc = kd_ref[pl.ds(k0, ka), :]
        vtc = vdt_ref[c]
        ss = [_dot(kdc, qst[g]) + bias2 for g in pairs]
        old = [accs[g][...] for g in pairs]
        out, new = [], []
        for g in pairs:
            m_old, l_old = carry[2 * g], carry[2 * g + 1]
            m_new = jnp.maximum(m_old, _col_max(ss[g]))
            alpha = jnp.exp2(m_old - m_new)
            p = jnp.exp2(ss[g] - m_new)
            new.append(alpha * old[g] + _dot(vtc, p.astype(BF16)))
            out += [m_new, alpha * l_old + _col_sum(p)]
        for g in pairs:
            accs[g][...] = new[g]
        return tuple(out)

    init = (jnp.full((1, 2 * tq), NEG, F32), jnp.zeros((1, 2 * tq), F32)) * len(pairs)
    fin = lax.fori_loop(0, (q0 + tq + ka - 1) // ka, attn_chunk, init)
    for g in pairs:
        out_t = accs[g][...] * (1.0 / fin[2 * g + 1])
        for j in range(2):
            hd = 2 * g + j
            o_ref[:, hd * HEAD_DIM:(hd + 1) * HEAD_DIM] = out_t[:, j * tq:(j + 1) * tq].T


def _dsa_attn(q_it, w_it, q_dt, k_i, k_d, v_dt, B, S):
    T = B * S
    tq = DSA_TQ
    nq = S // tq
    topk = min(DSA_MAX_TOPK, S // 4)
    qcol = lambda b, t: (0, b * nq + t)
    seq = lambda b, t: (b, 0)
    return pl.pallas_call(
        functools.partial(_dsa_kernel, topk=topk),
        out_shape=jax.ShapeDtypeStruct((T, DSA_W), F32),
        grid=(B, nq),
        in_specs=[
            pl.BlockSpec((IDX_HEADS * IDX_DIM, tq), qcol),
            pl.BlockSpec((IDX_HEADS, tq), qcol),
            pl.BlockSpec((DSA_W, tq), qcol),
            pl.BlockSpec((S, IDX_DIM), seq),
            pl.BlockSpec((S, HEAD_DIM), seq),
            pl.BlockSpec((S // DSA_KA, HEAD_DIM, DSA_KA), lambda b, t: (b, 0, 0)),
        ],
        out_specs=pl.BlockSpec((tq, DSA_W), lambda b, t: (b * nq + t, 0)),
        scratch_shapes=[pltpu.VMEM((S, tq), I32), pltpu.VMEM((S, tq), jnp.int16)]
        + [pltpu.VMEM((HEAD_DIM, 2 * tq), F32) for _ in range(DSA_HEADS // 2)],
        compiler_params=_cparams(("parallel", "arbitrary")),
        name="dsa_attn",
    )(q_it, w_it, q_dt, k_i, k_d, v_dt)


def _first_index_of_max(v, row_io, n_rows):
    m = jnp.max(v, axis=0, keepdims=True)
    idx = jnp.min(jnp.where(v == m, row_io, n_rows), axis=0, keepdims=True)
    return m, idx


def _post_kernel(x_ref, om_ref, od_ref, mod_ref, gm_ref, gd_ref, wout_ref, gffn_ref, wgu_ref, wds_ref,
                 wrh_ref, wrl_ref, rb_ref, tri_ref,
                 xpart_ref, h2r_ref, eidx_ref, rank_ref, gate_ref, cnt_ref, base_sc):
    i = pl.program_id(0)
    tm = x_ref.shape[0]
    gt1 = mod_ref[0, 2:3, :]
    sh2 = mod_ref[0, 3:4, :]
    sc2 = mod_ref[0, 4:5, :]
    gt2 = mod_ref[0, 5:6, :]

    mixed = jnp.concatenate([_rms(om_ref[...], gm_ref[...]), _rms(od_ref[...], gd_ref[...])], axis=1)
    x1 = x_ref[...] + gt1 * _dot(mixed.astype(BF16), wout_ref[...])
    h2 = _rms(x1, gffn_ref[...]) * (1.0 + sc2) + sh2
    h2b = h2.astype(BF16)

    au = _dot(h2b, wgu_ref[...])
    hs = (_silu(au[:, :D_SHARED]) * au[:, D_SHARED:]).astype(BF16)
    xpart_ref[...] = x1 + gt2 * _dot(hs, wds_ref[...])

    for c, slab in enumerate(_pack_row_words(h2)):
        _store_word_slab(h2r_ref, 0, tm, c, slab)

    h2lo = (h2 - h2b.astype(F32)).astype(BF16)
    logits = _dot_nt(wrh_ref[...], h2b) + _dot_nt(wrl_ref[...], h2b) + _dot_nt(wrh_ref[...], h2lo)
    scores = 1.0 / (1.0 + jnp.exp(-logits))
    biased = scores + rb_ref[...]

    g_io = lax.broadcasted_iota(I32, (GROUP_SIZE, tm), 0)
    gs_rows = []
    for g in range(N_GROUPS):
        blk = biased[g * GROUP_SIZE:(g + 1) * GROUP_SIZE, :]
        m1, i1 = _first_index_of_max(blk, g_io, GROUP_SIZE)
        m2 = jnp.max(jnp.where(g_io == i1, -jnp.inf, blk), axis=0, keepdims=True)
        gs_rows.append(m1 + m2)
    gs = jnp.concatenate(gs_rows, axis=0)
    gi = lax.broadcasted_iota(I32, (N_GROUPS, tm), 0)
    grank = jnp.zeros((N_GROUPS, tm), I32)
    for m in range(N_GROUPS):
        gm = gs[m:m + 1, :]
        grank = grank + jnp.where((gm > gs) | ((gm == gs) & (m < gi)), 1, 0)
    gsel = grank < TOPK_GROUPS
    masked = jnp.concatenate(
        [jnp.where(gsel[g:g + 1, :], biased[g * GROUP_SIZE:(g + 1) * GROUP_SIZE, :], -jnp.inf)
         for g in range(N_GROUPS)], axis=0)

    e_io = lax.broadcasted_iota(I32, (N_EXPERTS, tm), 0)
    e_rows, s_rows = [], []
    for _ in range(EXPERT_TOPK):
        _, idx = _first_index_of_max(masked, e_io, N_EXPERTS)
        hit = e_io == idx
        e_rows.append(idx)
        s_rows.append(jnp.sum(jnp.where(hit, scores, 0.0), axis=0, keepdims=True))
        masked = jnp.where(hit, -jnp.inf, masked)
    eidx = jnp.concatenate(e_rows, axis=0)
    sk = jnp.concatenate(s_rows, axis=0)
    gate_ref[...] = sk / jnp.sum(sk, axis=0, keepdims=True) * ROUTED_SCALE
    eidx_ref[...] = eidx

    @pl.when(i == 0)
    def _():
        base_sc[...] = jnp.zeros(base_sc.shape, F32)

    chosen = jnp.zeros((N_EXPERTS, tm), F32)
    for k in range(EXPERT_TOPK):
        chosen = chosen + jnp.where(e_io == e_rows[k], 1.0, 0.0)
    incl = _dot(chosen.astype(BF16), tri_ref[...])
    pos = base_sc[...] + incl - 1.0
    rank_ref[...] = jnp.concatenate(
        [jnp.sum(jnp.where(e_io == e_rows[k], pos, 0.0), axis=0, keepdims=True)
         for k in range(EXPERT_TOPK)], axis=0).astype(I32)
    base_sc[...] = base_sc[...] + incl[:, tm - 1:tm]
    cnt_ref[...] = jnp.broadcast_to(base_sc[...], cnt_ref.shape)


def _post_attn(x2, o_m, o_d, mod3, g_moba, g_dsa, w_out, g_ffn, w_gu_s, w_down_s, wr_hi, wr_lo, rbias, S):
    T, D = x2.shape
    tm = POST_TM
    nt_per_seq = S // tm
    row = lambda i: (i, 0)
    full = lambda i: (0, 0)
    tri = (jnp.arange(tm)[:, None] <= jnp.arange(tm)[None, :]).astype(BF16)
    n_words = D // ROW_WORDS
    return pl.pallas_call(
        _post_kernel,
        out_shape=[
            jax.ShapeDtypeStruct((T, D), F32),
            jax.ShapeDtypeStruct((T, n_words, LANES), I32),
            jax.ShapeDtypeStruct((EXPERT_TOPK, T), I32),
            jax.ShapeDtypeStruct((EXPERT_TOPK, T), I32),
            jax.ShapeDtypeStruct((EXPERT_TOPK, T), F32),
            jax.ShapeDtypeStruct((N_EXPERTS, LANES), F32),
        ],
        grid=(T // tm,),
        in_specs=[
            pl.BlockSpec((tm, D), row),
            pl.BlockSpec((tm, MOBA_W), row),
            pl.BlockSpec((tm, DSA_W), row),
            pl.BlockSpec((1, 6, D), lambda i: (i // nt_per_seq, 0, 0)),
            pl.BlockSpec((1, MOBA_W), full),
            pl.BlockSpec((1, DSA_W), full),
            pl.BlockSpec(w_out.shape, full),
            pl.BlockSpec((1, D), full),
            pl.BlockSpec(w_gu_s.shape, full),
            pl.BlockSpec(w_down_s.shape, full),
            pl.BlockSpec(wr_hi.shape, full),
            pl.BlockSpec(wr_lo.shape, full),
            pl.BlockSpec((N_EXPERTS, 1), full),
            pl.BlockSpec((tm, tm), full),
        ],
        out_specs=[
            pl.BlockSpec((tm, D), row),
            pl.BlockSpec((tm, n_words, LANES), lambda i: (i, 0, 0)),
            pl.BlockSpec((EXPERT_TOPK, tm), lambda i: (0, i)),
            pl.BlockSpec((EXPERT_TOPK, tm), lambda i: (0, i)),
            pl.BlockSpec((EXPERT_TOPK, tm), lambda i: (0, i)),
            pl.BlockSpec((N_EXPERTS, LANES), full),
        ],
        scratch_shapes=[pltpu.VMEM((N_EXPERTS, 1), F32)],
        compiler_params=_cparams(("arbitrary",)),
        name="post_attn",
    )(x2, o_m, o_d, mod3, g_moba, g_dsa, w_out, g_ffn, w_gu_s, w_down_s, wr_hi, wr_lo, rbias, tri)


def _row_copy_wait(rows_hbm, n_rows, sem):
    blk = rows_hbm.at[pl.ds(0, n_rows)]
    pltpu.make_async_copy(blk, blk, sem).wait()


def _slots_kernel(ps_ref, e_ref, r_ref, d_ref):
    e = e_ref[...]

    def body(x, acc):
        return jnp.where(e == x, ps_ref[x], acc)

    d_ref[...] = lax.fori_loop(0, N_EXPERTS, body, jnp.zeros(e.shape, I32)) + r_ref[...]


def _slots(pad_starts, eidx, rank):
    K, T = eidx.shape
    tm = min(T, 4096)
    blk = lambda: pl.BlockSpec((K, tm), lambda i, ps: (0, i))
    return pl.pallas_call(
        _slots_kernel,
        out_shape=jax.ShapeDtypeStruct((K, T), I32),
        grid_spec=pltpu.PrefetchScalarGridSpec(num_scalar_prefetch=1, grid=(T // tm,),
                                               in_specs=[blk(), blk()], out_specs=blk()),
        compiler_params=_cparams(("parallel",)),
        name="slots",
    )(pad_starts, eidx, rank)


def _dispatch_kernel(d_ref, h2r_ref, xs_ref, sem):
    tm = h2r_ref.shape[0]

    def body(t, carry):
        for k in range(EXPERT_TOPK):
            pltpu.make_async_copy(h2r_ref.at[t], xs_ref.at[d_ref[t * EXPERT_TOPK + k]],
                                  sem).start(priority=k % 2)
        return carry

    lax.fori_loop(0, tm, body, 0)
    _row_copy_wait(xs_ref, tm * EXPERT_TOPK, sem)


def _dispatch(dest_flat, h2r, n_rows_padded):
    T, n_words, _ = h2r.shape
    tm = DISP_TM
    return pl.pallas_call(
        _dispatch_kernel,
        out_shape=jax.ShapeDtypeStruct((n_rows_padded, n_words, LANES), I32),
        grid=(T // tm,),
        in_specs=[pl.BlockSpec((tm * EXPERT_TOPK,), lambda i: (i,), memory_space=pltpu.SMEM),
                  pl.BlockSpec((tm, n_words, LANES), lambda i: (i, 0, 0))],
        out_specs=pl.BlockSpec(memory_space=pl.ANY),
        scratch_shapes=[pltpu.SemaphoreType.DMA],
        compiler_params=_cparams(("arbitrary",), disable_bounds_checks=True),
        name="dispatch",
    )(dest_flat, h2r)


def _expert_kernel(ps_ref, cnt_ref, wg_ref, wu_ref, wd_ref, xs_ref, ys_ref,
                   wg_sc, wu_sc, wd_sc, xbuf, ybuf, sem_in, sem_out):
    e = pl.program_id(0)
    bm = EXP_BM
    n_words = xbuf.shape[1]
    cnt = cnt_ref[e]
    g0 = ps_ref[e] // bm
    nb = (cnt + bm - 1) // bm
    n_used = (ps_ref[N_EXPERTS - 1] + cnt_ref[N_EXPERTS - 1] + bm - 1) // bm

    def in_copy(g):
        slot = lax.rem(g, EXP_IN_SLOTS)
        return pltpu.make_async_copy(xs_ref.at[pl.ds(g * bm, bm)], xbuf.at[pl.ds(slot * bm, bm)],
                                     sem_in.at[slot])

    def out_copy(g):
        slot = lax.rem(g, EXP_OUT_SLOTS)
        return pltpu.make_async_copy(ybuf.at[pl.ds(slot * bm, bm)], ys_ref.at[pl.ds(g * bm, bm)],
                                     sem_out.at[slot])

    @pl.when(e == 0)
    def _():
        for g in range(EXP_AHEAD):
            @pl.when(g < n_used)
            def _():
                in_copy(g).start()

    @pl.when(nb > 0)
    def _():
        wg_sc[...] = wg_ref[...].astype(BF16)
        wu_sc[...] = wu_ref[...].astype(BF16)
        wd_sc[...] = wd_ref[...].astype(BF16)

        def body(b, carry):
            g = g0 + b

            @pl.when(g + EXP_AHEAD < n_used)
            def _():
                in_copy(g + EXP_AHEAD).start()

            in_copy(g).wait()

            @pl.when(g >= EXP_OUT_SLOTS)
            def _():
                out_copy(g - EXP_OUT_SLOTS).wait()

            xrow = lax.rem(g, EXP_IN_SLOTS) * bm
            yrow = lax.rem(g, EXP_OUT_SLOTS) * bm
            feats = []
            for c in range(n_words):
                feats += _unpack_row_words(_load_word_slab(xbuf, xrow, bm, c))
            x = jnp.concatenate(feats, axis=1)
            valid = lax.broadcasted_iota(I32, (bm, 1), 0) < cnt - b * bm
            xb = jnp.where(valid, x, 0.0).astype(BF16)
            a = _dot(xb, wg_sc[...])
            u = _dot(xb, wu_sc[...])
            hmid = (_silu(a) * u).astype(BF16)
            ob = _dot(hmid, wd_sc[...])
            for c, slab in enumerate(_pack_row_words(ob)):
                _store_word_slab(ybuf, yrow, bm, c, slab)
            out_copy(g).start()
            return carry

        lax.fori_loop(0, nb, body, 0)

    @pl.when(e == N_EXPERTS - 1)
    def _():
        for back in range(EXP_OUT_SLOTS, 0, -1):
            @pl.when(n_used >= back)
            def _():
                out_copy(n_used - back).wait()


def _experts(pad_starts, counts, xs, w_gate_e, w_up_e, w_down_e):
    E, D, DE = w_gate_e.shape
    n_words = xs.shape[1]
    bm = EXP_BM
    wsel = lambda e, ps, cnt: (e, 0, 0)
    anyspec = pl.BlockSpec(memory_space=pl.ANY)
    return pl.pallas_call(
        _expert_kernel,
        out_shape=jax.ShapeDtypeStruct(xs.shape, I32),
        grid_spec=pltpu.PrefetchScalarGridSpec(
            num_scalar_prefetch=2,
            grid=(E,),
            in_specs=[pl.BlockSpec((None, D, DE), wsel), pl.BlockSpec((None, D, DE), wsel),
                      pl.BlockSpec((None, DE, D), wsel), anyspec],
            out_specs=anyspec,
            scratch_shapes=[pltpu.VMEM((D, DE), BF16), pltpu.VMEM((D, DE), BF16), pltpu.VMEM((DE, D), BF16),
                            pltpu.VMEM((EXP_IN_SLOTS * bm, n_words, LANES), I32),
                            pltpu.VMEM((EXP_OUT_SLOTS * bm, n_words, LANES), I32),
                            pltpu.SemaphoreType.DMA((EXP_IN_SLOTS,)), pltpu.SemaphoreType.DMA((EXP_OUT_SLOTS,))],
        ),
        compiler_params=_cparams(("arbitrary",)),
        name="experts",
    )(pad_starts, counts, w_gate_e, w_up_e, w_down_e, xs)


def _combine_kernel(dcur_ref, dnxt_ref, g_ref, xpart_ref, mod_ref, gfin_ref, ys_ref, o_ref,
                    buf0, buf1, sem0, sem1):
    i = pl.program_id(0)
    n_steps = pl.num_programs(0)
    tm = xpart_ref.shape[0]
    n_words = buf0.shape[1]

    def issue(dest_ref, buf, sem):
        for t in range(tm):
            for k in range(EXPERT_TOPK):
                pltpu.make_async_copy(ys_ref.at[dest_ref[t * EXPERT_TOPK + k]], buf.at[k * tm + t],
                                      sem).start(priority=k % 2)

    def reduce_tile(buf):
        gt2 = mod_ref[0, 5:6, :]
        g = g_ref[...]
        cols = []
        for c in range(n_words):
            lo = jnp.zeros((tm, LANES), F32)
            hi = jnp.zeros((tm, LANES), F32)
            for k in range(EXPERT_TOPK):
                a, b = _unpack_row_words(_load_word_slab(buf, k * tm, tm, c))
                gk = g[:, k:k + 1]
                lo = lo + gk * a
                hi = hi + gk * b
            cols += [lo, hi]
        routed = jnp.concatenate(cols, axis=1)
        o_ref[...] = _rms(xpart_ref[...] + gt2 * routed, gfin_ref[...])

    @pl.when(i == 0)
    def _():
        issue(dcur_ref, buf0, sem0)

    for parity, (cur, nxt) in enumerate((((buf0, sem0), (buf1, sem1)), ((buf1, sem1), (buf0, sem0)))):
        @pl.when(i % 2 == parity)
        def _():
            _row_copy_wait(ys_ref, tm * EXPERT_TOPK, cur[1])
            issue(dnxt_ref, *nxt)
            reduce_tile(cur[0])

            @pl.when(i + 1 == n_steps)
            def _():
                _row_copy_wait(ys_ref, tm * EXPERT_TOPK, nxt[1])


def _combine(dest, gates_t, xpart, mod3, g_final, ys, S):
    T, D = xpart.shape
    tm = COMB_TM
    n_steps = T // tm
    nt_per_seq = S // tm
    n_words = ys.shape[1]
    return pl.pallas_call(
        _combine_kernel,
        out_shape=jax.ShapeDtypeStruct((T, D), F32),
        grid=(n_steps,),
        in_specs=[
            pl.BlockSpec((tm * EXPERT_TOPK,), lambda i: (i,), memory_space=pltpu.SMEM),
            pl.BlockSpec((tm * EXPERT_TOPK,), lambda i: (jnp.minimum(i + 1, n_steps - 1),),
                         memory_space=pltpu.SMEM),
            pl.BlockSpec((tm, EXPERT_TOPK), lambda i: (i, 0)),
            pl.BlockSpec((tm, D), lambda i: (i, 0)),
            pl.BlockSpec((1, 6, D), lambda i: (i // nt_per_seq, 0, 0)),
            pl.BlockSpec((1, D), lambda i: (0, 0)),
            pl.BlockSpec(memory_space=pl.ANY),
        ],
        out_specs=pl.BlockSpec((tm, D), lambda i: (i, 0)),
        scratch_shapes=[pltpu.VMEM((EXPERT_TOPK * tm, n_words, LANES), I32),
                        pltpu.VMEM((EXPERT_TOPK * tm, n_words, LANES), I32),
                        pltpu.SemaphoreType.DMA, pltpu.SemaphoreType.DMA],
        compiler_params=_cparams(("arbitrary",), disable_bounds_checks=True),
        name="combine",
    )(dest, dest, gates_t, xpart, mod3, g_final, ys)


def _layer(x2, mod3, S, g_mix, w_in, g_kv, w_kv_up, g_moba_out, g_dsa_out, w_out, g_ffn, w_router,
           router_bias, w_gate_e, w_up_e, w_down_e, w_gate_s, w_up_s, w_down_s, g_final, tab_h, half_h,
           tab_i, half_i):
    T, D = x2.shape
    B = T // S
    w_in_p = jnp.pad(w_in, ((0, 0), (0, _C_END - w_in.shape[1]))).astype(BF16)
    (q_mt, k_m, v_mt, kmean, q_dt, k_d, v_dt, q_it, k_i, w_it) = _in_proj(
        x2, mod3, g_mix.reshape(1, D), w_in_p, g_kv.reshape(1, KV_LORA), w_kv_up.astype(BF16),
        tab_h, half_h, tab_i, half_i, S)
    o_m = _moba_attn(q_mt, k_m, v_mt, kmean, B, S)
    o_d = _dsa_attn(q_it, w_it, q_dt, k_i, k_d, v_dt, B, S)

    wr_t = w_router.T
    wr_hi = wr_t.astype(BF16)
    wr_lo = (wr_t - wr_hi.astype(F32)).astype(BF16)
    w_gu_s = jnp.concatenate([w_gate_s, w_up_s], axis=1).astype(BF16)
    xpart, h2r, eidx, rank, gates, cnt = _post_attn(
        x2, o_m, o_d, mod3, g_moba_out.reshape(1, MOBA_W), g_dsa_out.reshape(1, DSA_W), w_out.astype(BF16),
        g_ffn.reshape(1, D), w_gu_s, w_down_s.astype(BF16), wr_hi, wr_lo,
        router_bias.reshape(N_EXPERTS, 1), S)

    bm = EXP_BM
    n_blocks = T * EXPERT_TOPK // bm + N_EXPERTS
    counts = cnt[:, 0].astype(I32)
    padded = (counts + bm - 1) // bm * bm
    pad_ends = jnp.cumsum(padded)
    pad_starts = (pad_ends - padded).astype(I32)
    dest_flat = _slots(pad_starts, eidx, rank).T.reshape(-1)
    xs = _dispatch(dest_flat, h2r, n_blocks * bm)
    ys = _experts(pad_starts, counts, xs, w_gate_e, w_up_e, w_down_e)
    return _combine(dest_flat, gates.T, xpart, mod3, g_final.reshape(1, D), ys, S)


def kernel(x, c, w_ada, b_ada, g_mix, w_in, g_kv, w_kv_up, g_moba_out, g_dsa_out, w_out, g_ffn, w_router,
           router_bias, w_gate_e, w_up_e, w_down_e, w_gate_s, w_up_s, w_down_s, g_final):
    B, S, D = x.shape
    depth = w_ada.shape[0]
    assert depth == 1, "the final norm is fused into the single layer"
    assert S % PROJ_TM == 0 and S % DSA_KC == 0 and S % POST_TM == 0 and S >= 4 * DSA_MAX_TOPK
    tab_h, half_h = _rope_tables(S, HEAD_DIM, 1)
    tab_i, half_i = _rope_tables(S, IDX_DIM, LANES // IDX_DIM)
    x2 = x.reshape(B * S, D)
    sq = lambda a: a.reshape(a.shape[1:])
    mod3 = _ada_mod(c, sq(w_ada), sq(b_ada)).reshape(B, 6, D)
    out = _layer(x2, mod3, S, sq(g_mix), sq(w_in), sq(g_kv), sq(w_kv_up), sq(g_moba_out), sq(g_dsa_out),
                 sq(w_out), sq(g_ffn), sq(w_router), sq(router_bias), sq(w_gate_e), sq(w_up_e), sq(w_down_e),
                 sq(w_gate_s), sq(w_up_s), sq(w_down_s), g_final, tab_h, half_h, tab_i, half_i)
    return out.reshape(B, S, D)
```

```python
import functools

import jax
import jax.numpy as jnp
from jax import lax
from jax.experimental import pallas as pl
from jax.experimental.pallas import tpu as pltpu

HEAD_DIM = 128
MOBA_HEADS = 4
DSA_HEADS = 4
MOBA_W = MOBA_HEADS * HEAD_DIM
DSA_W = DSA_HEADS * HEAD_DIM
MOBA_BLOCK = 256
MOBA_TOPK = 3
DSA_MAX_TOPK = 256
KV_LORA = 256
IDX_HEADS = 8
IDX_DIM = 64
ROPE_THETA = 500000.0
ROPE_FRACTION_DIV = 4
N_EXPERTS = 256
EXPERT_TOPK = 8
N_GROUPS = 8
TOPK_GROUPS = 4
GROUP_SIZE = N_EXPERTS // N_GROUPS
D_EXPERT = 256
D_SHARED = 256
ROUTED_SCALE = 2.5
EPS = 1e-6

LANES = 128
SUBLANES = 8
VMEM_LIMIT = 56 * 1024 * 1024

PROJ_TM = 512
DSA_TQ = 256
DSA_KC = 512
DSA_KA = 256
DSA_KS = 128
POST_TM = 512
DISP_TM = 256
EXP_BM = 256
EXP_AHEAD = 3
EXP_IN_SLOTS = EXP_AHEAD + 1
EXP_OUT_SLOTS = 2
COMB_TM = 128
REDUCE_CHAINS = 4
DENOM_ROWS = 16
NEG = -1e30
INT_MIN = -2147483648
LOG2E = 1.4426950408889634

F32 = jnp.float32
BF16 = jnp.bfloat16
I32 = jnp.int32


def _cparams(sem, **kw):
    return pltpu.CompilerParams(dimension_semantics=sem, vmem_limit_bytes=VMEM_LIMIT, **kw)


def _dot(a, b):
    return jnp.dot(a, b, preferred_element_type=F32)


def _dot_nt(a, b):
    return lax.dot_general(a, b, (((1,), (1,)), ((), ())), preferred_element_type=F32)


def _silu(x):
    return x * (1.0 / (1.0 + jnp.exp(-x)))


def _rms(x, g):
    return x * lax.rsqrt(jnp.mean(x * x, axis=-1, keepdims=True) + EPS) * g


def _rows_to_tile(op, x):
    parts = [x[i:i + SUBLANES] for i in range(0, x.shape[0], SUBLANES)]
    n_chains = min(REDUCE_CHAINS, len(parts))
    accs = parts[:n_chains]
    for i in range(n_chains, len(parts)):
        accs[i % n_chains] = op(accs[i % n_chains], parts[i])
    while len(accs) > 1:
        accs = [op(accs[i], accs[i + 1]) for i in range(0, len(accs) - 1, 2)] + ([accs[-1]] if len(accs) % 2 else [])
    return accs[0]


def _col_max(x):
    return jnp.max(_rows_to_tile(jnp.maximum, x), axis=0, keepdims=True)


def _col_sum(x):
    return jnp.sum(_rows_to_tile(jnp.add, x), axis=0, keepdims=True)


ROW_WORDS = 2 * LANES


def _pack_row_words(x):
    slabs = []
    for c in range(x.shape[1] // ROW_WORDS):
        lo = pltpu.bitcast(x[:, c * ROW_WORDS:c * ROW_WORDS + LANES].astype(BF16).astype(F32), I32)
        hi = pltpu.bitcast(x[:, c * ROW_WORDS + LANES:(c + 1) * ROW_WORDS].astype(BF16).astype(F32), I32)
        slabs.append(lax.shift_right_logical(lo, 16) | hi)
    return slabs


def _word_slab_index(rows_ref, row0, n_rows, c):
    n_words = rows_ref.shape[1]
    flat = rows_ref.reshape(rows_ref.shape[0] * n_words, LANES)
    return flat, pl.ds(row0 * n_words + c, n_rows, stride=n_words)


def _load_word_slab(rows_ref, row0, n_rows, c):
    flat, idx = _word_slab_index(rows_ref, row0, n_rows, c)
    return flat[idx, :]


def _store_word_slab(rows_ref, row0, n_rows, c, value):
    flat, idx = _word_slab_index(rows_ref, row0, n_rows, c)
    flat[idx, :] = value


def _unpack_row_words(u):
    return pltpu.bitcast(lax.shift_left(u, 16), F32), pltpu.bitcast(u & jnp.int32(-65536), F32)


def _ada_kernel(c_ref, w_ref, b_ref, o_ref):
    ca = _silu(c_ref[...])
    o_ref[...] = jnp.dot(ca, w_ref[...], preferred_element_type=F32,
                         precision=lax.Precision.HIGHEST) + b_ref[...]


def _ada_mod(c, w_ada, b_ada):
    B, D = c.shape
    N = w_ada.shape[1]
    tn = 1024
    return pl.pallas_call(
        _ada_kernel,
        out_shape=jax.ShapeDtypeStruct((B, N), F32),
        grid=(N // tn,),
        in_specs=[pl.BlockSpec((B, D), lambda j: (0, 0)),
                  pl.BlockSpec((D, tn), lambda j: (0, j)),
                  pl.BlockSpec((1, tn), lambda j: (0, j))],
        out_specs=pl.BlockSpec((B, tn), lambda j: (0, j)),
        compiler_params=_cparams(("arbitrary",)),
        name="ada_mod",
    )(c, w_ada, b_ada.reshape(1, N))


def _rope_tables(seq, head_dim, heads_per_vreg):
    rot = head_dim // ROPE_FRACTION_DIV
    half = rot // 2
    inv = jnp.float32(ROPE_THETA) ** (-(jnp.arange(0, rot, 2, dtype=F32) / rot))
    ang = jnp.arange(seq, dtype=F32)[:, None] * inv[None, :]
    cos, sin = jnp.cos(ang), jnp.sin(ang)
    ones = jnp.ones((seq, head_dim - rot), F32)
    zeros_h = jnp.zeros((seq, half), F32)
    zeros_r = jnp.zeros((seq, head_dim - rot), F32)
    c = jnp.concatenate([cos, cos, ones], axis=1)
    sp = jnp.concatenate([zeros_h, sin, zeros_r], axis=1)
    sm = jnp.concatenate([-sin, zeros_h, zeros_r], axis=1)
    rep = lambda t: jnp.tile(t, (1, heads_per_vreg))
    return jnp.stack([rep(c), rep(sp), rep(sm)], axis=0), half


def _rope(x, tab_ref, half):
    return (x * tab_ref[0] + pltpu.roll(x, half, 1) * tab_ref[1]
            + pltpu.roll(x, LANES - half, 1) * tab_ref[2])


_C_QM, _C_KM, _C_VM, _C_QD = 0, MOBA_W, 2 * MOBA_W, 3 * MOBA_W
_C_CKV = 3 * MOBA_W + DSA_W
_C_QI = _C_CKV + KV_LORA
_C_KI = _C_QI + IDX_HEADS * IDX_DIM
_C_END = _C_KI + LANES


def _in_proj_kernel(x_ref, mod_ref, gmix_ref, w_ref, gkv_ref, wkv_ref, tabh_ref, tabi_ref,
                    qmt_ref, km_ref, vmt_ref, kmean_ref, qdt_ref, kd_ref, vdt_ref, qit_ref, ki_ref, wit_ref,
                    *, half_h, half_i):
    tm = x_ref.shape[0]
    x = x_ref[...]
    sh1 = mod_ref[0, 0:1, :]
    sc1 = mod_ref[0, 1:2, :]
    h = (_rms(x, gmix_ref[...]) * (1.0 + sc1) + sh1).astype(BF16)

    def proj(c0, width):
        return _dot(h, w_ref[:, c0:c0 + width])

    q_scale = HEAD_DIM ** -0.5 * LOG2E
    nblk = tm // MOBA_BLOCK
    qm = proj(_C_QM, MOBA_W)
    km = proj(_C_KM, MOBA_W)
    vm = proj(_C_VM, MOBA_W)
    for hd in range(MOBA_HEADS):
        sl = slice(hd * HEAD_DIM, (hd + 1) * HEAD_DIM)
        qmt_ref[sl, :] = (_rope(qm[:, sl], tabh_ref, half_h) * q_scale).T.astype(BF16)
        kr = _rope(km[:, sl], tabh_ref, half_h)
        km_ref[:, sl] = kr.astype(BF16)
        for blk in range(nblk):
            rows = slice(blk * MOBA_BLOCK, (blk + 1) * MOBA_BLOCK)
            kmean_ref[blk:blk + 1, sl] = jnp.mean(kr[rows], axis=0, keepdims=True)
            vmt_ref[blk, sl, :] = vm[rows, sl].T.astype(BF16)
    qd = proj(_C_QD, DSA_W)
    for hd in range(DSA_HEADS):
        sl = slice(hd * HEAD_DIM, (hd + 1) * HEAD_DIM)
        qdt_ref[sl, :] = (_rope(qd[:, sl], tabh_ref, half_h) * q_scale).T.astype(BF16)
    ckv = proj(_C_CKV, KV_LORA)
    kv = _dot(_rms(ckv, gkv_ref[...]).astype(BF16), wkv_ref[...])
    kd_ref[...] = _rope(kv[:, :HEAD_DIM], tabh_ref, half_h).astype(BF16)
    for ch in range(tm // DSA_KA):
        vdt_ref[ch] = kv[ch * DSA_KA:(ch + 1) * DSA_KA, HEAD_DIM:].T.astype(BF16)
    qi = proj(_C_QI, IDX_HEADS * IDX_DIM)
    for j in range(IDX_HEADS * IDX_DIM // LANES):
        sl = slice(j * LANES, (j + 1) * LANES)
        qit_ref[sl, :] = _rope(qi[:, sl], tabi_ref, half_i).T.astype(BF16)
    kw = proj(_C_KI, LANES)
    ki_ref[...] = _rope(kw, tabi_ref, half_i)[:, :IDX_DIM].astype(BF16)
    wit_ref[...] = kw.T[IDX_DIM:IDX_DIM + IDX_HEADS, :] * (IDX_HEADS ** -0.5 * IDX_DIM ** -0.5)


def _in_proj(x2, mod3, g_mix, w_in_p, g_kv, w_kv_up, tab_h, half_h, tab_i, half_i, S):
    T, D = x2.shape
    tm = PROJ_TM
    nt_per_seq = S // tm
    row = lambda i: (i, 0)
    col = lambda i: (0, i)
    nb = tm // MOBA_BLOCK
    nc = tm // DSA_KA
    outs = [
        jax.ShapeDtypeStruct((MOBA_W, T), BF16),
        jax.ShapeDtypeStruct((T, MOBA_W), BF16),
        jax.ShapeDtypeStruct((T // MOBA_BLOCK, MOBA_W, MOBA_BLOCK), BF16),
        jax.ShapeDtypeStruct((T // tm, nb, MOBA_W), F32),
        jax.ShapeDtypeStruct((DSA_W, T), BF16),
        jax.ShapeDtypeStruct((T, HEAD_DIM), BF16),
        jax.ShapeDtypeStruct((T // DSA_KA, HEAD_DIM, DSA_KA), BF16),
        jax.ShapeDtypeStruct((IDX_HEADS * IDX_DIM, T), BF16),
        jax.ShapeDtypeStruct((T, IDX_DIM), BF16),
        jax.ShapeDtypeStruct((IDX_HEADS, T), F32),
    ]
    out_specs = [
        pl.BlockSpec((MOBA_W, tm), col), pl.BlockSpec((tm, MOBA_W), row),
        pl.BlockSpec((nb, MOBA_W, MOBA_BLOCK), lambda i: (i, 0, 0)),
        pl.BlockSpec((None, nb, MOBA_W), lambda i: (i, 0, 0)),
        pl.BlockSpec((DSA_W, tm), col), pl.BlockSpec((tm, HEAD_DIM), row),
        pl.BlockSpec((nc, HEAD_DIM, DSA_KA), lambda i: (i, 0, 0)),
        pl.BlockSpec((IDX_HEADS * IDX_DIM, tm), col), pl.BlockSpec((tm, IDX_DIM), row),
        pl.BlockSpec((IDX_HEADS, tm), col),
    ]
    res = pl.pallas_call(
        functools.partial(_in_proj_kernel, half_h=half_h, half_i=half_i),
        out_shape=outs,
        grid=(T // tm,),
        in_specs=[
            pl.BlockSpec((tm, D), row),
            pl.BlockSpec((1, 6, D), lambda i: (i // nt_per_seq, 0, 0)),
            pl.BlockSpec((1, D), lambda i: (0, 0)),
            pl.BlockSpec((D, _C_END), lambda i: (0, 0)),
            pl.BlockSpec((1, KV_LORA), lambda i: (0, 0)),
            pl.BlockSpec((KV_LORA, 2 * HEAD_DIM), lambda i: (0, 0)),
            pl.BlockSpec((3, tm, LANES), lambda i: (0, i % nt_per_seq, 0)),
            pl.BlockSpec((3, tm, LANES), lambda i: (0, i % nt_per_seq, 0)),
        ],
        out_specs=out_specs,
        compiler_params=_cparams(("parallel",)),
        name="in_proj",
    )(x2, mod3, g_mix, w_in_p, g_kv, w_kv_up, tab_h, tab_i)
    res = list(res)
    res[3] = res[3].reshape(T // MOBA_BLOCK, MOBA_W)
    return res


def _moba_kernel(qt_ref, k_ref, vt_ref, kmean_ref, o_ref, bias_sc, *head_scratch):
    accs, s_scs = head_scratch[:MOBA_HEADS], head_scratch[MOBA_HEADS:]
    qi = pl.program_id(1)
    blk = MOBA_BLOCK
    nb = kmean_ref.shape[0]
    heads = range(MOBA_HEADS)
    hsl = [slice(hd * HEAD_DIM, (hd + 1) * HEAD_DIM) for hd in heads]
    qts = [qt_ref[hsl[hd], :] for hd in heads]
    row = lax.broadcasted_iota(I32, (nb, blk), 0)
    past = row < qi
    start = pl.multiple_of(qi * blk, blk)
    k_io = lax.broadcasted_iota(I32, (blk, blk), 0)
    q_io = lax.broadcasted_iota(I32, (blk, blk), 1)

    gates = []
    for hd in heads:
        km = kmean_ref[:, hsl[hd]]
        km_hi = km.astype(BF16)
        km_lo = (km - km_hi.astype(F32)).astype(BF16)
        gate = _dot(km_hi, qts[hd]) + _dot(km_lo, qts[hd])
        gates.append(jnp.where(past, gate, -jnp.inf))
    k_own = k_ref[pl.ds(start, blk), :]
    own = [jnp.where(k_io <= q_io, _dot(k_own[:, hsl[hd]], qts[hd]), NEG) for hd in heads]
    biases = [jnp.full((nb, blk), NEG, F32) for _ in heads]
    for _ in range(MOBA_TOPK):
        for hd in heads:
            _, idx = _first_index_of_max(gates[hd], row, nb)
            hit = row == idx
            biases[hd] = jnp.where(hit, 0.0, biases[hd])
            gates[hd] = jnp.where(hit, -jnp.inf, gates[hd])
    ones_rows = jnp.ones((DENOM_ROWS, blk), BF16)

    def v_aug(n, hd):
        return jnp.concatenate([vt_ref[n, hsl[hd], :], ones_rows], axis=0)

    init = []
    for hd in heads:
        bias_sc[hd] = jnp.where(past, biases[hd], NEG)
        m0 = _col_max(own[hd])
        p = jnp.exp2((own[hd] - m0).astype(BF16))
        accs[hd][...] = _dot(v_aug(qi, hd), p)
        init.append(m0)

    def masked_scores(n):
        kb = k_ref[pl.ds(pl.multiple_of(n * blk, blk), blk), :]
        return [_dot(kb[:, hsl[hd]], qts[hd]) + bias_sc[hd, pl.ds(n, 1), :] for hd in heads]

    for hd, s0 in enumerate(masked_scores(0)):
        s_scs[hd][0] = s0

    def body(n, carry):
        slot = lax.rem(n, 2)
        sbs = [s_scs[hd][slot] for hd in heads]
        nxt = masked_scores(jnp.minimum(n + 1, qi - 1))
        out = []
        for hd in heads:
            m_old = carry[hd]
            m_new = jnp.maximum(m_old, _col_max(sbs[hd]))
            alpha = jnp.exp2(m_old - m_new)
            pb = jnp.exp2((sbs[hd] - m_new).astype(BF16))
            accs[hd][...] = alpha * accs[hd][...] + _dot(v_aug(n, hd), pb)
            out.append(m_new)
        for hd in heads:
            s_scs[hd][1 - slot] = nxt[hd]
        return tuple(out)

    lax.fori_loop(0, qi, body, tuple(init))
    for hd in heads:
        acc = accs[hd][...]
        o_ref[:, hsl[hd]] = (acc[:HEAD_DIM] * (1.0 / acc[HEAD_DIM:HEAD_DIM + 1])).T


def _moba_attn(q_mt, k_m, v_mt, kmean, B, S):
    T = B * S
    blk = MOBA_BLOCK
    nq = S // blk
    return pl.pallas_call(
        _moba_kernel,
        out_shape=jax.ShapeDtypeStruct((T, MOBA_W), F32),
        grid=(B, nq),
        in_specs=[
            pl.BlockSpec((MOBA_W, blk), lambda b, i: (0, b * nq + i)),
            pl.BlockSpec((S, MOBA_W), lambda b, i: (b, 0)),
            pl.BlockSpec((nq, MOBA_W, blk), lambda b, i: (b, 0, 0)),
            pl.BlockSpec((nq, MOBA_W), lambda b, i: (b, 0)),
        ],
        out_specs=pl.BlockSpec((blk, MOBA_W), lambda b, i: (b * nq + i, 0)),
        scratch_shapes=[pltpu.VMEM((MOBA_HEADS, nq, blk), F32)]
        + [pltpu.VMEM((HEAD_DIM + DENOM_ROWS, blk), F32) for _ in range(MOBA_HEADS)]
        + [pltpu.VMEM((2, blk, blk), F32) for _ in range(MOBA_HEADS)],
        compiler_params=_cparams(("parallel", "arbitrary")),
        name="moba_attn",
    )(q_mt, k_m, v_mt, kmean)


def _sortable_key(x):
    b = pltpu.bitcast(x, I32)
    return jnp.where(b >= 0, b, b ^ jnp.int32(0x7FFFFFFF))


def _dsa_kernel(qit_ref, wit_ref, qdt_ref, ki_ref, kd_ref, vdt_ref, o_ref, key_sc, high_sc, *pair_scratch, topk):
    accs, s_scs = pair_scratch[:DSA_HEADS // 2], pair_scratch[DSA_HEADS // 2:]
    t = pl.program_id(1)
    tq, kc = DSA_TQ, DSA_KC
    S = key_sc.shape[0]
    q0 = t * tq
    n_chunks = (q0 + tq + kc - 1) // kc
    key_io = lax.broadcasted_iota(I32, (kc, tq), 0)
    q_pos = q0 + lax.broadcasted_iota(I32, (kc, tq), 1)
    w = wit_ref[...]

    def score_chunk(c, carry):
        k0 = pl.multiple_of(c * kc, kc)
        kic = ki_ref[pl.ds(k0, kc), :]
        acc = jnp.zeros((kc, tq), F32)
        for hd in range(IDX_HEADS):
            lg = _dot(kic, qit_ref[hd * IDX_DIM:(hd + 1) * IDX_DIM, :])
            acc = acc + jnp.maximum(lg, 0.0) * w[hd:hd + 1, :]
        acc = jnp.where(acc == 0.0, 0.0, acc)
        key = _sortable_key(acc)
        key = jnp.where(k0 + key_io <= q_pos, key, INT_MIN)
        key_sc[pl.ds(k0, kc), :] = key
        high_sc[pl.ds(k0, kc), :] = lax.shift_right_arithmetic(key, 16).astype(jnp.int16)
        return carry

    lax.fori_loop(0, n_chunks, score_chunk, 0)

    ks = DSA_KS
    ks_io = lax.broadcasted_iota(I32, (ks, tq), 0)

    def column_total(slab_fn, dtype):
        rows = SUBLANES * (4 // jnp.dtype(dtype).itemsize)

        def cbody(c, accs):
            vals = slab_fn(pl.multiple_of(c * ks, ks))
            accs = list(accs)
            for i in range(ks // rows):
                j = i % REDUCE_CHAINS
                accs[j] = accs[j] + vals[i * rows:(i + 1) * rows]
            return tuple(accs)
        zero = jnp.zeros((rows, tq), dtype)
        accs = lax.fori_loop(0, n_chunks * (kc // ks), cbody, (zero,) * REDUCE_CHAINS)
        total = accs[0].astype(I32)
        for a in accs[1:]:
            total = total + a.astype(I32)
        return jnp.sum(total, axis=0, keepdims=True)

    def count(pred_fn):
        return column_total(lambda k0: jnp.where(pred_fn(key_sc[pl.ds(k0, ks), :], k0), 1, 0), I32)

    def count_ge_high(cand):
        c16 = lax.shift_right_arithmetic(cand, 16).astype(jnp.int16)
        one, zero = jnp.int16(1), jnp.int16(0)
        return column_total(lambda k0: jnp.where(high_sc[pl.ds(k0, ks), :] >= c16, one, zero), jnp.int16)

    thr0 = jnp.where(count_ge_high(jnp.zeros((1, tq), I32)) >= topk, 0, INT_MIN).astype(I32)

    def high_step(i, thr):
        cand = thr | lax.shift_left(jnp.int32(1), 30 - i)
        return jnp.where(count_ge_high(cand) >= topk, cand, thr)

    thr = lax.fori_loop(0, 15, high_step, thr0)

    high = lax.shift_right_arithmetic(thr, 16)
    h16 = high.astype(jnp.int16)
    one16, zero16 = jnp.int16(1), jnp.int16(0)
    n_above = column_total(lambda k0: jnp.where(high_sc[pl.ds(k0, ks), :] > h16, one16, zero16), jnp.int16)

    def repack(c, carry):
        k0 = pl.multiple_of(c * kc, kc)
        kk = key_sc[pl.ds(k0, kc), :]
        low = (kk & 0xFFFF) - 32768
        same = lax.shift_right_arithmetic(kk, 16) == high
        high_sc[pl.ds(k0, kc), :] = jnp.where(same, low, -32768).astype(jnp.int16)
        return carry

    lax.fori_loop(0, n_chunks, repack, 0)

    def low_step(i, low_bits):
        cand = low_bits | lax.shift_left(jnp.int32(1), 15 - i)
        c16 = (cand - 32768).astype(jnp.int16)
        cnt = n_above + column_total(
            lambda k0: jnp.where(high_sc[pl.ds(k0, ks), :] >= c16, one16, zero16), jnp.int16)
        return jnp.where(cnt >= topk, cand, low_bits)

    thr = thr | lax.fori_loop(0, 16, low_step, jnp.zeros((1, tq), I32))

    n_gt = count(lambda kk, k0: kk > thr)
    n_ge = count(lambda kk, k0: kk >= thr)
    need = topk - n_gt
    overflow = (n_ge > topk) & (thr != INT_MIN)
    any_overflow = jnp.max(jnp.where(overflow, 1, 0)) > 0
    nbits = max(1, (S - 1).bit_length())

    def cut_search():
        def step(i, lo):
            cand = lo | lax.shift_left(jnp.int32(1), nbits - 1 - i)
            cnt = count(lambda kk, k0: (kk == thr) & (k0 + ks_io < cand))
            return jnp.where(cnt >= need, lo, cand)
        return lax.fori_loop(0, nbits, step, jnp.zeros((1, tq), I32))

    jcut = lax.cond(any_overflow, cut_search, lambda: jnp.zeros((1, tq), I32))
    jcut = jnp.where(overflow, jcut, S)
    thr_ge = jnp.where(thr == INT_MIN, INT_MIN + 1, thr)

    pairs = range(DSA_HEADS // 2)
    qst = [jnp.concatenate([qdt_ref[(2 * g + j) * HEAD_DIM:(2 * g + j + 1) * HEAD_DIM, :] for j in range(2)],
                           axis=1) for g in pairs]
    for g in pairs:
        accs[g][...] = jnp.zeros(accs[g].shape, F32)

    ka = DSA_KA
    pos_io = lax.broadcasted_iota(I32, (ka, tq), 0)

    n_sub = (q0 + tq + ka - 1) // ka

    def masked_scores(c):
        k0 = pl.multiple_of(c * ka, ka)
        kk = key_sc[pl.ds(k0, ka), :]
        bias = lax.cond(
            any_overflow,
            lambda: jnp.where((kk > thr_ge) | ((kk == thr_ge) & (k0 + pos_io <= jcut)), 0.0, NEG),
            lambda: jnp.where(kk >= thr_ge, 0.0, NEG))
        bias2 = jnp.concatenate([bias, bias], axis=1)
        kdc = kd_ref[pl.ds(k0, ka), :]
        return [_dot(kdc, qst[g]) + bias2 for g in pairs]

    for g, s0 in enumerate(masked_scores(0)):
        s_scs[g][0] = s0

    def attn_chunk(c, carry):
        slot = lax.rem(c, 2)
        ss = [s_scs[g][slot] for g in pairs]
        ss_next = masked_scores(jnp.minimum(c + 1, n_sub - 1))
        vaug = jnp.concatenate([vdt_ref[c], ones_rows], axis=0)
        old = [accs[g][...] for g in pairs]
        out, new = [], []
        for g in pairs:
            m_old = carry[g]
            m_new = jnp.maximum(m_old, _col_max(ss[g]))
            alpha = jnp.exp2(m_old - m_new)
            p = jnp.exp2((ss[g] - m_new).astype(BF16))
            new.append(alpha * old[g] + _dot(vaug, p))
            out.append(m_new)
        for g in pairs:
            accs[g][...] = new[g]
            s_scs[g][1 - slot] = ss_next[g]
        return tuple(out)

    ones_rows = jnp.ones((DENOM_ROWS, ka), BF16)
    init = (jnp.full((1, 2 * tq), NEG, F32),) * len(pairs)
    lax.fori_loop(0, n_sub, attn_chunk, init)
    for g in pairs:
        acc = accs[g][...]
        out_t = acc[:HEAD_DIM] * (1.0 / acc[HEAD_DIM:HEAD_DIM + 1])
        for j in range(2):
            hd = 2 * g + j
            o_ref[:, hd * HEAD_DIM:(hd + 1) * HEAD_DIM] = out_t[:, j * tq:(j + 1) * tq].T


def _dsa_attn(q_it, w_it, q_dt, k_i, k_d, v_dt, B, S):
    T = B * S
    tq = DSA_TQ
    nq = S // tq
    topk = min(DSA_MAX_TOPK, S // 4)
    qcol = lambda b, t: (0, b * nq + t)
    seq = lambda b, t: (b, 0)
    return pl.pallas_call(
        functools.partial(_dsa_kernel, topk=topk),
        out_shape=jax.ShapeDtypeStruct((T, DSA_W), F32),
        grid=(B, nq),
        in_specs=[
            pl.BlockSpec((IDX_HEADS * IDX_DIM, tq), qcol),
            pl.BlockSpec((IDX_HEADS, tq), qcol),
            pl.BlockSpec((DSA_W, tq), qcol),
            pl.BlockSpec((S, IDX_DIM), seq),
            pl.BlockSpec((S, HEAD_DIM), seq),
            pl.BlockSpec((S // DSA_KA, HEAD_DIM, DSA_KA), lambda b, t: (b, 0, 0)),
        ],
        out_specs=pl.BlockSpec((tq, DSA_W), lambda b, t: (b * nq + t, 0)),
        scratch_shapes=[pltpu.VMEM((S, tq), I32), pltpu.VMEM((S, tq), jnp.int16)]
        + [pltpu.VMEM((HEAD_DIM + DENOM_ROWS, 2 * tq), F32) for _ in range(DSA_HEADS // 2)]
        + [pltpu.VMEM((2, DSA_KA, 2 * tq), F32) for _ in range(DSA_HEADS // 2)],
        compiler_params=_cparams(("parallel", "arbitrary")),
        name="dsa_attn",
    )(q_it, w_it, q_dt, k_i, k_d, v_dt)


def _first_index_of_max(v, row_io, n_rows):
    m = jnp.max(v, axis=0, keepdims=True)
    idx = jnp.min(jnp.where(v == m, row_io, n_rows), axis=0, keepdims=True)
    return m, idx


def _post_kernel(x_ref, om_ref, od_ref, mod_ref, gm_ref, gd_ref, wout_ref, gffn_ref, wgu_ref, wds_ref,
                 wrh_ref, wrl_ref, rb_ref, tri_ref,
                 xpart_ref, h2r_ref, eidx_ref, rank_ref, gate_ref, cnt_ref, base_sc):
    i = pl.program_id(0)
    tm = x_ref.shape[0]
    gt1 = mod_ref[0, 2:3, :]
    sh2 = mod_ref[0, 3:4, :]
    sc2 = mod_ref[0, 4:5, :]
    gt2 = mod_ref[0, 5:6, :]

    mixed = jnp.concatenate([_rms(om_ref[...], gm_ref[...]), _rms(od_ref[...], gd_ref[...])], axis=1)
    x1 = x_ref[...] + gt1 * _dot(mixed.astype(BF16), wout_ref[...])
    h2 = _rms(x1, gffn_ref[...]) * (1.0 + sc2) + sh2
    h2b = h2.astype(BF16)

    au = _dot(h2b, wgu_ref[...])
    hs = (_silu(au[:, :D_SHARED]) * au[:, D_SHARED:]).astype(BF16)
    xpart_ref[...] = x1 + gt2 * _dot(hs, wds_ref[...])

    for c, slab in enumerate(_pack_row_words(h2)):
        _store_word_slab(h2r_ref, 0, tm, c, slab)

    h2lo = (h2 - h2b.astype(F32)).astype(BF16)
    logits = _dot_nt(wrh_ref[...], h2b) + _dot_nt(wrl_ref[...], h2b) + _dot_nt(wrh_ref[...], h2lo)
    scores = 1.0 / (1.0 + jnp.exp(-logits))
    biased = scores + rb_ref[...]

    g_io = lax.broadcasted_iota(I32, (GROUP_SIZE, tm), 0)
    gs_rows = []
    for g in range(N_GROUPS):
        blk = biased[g * GROUP_SIZE:(g + 1) * GROUP_SIZE, :]
        m1, i1 = _first_index_of_max(blk, g_io, GROUP_SIZE)
        m2 = jnp.max(jnp.where(g_io == i1, -jnp.inf, blk), axis=0, keepdims=True)
        gs_rows.append(m1 + m2)
    gs = jnp.concatenate(gs_rows, axis=0)
    gi = lax.broadcasted_iota(I32, (N_GROUPS, tm), 0)
    grank = jnp.zeros((N_GROUPS, tm), I32)
    for m in range(N_GROUPS):
        gm = gs[m:m + 1, :]
        grank = grank + jnp.where((gm > gs) | ((gm == gs) & (m < gi)), 1, 0)
    gsel = grank < TOPK_GROUPS
    masked = jnp.concatenate(
        [jnp.where(gsel[g:g + 1, :], biased[g * GROUP_SIZE:(g + 1) * GROUP_SIZE, :], -jnp.inf)
         for g in range(N_GROUPS)], axis=0)

    e_io = lax.broadcasted_iota(I32, (N_EXPERTS, tm), 0)
    e_rows, s_rows = [], []
    for _ in range(EXPERT_TOPK):
        _, idx = _first_index_of_max(masked, e_io, N_EXPERTS)
        hit = e_io == idx
        e_rows.append(idx)
        s_rows.append(jnp.sum(jnp.where(hit, scores, 0.0), axis=0, keepdims=True))
        masked = jnp.where(hit, -jnp.inf, masked)
    eidx = jnp.concatenate(e_rows, axis=0)
    sk = jnp.concatenate(s_rows, axis=0)
    gate_ref[...] = sk / jnp.sum(sk, axis=0, keepdims=True) * ROUTED_SCALE
    eidx_ref[...] = eidx

    @pl.when(i == 0)
    def _():
        base_sc[...] = jnp.zeros(base_sc.shape, F32)

    chosen = jnp.zeros((N_EXPERTS, tm), F32)
    for k in range(EXPERT_TOPK):
        chosen = chosen + jnp.where(e_io == e_rows[k], 1.0, 0.0)
    incl = _dot(chosen.astype(BF16), tri_ref[...])
    pos = base_sc[...] + incl - 1.0
    rank_ref[...] = jnp.concatenate(
        [jnp.sum(jnp.where(e_io == e_rows[k], pos, 0.0), axis=0, keepdims=True)
         for k in range(EXPERT_TOPK)], axis=0).astype(I32)
    base_sc[...] = base_sc[...] + incl[:, tm - 1:tm]
    cnt_ref[...] = jnp.broadcast_to(base_sc[...], cnt_ref.shape)


def _post_attn(x2, o_m, o_d, mod3, g_moba, g_dsa, w_out, g_ffn, w_gu_s, w_down_s, wr_hi, wr_lo, rbias, S):
    T, D = x2.shape
    tm = POST_TM
    nt_per_seq = S // tm
    row = lambda i: (i, 0)
    full = lambda i: (0, 0)
    tri = (jnp.arange(tm)[:, None] <= jnp.arange(tm)[None, :]).astype(BF16)
    n_words = D // ROW_WORDS
    return pl.pallas_call(
        _post_kernel,
        out_shape=[
            jax.ShapeDtypeStruct((T, D), F32),
            jax.ShapeDtypeStruct((T, n_words, LANES), I32),
            jax.ShapeDtypeStruct((EXPERT_TOPK, T), I32),
            jax.ShapeDtypeStruct((EXPERT_TOPK, T), I32),
            jax.ShapeDtypeStruct((EXPERT_TOPK, T), F32),
            jax.ShapeDtypeStruct((N_EXPERTS, LANES), F32),
        ],
        grid=(T // tm,),
        in_specs=[
            pl.BlockSpec((tm, D), row),
            pl.BlockSpec((tm, MOBA_W), row),
            pl.BlockSpec((tm, DSA_W), row),
            pl.BlockSpec((1, 6, D), lambda i: (i // nt_per_seq, 0, 0)),
            pl.BlockSpec((1, MOBA_W), full),
            pl.BlockSpec((1, DSA_W), full),
            pl.BlockSpec(w_out.shape, full),
            pl.BlockSpec((1, D), full),
            pl.BlockSpec(w_gu_s.shape, full),
            pl.BlockSpec(w_down_s.shape, full),
            pl.BlockSpec(wr_hi.shape, full),
            pl.BlockSpec(wr_lo.shape, full),
            pl.BlockSpec((N_EXPERTS, 1), full),
            pl.BlockSpec((tm, tm), full),
        ],
        out_specs=[
            pl.BlockSpec((tm, D), row),
            pl.BlockSpec((tm, n_words, LANES), lambda i: (i, 0, 0)),
            pl.BlockSpec((EXPERT_TOPK, tm), lambda i: (0, i)),
            pl.BlockSpec((EXPERT_TOPK, tm), lambda i: (0, i)),
            pl.BlockSpec((EXPERT_TOPK, tm), lambda i: (0, i)),
            pl.BlockSpec((N_EXPERTS, LANES), full),
        ],
        scratch_shapes=[pltpu.VMEM((N_EXPERTS, 1), F32)],
        compiler_params=_cparams(("arbitrary",)),
        name="post_attn",
    )(x2, o_m, o_d, mod3, g_moba, g_dsa, w_out, g_ffn, w_gu_s, w_down_s, wr_hi, wr_lo, rbias, tri)


def _row_copy_wait(rows_hbm, n_rows, sem):
    blk = rows_hbm.at[pl.ds(0, n_rows)]
    pltpu.make_async_copy(blk, blk, sem).wait()


def _slots_kernel(ps_ref, e_ref, r_ref, d_ref):
    e = e_ref[...]

    def body(x, acc):
        return jnp.where(e == x, ps_ref[x], acc)

    d_ref[...] = lax.fori_loop(0, N_EXPERTS, body, jnp.zeros(e.shape, I32)) + r_ref[...]


def _slots(pad_starts, eidx, rank):
    K, T = eidx.shape
    tm = min(T, 4096)
    blk = lambda: pl.BlockSpec((K, tm), lambda i, ps: (0, i))
    return pl.pallas_call(
        _slots_kernel,
        out_shape=jax.ShapeDtypeStruct((K, T), I32),
        grid_spec=pltpu.PrefetchScalarGridSpec(num_scalar_prefetch=1, grid=(T // tm,),
                                               in_specs=[blk(), blk()], out_specs=blk()),
        compiler_params=_cparams(("parallel",)),
        name="slots",
    )(pad_starts, eidx, rank)


def _dispatch_kernel(d_ref, h2r_ref, xs_ref, sem):
    tm = h2r_ref.shape[0]

    def body(t, carry):
        for k in range(EXPERT_TOPK):
            pltpu.make_async_copy(h2r_ref.at[t], xs_ref.at[d_ref[t * EXPERT_TOPK + k]],
                                  sem).start(priority=k % 2)
        return carry

    lax.fori_loop(0, tm, body, 0)
    _row_copy_wait(xs_ref, tm * EXPERT_TOPK, sem)


def _dispatch(dest_flat, h2r, n_rows_padded):
    T, n_words, _ = h2r.shape
    tm = DISP_TM
    return pl.pallas_call(
        _dispatch_kernel,
        out_shape=jax.ShapeDtypeStruct((n_rows_padded, n_words, LANES), I32),
        grid=(T // tm,),
        in_specs=[pl.BlockSpec((tm * EXPERT_TOPK,), lambda i: (i,), memory_space=pltpu.SMEM),
                  pl.BlockSpec((tm, n_words, LANES), lambda i: (i, 0, 0))],
        out_specs=pl.BlockSpec(memory_space=pl.ANY),
        scratch_shapes=[pltpu.SemaphoreType.DMA],
        compiler_params=_cparams(("arbitrary",), disable_bounds_checks=True),
        name="dispatch",
    )(dest_flat, h2r)


def _expert_kernel(ps_ref, cnt_ref, wg_ref, wu_ref, wd_ref, xs_ref, ys_ref,
                   wg_sc, wu_sc, wd_sc, xbuf, ybuf, sem_in, sem_out):
    e = pl.program_id(0)
    bm = EXP_BM
    n_words = xbuf.shape[1]
    cnt = cnt_ref[e]
    g0 = ps_ref[e] // bm
    nb = (cnt + bm - 1) // bm
    n_used = (ps_ref[N_EXPERTS - 1] + cnt_ref[N_EXPERTS - 1] + bm - 1) // bm

    def in_copy(g):
        slot = lax.rem(g, EXP_IN_SLOTS)
        return pltpu.make_async_copy(xs_ref.at[pl.ds(g * bm, bm)], xbuf.at[pl.ds(slot * bm, bm)],
                                     sem_in.at[slot])

    def out_copy(g):
        slot = lax.rem(g, EXP_OUT_SLOTS)
        return pltpu.make_async_copy(ybuf.at[pl.ds(slot * bm, bm)], ys_ref.at[pl.ds(g * bm, bm)],
                                     sem_out.at[slot])

    @pl.when(e == 0)
    def _():
        for g in range(EXP_AHEAD):
            @pl.when(g < n_used)
            def _():
                in_copy(g).start()

    @pl.when(nb > 0)
    def _():
        wg_sc[...] = wg_ref[...].astype(BF16)
        wu_sc[...] = wu_ref[...].astype(BF16)
        wd_sc[...] = wd_ref[...].astype(BF16)

        def body(b, carry):
            g = g0 + b

            @pl.when(g + EXP_AHEAD < n_used)
            def _():
                in_copy(g + EXP_AHEAD).start()

            in_copy(g).wait()

            @pl.when(g >= EXP_OUT_SLOTS)
            def _():
                out_copy(g - EXP_OUT_SLOTS).wait()

            xrow = lax.rem(g, EXP_IN_SLOTS) * bm
            yrow = lax.rem(g, EXP_OUT_SLOTS) * bm
            feats = []
            for c in range(n_words):
                feats += _unpack_row_words(_load_word_slab(xbuf, xrow, bm, c))
            x = jnp.concatenate(feats, axis=1)
            valid = lax.broadcasted_iota(I32, (bm, 1), 0) < cnt - b * bm
            xb = jnp.where(valid, x, 0.0).astype(BF16)
            a = _dot(xb, wg_sc[...])
            u = _dot(xb, wu_sc[...])
            hmid = (_silu(a) * u).astype(BF16)
            ob = _dot(hmid, wd_sc[...])
            for c, slab in enumerate(_pack_row_words(ob)):
                _store_word_slab(ybuf, yrow, bm, c, slab)
            out_copy(g).start()
            return carry

        lax.fori_loop(0, nb, body, 0)

    @pl.when(e == N_EXPERTS - 1)
    def _():
        for back in range(EXP_OUT_SLOTS, 0, -1):
            @pl.when(n_used >= back)
            def _():
                out_copy(n_used - back).wait()


def _experts(pad_starts, counts, xs, w_gate_e, w_up_e, w_down_e):
    E, D, DE = w_gate_e.shape
    n_words = xs.shape[1]
    bm = EXP_BM
    wsel = lambda e, ps, cnt: (e, 0, 0)
    anyspec = pl.BlockSpec(memory_space=pl.ANY)
    return pl.pallas_call(
        _expert_kernel,
        out_shape=jax.ShapeDtypeStruct(xs.shape, I32),
        grid_spec=pltpu.PrefetchScalarGridSpec(
            num_scalar_prefetch=2,
            grid=(E,),
            in_specs=[pl.BlockSpec((None, D, DE), wsel), pl.BlockSpec((None, D, DE), wsel),
                      pl.BlockSpec((None, DE, D), wsel), anyspec],
            out_specs=anyspec,
            scratch_shapes=[pltpu.VMEM((D, DE), BF16), pltpu.VMEM((D, DE), BF16), pltpu.VMEM((DE, D), BF16),
                            pltpu.VMEM((EXP_IN_SLOTS * bm, n_words, LANES), I32),
                            pltpu.VMEM((EXP_OUT_SLOTS * bm, n_words, LANES), I32),
                            pltpu.SemaphoreType.DMA((EXP_IN_SLOTS,)), pltpu.SemaphoreType.DMA((EXP_OUT_SLOTS,))],
        ),
        compiler_params=_cparams(("arbitrary",)),
        name="experts",
    )(pad_starts, counts, w_gate_e, w_up_e, w_down_e, xs)


def _combine_kernel(dcur_ref, dnxt_ref, g_ref, xpart_ref, mod_ref, gfin_ref, ys_ref, o_ref,
                    buf0, buf1, sem0, sem1):
    i = pl.program_id(0)
    n_steps = pl.num_programs(0)
    tm = xpart_ref.shape[0]
    n_words = buf0.shape[1]

    def issue(dest_ref, buf, sem):
        for t in range(tm):
            for k in range(EXPERT_TOPK):
                pltpu.make_async_copy(ys_ref.at[dest_ref[t * EXPERT_TOPK + k]], buf.at[k * tm + t],
                                      sem).start(priority=k % 2)

    def reduce_tile(buf):
        gt2 = mod_ref[0, 5:6, :]
        g = g_ref[...]
        cols = []
        for c in range(n_words):
            lo = jnp.zeros((tm, LANES), F32)
            hi = jnp.zeros((tm, LANES), F32)
            for k in range(EXPERT_TOPK):
                a, b = _unpack_row_words(_load_word_slab(buf, k * tm, tm, c))
                gk = g[:, k:k + 1]
                lo = lo + gk * a
                hi = hi + gk * b
            cols += [lo, hi]
        routed = jnp.concatenate(cols, axis=1)
        o_ref[...] = _rms(xpart_ref[...] + gt2 * routed, gfin_ref[...])

    @pl.when(i == 0)
    def _():
        issue(dcur_ref, buf0, sem0)

    for parity, (cur, nxt) in enumerate((((buf0, sem0), (buf1, sem1)), ((buf1, sem1), (buf0, sem0)))):
        @pl.when(i % 2 == parity)
        def _():
            _row_copy_wait(ys_ref, tm * EXPERT_TOPK, cur[1])
            issue(dnxt_ref, *nxt)
            reduce_tile(cur[0])

            @pl.when(i + 1 == n_steps)
            def _():
                _row_copy_wait(ys_ref, tm * EXPERT_TOPK, nxt[1])


def _combine(dest, gates_t, xpart, mod3, g_final, ys, S):
    T, D = xpart.shape
    tm = COMB_TM
    n_steps = T // tm
    nt_per_seq = S // tm
    n_words = ys.shape[1]
    return pl.pallas_call(
        _combine_kernel,
        out_shape=jax.ShapeDtypeStruct((T, D), F32),
        grid=(n_steps,),
        in_specs=[
            pl.BlockSpec((tm * EXPERT_TOPK,), lambda i: (i,), memory_space=pltpu.SMEM),
            pl.BlockSpec((tm * EXPERT_TOPK,), lambda i: (jnp.minimum(i + 1, n_steps - 1),),
                         memory_space=pltpu.SMEM),
            pl.BlockSpec((tm, EXPERT_TOPK), lambda i: (i, 0)),
            pl.BlockSpec((tm, D), lambda i: (i, 0)),
            pl.BlockSpec((1, 6, D), lambda i: (i // nt_per_seq, 0, 0)),
            pl.BlockSpec((1, D), lambda i: (0, 0)),
            pl.BlockSpec(memory_space=pl.ANY),
        ],
        out_specs=pl.BlockSpec((tm, D), lambda i: (i, 0)),
        scratch_shapes=[pltpu.VMEM((EXPERT_TOPK * tm, n_words, LANES), I32),
                        pltpu.VMEM((EXPERT_TOPK * tm, n_words, LANES), I32),
                        pltpu.SemaphoreType.DMA, pltpu.SemaphoreType.DMA],
        compiler_params=_cparams(("arbitrary",), disable_bounds_checks=True),
        name="combine",
    )(dest, dest, gates_t, xpart, mod3, g_final, ys)


def _layer(x2, mod3, S, g_mix, w_in, g_kv, w_kv_up, g_moba_out, g_dsa_out, w_out, g_ffn, w_router,
           router_bias, w_gate_e, w_up_e, w_down_e, w_gate_s, w_up_s, w_down_s, g_final, tab_h, half_h,
           tab_i, half_i):
    T, D = x2.shape
    B = T // S
    w_in_p = jnp.pad(w_in, ((0, 0), (0, _C_END - w_in.shape[1]))).astype(BF16)
    (q_mt, k_m, v_mt, kmean, q_dt, k_d, v_dt, q_it, k_i, w_it) = _in_proj(
        x2, mod3, g_mix.reshape(1, D), w_in_p, g_kv.reshape(1, KV_LORA), w_kv_up.astype(BF16),
        tab_h, half_h, tab_i, half_i, S)
    o_m = _moba_attn(q_mt, k_m, v_mt, kmean, B, S)
    o_d = _dsa_attn(q_it, w_it, q_dt, k_i, k_d, v_dt, B, S)

    wr_t = w_router.T
    wr_hi = wr_t.astype(BF16)
    wr_lo = (wr_t - wr_hi.astype(F32)).astype(BF16)
    w_gu_s = jnp.concatenate([w_gate_s, w_up_s], axis=1).astype(BF16)
    xpart, h2r, eidx, rank, gates, cnt = _post_attn(
        x2, o_m, o_d, mod3, g_moba_out.reshape(1, MOBA_W), g_dsa_out.reshape(1, DSA_W), w_out.astype(BF16),
        g_ffn.reshape(1, D), w_gu_s, w_down_s.astype(BF16), wr_hi, wr_lo,
        router_bias.reshape(N_EXPERTS, 1), S)

    bm = EXP_BM
    n_blocks = T * EXPERT_TOPK // bm + N_EXPERTS
    counts = cnt[:, 0].astype(I32)
    padded = (counts + bm - 1) // bm * bm
    pad_ends = jnp.cumsum(padded)
    pad_starts = (pad_ends - padded).astype(I32)
    dest_flat = _slots(pad_starts, eidx, rank).T.reshape(-1)
    xs = _dispatch(dest_flat, h2r, n_blocks * bm)
    ys = _experts(pad_starts, counts, xs, w_gate_e, w_up_e, w_down_e)
    return _combine(dest_flat, gates.T, xpart, mod3, g_final.reshape(1, D), ys, S)


def kernel(x, c, w_ada, b_ada, g_mix, w_in, g_kv, w_kv_up, g_moba_out, g_dsa_out, w_out, g_ffn, w_router,
           router_bias, w_gate_e, w_up_e, w_down_e, w_gate_s, w_up_s, w_down_s, g_final):
    B, S, D = x.shape
    depth = w_ada.shape[0]
    assert depth == 1, "the final norm is fused into the single layer"
    assert S % PROJ_TM == 0 and S % DSA_KC == 0 and S % POST_TM == 0 and S >= 4 * DSA_MAX_TOPK
    tab_h, half_h = _rope_tables(S, HEAD_DIM, 1)
    tab_i, half_i = _rope_tables(S, IDX_DIM, LANES // IDX_DIM)
    x2 = x.reshape(B * S, D)
    sq = lambda a: a.reshape(a.shape[1:])
    mod3 = _ada_mod(c, sq(w_ada), sq(b_ada)).reshape(B, 6, D)
    out = _layer(x2, mod3, S, sq(g_mix), sq(w_in), sq(g_kv), sq(w_kv_up), sq(g_moba_out), sq(g_dsa_out),
                 sq(w_out), sq(g_ffn), sq(w_router), sq(router_bias), sq(w_gate_e), sq(w_up_e), sq(w_down_e),
                 sq(w_gate_s), sq(w_up_s), sq(w_down_s), g_final, tab_h, half_h, tab_i, half_i)
    return out.reshape(B, S, D)
```

```python
import functools

import jax
import jax.numpy as jnp
from jax import lax
from jax.experimental import pallas as pl
from jax.experimental.pallas import tpu as pltpu

HEAD_DIM = 128
MOBA_HEADS = 4
DSA_HEADS = 4
MOBA_W = MOBA_HEADS * HEAD_DIM
DSA_W = DSA_HEADS * HEAD_DIM
MOBA_BLOCK = 256
MOBA_TOPK = 3
DSA_MAX_TOPK = 256
KV_LORA = 256
IDX_HEADS = 8
IDX_DIM = 64
ROPE_THETA = 500000.0
ROPE_FRACTION_DIV = 4
N_EXPERTS = 256
EXPERT_TOPK = 8
N_GROUPS = 8
TOPK_GROUPS = 4
GROUP_SIZE = N_EXPERTS // N_GROUPS
D_EXPERT = 256
D_SHARED = 256
ROUTED_SCALE = 2.5
EPS = 1e-6

LANES = 128
SUBLANES = 8
VMEM_LIMIT = 56 * 1024 * 1024

PROJ_TM = 512
DSA_TQ = 256
DSA_KC = 512
DSA_KA = 256
DSA_KS = 128
POST_TM = 512
DISP_TM = 1024
EXP_BM = 256
EXP_AHEAD = 4
EXP_IN_SLOTS = EXP_AHEAD + 1
EXP_OUT_SLOTS = 3
COMB_TM = 128
REDUCE_CHAINS = 4
DENOM_ROWS = 16
NEG = -1e30
INT_MIN = -2147483648
LOG2E = 1.4426950408889634

F32 = jnp.float32
BF16 = jnp.bfloat16
I32 = jnp.int32


def _cparams(sem, **kw):
    return pltpu.CompilerParams(dimension_semantics=sem, vmem_limit_bytes=VMEM_LIMIT, **kw)


def _dot(a, b):
    return jnp.dot(a, b, preferred_element_type=F32)


def _dot_nt(a, b):
    return lax.dot_general(a, b, (((1,), (1,)), ((), ())), preferred_element_type=F32)


def _silu(x):
    return x * (1.0 / (1.0 + jnp.exp(-x)))


def _rms(x, g):
    return x * lax.rsqrt(jnp.mean(x * x, axis=-1, keepdims=True) + EPS) * g


def _rows_to_tile(op, x):
    parts = [x[i:i + SUBLANES] for i in range(0, x.shape[0], SUBLANES)]
    n_chains = min(REDUCE_CHAINS, len(parts))
    accs = parts[:n_chains]
    for i in range(n_chains, len(parts)):
        accs[i % n_chains] = op(accs[i % n_chains], parts[i])
    while len(accs) > 1:
        accs = [op(accs[i], accs[i + 1]) for i in range(0, len(accs) - 1, 2)] + ([accs[-1]] if len(accs) % 2 else [])
    return accs[0]


def _col_max(x):
    return jnp.max(_rows_to_tile(jnp.maximum, x), axis=0, keepdims=True)


def _col_sum(x):
    return jnp.sum(_rows_to_tile(jnp.add, x), axis=0, keepdims=True)


ROW_WORDS = 2 * LANES


def _pack_row_words(x):
    slabs = []
    for c in range(x.shape[1] // ROW_WORDS):
        lo = pltpu.bitcast(x[:, c * ROW_WORDS:c * ROW_WORDS + LANES].astype(BF16).astype(F32), I32)
        hi = pltpu.bitcast(x[:, c * ROW_WORDS + LANES:(c + 1) * ROW_WORDS].astype(BF16).astype(F32), I32)
        slabs.append(lax.shift_right_logical(lo, 16) | hi)
    return slabs


def _word_slab_index(rows_ref, row0, n_rows, c):
    n_words = rows_ref.shape[1]
    flat = rows_ref.reshape(rows_ref.shape[0] * n_words, LANES)
    return flat, pl.ds(row0 * n_words + c, n_rows, stride=n_words)


def _load_word_slab(rows_ref, row0, n_rows, c):
    flat, idx = _word_slab_index(rows_ref, row0, n_rows, c)
    return flat[idx, :]


def _store_word_slab(rows_ref, row0, n_rows, c, value):
    flat, idx = _word_slab_index(rows_ref, row0, n_rows, c)
    flat[idx, :] = value


def _unpack_row_words(u):
    return pltpu.bitcast(lax.shift_left(u, 16), F32), pltpu.bitcast(u & jnp.int32(-65536), F32)


def _ada_kernel(c_ref, w_ref, b_ref, o_ref):
    ca = _silu(c_ref[...])
    o_ref[...] = jnp.dot(ca, w_ref[...], preferred_element_type=F32,
                         precision=lax.Precision.HIGHEST) + b_ref[...]


def _ada_mod(c, w_ada, b_ada):
    B, D = c.shape
    N = w_ada.shape[1]
    tn = 1024
    return pl.pallas_call(
        _ada_kernel,
        out_shape=jax.ShapeDtypeStruct((B, N), F32),
        grid=(N // tn,),
        in_specs=[pl.BlockSpec((B, D), lambda j: (0, 0)),
                  pl.BlockSpec((D, tn), lambda j: (0, j)),
                  pl.BlockSpec((1, tn), lambda j: (0, j))],
        out_specs=pl.BlockSpec((B, tn), lambda j: (0, j)),
        compiler_params=_cparams(("arbitrary",)),
        name="ada_mod",
    )(c, w_ada, b_ada.reshape(1, N))


def _rope_tables(seq, head_dim, heads_per_vreg):
    rot = head_dim // ROPE_FRACTION_DIV
    half = rot // 2
    inv = jnp.float32(ROPE_THETA) ** (-(jnp.arange(0, rot, 2, dtype=F32) / rot))
    ang = jnp.arange(seq, dtype=F32)[:, None] * inv[None, :]
    cos, sin = jnp.cos(ang), jnp.sin(ang)
    ones = jnp.ones((seq, head_dim - rot), F32)
    zeros_h = jnp.zeros((seq, half), F32)
    zeros_r = jnp.zeros((seq, head_dim - rot), F32)
    c = jnp.concatenate([cos, cos, ones], axis=1)
    sp = jnp.concatenate([zeros_h, sin, zeros_r], axis=1)
    sm = jnp.concatenate([-sin, zeros_h, zeros_r], axis=1)
    rep = lambda t: jnp.tile(t, (1, heads_per_vreg))
    return jnp.stack([rep(c), rep(sp), rep(sm)], axis=0), half


def _rope(x, tab_ref, half):
    return (x * tab_ref[0] + pltpu.roll(x, half, 1) * tab_ref[1]
            + pltpu.roll(x, LANES - half, 1) * tab_ref[2])


_C_QM, _C_KM, _C_VM, _C_QD = 0, MOBA_W, 2 * MOBA_W, 3 * MOBA_W
_C_CKV = 3 * MOBA_W + DSA_W
_C_QI = _C_CKV + KV_LORA
_C_KI = _C_QI + IDX_HEADS * IDX_DIM
_C_END = _C_KI + LANES


def _in_proj_kernel(x_ref, mod_ref, gmix_ref, w_ref, gkv_ref, wkv_ref, tabh_ref, tabi_ref,
                    qmt_ref, km_ref, vmt_ref, kmean_ref, qdt_ref, kd_ref, vdt_ref, qit_ref, ki_ref, wit_ref,
                    *, half_h, half_i):
    tm = x_ref.shape[0]
    x = x_ref[...]
    sh1 = mod_ref[0, 0:1, :]
    sc1 = mod_ref[0, 1:2, :]
    h = (_rms(x, gmix_ref[...]) * (1.0 + sc1) + sh1).astype(BF16)

    def proj(c0, width):
        return _dot(h, w_ref[:, c0:c0 + width])

    q_scale = HEAD_DIM ** -0.5 * LOG2E
    nblk = tm // MOBA_BLOCK
    qm = proj(_C_QM, MOBA_W)
    km = proj(_C_KM, MOBA_W)
    vm = proj(_C_VM, MOBA_W)
    for hd in range(MOBA_HEADS):
        sl = slice(hd * HEAD_DIM, (hd + 1) * HEAD_DIM)
        qmt_ref[sl, :] = (_rope(qm[:, sl], tabh_ref, half_h) * q_scale).T.astype(BF16)
        kr = _rope(km[:, sl], tabh_ref, half_h)
        km_ref[:, sl] = kr.astype(BF16)
        for blk in range(nblk):
            rows = slice(blk * MOBA_BLOCK, (blk + 1) * MOBA_BLOCK)
            kmean_ref[blk:blk + 1, sl] = jnp.mean(kr[rows], axis=0, keepdims=True)
            vmt_ref[blk, sl, :] = vm[rows, sl].T.astype(BF16)
    qd = proj(_C_QD, DSA_W)
    for hd in range(DSA_HEADS):
        sl = slice(hd * HEAD_DIM, (hd + 1) * HEAD_DIM)
        qdt_ref[sl, :] = (_rope(qd[:, sl], tabh_ref, half_h) * q_scale).T.astype(BF16)
    ckv = proj(_C_CKV, KV_LORA)
    kv = _dot(_rms(ckv, gkv_ref[...]).astype(BF16), wkv_ref[...])
    kd_ref[...] = _rope(kv[:, :HEAD_DIM], tabh_ref, half_h).astype(BF16)
    for ch in range(tm // DSA_KA):
        vdt_ref[ch] = kv[ch * DSA_KA:(ch + 1) * DSA_KA, HEAD_DIM:].T.astype(BF16)
    qi = proj(_C_QI, IDX_HEADS * IDX_DIM)
    for j in range(IDX_HEADS * IDX_DIM // LANES):
        sl = slice(j * LANES, (j + 1) * LANES)
        qit_ref[sl, :] = _rope(qi[:, sl], tabi_ref, half_i).T.astype(BF16)
    kw = proj(_C_KI, LANES)
    ki_ref[...] = _rope(kw, tabi_ref, half_i)[:, :IDX_DIM].astype(BF16)
    wit_ref[...] = kw.T[IDX_DIM:IDX_DIM + IDX_HEADS, :] * (IDX_HEADS ** -0.5 * IDX_DIM ** -0.5)


def _in_proj(x2, mod3, g_mix, w_in_p, g_kv, w_kv_up, tab_h, half_h, tab_i, half_i, S):
    T, D = x2.shape
    tm = PROJ_TM
    nt_per_seq = S // tm
    row = lambda i: (i, 0)
    col = lambda i: (0, i)
    nb = tm // MOBA_BLOCK
    nc = tm // DSA_KA
    outs = [
        jax.ShapeDtypeStruct((MOBA_W, T), BF16),
        jax.ShapeDtypeStruct((T, MOBA_W), BF16),
        jax.ShapeDtypeStruct((T // MOBA_BLOCK, MOBA_W, MOBA_BLOCK), BF16),
        jax.ShapeDtypeStruct((T // tm, nb, MOBA_W), F32),
        jax.ShapeDtypeStruct((DSA_W, T), BF16),
        jax.ShapeDtypeStruct((T, HEAD_DIM), BF16),
        jax.ShapeDtypeStruct((T // DSA_KA, HEAD_DIM, DSA_KA), BF16),
        jax.ShapeDtypeStruct((IDX_HEADS * IDX_DIM, T), BF16),
        jax.ShapeDtypeStruct((T, IDX_DIM), BF16),
        jax.ShapeDtypeStruct((IDX_HEADS, T), F32),
    ]
    out_specs = [
        pl.BlockSpec((MOBA_W, tm), col), pl.BlockSpec((tm, MOBA_W), row),
        pl.BlockSpec((nb, MOBA_W, MOBA_BLOCK), lambda i: (i, 0, 0)),
        pl.BlockSpec((None, nb, MOBA_W), lambda i: (i, 0, 0)),
        pl.BlockSpec((DSA_W, tm), col), pl.BlockSpec((tm, HEAD_DIM), row),
        pl.BlockSpec((nc, HEAD_DIM, DSA_KA), lambda i: (i, 0, 0)),
        pl.BlockSpec((IDX_HEADS * IDX_DIM, tm), col), pl.BlockSpec((tm, IDX_DIM), row),
        pl.BlockSpec((IDX_HEADS, tm), col),
    ]
    res = pl.pallas_call(
        functools.partial(_in_proj_kernel, half_h=half_h, half_i=half_i),
        out_shape=outs,
        grid=(T // tm,),
        in_specs=[
            pl.BlockSpec((tm, D), row),
            pl.BlockSpec((1, 6, D), lambda i: (i // nt_per_seq, 0, 0)),
            pl.BlockSpec((1, D), lambda i: (0, 0)),
            pl.BlockSpec((D, _C_END), lambda i: (0, 0)),
            pl.BlockSpec((1, KV_LORA), lambda i: (0, 0)),
            pl.BlockSpec((KV_LORA, 2 * HEAD_DIM), lambda i: (0, 0)),
            pl.BlockSpec((3, tm, LANES), lambda i: (0, i % nt_per_seq, 0)),
            pl.BlockSpec((3, tm, LANES), lambda i: (0, i % nt_per_seq, 0)),
        ],
        out_specs=out_specs,
        compiler_params=_cparams(("parallel",)),
        name="in_proj",
    )(x2, mod3, g_mix, w_in_p, g_kv, w_kv_up, tab_h, tab_i)
    res = list(res)
    res[3] = res[3].reshape(T // MOBA_BLOCK, MOBA_W)
    return res


def _moba_kernel(qt_ref, k_ref, vt_ref, kmean_ref, o_ref, bias_sc, *head_scratch):
    accs, s_scs = head_scratch[:MOBA_HEADS], head_scratch[MOBA_HEADS:]
    qi = pl.program_id(1)
    blk = MOBA_BLOCK
    nb = kmean_ref.shape[0]
    heads = range(MOBA_HEADS)
    hsl = [slice(hd * HEAD_DIM, (hd + 1) * HEAD_DIM) for hd in heads]
    qts = [qt_ref[hsl[hd], :] for hd in heads]
    row = lax.broadcasted_iota(I32, (nb, blk), 0)
    past = row < qi
    start = pl.multiple_of(qi * blk, blk)
    k_io = lax.broadcasted_iota(I32, (blk, blk), 0)
    q_io = lax.broadcasted_iota(I32, (blk, blk), 1)

    gates = []
    for hd in heads:
        km = kmean_ref[:, hsl[hd]]
        km_hi = km.astype(BF16)
        km_lo = (km - km_hi.astype(F32)).astype(BF16)
        gate = _dot(km_hi, qts[hd]) + _dot(km_lo, qts[hd])
        gates.append(jnp.where(past, gate, -jnp.inf))
    k_own = k_ref[pl.ds(start, blk), :]
    own = [jnp.where(k_io <= q_io, _dot(k_own[:, hsl[hd]], qts[hd]), NEG) for hd in heads]
    biases = [jnp.full((nb, blk), NEG, F32) for _ in heads]
    for _ in range(MOBA_TOPK):
        for hd in heads:
            _, idx = _first_index_of_max(gates[hd], row, nb)
            hit = row == idx
            biases[hd] = jnp.where(hit, 0.0, biases[hd])
            gates[hd] = jnp.where(hit, -jnp.inf, gates[hd])
    ones_rows = jnp.ones((DENOM_ROWS, blk), BF16)

    def v_aug(n, hd):
        return jnp.concatenate([vt_ref[n, hsl[hd], :], ones_rows], axis=0)

    init = []
    for hd in heads:
        bias_sc[hd] = jnp.where(past, biases[hd], NEG)
        m0 = _col_max(own[hd])
        p = jnp.exp2((own[hd] - m0).astype(BF16))
        accs[hd][...] = _dot(v_aug(qi, hd), p)
        init.append(m0)

    def masked_scores(n):
        kb = k_ref[pl.ds(pl.multiple_of(n * blk, blk), blk), :]
        return [_dot(kb[:, hsl[hd]], qts[hd]) + bias_sc[hd, pl.ds(n, 1), :] for hd in heads]

    for hd, s0 in enumerate(masked_scores(0)):
        s_scs[hd][0] = s0

    def body(n, carry):
        slot = lax.rem(n, 2)
        sbs = [s_scs[hd][slot] for hd in heads]
        nxt = masked_scores(jnp.minimum(n + 1, qi - 1))
        out = []
        for hd in heads:
            m_old = carry[hd]
            m_new = jnp.maximum(m_old, _col_max(sbs[hd]))
            alpha = jnp.exp2(m_old - m_new)
            pb = jnp.exp2((sbs[hd] - m_new).astype(BF16))
            accs[hd][...] = alpha * accs[hd][...] + _dot(v_aug(n, hd), pb)
            out.append(m_new)
        for hd in heads:
            s_scs[hd][1 - slot] = nxt[hd]
        return tuple(out)

    lax.fori_loop(0, qi, body, tuple(init))
    for hd in heads:
        acc = accs[hd][...]
        o_ref[:, hsl[hd]] = (acc[:HEAD_DIM] * (1.0 / acc[HEAD_DIM:HEAD_DIM + 1])).T


def _moba_attn(q_mt, k_m, v_mt, kmean, B, S):
    T = B * S
    blk = MOBA_BLOCK
    nq = S // blk
    return pl.pallas_call(
        _moba_kernel,
        out_shape=jax.ShapeDtypeStruct((T, MOBA_W), F32),
        grid=(B, nq),
        in_specs=[
            pl.BlockSpec((MOBA_W, blk), lambda b, i: (0, b * nq + i)),
            pl.BlockSpec((S, MOBA_W), lambda b, i: (b, 0)),
            pl.BlockSpec((nq, MOBA_W, blk), lambda b, i: (b, 0, 0)),
            pl.BlockSpec((nq, MOBA_W), lambda b, i: (b, 0)),
        ],
        out_specs=pl.BlockSpec((blk, MOBA_W), lambda b, i: (b * nq + i, 0)),
        scratch_shapes=[pltpu.VMEM((MOBA_HEADS, nq, blk), F32)]
        + [pltpu.VMEM((HEAD_DIM + DENOM_ROWS, blk), F32) for _ in range(MOBA_HEADS)]
        + [pltpu.VMEM((2, blk, blk), F32) for _ in range(MOBA_HEADS)],
        compiler_params=_cparams(("parallel", "arbitrary")),
        name="moba_attn",
    )(q_mt, k_m, v_mt, kmean)


def _sortable_key(x):
    b = pltpu.bitcast(x, I32)
    return jnp.where(b >= 0, b, b ^ jnp.int32(0x7FFFFFFF))


def _dsa_kernel(qit_ref, wit_ref, qdt_ref, ki_ref, kd_ref, vdt_ref, o_ref, key_sc, high_sc, *pair_scratch, topk):
    accs, s_scs = pair_scratch[:DSA_HEADS // 2], pair_scratch[DSA_HEADS // 2:]
    t = pl.program_id(1)
    tq, kc = DSA_TQ, DSA_KC
    S = key_sc.shape[0]
    q0 = t * tq
    n_chunks = (q0 + tq + kc - 1) // kc
    key_io = lax.broadcasted_iota(I32, (kc, tq), 0)
    q_pos = q0 + lax.broadcasted_iota(I32, (kc, tq), 1)
    w = wit_ref[...]

    def score_chunk(c, carry):
        k0 = pl.multiple_of(c * kc, kc)
        kic = ki_ref[pl.ds(k0, kc), :]
        acc = jnp.zeros((kc, tq), F32)
        for hd in range(IDX_HEADS):
            lg = _dot(kic, qit_ref[hd * IDX_DIM:(hd + 1) * IDX_DIM, :])
            acc = acc + jnp.maximum(lg, 0.0) * w[hd:hd + 1, :]
        acc = jnp.where(acc == 0.0, 0.0, acc)
        key = _sortable_key(acc)
        key = jnp.where(k0 + key_io <= q_pos, key, INT_MIN)
        key_sc[pl.ds(k0, kc), :] = key
        high_sc[pl.ds(k0, kc), :] = lax.shift_right_arithmetic(key, 16).astype(jnp.int16)
        return carry

    lax.fori_loop(0, n_chunks, score_chunk, 0)

    ks = DSA_KS
    ks_io = lax.broadcasted_iota(I32, (ks, tq), 0)

    def column_total(slab_fn, dtype):
        rows = SUBLANES * (4 // jnp.dtype(dtype).itemsize)

        def cbody(c, accs):
            vals = slab_fn(pl.multiple_of(c * ks, ks))
            accs = list(accs)
            for i in range(ks // rows):
                j = i % REDUCE_CHAINS
                accs[j] = accs[j] + vals[i * rows:(i + 1) * rows]
            return tuple(accs)
        zero = jnp.zeros((rows, tq), dtype)
        accs = lax.fori_loop(0, n_chunks * (kc // ks), cbody, (zero,) * REDUCE_CHAINS)
        total = accs[0].astype(I32)
        for a in accs[1:]:
            total = total + a.astype(I32)
        return jnp.sum(total, axis=0, keepdims=True)

    def count(pred_fn):
        return column_total(lambda k0: jnp.where(pred_fn(key_sc[pl.ds(k0, ks), :], k0), 1, 0), I32)

    def count_ge_high(cand):
        c16 = lax.shift_right_arithmetic(cand, 16).astype(jnp.int16)
        one, zero = jnp.int16(1), jnp.int16(0)
        return column_total(lambda k0: jnp.where(high_sc[pl.ds(k0, ks), :] >= c16, one, zero), jnp.int16)

    thr0 = jnp.where(count_ge_high(jnp.zeros((1, tq), I32)) >= topk, 0, INT_MIN).astype(I32)

    def high_step(i, thr):
        cand = thr | lax.shift_left(jnp.int32(1), 30 - i)
        return jnp.where(count_ge_high(cand) >= topk, cand, thr)

    thr = lax.fori_loop(0, 15, high_step, thr0)

    high = lax.shift_right_arithmetic(thr, 16)
    h16 = high.astype(jnp.int16)
    one16, zero16 = jnp.int16(1), jnp.int16(0)
    n_above = column_total(lambda k0: jnp.where(high_sc[pl.ds(k0, ks), :] > h16, one16, zero16), jnp.int16)

    def repack(c, carry):
        k0 = pl.multiple_of(c * kc, kc)
        kk = key_sc[pl.ds(k0, kc), :]
        low = (kk & 0xFFFF) - 32768
        same = lax.shift_right_arithmetic(kk, 16) == high
        high_sc[pl.ds(k0, kc), :] = jnp.where(same, low, -32768).astype(jnp.int16)
        return carry

    lax.fori_loop(0, n_chunks, repack, 0)

    def low_step(i, low_bits):
        cand = low_bits | lax.shift_left(jnp.int32(1), 15 - i)
        c16 = (cand - 32768).astype(jnp.int16)
        cnt = n_above + column_total(
            lambda k0: jnp.where(high_sc[pl.ds(k0, ks), :] >= c16, one16, zero16), jnp.int16)
        return jnp.where(cnt >= topk, cand, low_bits)

    thr = thr | lax.fori_loop(0, 16, low_step, jnp.zeros((1, tq), I32))

    n_gt = count(lambda kk, k0: kk > thr)
    n_ge = count(lambda kk, k0: kk >= thr)
    need = topk - n_gt
    overflow = (n_ge > topk) & (thr != INT_MIN)
    any_overflow = jnp.max(jnp.where(overflow, 1, 0)) > 0
    nbits = max(1, (S - 1).bit_length())

    def cut_search():
        def step(i, lo):
            cand = lo | lax.shift_left(jnp.int32(1), nbits - 1 - i)
            cnt = count(lambda kk, k0: (kk == thr) & (k0 + ks_io < cand))
            return jnp.where(cnt >= need, lo, cand)
        return lax.fori_loop(0, nbits, step, jnp.zeros((1, tq), I32))

    jcut = lax.cond(any_overflow, cut_search, lambda: jnp.zeros((1, tq), I32))
    jcut = jnp.where(overflow, jcut, S)
    thr_ge = jnp.where(thr == INT_MIN, INT_MIN + 1, thr)

    pairs = range(DSA_HEADS // 2)
    qst = [jnp.concatenate([qdt_ref[(2 * g + j) * HEAD_DIM:(2 * g + j + 1) * HEAD_DIM, :] for j in range(2)],
                           axis=1) for g in pairs]
    for g in pairs:
        accs[g][...] = jnp.zeros(accs[g].shape, F32)

    ka = DSA_KA
    pos_io = lax.broadcasted_iota(I32, (ka, tq), 0)

    n_sub = (q0 + tq + ka - 1) // ka

    def masked_scores(c):
        k0 = pl.multiple_of(c * ka, ka)
        kk = key_sc[pl.ds(k0, ka), :]
        bias = lax.cond(
            any_overflow,
            lambda: jnp.where((kk > thr_ge) | ((kk == thr_ge) & (k0 + pos_io <= jcut)), 0.0, NEG),
            lambda: jnp.where(kk >= thr_ge, 0.0, NEG))
        bias2 = jnp.concatenate([bias, bias], axis=1)
        kdc = kd_ref[pl.ds(k0, ka), :]
        return [_dot(kdc, qst[g]) + bias2 for g in pairs]

    for g, s0 in enumerate(masked_scores(0)):
        s_scs[g][0] = s0

    def attn_chunk(c, carry):
        slot = lax.rem(c, 2)
        ss = [s_scs[g][slot] for g in pairs]
        ss_next = masked_scores(jnp.minimum(c + 1, n_sub - 1))
        vaug = jnp.concatenate([vdt_ref[c], ones_rows], axis=0)
        old = [accs[g][...] for g in pairs]
        out, new = [], []
        for g in pairs:
            m_old = carry[g]
            m_new = jnp.maximum(m_old, _col_max(ss[g]))
            alpha = jnp.exp2(m_old - m_new)
            p = jnp.exp2((ss[g] - m_new).astype(BF16))
            new.append(alpha * old[g] + _dot(vaug, p))
            out.append(m_new)
        for g in pairs:
            accs[g][...] = new[g]
            s_scs[g][1 - slot] = ss_next[g]
        return tuple(out)

    ones_rows = jnp.ones((DENOM_ROWS, ka), BF16)
    init = (jnp.full((1, 2 * tq), NEG, F32),) * len(pairs)
    lax.fori_loop(0, n_sub, attn_chunk, init)
    for g in pairs:
        acc = accs[g][...]
        out_t = acc[:HEAD_DIM] * (1.0 / acc[HEAD_DIM:HEAD_DIM + 1])
        for j in range(2):
            hd = 2 * g + j
            o_ref[:, hd * HEAD_DIM:(hd + 1) * HEAD_DIM] = out_t[:, j * tq:(j + 1) * tq].T


def _dsa_attn(q_it, w_it, q_dt, k_i, k_d, v_dt, B, S):
    T = B * S
    tq = DSA_TQ
    nq = S // tq
    topk = min(DSA_MAX_TOPK, S // 4)
    qcol = lambda b, t: (0, b * nq + t)
    seq = lambda b, t: (b, 0)
    return pl.pallas_call(
        functools.partial(_dsa_kernel, topk=topk),
        out_shape=jax.ShapeDtypeStruct((T, DSA_W), F32),
        grid=(B, nq),
        in_specs=[
            pl.BlockSpec((IDX_HEADS * IDX_DIM, tq), qcol),
            pl.BlockSpec((IDX_HEADS, tq), qcol),
            pl.BlockSpec((DSA_W, tq), qcol),
            pl.BlockSpec((S, IDX_DIM), seq),
            pl.BlockSpec((S, HEAD_DIM), seq),
            pl.BlockSpec((S // DSA_KA, HEAD_DIM, DSA_KA), lambda b, t: (b, 0, 0)),
        ],
        out_specs=pl.BlockSpec((tq, DSA_W), lambda b, t: (b * nq + t, 0)),
        scratch_shapes=[pltpu.VMEM((S, tq), I32), pltpu.VMEM((S, tq), jnp.int16)]
        + [pltpu.VMEM((HEAD_DIM + DENOM_ROWS, 2 * tq), F32) for _ in range(DSA_HEADS // 2)]
        + [pltpu.VMEM((2, DSA_KA, 2 * tq), F32) for _ in range(DSA_HEADS // 2)],
        compiler_params=_cparams(("parallel", "arbitrary")),
        name="dsa_attn",
    )(q_it, w_it, q_dt, k_i, k_d, v_dt)


def _first_index_of_max(v, row_io, n_rows):
    m = jnp.max(v, axis=0, keepdims=True)
    idx = jnp.min(jnp.where(v == m, row_io, n_rows), axis=0, keepdims=True)
    return m, idx


def _post_kernel(x_ref, om_ref, od_ref, mod_ref, gm_ref, gd_ref, wout_ref, gffn_ref, wgu_ref, wds_ref,
                 wrh_ref, wrl_ref, rb_ref, tri_ref,
                 xpart_ref, h2r_ref, eidx_ref, rank_ref, gate_ref, cnt_ref, base_sc):
    i = pl.program_id(0)
    tm = x_ref.shape[0]
    gt1 = mod_ref[0, 2:3, :]
    sh2 = mod_ref[0, 3:4, :]
    sc2 = mod_ref[0, 4:5, :]
    gt2 = mod_ref[0, 5:6, :]

    mixed = jnp.concatenate([_rms(om_ref[...], gm_ref[...]), _rms(od_ref[...], gd_ref[...])], axis=1)
    x1 = x_ref[...] + gt1 * _dot(mixed.astype(BF16), wout_ref[...])
    h2 = _rms(x1, gffn_ref[...]) * (1.0 + sc2) + sh2
    h2b = h2.astype(BF16)

    au = _dot(h2b, wgu_ref[...])
    hs = (_silu(au[:, :D_SHARED]) * au[:, D_SHARED:]).astype(BF16)
    xpart_ref[...] = x1 + gt2 * _dot(hs, wds_ref[...])

    for c, slab in enumerate(_pack_row_words(h2)):
        _store_word_slab(h2r_ref, 0, tm, c, slab)

    h2lo = (h2 - h2b.astype(F32)).astype(BF16)
    logits = _dot_nt(wrh_ref[...], h2b) + _dot_nt(wrl_ref[...], h2b) + _dot_nt(wrh_ref[...], h2lo)
    scores = 1.0 / (1.0 + jnp.exp(-logits))
    biased = scores + rb_ref[...]

    g_io = lax.broadcasted_iota(I32, (GROUP_SIZE, tm), 0)
    gs_rows = []
    for g in range(N_GROUPS):
        blk = biased[g * GROUP_SIZE:(g + 1) * GROUP_SIZE, :]
        m1, i1 = _first_index_of_max(blk, g_io, GROUP_SIZE)
        m2 = jnp.max(jnp.where(g_io == i1, -jnp.inf, blk), axis=0, keepdims=True)
        gs_rows.append(m1 + m2)
    gs = jnp.concatenate(gs_rows, axis=0)
    gi = lax.broadcasted_iota(I32, (N_GROUPS, tm), 0)
    grank = jnp.zeros((N_GROUPS, tm), I32)
    for m in range(N_GROUPS):
        gm = gs[m:m + 1, :]
        grank = grank + jnp.where((gm > gs) | ((gm == gs) & (m < gi)), 1, 0)
    gsel = grank < TOPK_GROUPS
    masked = jnp.concatenate(
        [jnp.where(gsel[g:g + 1, :], biased[g * GROUP_SIZE:(g + 1) * GROUP_SIZE, :], -jnp.inf)
         for g in range(N_GROUPS)], axis=0)

    e_io = lax.broadcasted_iota(I32, (N_EXPERTS, tm), 0)
    e_rows, s_rows = [], []
    for _ in range(EXPERT_TOPK):
        _, idx = _first_index_of_max(masked, e_io, N_EXPERTS)
        hit = e_io == idx
        e_rows.append(idx)
        s_rows.append(jnp.sum(jnp.where(hit, scores, 0.0), axis=0, keepdims=True))
        masked = jnp.where(hit, -jnp.inf, masked)
    eidx = jnp.concatenate(e_rows, axis=0)
    sk = jnp.concatenate(s_rows, axis=0)
    gate_ref[...] = sk / jnp.sum(sk, axis=0, keepdims=True) * ROUTED_SCALE
    eidx_ref[...] = eidx

    @pl.when(i == 0)
    def _():
        base_sc[...] = jnp.zeros(base_sc.shape, F32)

    chosen = jnp.zeros((N_EXPERTS, tm), F32)
    for k in range(EXPERT_TOPK):
        chosen = chosen + jnp.where(e_io == e_rows[k], 1.0, 0.0)
    incl = _dot(chosen.astype(BF16), tri_ref[...])
    pos = base_sc[...] + incl - 1.0
    rank_ref[...] = jnp.concatenate(
        [jnp.sum(jnp.where(e_io == e_rows[k], pos, 0.0), axis=0, keepdims=True)
         for k in range(EXPERT_TOPK)], axis=0).astype(I32)
    base_sc[...] = base_sc[...] + incl[:, tm - 1:tm]
    cnt_ref[...] = jnp.broadcast_to(base_sc[...], cnt_ref.shape)


def _post_attn(x2, o_m, o_d, mod3, g_moba, g_dsa, w_out, g_ffn, w_gu_s, w_down_s, wr_hi, wr_lo, rbias, S):
    T, D = x2.shape
    tm = POST_TM
    nt_per_seq = S // tm
    row = lambda i: (i, 0)
    full = lambda i: (0, 0)
    tri = (jnp.arange(tm)[:, None] <= jnp.arange(tm)[None, :]).astype(BF16)
    n_words = D // ROW_WORDS
    return pl.pallas_call(
        _post_kernel,
        out_shape=[
            jax.ShapeDtypeStruct((T, D), F32),
            jax.ShapeDtypeStruct((T, n_words, LANES), I32),
            jax.ShapeDtypeStruct((EXPERT_TOPK, T), I32),
            jax.ShapeDtypeStruct((EXPERT_TOPK, T), I32),
            jax.ShapeDtypeStruct((EXPERT_TOPK, T), F32),
            jax.ShapeDtypeStruct((N_EXPERTS, LANES), F32),
        ],
        grid=(T // tm,),
        in_specs=[
            pl.BlockSpec((tm, D), row),
            pl.BlockSpec((tm, MOBA_W), row),
            pl.BlockSpec((tm, DSA_W), row),
            pl.BlockSpec((1, 6, D), lambda i: (i // nt_per_seq, 0, 0)),
            pl.BlockSpec((1, MOBA_W), full),
            pl.BlockSpec((1, DSA_W), full),
            pl.BlockSpec(w_out.shape, full),
            pl.BlockSpec((1, D), full),
            pl.BlockSpec(w_gu_s.shape, full),
            pl.BlockSpec(w_down_s.shape, full),
            pl.BlockSpec(wr_hi.shape, full),
            pl.BlockSpec(wr_lo.shape, full),
            pl.BlockSpec((N_EXPERTS, 1), full),
            pl.BlockSpec((tm, tm), full),
        ],
        out_specs=[
            pl.BlockSpec((tm, D), row),
            pl.BlockSpec((tm, n_words, LANES), lambda i: (i, 0, 0)),
            pl.BlockSpec((EXPERT_TOPK, tm), lambda i: (0, i)),
            pl.BlockSpec((EXPERT_TOPK, tm), lambda i: (0, i)),
            pl.BlockSpec((EXPERT_TOPK, tm), lambda i: (0, i)),
            pl.BlockSpec((N_EXPERTS, LANES), full),
        ],
        scratch_shapes=[pltpu.VMEM((N_EXPERTS, 1), F32)],
        compiler_params=_cparams(("arbitrary",)),
        name="post_attn",
    )(x2, o_m, o_d, mod3, g_moba, g_dsa, w_out, g_ffn, w_gu_s, w_down_s, wr_hi, wr_lo, rbias, tri)


def _row_copy_wait(rows_hbm, n_rows, sem):
    blk = rows_hbm.at[pl.ds(0, n_rows)]
    pltpu.make_async_copy(blk, blk, sem).wait()


def _slots_kernel(ps_ref, e_ref, r_ref, d_ref):
    e = e_ref[...]

    def body(x, acc):
        return jnp.where(e == x, ps_ref[x], acc)

    d_ref[...] = lax.fori_loop(0, N_EXPERTS, body, jnp.zeros(e.shape, I32)) + r_ref[...]


def _slots(pad_starts, eidx, rank):
    K, T = eidx.shape
    tm = min(T, 4096)
    blk = lambda: pl.BlockSpec((K, tm), lambda i, ps: (0, i))
    return pl.pallas_call(
        _slots_kernel,
        out_shape=jax.ShapeDtypeStruct((K, T), I32),
        grid_spec=pltpu.PrefetchScalarGridSpec(num_scalar_prefetch=1, grid=(T // tm,),
                                               in_specs=[blk(), blk()], out_specs=blk()),
        compiler_params=_cparams(("parallel",)),
        name="slots",
    )(pad_starts, eidx, rank)


def _dispatch_kernel(d_ref, h2r_ref, xs_ref, sem):
    tm = h2r_ref.shape[0]

    def body(t, carry):
        for k in range(EXPERT_TOPK):
            pltpu.make_async_copy(h2r_ref.at[t], xs_ref.at[d_ref[t * EXPERT_TOPK + k]],
                                  sem).start(priority=k % 2)
        return carry

    lax.fori_loop(0, tm, body, 0)
    _row_copy_wait(xs_ref, tm * EXPERT_TOPK, sem)


def _dispatch(dest_flat, h2r, n_rows_padded):
    T, n_words, _ = h2r.shape
    tm = DISP_TM
    return pl.pallas_call(
        _dispatch_kernel,
        out_shape=jax.ShapeDtypeStruct((n_rows_padded, n_words, LANES), I32),
        grid=(T // tm,),
        in_specs=[pl.BlockSpec((tm * EXPERT_TOPK,), lambda i: (i,), memory_space=pltpu.SMEM),
                  pl.BlockSpec((tm, n_words, LANES), lambda i: (i, 0, 0))],
        out_specs=pl.BlockSpec(memory_space=pl.ANY),
        scratch_shapes=[pltpu.SemaphoreType.DMA],
        compiler_params=_cparams(("arbitrary",), disable_bounds_checks=True),
        name="dispatch",
    )(dest_flat, h2r)


def _expert_kernel(ps_ref, cnt_ref, wg_ref, wu_ref, wd_ref, xs_ref, ys_ref,
                   wg_sc, wu_sc, wd_sc, xbuf, ybuf, sem_in, sem_out):
    e = pl.program_id(0)
    bm = EXP_BM
    n_words = xbuf.shape[1]
    cnt = cnt_ref[e]
    g0 = ps_ref[e] // bm
    nb = (cnt + bm - 1) // bm
    n_used = (ps_ref[N_EXPERTS - 1] + cnt_ref[N_EXPERTS - 1] + bm - 1) // bm

    def in_copy(g):
        slot = lax.rem(g, EXP_IN_SLOTS)
        return pltpu.make_async_copy(xs_ref.at[pl.ds(g * bm, bm)], xbuf.at[pl.ds(slot * bm, bm)],
                                     sem_in.at[slot])

    def out_copy(g):
        slot = lax.rem(g, EXP_OUT_SLOTS)
        return pltpu.make_async_copy(ybuf.at[pl.ds(slot * bm, bm)], ys_ref.at[pl.ds(g * bm, bm)],
                                     sem_out.at[slot])

    @pl.when(e == 0)
    def _():
        for g in range(EXP_AHEAD):
            @pl.when(g < n_used)
            def _():
                in_copy(g).start()

    @pl.when(nb > 0)
    def _():
        wg_sc[...] = wg_ref[...].astype(BF16)
        wu_sc[...] = wu_ref[...].astype(BF16)
        wd_sc[...] = wd_ref[...].astype(BF16)

        def body(b, carry):
            g = g0 + b

            @pl.when(g + EXP_AHEAD < n_used)
            def _():
                in_copy(g + EXP_AHEAD).start()

            in_copy(g).wait()

            @pl.when(g >= EXP_OUT_SLOTS)
            def _():
                out_copy(g - EXP_OUT_SLOTS).wait()

            xrow = lax.rem(g, EXP_IN_SLOTS) * bm
            yrow = lax.rem(g, EXP_OUT_SLOTS) * bm
            feats = []
            for c in range(n_words):
                feats += _unpack_row_words(_load_word_slab(xbuf, xrow, bm, c))
            x = jnp.concatenate(feats, axis=1)
            valid = lax.broadcasted_iota(I32, (bm, 1), 0) < cnt - b * bm
            xb = jnp.where(valid, x, 0.0).astype(BF16)
            a = _dot(xb, wg_sc[...])
            u = _dot(xb, wu_sc[...])
            hmid = (_silu(a) * u).astype(BF16)
            ob = _dot(hmid, wd_sc[...])
            for c, slab in enumerate(_pack_row_words(ob)):
                _store_word_slab(ybuf, yrow, bm, c, slab)
            out_copy(g).start()
            return carry

        lax.fori_loop(0, nb, body, 0)

    @pl.when(e == N_EXPERTS - 1)
    def _():
        for back in range(EXP_OUT_SLOTS, 0, -1):
            @pl.when(n_used >= back)
            def _():
                out_copy(n_used - back).wait()


def _experts(pad_starts, counts, xs, w_gate_e, w_up_e, w_down_e):
    E, D, DE = w_gate_e.shape
    n_words = xs.shape[1]
    bm = EXP_BM
    wsel = lambda e, ps, cnt: (e, 0, 0)
    anyspec = pl.BlockSpec(memory_space=pl.ANY)
    return pl.pallas_call(
        _expert_kernel,
        out_shape=jax.ShapeDtypeStruct(xs.shape, I32),
        grid_spec=pltpu.PrefetchScalarGridSpec(
            num_scalar_prefetch=2,
            grid=(E,),
            in_specs=[pl.BlockSpec((None, D, DE), wsel), pl.BlockSpec((None, D, DE), wsel),
                      pl.BlockSpec((None, DE, D), wsel), anyspec],
            out_specs=anyspec,
            scratch_shapes=[pltpu.VMEM((D, DE), BF16), pltpu.VMEM((D, DE), BF16), pltpu.VMEM((DE, D), BF16),
                            pltpu.VMEM((EXP_IN_SLOTS * bm, n_words, LANES), I32),
                            pltpu.VMEM((EXP_OUT_SLOTS * bm, n_words, LANES), I32),
                            pltpu.SemaphoreType.DMA((EXP_IN_SLOTS,)), pltpu.SemaphoreType.DMA((EXP_OUT_SLOTS,))],
        ),
        compiler_params=_cparams(("arbitrary",)),
        name="experts",
    )(pad_starts, counts, w_gate_e, w_up_e, w_down_e, xs)


def _combine_kernel(dcur_ref, dnxt_ref, g_ref, xpart_ref, mod_ref, gfin_ref, ys_ref, o_ref,
                    buf0, buf1, sem0, sem1):
    i = pl.program_id(0)
    n_steps = pl.num_programs(0)
    tm = xpart_ref.shape[0]
    n_words = buf0.shape[1]

    def issue(dest_ref, buf, sem):
        for t in range(tm):
            for k in range(EXPERT_TOPK):
                pltpu.make_async_copy(ys_ref.at[dest_ref[t * EXPERT_TOPK + k]], buf.at[k * tm + t],
                                      sem).start(priority=k % 2)

    def reduce_tile(buf):
        gt2 = mod_ref[0, 5:6, :]
        g = g_ref[...]
        cols = []
        for c in range(n_words):
            lo = jnp.zeros((tm, LANES), F32)
            hi = jnp.zeros((tm, LANES), F32)
            for k in range(EXPERT_TOPK):
                a, b = _unpack_row_words(_load_word_slab(buf, k * tm, tm, c))
                gk = g[:, k:k + 1]
                lo = lo + gk * a
                hi = hi + gk * b
            cols += [lo, hi]
        routed = jnp.concatenate(cols, axis=1)
        o_ref[...] = _rms(xpart_ref[...] + gt2 * routed, gfin_ref[...])

    @pl.when(i == 0)
    def _():
        issue(dcur_ref, buf0, sem0)

    for parity, (cur, nxt) in enumerate((((buf0, sem0), (buf1, sem1)), ((buf1, sem1), (buf0, sem0)))):
        @pl.when(i % 2 == parity)
        def _():
            _row_copy_wait(ys_ref, tm * EXPERT_TOPK, cur[1])
            issue(dnxt_ref, *nxt)
            reduce_tile(cur[0])

            @pl.when(i + 1 == n_steps)
            def _():
                _row_copy_wait(ys_ref, tm * EXPERT_TOPK, nxt[1])


def _combine(dest, gates_t, xpart, mod3, g_final, ys, S):
    T, D = xpart.shape
    tm = COMB_TM
    n_steps = T // tm
    nt_per_seq = S // tm
    n_words = ys.shape[1]
    return pl.pallas_call(
        _combine_kernel,
        out_shape=jax.ShapeDtypeStruct((T, D), F32),
        grid=(n_steps,),
        in_specs=[
            pl.BlockSpec((tm * EXPERT_TOPK,), lambda i: (i,), memory_space=pltpu.SMEM),
            pl.BlockSpec((tm * EXPERT_TOPK,), lambda i: (jnp.minimum(i + 1, n_steps - 1),),
                         memory_space=pltpu.SMEM),
            pl.BlockSpec((tm, EXPERT_TOPK), lambda i: (i, 0)),
            pl.BlockSpec((tm, D), lambda i: (i, 0)),
            pl.BlockSpec((1, 6, D), lambda i: (i // nt_per_seq, 0, 0)),
            pl.BlockSpec((1, D), lambda i: (0, 0)),
            pl.BlockSpec(memory_space=pl.ANY),
        ],
        out_specs=pl.BlockSpec((tm, D), lambda i: (i, 0)),
        scratch_shapes=[pltpu.VMEM((EXPERT_TOPK * tm, n_words, LANES), I32),
                        pltpu.VMEM((EXPERT_TOPK * tm, n_words, LANES), I32),
                        pltpu.SemaphoreType.DMA, pltpu.SemaphoreType.DMA],
        compiler_params=_cparams(("arbitrary",), disable_bounds_checks=True),
        name="combine",
    )(dest, dest, gates_t, xpart, mod3, g_final, ys)


def _layer(x2, mod3, S, g_mix, w_in, g_kv, w_kv_up, g_moba_out, g_dsa_out, w_out, g_ffn, w_router,
           router_bias, w_gate_e, w_up_e, w_down_e, w_gate_s, w_up_s, w_down_s, g_final, tab_h, half_h,
           tab_i, half_i):
    T, D = x2.shape
    B = T // S
    w_in_p = jnp.pad(w_in, ((0, 0), (0, _C_END - w_in.shape[1]))).astype(BF16)
    (q_mt, k_m, v_mt, kmean, q_dt, k_d, v_dt, q_it, k_i, w_it) = _in_proj(
        x2, mod3, g_mix.reshape(1, D), w_in_p, g_kv.reshape(1, KV_LORA), w_kv_up.astype(BF16),
        tab_h, half_h, tab_i, half_i, S)
    o_m = _moba_attn(q_mt, k_m, v_mt, kmean, B, S)
    o_d = _dsa_attn(q_it, w_it, q_dt, k_i, k_d, v_dt, B, S)

    wr_t = w_router.T
    wr_hi = wr_t.astype(BF16)
    wr_lo = (wr_t - wr_hi.astype(F32)).astype(BF16)
    w_gu_s = jnp.concatenate([w_gate_s, w_up_s], axis=1).astype(BF16)
    xpart, h2r, eidx, rank, gates, cnt = _post_attn(
        x2, o_m, o_d, mod3, g_moba_out.reshape(1, MOBA_W), g_dsa_out.reshape(1, DSA_W), w_out.astype(BF16),
        g_ffn.reshape(1, D), w_gu_s, w_down_s.astype(BF16), wr_hi, wr_lo,
        router_bias.reshape(N_EXPERTS, 1), S)

    bm = EXP_BM
    n_blocks = T * EXPERT_TOPK // bm + N_EXPERTS
    counts = cnt[:, 0].astype(I32)
    padded = (counts + bm - 1) // bm * bm
    pad_ends = jnp.cumsum(padded)
    pad_starts = (pad_ends - padded).astype(I32)
    dest_flat = _slots(pad_starts, eidx, rank).T.reshape(-1)
    xs = _dispatch(dest_flat, h2r, n_blocks * bm)
    ys = _experts(pad_starts, counts, xs, w_gate_e, w_up_e, w_down_e)
    return _combine(dest_flat, gates.T, xpart, mod3, g_final.reshape(1, D), ys, S)


def kernel(x, c, w_ada, b_ada, g_mix, w_in, g_kv, w_kv_up, g_moba_out, g_dsa_out, w_out, g_ffn, w_router,
           router_bias, w_gate_e, w_up_e, w_down_e, w_gate_s, w_up_s, w_down_s, g_final):
    B, S, D = x.shape
    depth = w_ada.shape[0]
    assert depth == 1, "the final norm is fused into the single layer"
    assert S % PROJ_TM == 0 and S % DSA_KC == 0 and S % POST_TM == 0 and S >= 4 * DSA_MAX_TOPK
    tab_h, half_h = _rope_tables(S, HEAD_DIM, 1)
    tab_i, half_i = _rope_tables(S, IDX_DIM, LANES // IDX_DIM)
    x2 = x.reshape(B * S, D)
    sq = lambda a: a.reshape(a.shape[1:])
    mod3 = _ada_mod(c, sq(w_ada), sq(b_ada)).reshape(B, 6, D)
    out = _layer(x2, mod3, S, sq(g_mix), sq(w_in), sq(g_kv), sq(w_kv_up), sq(g_moba_out), sq(g_dsa_out),
                 sq(w_out), sq(g_ffn), sq(w_router), sq(router_bias), sq(w_gate_e), sq(w_up_e), sq(w_down_e),
                 sq(w_gate_s), sq(w_up_s), sq(w_down_s), g_final, tab_h, half_h, tab_i, half_i)
    return out.reshape(B, S, D)
```

```python
import functools

import jax
import jax.numpy as jnp
from jax import lax
from jax.experimental import pallas as pl
from jax.experimental.pallas import tpu as pltpu

HEAD_DIM = 128
MOBA_HEADS = 4
DSA_HEADS = 4
MOBA_W = MOBA_HEADS * HEAD_DIM
DSA_W = DSA_HEADS * HEAD_DIM
MOBA_BLOCK = 256
MOBA_TOPK = 3
DSA_MAX_TOPK = 256
KV_LORA = 256
IDX_HEADS = 8
IDX_DIM = 64
ROPE_THETA = 500000.0
ROPE_FRACTION_DIV = 4
N_EXPERTS = 256
EXPERT_TOPK = 8
N_GROUPS = 8
TOPK_GROUPS = 4
GROUP_SIZE = N_EXPERTS // N_GROUPS
D_EXPERT = 256
D_SHARED = 256
ROUTED_SCALE = 2.5
EPS = 1e-6

LANES = 128
SUBLANES = 8
VMEM_LIMIT = 56 * 1024 * 1024

PROJ_TM = 512
DSA_TQ = 256
DSA_KC = 256
DSA_KA = 256
DSA_KS = 128
POST_TM = 512
DISP_TM = 2048
EXP_BM = 256
EXP_AHEAD = 4
EXP_IN_SLOTS = EXP_AHEAD + 1
EXP_OUT_SLOTS = 3
COMB_TM = 128
REDUCE_CHAINS = 4
DENOM_ROWS = 16
NEG = -1e30
INT_MIN = -2147483648
LOG2E = 1.4426950408889634

F32 = jnp.float32
BF16 = jnp.bfloat16
I32 = jnp.int32


def _cparams(sem, **kw):
    return pltpu.CompilerParams(dimension_semantics=sem, vmem_limit_bytes=VMEM_LIMIT, **kw)


def _dot(a, b):
    return jnp.dot(a, b, preferred_element_type=F32)


def _dot_nt(a, b):
    return lax.dot_general(a, b, (((1,), (1,)), ((), ())), preferred_element_type=F32)


def _silu(x):
    return x * (1.0 / (1.0 + jnp.exp(-x)))


def _rms(x, g):
    return x * lax.rsqrt(jnp.mean(x * x, axis=-1, keepdims=True) + EPS) * g


def _rows_to_tile(op, x):
    parts = [x[i:i + SUBLANES] for i in range(0, x.shape[0], SUBLANES)]
    n_chains = min(REDUCE_CHAINS, len(parts))
    accs = parts[:n_chains]
    for i in range(n_chains, len(parts)):
        accs[i % n_chains] = op(accs[i % n_chains], parts[i])
    while len(accs) > 1:
        accs = [op(accs[i], accs[i + 1]) for i in range(0, len(accs) - 1, 2)] + ([accs[-1]] if len(accs) % 2 else [])
    return accs[0]


def _col_max(x):
    return jnp.max(_rows_to_tile(jnp.maximum, x), axis=0, keepdims=True)


def _col_sum(x):
    return jnp.sum(_rows_to_tile(jnp.add, x), axis=0, keepdims=True)


ROW_WORDS = 2 * LANES


def _pack_row_words(x):
    slabs = []
    for c in range(x.shape[1] // ROW_WORDS):
        lo = pltpu.bitcast(x[:, c * ROW_WORDS:c * ROW_WORDS + LANES].astype(BF16).astype(F32), I32)
        hi = pltpu.bitcast(x[:, c * ROW_WORDS + LANES:(c + 1) * ROW_WORDS].astype(BF16).astype(F32), I32)
        slabs.append(lax.shift_right_logical(lo, 16) | hi)
    return slabs


def _word_slab_index(rows_ref, row0, n_rows, c):
    n_words = rows_ref.shape[1]
    flat = rows_ref.reshape(rows_ref.shape[0] * n_words, LANES)
    return flat, pl.ds(row0 * n_words + c, n_rows, stride=n_words)


def _load_word_slab(rows_ref, row0, n_rows, c):
    flat, idx = _word_slab_index(rows_ref, row0, n_rows, c)
    return flat[idx, :]


def _store_word_slab(rows_ref, row0, n_rows, c, value):
    flat, idx = _word_slab_index(rows_ref, row0, n_rows, c)
    flat[idx, :] = value


def _unpack_row_words(u):
    return pltpu.bitcast(lax.shift_left(u, 16), F32), pltpu.bitcast(u & jnp.int32(-65536), F32)


def _ada_kernel(c_ref, w_ref, b_ref, o_ref):
    ca = _silu(c_ref[...])
    o_ref[...] = jnp.dot(ca, w_ref[...], preferred_element_type=F32,
                         precision=lax.Precision.HIGHEST) + b_ref[...]


def _ada_mod(c, w_ada, b_ada):
    B, D = c.shape
    N = w_ada.shape[1]
    tn = 1024
    return pl.pallas_call(
        _ada_kernel,
        out_shape=jax.ShapeDtypeStruct((B, N), F32),
        grid=(N // tn,),
        in_specs=[pl.BlockSpec((B, D), lambda j: (0, 0)),
                  pl.BlockSpec((D, tn), lambda j: (0, j)),
                  pl.BlockSpec((1, tn), lambda j: (0, j))],
        out_specs=pl.BlockSpec((B, tn), lambda j: (0, j)),
        compiler_params=_cparams(("arbitrary",)),
        name="ada_mod",
    )(c, w_ada, b_ada.reshape(1, N))


def _rope_tables(seq, head_dim, heads_per_vreg):
    rot = head_dim // ROPE_FRACTION_DIV
    half = rot // 2
    inv = jnp.float32(ROPE_THETA) ** (-(jnp.arange(0, rot, 2, dtype=F32) / rot))
    ang = jnp.arange(seq, dtype=F32)[:, None] * inv[None, :]
    cos, sin = jnp.cos(ang), jnp.sin(ang)
    ones = jnp.ones((seq, head_dim - rot), F32)
    zeros_h = jnp.zeros((seq, half), F32)
    zeros_r = jnp.zeros((seq, head_dim - rot), F32)
    c = jnp.concatenate([cos, cos, ones], axis=1)
    sp = jnp.concatenate([zeros_h, sin, zeros_r], axis=1)
    sm = jnp.concatenate([-sin, zeros_h, zeros_r], axis=1)
    rep = lambda t: jnp.tile(t, (1, heads_per_vreg))
    return jnp.stack([rep(c), rep(sp), rep(sm)], axis=0), half


def _rope(x, tab_ref, half):
    return (x * tab_ref[0] + pltpu.roll(x, half, 1) * tab_ref[1]
            + pltpu.roll(x, LANES - half, 1) * tab_ref[2])


_C_QM, _C_KM, _C_VM, _C_QD = 0, MOBA_W, 2 * MOBA_W, 3 * MOBA_W
_C_CKV = 3 * MOBA_W + DSA_W
_C_QI = _C_CKV + KV_LORA
_C_KI = _C_QI + IDX_HEADS * IDX_DIM
_C_END = _C_KI + LANES


def _in_proj_kernel(x_ref, mod_ref, gmix_ref, w_ref, gkv_ref, wkv_ref, tabh_ref, tabi_ref,
                    qmt_ref, km_ref, vmt_ref, kmean_ref, qdt_ref, kd_ref, vdt_ref, qit_ref, ki_ref, wit_ref,
                    *, half_h, half_i):
    tm = x_ref.shape[0]
    x = x_ref[...]
    sh1 = mod_ref[0, 0:1, :]
    sc1 = mod_ref[0, 1:2, :]
    h = (_rms(x, gmix_ref[...]) * (1.0 + sc1) + sh1).astype(BF16)

    def proj(c0, width):
        return _dot(h, w_ref[:, c0:c0 + width])

    q_scale = HEAD_DIM ** -0.5 * LOG2E
    nblk = tm // MOBA_BLOCK
    qm = proj(_C_QM, MOBA_W)
    km = proj(_C_KM, MOBA_W)
    vm = proj(_C_VM, MOBA_W)
    for hd in range(MOBA_HEADS):
        sl = slice(hd * HEAD_DIM, (hd + 1) * HEAD_DIM)
        qmt_ref[sl, :] = (_rope(qm[:, sl], tabh_ref, half_h) * q_scale).T.astype(BF16)
        kr = _rope(km[:, sl], tabh_ref, half_h)
        km_ref[:, sl] = kr.astype(BF16)
        for blk in range(nblk):
            rows = slice(blk * MOBA_BLOCK, (blk + 1) * MOBA_BLOCK)
            kmean_ref[blk:blk + 1, sl] = jnp.mean(kr[rows], axis=0, keepdims=True)
            vmt_ref[blk, sl, :] = vm[rows, sl].T.astype(BF16)
    qd = proj(_C_QD, DSA_W)
    for hd in range(DSA_HEADS):
        sl = slice(hd * HEAD_DIM, (hd + 1) * HEAD_DIM)
        qdt_ref[sl, :] = (_rope(qd[:, sl], tabh_ref, half_h) * q_scale).T.astype(BF16)
    ckv = proj(_C_CKV, KV_LORA)
    kv = _dot(_rms(ckv, gkv_ref[...]).astype(BF16), wkv_ref[...])
    kd_ref[...] = _rope(kv[:, :HEAD_DIM], tabh_ref, half_h).astype(BF16)
    for ch in range(tm // DSA_KA):
        vdt_ref[ch] = kv[ch * DSA_KA:(ch + 1) * DSA_KA, HEAD_DIM:].T.astype(BF16)
    qi = proj(_C_QI, IDX_HEADS * IDX_DIM)
    for j in range(IDX_HEADS * IDX_DIM // LANES):
        sl = slice(j * LANES, (j + 1) * LANES)
        qit_ref[sl, :] = _rope(qi[:, sl], tabi_ref, half_i).T.astype(BF16)
    kw = proj(_C_KI, LANES)
    ki_ref[...] = _rope(kw, tabi_ref, half_i)[:, :IDX_DIM].astype(BF16)
    wit_ref[...] = kw.T[IDX_DIM:IDX_DIM + IDX_HEADS, :] * (IDX_HEADS ** -0.5 * IDX_DIM ** -0.5)


def _in_proj(x2, mod3, g_mix, w_in_p, g_kv, w_kv_up, tab_h, half_h, tab_i, half_i, S):
    T, D = x2.shape
    tm = PROJ_TM
    nt_per_seq = S // tm
    row = lambda i: (i, 0)
    col = lambda i: (0, i)
    nb = tm // MOBA_BLOCK
    nc = tm // DSA_KA
    outs = [
        jax.ShapeDtypeStruct((MOBA_W, T), BF16),
        jax.ShapeDtypeStruct((T, MOBA_W), BF16),
        jax.ShapeDtypeStruct((T // MOBA_BLOCK, MOBA_W, MOBA_BLOCK), BF16),
        jax.ShapeDtypeStruct((T // tm, nb, MOBA_W), F32),
        jax.ShapeDtypeStruct((DSA_W, T), BF16),
        jax.ShapeDtypeStruct((T, HEAD_DIM), BF16),
        jax.ShapeDtypeStruct((T // DSA_KA, HEAD_DIM, DSA_KA), BF16),
        jax.ShapeDtypeStruct((IDX_HEADS * IDX_DIM, T), BF16),
        jax.ShapeDtypeStruct((T, IDX_DIM), BF16),
        jax.ShapeDtypeStruct((IDX_HEADS, T), F32),
    ]
    out_specs = [
        pl.BlockSpec((MOBA_W, tm), col), pl.BlockSpec((tm, MOBA_W), row),
        pl.BlockSpec((nb, MOBA_W, MOBA_BLOCK), lambda i: (i, 0, 0)),
        pl.BlockSpec((None, nb, MOBA_W), lambda i: (i, 0, 0)),
        pl.BlockSpec((DSA_W, tm), col), pl.BlockSpec((tm, HEAD_DIM), row),
        pl.BlockSpec((nc, HEAD_DIM, DSA_KA), lambda i: (i, 0, 0)),
        pl.BlockSpec((IDX_HEADS * IDX_DIM, tm), col), pl.BlockSpec((tm, IDX_DIM), row),
        pl.BlockSpec((IDX_HEADS, tm), col),
    ]
    res = pl.pallas_call(
        functools.partial(_in_proj_kernel, half_h=half_h, half_i=half_i),
        out_shape=outs,
        grid=(T // tm,),
        in_specs=[
            pl.BlockSpec((tm, D), row),
            pl.BlockSpec((1, 6, D), lambda i: (i // nt_per_seq, 0, 0)),
            pl.BlockSpec((1, D), lambda i: (0, 0)),
            pl.BlockSpec((D, _C_END), lambda i: (0, 0)),
            pl.BlockSpec((1, KV_LORA), lambda i: (0, 0)),
            pl.BlockSpec((KV_LORA, 2 * HEAD_DIM), lambda i: (0, 0)),
            pl.BlockSpec((3, tm, LANES), lambda i: (0, i % nt_per_seq, 0)),
            pl.BlockSpec((3, tm, LANES), lambda i: (0, i % nt_per_seq, 0)),
        ],
        out_specs=out_specs,
        compiler_params=_cparams(("parallel",)),
        name="in_proj",
    )(x2, mod3, g_mix, w_in_p, g_kv, w_kv_up, tab_h, tab_i)
    res = list(res)
    res[3] = res[3].reshape(T // MOBA_BLOCK, MOBA_W)
    return res


def _moba_kernel(qt_ref, k_ref, vt_ref, kmean_ref, o_ref, bias_sc, *head_scratch):
    accs, s_scs = head_scratch[:MOBA_HEADS], head_scratch[MOBA_HEADS:]
    qi = pl.program_id(1)
    blk = MOBA_BLOCK
    nb = kmean_ref.shape[0]
    heads = range(MOBA_HEADS)
    hsl = [slice(hd * HEAD_DIM, (hd + 1) * HEAD_DIM) for hd in heads]
    qts = [qt_ref[hsl[hd], :] for hd in heads]
    row = lax.broadcasted_iota(I32, (nb, blk), 0)
    past = row < qi
    start = pl.multiple_of(qi * blk, blk)
    k_io = lax.broadcasted_iota(I32, (blk, blk), 0)
    q_io = lax.broadcasted_iota(I32, (blk, blk), 1)

    gates = []
    for hd in heads:
        km = kmean_ref[:, hsl[hd]]
        km_hi = km.astype(BF16)
        km_lo = (km - km_hi.astype(F32)).astype(BF16)
        gate = _dot(km_hi, qts[hd]) + _dot(km_lo, qts[hd])
        gates.append(jnp.where(past, gate, -jnp.inf))
    k_own = k_ref[pl.ds(start, blk), :]
    own = [jnp.where(k_io <= q_io, _dot(k_own[:, hsl[hd]], qts[hd]), NEG) for hd in heads]
    biases = [jnp.full((nb, blk), NEG, F32) for _ in heads]
    for _ in range(MOBA_TOPK):
        for hd in heads:
            _, idx = _first_index_of_max(gates[hd], row, nb)
            hit = row == idx
            biases[hd] = jnp.where(hit, 0.0, biases[hd])
            gates[hd] = jnp.where(hit, -jnp.inf, gates[hd])
    ones_rows = jnp.ones((DENOM_ROWS, blk), BF16)

    def v_aug(n, hd):
        return jnp.concatenate([vt_ref[n, hsl[hd], :], ones_rows], axis=0)

    init = []
    for hd in heads:
        bias_sc[hd] = jnp.where(past, biases[hd], NEG)
        m0 = _col_max(own[hd])
        p = jnp.exp2((own[hd] - m0).astype(BF16))
        accs[hd][...] = _dot(v_aug(qi, hd), p)
        init.append(m0)

    def masked_scores(n):
        kb = k_ref[pl.ds(pl.multiple_of(n * blk, blk), blk), :]
        return [_dot(kb[:, hsl[hd]], qts[hd]) + bias_sc[hd, pl.ds(n, 1), :] for hd in heads]

    for hd, s0 in enumerate(masked_scores(0)):
        s_scs[hd][0] = s0

    def body(n, carry):
        slot = lax.rem(n, 2)
        sbs = [s_scs[hd][slot] for hd in heads]
        nxt = masked_scores(jnp.minimum(n + 1, qi - 1))
        out = []
        for hd in heads:
            m_old = carry[hd]
            m_new = jnp.maximum(m_old, _col_max(sbs[hd]))
            alpha = jnp.exp2(m_old - m_new)
            pb = jnp.exp2((sbs[hd] - m_new).astype(BF16))
            accs[hd][...] = alpha * accs[hd][...] + _dot(v_aug(n, hd), pb)
            out.append(m_new)
        for hd in heads:
            s_scs[hd][1 - slot] = nxt[hd]
        return tuple(out)

    lax.fori_loop(0, qi, body, tuple(init))
    for hd in heads:
        acc = accs[hd][...]
        o_ref[:, hsl[hd]] = (acc[:HEAD_DIM] * (1.0 / acc[HEAD_DIM:HEAD_DIM + 1])).T


def _moba_attn(q_mt, k_m, v_mt, kmean, B, S):
    T = B * S
    blk = MOBA_BLOCK
    nq = S // blk
    return pl.pallas_call(
        _moba_kernel,
        out_shape=jax.ShapeDtypeStruct((T, MOBA_W), F32),
        grid=(B, nq),
        in_specs=[
            pl.BlockSpec((MOBA_W, blk), lambda b, i: (0, b * nq + i)),
            pl.BlockSpec((S, MOBA_W), lambda b, i: (b, 0)),
            pl.BlockSpec((nq, MOBA_W, blk), lambda b, i: (b, 0, 0)),
            pl.BlockSpec((nq, MOBA_W), lambda b, i: (b, 0)),
        ],
        out_specs=pl.BlockSpec((blk, MOBA_W), lambda b, i: (b * nq + i, 0)),
        scratch_shapes=[pltpu.VMEM((MOBA_HEADS, nq, blk), F32)]
        + [pltpu.VMEM((HEAD_DIM + DENOM_ROWS, blk), F32) for _ in range(MOBA_HEADS)]
        + [pltpu.VMEM((2, blk, blk), F32) for _ in range(MOBA_HEADS)],
        compiler_params=_cparams(("parallel", "arbitrary")),
        name="moba_attn",
    )(q_mt, k_m, v_mt, kmean)


def _sortable_key(x):
    b = pltpu.bitcast(x, I32)
    return jnp.where(b >= 0, b, b ^ jnp.int32(0x7FFFFFFF))


def _dsa_kernel(qit_ref, wit_ref, qdt_ref, ki_ref, kd_ref, vdt_ref, o_ref, key_sc, high_sc, *pair_scratch, topk):
    accs, s_scs = pair_scratch[:DSA_HEADS // 2], pair_scratch[DSA_HEADS // 2:]
    t = pl.program_id(1)
    tq, kc = DSA_TQ, DSA_KC
    S = key_sc.shape[0]
    q0 = t * tq
    n_chunks = (q0 + tq + kc - 1) // kc
    key_io = lax.broadcasted_iota(I32, (kc, tq), 0)
    q_pos = q0 + lax.broadcasted_iota(I32, (kc, tq), 1)
    w = wit_ref[...]

    def score_chunk(c, carry):
        k0 = pl.multiple_of(c * kc, kc)
        kic = ki_ref[pl.ds(k0, kc), :]
        acc = jnp.zeros((kc, tq), F32)
        for hd in range(IDX_HEADS):
            lg = _dot(kic, qit_ref[hd * IDX_DIM:(hd + 1) * IDX_DIM, :])
            acc = acc + jnp.maximum(lg, 0.0) * w[hd:hd + 1, :]
        acc = jnp.where(acc == 0.0, 0.0, acc)
        key = _sortable_key(acc)
        key = jnp.where(k0 + key_io <= q_pos, key, INT_MIN)
        key_sc[pl.ds(k0, kc), :] = key
        high_sc[pl.ds(k0, kc), :] = lax.shift_right_arithmetic(key, 16).astype(jnp.int16)
        return carry

    lax.fori_loop(0, n_chunks, score_chunk, 0)

    ks = DSA_KS
    ks_io = lax.broadcasted_iota(I32, (ks, tq), 0)

    def column_total(slab_fn, dtype):
        rows = SUBLANES * (4 // jnp.dtype(dtype).itemsize)

        def cbody(c, accs):
            vals = slab_fn(pl.multiple_of(c * ks, ks))
            accs = list(accs)
            for i in range(ks // rows):
                j = i % REDUCE_CHAINS
                accs[j] = accs[j] + vals[i * rows:(i + 1) * rows]
            return tuple(accs)
        zero = jnp.zeros((rows, tq), dtype)
        accs = lax.fori_loop(0, n_chunks * (kc // ks), cbody, (zero,) * REDUCE_CHAINS)
        total = accs[0].astype(I32)
        for a in accs[1:]:
            total = total + a.astype(I32)
        return jnp.sum(total, axis=0, keepdims=True)

    def count(pred_fn):
        return column_total(lambda k0: jnp.where(pred_fn(key_sc[pl.ds(k0, ks), :], k0), 1, 0), I32)

    def count_ge_high(cand):
        c16 = lax.shift_right_arithmetic(cand, 16).astype(jnp.int16)
        one, zero = jnp.int16(1), jnp.int16(0)
        return column_total(lambda k0: jnp.where(high_sc[pl.ds(k0, ks), :] >= c16, one, zero), jnp.int16)

    thr0 = jnp.where(count_ge_high(jnp.zeros((1, tq), I32)) >= topk, 0, INT_MIN).astype(I32)

    def high_step(i, thr):
        cand = thr | lax.shift_left(jnp.int32(1), 30 - i)
        return jnp.where(count_ge_high(cand) >= topk, cand, thr)

    thr = lax.fori_loop(0, 15, high_step, thr0)

    high = lax.shift_right_arithmetic(thr, 16)
    h16 = high.astype(jnp.int16)
    one16, zero16 = jnp.int16(1), jnp.int16(0)
    n_above = column_total(lambda k0: jnp.where(high_sc[pl.ds(k0, ks), :] > h16, one16, zero16), jnp.int16)

    def repack(c, carry):
        k0 = pl.multiple_of(c * kc, kc)
        kk = key_sc[pl.ds(k0, kc), :]
        low = (kk & 0xFFFF) - 32768
        same = lax.shift_right_arithmetic(kk, 16) == high
        high_sc[pl.ds(k0, kc), :] = jnp.where(same, low, -32768).astype(jnp.int16)
        return carry

    lax.fori_loop(0, n_chunks, repack, 0)

    def low_step(i, low_bits):
        cand = low_bits | lax.shift_left(jnp.int32(1), 15 - i)
        c16 = (cand - 32768).astype(jnp.int16)
        cnt = n_above + column_total(
            lambda k0: jnp.where(high_sc[pl.ds(k0, ks), :] >= c16, one16, zero16), jnp.int16)
        return jnp.where(cnt >= topk, cand, low_bits)

    thr = thr | lax.fori_loop(0, 16, low_step, jnp.zeros((1, tq), I32))

    n_gt = count(lambda kk, k0: kk > thr)
    n_ge = count(lambda kk, k0: kk >= thr)
    need = topk - n_gt
    overflow = (n_ge > topk) & (thr != INT_MIN)
    any_overflow = jnp.max(jnp.where(overflow, 1, 0)) > 0
    nbits = max(1, (S - 1).bit_length())

    def cut_search():
        def step(i, lo):
            cand = lo | lax.shift_left(jnp.int32(1), nbits - 1 - i)
            cnt = count(lambda kk, k0: (kk == thr) & (k0 + ks_io < cand))
            return jnp.where(cnt >= need, lo, cand)
        return lax.fori_loop(0, nbits, step, jnp.zeros((1, tq), I32))

    jcut = lax.cond(any_overflow, cut_search, lambda: jnp.zeros((1, tq), I32))
    jcut = jnp.where(overflow, jcut, S)
    thr_ge = jnp.where(thr == INT_MIN, INT_MIN + 1, thr)

    pairs = range(DSA_HEADS // 2)
    qst = [jnp.concatenate([qdt_ref[(2 * g + j) * HEAD_DIM:(2 * g + j + 1) * HEAD_DIM, :] for j in range(2)],
                           axis=1) for g in pairs]
    for g in pairs:
        accs[g][...] = jnp.zeros(accs[g].shape, F32)

    ka = DSA_KA
    pos_io = lax.broadcasted_iota(I32, (ka, tq), 0)

    n_sub = (q0 + tq + ka - 1) // ka

    def masked_scores(c):
        k0 = pl.multiple_of(c * ka, ka)
        kk = key_sc[pl.ds(k0, ka), :]
        bias = lax.cond(
            any_overflow,
            lambda: jnp.where((kk > thr_ge) | ((kk == thr_ge) & (k0 + pos_io <= jcut)), 0.0, NEG),
            lambda: jnp.where(kk >= thr_ge, 0.0, NEG))
        bias2 = jnp.concatenate([bias, bias], axis=1)
        kdc = kd_ref[pl.ds(k0, ka), :]
        return [_dot(kdc, qst[g]) + bias2 for g in pairs]

    for g, s0 in enumerate(masked_scores(0)):
        s_scs[g][0] = s0

    def attn_chunk(c, carry):
        slot = lax.rem(c, 2)
        ss = [s_scs[g][slot] for g in pairs]
        ss_next = masked_scores(jnp.minimum(c + 1, n_sub - 1))
        vaug = jnp.concatenate([vdt_ref[c], ones_rows], axis=0)
        old = [accs[g][...] for g in pairs]
        out, new = [], []
        for g in pairs:
            m_old = carry[g]
            m_new = jnp.maximum(m_old, _col_max(ss[g]))
            alpha = jnp.exp2(m_old - m_new)
            p = jnp.exp2((ss[g] - m_new).astype(BF16))
            new.append(alpha * old[g] + _dot(vaug, p))
            out.append(m_new)
        for g in pairs:
            accs[g][...] = new[g]
            s_scs[g][1 - slot] = ss_next[g]
        return tuple(out)

    ones_rows = jnp.ones((DENOM_ROWS, ka), BF16)
    init = (jnp.full((1, 2 * tq), NEG, F32),) * len(pairs)
    lax.fori_loop(0, n_sub, attn_chunk, init)
    for g in pairs:
        acc = accs[g][...]
        out_t = acc[:HEAD_DIM] * (1.0 / acc[HEAD_DIM:HEAD_DIM + 1])
        for j in range(2):
            hd = 2 * g + j
            o_ref[:, hd * HEAD_DIM:(hd + 1) * HEAD_DIM] = out_t[:, j * tq:(j + 1) * tq].T


def _dsa_attn(q_it, w_it, q_dt, k_i, k_d, v_dt, B, S):
    T = B * S
    tq = DSA_TQ
    nq = S // tq
    topk = min(DSA_MAX_TOPK, S // 4)
    qcol = lambda b, t: (0, b * nq + t)
    seq = lambda b, t: (b, 0)
    return pl.pallas_call(
        functools.partial(_dsa_kernel, topk=topk),
        out_shape=jax.ShapeDtypeStruct((T, DSA_W), F32),
        grid=(B, nq),
        in_specs=[
            pl.BlockSpec((IDX_HEADS * IDX_DIM, tq), qcol),
            pl.BlockSpec((IDX_HEADS, tq), qcol),
            pl.BlockSpec((DSA_W, tq), qcol),
            pl.BlockSpec((S, IDX_DIM), seq),
            pl.BlockSpec((S, HEAD_DIM), seq),
            pl.BlockSpec((S // DSA_KA, HEAD_DIM, DSA_KA), lambda b, t: (b, 0, 0)),
        ],
        out_specs=pl.BlockSpec((tq, DSA_W), lambda b, t: (b * nq + t, 0)),
        scratch_shapes=[pltpu.VMEM((S, tq), I32), pltpu.VMEM((S, tq), jnp.int16)]
        + [pltpu.VMEM((HEAD_DIM + DENOM_ROWS, 2 * tq), F32) for _ in range(DSA_HEADS // 2)]
        + [pltpu.VMEM((2, DSA_KA, 2 * tq), F32) for _ in range(DSA_HEADS // 2)],
        compiler_params=_cparams(("parallel", "arbitrary")),
        name="dsa_attn",
    )(q_it, w_it, q_dt, k_i, k_d, v_dt)


def _first_index_of_max(v, row_io, n_rows):
    m = jnp.max(v, axis=0, keepdims=True)
    idx = jnp.min(jnp.where(v == m, row_io, n_rows), axis=0, keepdims=True)
    return m, idx


def _post_kernel(x_ref, om_ref, od_ref, mod_ref, gm_ref, gd_ref, wout_ref, gffn_ref, wgu_ref, wds_ref,
                 wrh_ref, wrl_ref, rb_ref, tri_ref,
                 xpart_ref, h2r_ref, eidx_ref, rank_ref, gate_ref, cnt_ref, base_sc):
    i = pl.program_id(0)
    tm = x_ref.shape[0]
    gt1 = mod_ref[0, 2:3, :]
    sh2 = mod_ref[0, 3:4, :]
    sc2 = mod_ref[0, 4:5, :]
    gt2 = mod_ref[0, 5:6, :]

    mixed = jnp.concatenate([_rms(om_ref[...], gm_ref[...]), _rms(od_ref[...], gd_ref[...])], axis=1)
    x1 = x_ref[...] + gt1 * _dot(mixed.astype(BF16), wout_ref[...])
    h2 = _rms(x1, gffn_ref[...]) * (1.0 + sc2) + sh2
    h2b = h2.astype(BF16)

    au = _dot(h2b, wgu_ref[...])
    hs = (_silu(au[:, :D_SHARED]) * au[:, D_SHARED:]).astype(BF16)
    xpart_ref[...] = x1 + gt2 * _dot(hs, wds_ref[...])

    for c, slab in enumerate(_pack_row_words(h2)):
        _store_word_slab(h2r_ref, 0, tm, c, slab)

    h2lo = (h2 - h2b.astype(F32)).astype(BF16)
    logits = _dot_nt(wrh_ref[...], h2b) + _dot_nt(wrl_ref[...], h2b) + _dot_nt(wrh_ref[...], h2lo)
    scores = 1.0 / (1.0 + jnp.exp(-logits))
    biased = scores + rb_ref[...]

    g_io = lax.broadcasted_iota(I32, (GROUP_SIZE, tm), 0)
    gs_rows = []
    for g in range(N_GROUPS):
        blk = biased[g * GROUP_SIZE:(g + 1) * GROUP_SIZE, :]
        m1, i1 = _first_index_of_max(blk, g_io, GROUP_SIZE)
        m2 = jnp.max(jnp.where(g_io == i1, -jnp.inf, blk), axis=0, keepdims=True)
        gs_rows.append(m1 + m2)
    gs = jnp.concatenate(gs_rows, axis=0)
    gi = lax.broadcasted_iota(I32, (N_GROUPS, tm), 0)
    grank = jnp.zeros((N_GROUPS, tm), I32)
    for m in range(N_GROUPS):
        gm = gs[m:m + 1, :]
        grank = grank + jnp.where((gm > gs) | ((gm == gs) & (m < gi)), 1, 0)
    gsel = grank < TOPK_GROUPS
    masked = jnp.concatenate(
        [jnp.where(gsel[g:g + 1, :], biased[g * GROUP_SIZE:(g + 1) * GROUP_SIZE, :], -jnp.inf)
         for g in range(N_GROUPS)], axis=0)

    e_io = lax.broadcasted_iota(I32, (N_EXPERTS, tm), 0)
    e_rows, s_rows = [], []
    for _ in range(EXPERT_TOPK):
        _, idx = _first_index_of_max(masked, e_io, N_EXPERTS)
        hit = e_io == idx
        e_rows.append(idx)
        s_rows.append(jnp.sum(jnp.where(hit, scores, 0.0), axis=0, keepdims=True))
        masked = jnp.where(hit, -jnp.inf, masked)
    eidx = jnp.concatenate(e_rows, axis=0)
    sk = jnp.concatenate(s_rows, axis=0)
    gate_ref[...] = sk / jnp.sum(sk, axis=0, keepdims=True) * ROUTED_SCALE
    eidx_ref[...] = eidx

    @pl.when(i == 0)
    def _():
        base_sc[...] = jnp.zeros(base_sc.shape, F32)

    chosen = jnp.zeros((N_EXPERTS, tm), F32)
    for k in range(EXPERT_TOPK):
        chosen = chosen + jnp.where(e_io == e_rows[k], 1.0, 0.0)
    incl = _dot(chosen.astype(BF16), tri_ref[...])
    pos = base_sc[...] + incl - 1.0
    rank_ref[...] = jnp.concatenate(
        [jnp.sum(jnp.where(e_io == e_rows[k], pos, 0.0), axis=0, keepdims=True)
         for k in range(EXPERT_TOPK)], axis=0).astype(I32)
    base_sc[...] = base_sc[...] + incl[:, tm - 1:tm]
    cnt_ref[...] = jnp.broadcast_to(base_sc[...], cnt_ref.shape)


def _post_attn(x2, o_m, o_d, mod3, g_moba, g_dsa, w_out, g_ffn, w_gu_s, w_down_s, wr_hi, wr_lo, rbias, S):
    T, D = x2.shape
    tm = POST_TM
    nt_per_seq = S // tm
    row = lambda i: (i, 0)
    full = lambda i: (0, 0)
    tri = (jnp.arange(tm)[:, None] <= jnp.arange(tm)[None, :]).astype(BF16)
    n_words = D // ROW_WORDS
    return pl.pallas_call(
        _post_kernel,
        out_shape=[
            jax.ShapeDtypeStruct((T, D), F32),
            jax.ShapeDtypeStruct((T, n_words, LANES), I32),
            jax.ShapeDtypeStruct((EXPERT_TOPK, T), I32),
            jax.ShapeDtypeStruct((EXPERT_TOPK, T), I32),
            jax.ShapeDtypeStruct((EXPERT_TOPK, T), F32),
            jax.ShapeDtypeStruct((N_EXPERTS, LANES), F32),
        ],
        grid=(T // tm,),
        in_specs=[
            pl.BlockSpec((tm, D), row),
            pl.BlockSpec((tm, MOBA_W), row),
            pl.BlockSpec((tm, DSA_W), row),
            pl.BlockSpec((1, 6, D), lambda i: (i // nt_per_seq, 0, 0)),
            pl.BlockSpec((1, MOBA_W), full),
            pl.BlockSpec((1, DSA_W), full),
            pl.BlockSpec(w_out.shape, full),
            pl.BlockSpec((1, D), full),
            pl.BlockSpec(w_gu_s.shape, full),
            pl.BlockSpec(w_down_s.shape, full),
            pl.BlockSpec(wr_hi.shape, full),
            pl.BlockSpec(wr_lo.shape, full),
            pl.BlockSpec((N_EXPERTS, 1), full),
            pl.BlockSpec((tm, tm), full),
        ],
        out_specs=[
            pl.BlockSpec((tm, D), row),
            pl.BlockSpec((tm, n_words, LANES), lambda i: (i, 0, 0)),
            pl.BlockSpec((EXPERT_TOPK, tm), lambda i: (0, i)),
            pl.BlockSpec((EXPERT_TOPK, tm), lambda i: (0, i)),
            pl.BlockSpec((EXPERT_TOPK, tm), lambda i: (0, i)),
            pl.BlockSpec((N_EXPERTS, LANES), full),
        ],
        scratch_shapes=[pltpu.VMEM((N_EXPERTS, 1), F32)],
        compiler_params=_cparams(("arbitrary",)),
        name="post_attn",
    )(x2, o_m, o_d, mod3, g_moba, g_dsa, w_out, g_ffn, w_gu_s, w_down_s, wr_hi, wr_lo, rbias, tri)


def _row_copy_wait(rows_hbm, n_rows, sem):
    blk = rows_hbm.at[pl.ds(0, n_rows)]
    pltpu.make_async_copy(blk, blk, sem).wait()


def _slots_kernel(ps_ref, e_ref, r_ref, d_ref):
    e = e_ref[...]

    def body(x, acc):
        return jnp.where(e == x, ps_ref[x], acc)

    d_ref[...] = lax.fori_loop(0, N_EXPERTS, body, jnp.zeros(e.shape, I32)) + r_ref[...]


def _slots(pad_starts, eidx, rank):
    K, T = eidx.shape
    tm = min(T, 4096)
    blk = lambda: pl.BlockSpec((K, tm), lambda i, ps: (0, i))
    return pl.pallas_call(
        _slots_kernel,
        out_shape=jax.ShapeDtypeStruct((K, T), I32),
        grid_spec=pltpu.PrefetchScalarGridSpec(num_scalar_prefetch=1, grid=(T // tm,),
                                               in_specs=[blk(), blk()], out_specs=blk()),
        compiler_params=_cparams(("parallel",)),
        name="slots",
    )(pad_starts, eidx, rank)


def _dispatch_kernel(d_ref, h2r_ref, xs_ref, sem):
    tm = h2r_ref.shape[0]

    def body(t, carry):
        for k in range(EXPERT_TOPK):
            pltpu.make_async_copy(h2r_ref.at[t], xs_ref.at[d_ref[t * EXPERT_TOPK + k]],
                                  sem).start(priority=k % 2)
        return carry

    lax.fori_loop(0, tm, body, 0)
    _row_copy_wait(xs_ref, tm * EXPERT_TOPK, sem)


def _dispatch(dest_flat, h2r, n_rows_padded):
    T, n_words, _ = h2r.shape
    tm = DISP_TM
    return pl.pallas_call(
        _dispatch_kernel,
        out_shape=jax.ShapeDtypeStruct((n_rows_padded, n_words, LANES), I32),
        grid=(T // tm,),
        in_specs=[pl.BlockSpec((tm * EXPERT_TOPK,), lambda i: (i,), memory_space=pltpu.SMEM),
                  pl.BlockSpec((tm, n_words, LANES), lambda i: (i, 0, 0))],
        out_specs=pl.BlockSpec(memory_space=pl.ANY),
        scratch_shapes=[pltpu.SemaphoreType.DMA],
        compiler_params=_cparams(("arbitrary",), disable_bounds_checks=True),
        name="dispatch",
    )(dest_flat, h2r)


def _expert_kernel(ps_ref, cnt_ref, wg_ref, wu_ref, wd_ref, xs_ref, ys_ref,
                   wg_sc, wu_sc, wd_sc, xbuf, ybuf, sem_in, sem_out):
    e = pl.program_id(0)
    bm = EXP_BM
    n_words = xbuf.shape[1]
    cnt = cnt_ref[e]
    g0 = ps_ref[e] // bm
    nb = (cnt + bm - 1) // bm
    n_used = (ps_ref[N_EXPERTS - 1] + cnt_ref[N_EXPERTS - 1] + bm - 1) // bm

    def in_copy(g):
        slot = lax.rem(g, EXP_IN_SLOTS)
        return pltpu.make_async_copy(xs_ref.at[pl.ds(g * bm, bm)], xbuf.at[pl.ds(slot * bm, bm)],
                                     sem_in.at[slot])

    def out_copy(g):
        slot = lax.rem(g, EXP_OUT_SLOTS)
        return pltpu.make_async_copy(ybuf.at[pl.ds(slot * bm, bm)], ys_ref.at[pl.ds(g * bm, bm)],
                                     sem_out.at[slot])

    @pl.when(e == 0)
    def _():
        for g in range(EXP_AHEAD):
            @pl.when(g < n_used)
            def _():
                in_copy(g).start()

    @pl.when(nb > 0)
    def _():
        wg_sc[...] = wg_ref[...].astype(BF16)
        wu_sc[...] = wu_ref[...].astype(BF16)
        wd_sc[...] = wd_ref[...].astype(BF16)

        def body(b, carry):
            g = g0 + b

            @pl.when(g + EXP_AHEAD < n_used)
            def _():
                in_copy(g + EXP_AHEAD).start()

            in_copy(g).wait()

            @pl.when(g >= EXP_OUT_SLOTS)
            def _():
                out_copy(g - EXP_OUT_SLOTS).wait()

            xrow = lax.rem(g, EXP_IN_SLOTS) * bm
            yrow = lax.rem(g, EXP_OUT_SLOTS) * bm
            feats = []
            for c in range(n_words):
                feats += _unpack_row_words(_load_word_slab(xbuf, xrow, bm, c))
            x = jnp.concatenate(feats, axis=1)
            valid = lax.broadcasted_iota(I32, (bm, 1), 0) < cnt - b * bm
            xb = jnp.where(valid, x, 0.0).astype(BF16)
            a = _dot(xb, wg_sc[...])
            u = _dot(xb, wu_sc[...])
            hmid = (_silu(a) * u).astype(BF16)
            ob = _dot(hmid, wd_sc[...])
            for c, slab in enumerate(_pack_row_words(ob)):
                _store_word_slab(ybuf, yrow, bm, c, slab)
            out_copy(g).start()
            return carry

        lax.fori_loop(0, nb, body, 0)

    @pl.when(e == N_EXPERTS - 1)
    def _():
        for back in range(EXP_OUT_SLOTS, 0, -1):
            @pl.when(n_used >= back)
            def _():
                out_copy(n_used - back).wait()


def _experts(pad_starts, counts, xs, w_gate_e, w_up_e, w_down_e):
    E, D, DE = w_gate_e.shape
    n_words = xs.shape[1]
    bm = EXP_BM
    wsel = lambda e, ps, cnt: (e, 0, 0)
    anyspec = pl.BlockSpec(memory_space=pl.ANY)
    return pl.pallas_call(
        _expert_kernel,
        out_shape=jax.ShapeDtypeStruct(xs.shape, I32),
        grid_spec=pltpu.PrefetchScalarGridSpec(
            num_scalar_prefetch=2,
            grid=(E,),
            in_specs=[pl.BlockSpec((None, D, DE), wsel), pl.BlockSpec((None, D, DE), wsel),
                      pl.BlockSpec((None, DE, D), wsel), anyspec],
            out_specs=anyspec,
            scratch_shapes=[pltpu.VMEM((D, DE), BF16), pltpu.VMEM((D, DE), BF16), pltpu.VMEM((DE, D), BF16),
                            pltpu.VMEM((EXP_IN_SLOTS * bm, n_words, LANES), I32),
                            pltpu.VMEM((EXP_OUT_SLOTS * bm, n_words, LANES), I32),
                            pltpu.SemaphoreType.DMA((EXP_IN_SLOTS,)), pltpu.SemaphoreType.DMA((EXP_OUT_SLOTS,))],
        ),
        compiler_params=_cparams(("arbitrary",)),
        name="experts",
    )(pad_starts, counts, w_gate_e, w_up_e, w_down_e, xs)


def _combine_kernel(dcur_ref, dnxt_ref, g_ref, xpart_ref, mod_ref, gfin_ref, ys_ref, o_ref,
                    buf0, buf1, sem0, sem1):
    i = pl.program_id(0)
    n_steps = pl.num_programs(0)
    tm = xpart_ref.shape[0]
    n_words = buf0.shape[1]

    def issue(dest_ref, buf, sem):
        for t in range(tm):
            for k in range(EXPERT_TOPK):
                pltpu.make_async_copy(ys_ref.at[dest_ref[t * EXPERT_TOPK + k]], buf.at[k * tm + t],
                                      sem).start(priority=k % 2)

    def reduce_tile(buf):
        gt2 = mod_ref[0, 5:6, :]
        g = g_ref[...]
        cols = []
        for c in range(n_words):
            lo = jnp.zeros((tm, LANES), F32)
            hi = jnp.zeros((tm, LANES), F32)
            for k in range(EXPERT_TOPK):
                a, b = _unpack_row_words(_load_word_slab(buf, k * tm, tm, c))
                gk = g[:, k:k + 1]
                lo = lo + gk * a
                hi = hi + gk * b
            cols += [lo, hi]
        routed = jnp.concatenate(cols, axis=1)
        o_ref[...] = _rms(xpart_ref[...] + gt2 * routed, gfin_ref[...])

    @pl.when(i == 0)
    def _():
        issue(dcur_ref, buf0, sem0)

    for parity, (cur, nxt) in enumerate((((buf0, sem0), (buf1, sem1)), ((buf1, sem1), (buf0, sem0)))):
        @pl.when(i % 2 == parity)
        def _():
            _row_copy_wait(ys_ref, tm * EXPERT_TOPK, cur[1])
            issue(dnxt_ref, *nxt)
            reduce_tile(cur[0])

            @pl.when(i + 1 == n_steps)
            def _():
                _row_copy_wait(ys_ref, tm * EXPERT_TOPK, nxt[1])


def _combine(dest, gates_t, xpart, mod3, g_final, ys, S):
    T, D = xpart.shape
    tm = COMB_TM
    n_steps = T // tm
    nt_per_seq = S // tm
    n_words = ys.shape[1]
    return pl.pallas_call(
        _combine_kernel,
        out_shape=jax.ShapeDtypeStruct((T, D), F32),
        grid=(n_steps,),
        in_specs=[
            pl.BlockSpec((tm * EXPERT_TOPK,), lambda i: (i,), memory_space=pltpu.SMEM),
            pl.BlockSpec((tm * EXPERT_TOPK,), lambda i: (jnp.minimum(i + 1, n_steps - 1),),
                         memory_space=pltpu.SMEM),
            pl.BlockSpec((tm, EXPERT_TOPK), lambda i: (i, 0)),
            pl.BlockSpec((tm, D), lambda i: (i, 0)),
            pl.BlockSpec((1, 6, D), lambda i: (i // nt_per_seq, 0, 0)),
            pl.BlockSpec((1, D), lambda i: (0, 0)),
            pl.BlockSpec(memory_space=pl.ANY),
        ],
        out_specs=pl.BlockSpec((tm, D), lambda i: (i, 0)),
        scratch_shapes=[pltpu.VMEM((EXPERT_TOPK * tm, n_words, LANES), I32),
                        pltpu.VMEM((EXPERT_TOPK * tm, n_words, LANES), I32),
                        pltpu.SemaphoreType.DMA, pltpu.SemaphoreType.DMA],
        compiler_params=_cparams(("arbitrary",), disable_bounds_checks=True),
        name="combine",
    )(dest, dest, gates_t, xpart, mod3, g_final, ys)


def _layer(x2, mod3, S, g_mix, w_in, g_kv, w_kv_up, g_moba_out, g_dsa_out, w_out, g_ffn, w_router,
           router_bias, w_gate_e, w_up_e, w_down_e, w_gate_s, w_up_s, w_down_s, g_final, tab_h, half_h,
           tab_i, half_i):
    T, D = x2.shape
    B = T // S
    w_in_p = jnp.pad(w_in, ((0, 0), (0, _C_END - w_in.shape[1]))).astype(BF16)
    (q_mt, k_m, v_mt, kmean, q_dt, k_d, v_dt, q_it, k_i, w_it) = _in_proj(
        x2, mod3, g_mix.reshape(1, D), w_in_p, g_kv.reshape(1, KV_LORA), w_kv_up.astype(BF16),
        tab_h, half_h, tab_i, half_i, S)
    o_m = _moba_attn(q_mt, k_m, v_mt, kmean, B, S)
    o_d = _dsa_attn(q_it, w_it, q_dt, k_i, k_d, v_dt, B, S)

    wr_t = w_router.T
    wr_hi = wr_t.astype(BF16)
    wr_lo = (wr_t - wr_hi.astype(F32)).astype(BF16)
    w_gu_s = jnp.concatenate([w_gate_s, w_up_s], axis=1).astype(BF16)
    xpart, h2r, eidx, rank, gates, cnt = _post_attn(
        x2, o_m, o_d, mod3, g_moba_out.reshape(1, MOBA_W), g_dsa_out.reshape(1, DSA_W), w_out.astype(BF16),
        g_ffn.reshape(1, D), w_gu_s, w_down_s.astype(BF16), wr_hi, wr_lo,
        router_bias.reshape(N_EXPERTS, 1), S)

    bm = EXP_BM
    n_blocks = T * EXPERT_TOPK // bm + N_EXPERTS
    counts = cnt[:, 0].astype(I32)
    padded = (counts + bm - 1) // bm * bm
    pad_ends = jnp.cumsum(padded)
    pad_starts = (pad_ends - padded).astype(I32)
    dest_flat = _slots(pad_starts, eidx, rank).T.reshape(-1)
    xs = _dispatch(dest_flat, h2r, n_blocks * bm)
    ys = _experts(pad_starts, counts, xs, w_gate_e, w_up_e, w_down_e)
    return _combine(dest_flat, gates.T, xpart, mod3, g_final.reshape(1, D), ys, S)


def kernel(x, c, w_ada, b_ada, g_mix, w_in, g_kv, w_kv_up, g_moba_out, g_dsa_out, w_out, g_ffn, w_router,
           router_bias, w_gate_e, w_up_e, w_down_e, w_gate_s, w_up_s, w_down_s, g_final):
    B, S, D = x.shape
    depth = w_ada.shape[0]
    assert depth == 1, "the final norm is fused into the single layer"
    assert S % PROJ_TM == 0 and S % DSA_KC == 0 and S % POST_TM == 0 and S >= 4 * DSA_MAX_TOPK
    tab_h, half_h = _rope_tables(S, HEAD_DIM, 1)
    tab_i, half_i = _rope_tables(S, IDX_DIM, LANES // IDX_DIM)
    x2 = x.reshape(B * S, D)
    sq = lambda a: a.reshape(a.shape[1:])
    mod3 = _ada_mod(c, sq(w_ada), sq(b_ada)).reshape(B, 6, D)
    out = _layer(x2, mod3, S, sq(g_mix), sq(w_in), sq(g_kv), sq(w_kv_up), sq(g_moba_out), sq(g_dsa_out),
                 sq(w_out), sq(g_ffn), sq(w_router), sq(router_bias), sq(w_gate_e), sq(w_up_e), sq(w_down_e),
                 sq(w_gate_s), sq(w_up_s), sq(w_down_s), g_final, tab_h, half_h, tab_i, half_i)
    return out.reshape(B, S, D)
```

```python
import functools

import jax
import jax.numpy as jnp
from jax import lax
from jax.experimental import pallas as pl
from jax.experimental.pallas import tpu as pltpu

HEAD_DIM = 128
MOBA_HEADS = 4
DSA_HEADS = 4
MOBA_W = MOBA_HEADS * HEAD_DIM
DSA_W = DSA_HEADS * HEAD_DIM
MOBA_BLOCK = 256
MOBA_TOPK = 3
DSA_MAX_TOPK = 256
KV_LORA = 256
IDX_HEADS = 8
IDX_DIM = 64
ROPE_THETA = 500000.0
ROPE_FRACTION_DIV = 4
N_EXPERTS = 256
EXPERT_TOPK = 8
N_GROUPS = 8
TOPK_GROUPS = 4
GROUP_SIZE = N_EXPERTS // N_GROUPS
D_EXPERT = 256
D_SHARED = 256
ROUTED_SCALE = 2.5
EPS = 1e-6

LANES = 128
SUBLANES = 8
VMEM_LIMIT = 56 * 1024 * 1024

PROJ_TM = 512
DSA_TQ = 256
DSA_KC = 256
DSA_KA = 256
DSA_KS = 128
POST_TM = 512
DISP_TM = 2048
EXP_BM = 256
EXP_AHEAD = 4
EXP_IN_SLOTS = EXP_AHEAD + 1
EXP_OUT_SLOTS = 3
COMB_TM = 256
REDUCE_CHAINS = 4
DENOM_ROWS = 16
NEG = -1e30
INT_MIN = -2147483648
LOG2E = 1.4426950408889634

F32 = jnp.float32
BF16 = jnp.bfloat16
I32 = jnp.int32


def _cparams(sem, **kw):
    return pltpu.CompilerParams(dimension_semantics=sem, vmem_limit_bytes=VMEM_LIMIT, **kw)


def _dot(a, b):
    return jnp.dot(a, b, preferred_element_type=F32)


def _dot_nt(a, b):
    return lax.dot_general(a, b, (((1,), (1,)), ((), ())), preferred_element_type=F32)


def _silu(x):
    return x * (1.0 / (1.0 + jnp.exp(-x)))


def _rms(x, g):
    return x * lax.rsqrt(jnp.mean(x * x, axis=-1, keepdims=True) + EPS) * g


def _rows_to_tile(op, x):
    parts = [x[i:i + SUBLANES] for i in range(0, x.shape[0], SUBLANES)]
    n_chains = min(REDUCE_CHAINS, len(parts))
    accs = parts[:n_chains]
    for i in range(n_chains, len(parts)):
        accs[i % n_chains] = op(accs[i % n_chains], parts[i])
    while len(accs) > 1:
        accs = [op(accs[i], accs[i + 1]) for i in range(0, len(accs) - 1, 2)] + ([accs[-1]] if len(accs) % 2 else [])
    return accs[0]


def _col_max(x):
    return jnp.max(_rows_to_tile(jnp.maximum, x), axis=0, keepdims=True)


def _col_sum(x):
    return jnp.sum(_rows_to_tile(jnp.add, x), axis=0, keepdims=True)


ROW_WORDS = 2 * LANES


def _pack_row_words(x):
    slabs = []
    for c in range(x.shape[1] // ROW_WORDS):
        lo = pltpu.bitcast(x[:, c * ROW_WORDS:c * ROW_WORDS + LANES].astype(BF16).astype(F32), I32)
        hi = pltpu.bitcast(x[:, c * ROW_WORDS + LANES:(c + 1) * ROW_WORDS].astype(BF16).astype(F32), I32)
        slabs.append(lax.shift_right_logical(lo, 16) | hi)
    return slabs


def _word_slab_index(rows_ref, row0, n_rows, c):
    n_words = rows_ref.shape[1]
    flat = rows_ref.reshape(rows_ref.shape[0] * n_words, LANES)
    return flat, pl.ds(row0 * n_words + c, n_rows, stride=n_words)


def _load_word_slab(rows_ref, row0, n_rows, c):
    flat, idx = _word_slab_index(rows_ref, row0, n_rows, c)
    return flat[idx, :]


def _store_word_slab(rows_ref, row0, n_rows, c, value):
    flat, idx = _word_slab_index(rows_ref, row0, n_rows, c)
    flat[idx, :] = value


def _unpack_row_words(u):
    return pltpu.bitcast(lax.shift_left(u, 16), F32), pltpu.bitcast(u & jnp.int32(-65536), F32)


def _ada_kernel(c_ref, w_ref, b_ref, o_ref):
    ca = _silu(c_ref[...])
    o_ref[...] = jnp.dot(ca, w_ref[...], preferred_element_type=F32,
                         precision=lax.Precision.HIGHEST) + b_ref[...]


def _ada_mod(c, w_ada, b_ada):
    B, D = c.shape
    N = w_ada.shape[1]
    tn = 1024
    return pl.pallas_call(
        _ada_kernel,
        out_shape=jax.ShapeDtypeStruct((B, N), F32),
        grid=(N // tn,),
        in_specs=[pl.BlockSpec((B, D), lambda j: (0, 0)),
                  pl.BlockSpec((D, tn), lambda j: (0, j)),
                  pl.BlockSpec((1, tn), lambda j: (0, j))],
        out_specs=pl.BlockSpec((B, tn), lambda j: (0, j)),
        compiler_params=_cparams(("arbitrary",)),
        name="ada_mod",
    )(c, w_ada, b_ada.reshape(1, N))


def _rope_tables(seq, head_dim, heads_per_vreg):
    rot = head_dim // ROPE_FRACTION_DIV
    half = rot // 2
    inv = jnp.float32(ROPE_THETA) ** (-(jnp.arange(0, rot, 2, dtype=F32) / rot))
    ang = jnp.arange(seq, dtype=F32)[:, None] * inv[None, :]
    cos, sin = jnp.cos(ang), jnp.sin(ang)
    ones = jnp.ones((seq, head_dim - rot), F32)
    zeros_h = jnp.zeros((seq, half), F32)
    zeros_r = jnp.zeros((seq, head_dim - rot), F32)
    c = jnp.concatenate([cos, cos, ones], axis=1)
    sp = jnp.concatenate([zeros_h, sin, zeros_r], axis=1)
    sm = jnp.concatenate([-sin, zeros_h, zeros_r], axis=1)
    rep = lambda t: jnp.tile(t, (1, heads_per_vreg))
    return jnp.stack([rep(c), rep(sp), rep(sm)], axis=0), half


def _rope(x, tab_ref, half):
    return (x * tab_ref[0] + pltpu.roll(x, half, 1) * tab_ref[1]
            + pltpu.roll(x, LANES - half, 1) * tab_ref[2])


_C_QM, _C_KM, _C_VM, _C_QD = 0, MOBA_W, 2 * MOBA_W, 3 * MOBA_W
_C_CKV = 3 * MOBA_W + DSA_W
_C_QI = _C_CKV + KV_LORA
_C_KI = _C_QI + IDX_HEADS * IDX_DIM
_C_END = _C_KI + LANES


def _in_proj_kernel(x_ref, mod_ref, gmix_ref, w_ref, gkv_ref, wkv_ref, tabh_ref, tabi_ref,
                    qmt_ref, km_ref, vmt_ref, kmean_ref, qdt_ref, kd_ref, vdt_ref, qit_ref, ki_ref, wit_ref,
                    *, half_h, half_i):
    tm = x_ref.shape[0]
    x = x_ref[...]
    sh1 = mod_ref[0, 0:1, :]
    sc1 = mod_ref[0, 1:2, :]
    h = (_rms(x, gmix_ref[...]) * (1.0 + sc1) + sh1).astype(BF16)

    def proj(c0, width):
        return _dot(h, w_ref[:, c0:c0 + width])

    q_scale = HEAD_DIM ** -0.5 * LOG2E
    nblk = tm // MOBA_BLOCK
    qm = proj(_C_QM, MOBA_W)
    km = proj(_C_KM, MOBA_W)
    vm = proj(_C_VM, MOBA_W)
    for hd in range(MOBA_HEADS):
        sl = slice(hd * HEAD_DIM, (hd + 1) * HEAD_DIM)
        qmt_ref[sl, :] = (_rope(qm[:, sl], tabh_ref, half_h) * q_scale).T.astype(BF16)
        kr = _rope(km[:, sl], tabh_ref, half_h)
        km_ref[:, sl] = kr.astype(BF16)
        for blk in range(nblk):
            rows = slice(blk * MOBA_BLOCK, (blk + 1) * MOBA_BLOCK)
            kmean_ref[blk:blk + 1, sl] = jnp.mean(kr[rows], axis=0, keepdims=True)
            vmt_ref[blk, sl, :] = vm[rows, sl].T.astype(BF16)
    qd = proj(_C_QD, DSA_W)
    for hd in range(DSA_HEADS):
        sl = slice(hd * HEAD_DIM, (hd + 1) * HEAD_DIM)
        qdt_ref[sl, :] = (_rope(qd[:, sl], tabh_ref, half_h) * q_scale).T.astype(BF16)
    ckv = proj(_C_CKV, KV_LORA)
    kv = _dot(_rms(ckv, gkv_ref[...]).astype(BF16), wkv_ref[...])
    kd_ref[...] = _rope(kv[:, :HEAD_DIM], tabh_ref, half_h).astype(BF16)
    for ch in range(tm // DSA_KA):
        vdt_ref[ch] = kv[ch * DSA_KA:(ch + 1) * DSA_KA, HEAD_DIM:].T.astype(BF16)
    qi = proj(_C_QI, IDX_HEADS * IDX_DIM)
    for j in range(IDX_HEADS * IDX_DIM // LANES):
        sl = slice(j * LANES, (j + 1) * LANES)
        qit_ref[sl, :] = _rope(qi[:, sl], tabi_ref, half_i).T.astype(BF16)
    kw = proj(_C_KI, LANES)
    ki_ref[...] = _rope(kw, tabi_ref, half_i)[:, :IDX_DIM].astype(BF16)
    wit_ref[...] = kw.T[IDX_DIM:IDX_DIM + IDX_HEADS, :] * (IDX_HEADS ** -0.5 * IDX_DIM ** -0.5)


def _in_proj(x2, mod3, g_mix, w_in_p, g_kv, w_kv_up, tab_h, half_h, tab_i, half_i, S):
    T, D = x2.shape
    tm = PROJ_TM
    nt_per_seq = S // tm
    row = lambda i: (i, 0)
    col = lambda i: (0, i)
    nb = tm // MOBA_BLOCK
    nc = tm // DSA_KA
    outs = [
        jax.ShapeDtypeStruct((MOBA_W, T), BF16),
        jax.ShapeDtypeStruct((T, MOBA_W), BF16),
        jax.ShapeDtypeStruct((T // MOBA_BLOCK, MOBA_W, MOBA_BLOCK), BF16),
        jax.ShapeDtypeStruct((T // tm, nb, MOBA_W), F32),
        jax.ShapeDtypeStruct((DSA_W, T), BF16),
        jax.ShapeDtypeStruct((T, HEAD_DIM), BF16),
        jax.ShapeDtypeStruct((T // DSA_KA, HEAD_DIM, DSA_KA), BF16),
        jax.ShapeDtypeStruct((IDX_HEADS * IDX_DIM, T), BF16),
        jax.ShapeDtypeStruct((T, IDX_DIM), BF16),
        jax.ShapeDtypeStruct((IDX_HEADS, T), F32),
    ]
    out_specs = [
        pl.BlockSpec((MOBA_W, tm), col), pl.BlockSpec((tm, MOBA_W), row),
        pl.BlockSpec((nb, MOBA_W, MOBA_BLOCK), lambda i: (i, 0, 0)),
        pl.BlockSpec((None, nb, MOBA_W), lambda i: (i, 0, 0)),
        pl.BlockSpec((DSA_W, tm), col), pl.BlockSpec((tm, HEAD_DIM), row),
        pl.BlockSpec((nc, HEAD_DIM, DSA_KA), lambda i: (i, 0, 0)),
        pl.BlockSpec((IDX_HEADS * IDX_DIM, tm), col), pl.BlockSpec((tm, IDX_DIM), row),
        pl.BlockSpec((IDX_HEADS, tm), col),
    ]
    res = pl.pallas_call(
        functools.partial(_in_proj_kernel, half_h=half_h, half_i=half_i),
        out_shape=outs,
        grid=(T // tm,),
        in_specs=[
            pl.BlockSpec((tm, D), row),
            pl.BlockSpec((1, 6, D), lambda i: (i // nt_per_seq, 0, 0)),
            pl.BlockSpec((1, D), lambda i: (0, 0)),
            pl.BlockSpec((D, _C_END), lambda i: (0, 0)),
            pl.BlockSpec((1, KV_LORA), lambda i: (0, 0)),
            pl.BlockSpec((KV_LORA, 2 * HEAD_DIM), lambda i: (0, 0)),
            pl.BlockSpec((3, tm, LANES), lambda i: (0, i % nt_per_seq, 0)),
            pl.BlockSpec((3, tm, LANES), lambda i: (0, i % nt_per_seq, 0)),
        ],
        out_specs=out_specs,
        compiler_params=_cparams(("parallel",)),
        name="in_proj",
    )(x2, mod3, g_mix, w_in_p, g_kv, w_kv_up, tab_h, tab_i)
    res = list(res)
    res[3] = res[3].reshape(T // MOBA_BLOCK, MOBA_W)
    return res


def _moba_kernel(qt_ref, k_ref, vt_ref, kmean_ref, o_ref, bias_sc, *head_scratch):
    accs, s_scs = head_scratch[:MOBA_HEADS], head_scratch[MOBA_HEADS:]
    qi = pl.program_id(1)
    blk = MOBA_BLOCK
    nb = kmean_ref.shape[0]
    heads = range(MOBA_HEADS)
    hsl = [slice(hd * HEAD_DIM, (hd + 1) * HEAD_DIM) for hd in heads]
    qts = [qt_ref[hsl[hd], :] for hd in heads]
    row = lax.broadcasted_iota(I32, (nb, blk), 0)
    past = row < qi
    start = pl.multiple_of(qi * blk, blk)
    k_io = lax.broadcasted_iota(I32, (blk, blk), 0)
    q_io = lax.broadcasted_iota(I32, (blk, blk), 1)

    gates = []
    for hd in heads:
        km = kmean_ref[:, hsl[hd]]
        km_hi = km.astype(BF16)
        km_lo = (km - km_hi.astype(F32)).astype(BF16)
        gate = _dot(km_hi, qts[hd]) + _dot(km_lo, qts[hd])
        gates.append(jnp.where(past, gate, -jnp.inf))
    k_own = k_ref[pl.ds(start, blk), :]
    own = [jnp.where(k_io <= q_io, _dot(k_own[:, hsl[hd]], qts[hd]), NEG) for hd in heads]
    biases = [jnp.full((nb, blk), NEG, F32) for _ in heads]
    for _ in range(MOBA_TOPK):
        for hd in heads:
            _, idx = _first_index_of_max(gates[hd], row, nb)
            hit = row == idx
            biases[hd] = jnp.where(hit, 0.0, biases[hd])
            gates[hd] = jnp.where(hit, -jnp.inf, gates[hd])
    ones_rows = jnp.ones((DENOM_ROWS, blk), BF16)

    def v_aug(n, hd):
        return jnp.concatenate([vt_ref[n, hsl[hd], :], ones_rows], axis=0)

    init = []
    for hd in heads:
        bias_sc[hd] = jnp.where(past, biases[hd], NEG)
        m0 = _col_max(own[hd])
        p = jnp.exp2((own[hd] - m0).astype(BF16))
        accs[hd][...] = _dot(v_aug(qi, hd), p)
        init.append(m0)

    def masked_scores(n):
        kb = k_ref[pl.ds(pl.multiple_of(n * blk, blk), blk), :]
        return [_dot(kb[:, hsl[hd]], qts[hd]) + bias_sc[hd, pl.ds(n, 1), :] for hd in heads]

    for hd, s0 in enumerate(masked_scores(0)):
        s_scs[hd][0] = s0

    def body(n, carry):
        slot = lax.rem(n, 2)
        sbs = [s_scs[hd][slot] for hd in heads]
        nxt = masked_scores(jnp.minimum(n + 1, qi - 1))
        out = []
        for hd in heads:
            m_old = carry[hd]
            m_new = jnp.maximum(m_old, _col_max(sbs[hd]))
            alpha = jnp.exp2(m_old - m_new)
            pb = jnp.exp2((sbs[hd] - m_new).astype(BF16))
            accs[hd][...] = alpha * accs[hd][...] + _dot(v_aug(n, hd), pb)
            out.append(m_new)
        for hd in heads:
            s_scs[hd][1 - slot] = nxt[hd]
        return tuple(out)

    lax.fori_loop(0, qi, body, tuple(init))
    for hd in heads:
        acc = accs[hd][...]
        o_ref[:, hsl[hd]] = (acc[:HEAD_DIM] * (1.0 / acc[HEAD_DIM:HEAD_DIM + 1])).T


def _moba_attn(q_mt, k_m, v_mt, kmean, B, S):
    T = B * S
    blk = MOBA_BLOCK
    nq = S // blk
    return pl.pallas_call(
        _moba_kernel,
        out_shape=jax.ShapeDtypeStruct((T, MOBA_W), F32),
        grid=(B, nq),
        in_specs=[
            pl.BlockSpec((MOBA_W, blk), lambda b, i: (0, b * nq + i)),
            pl.BlockSpec((S, MOBA_W), lambda b, i: (b, 0)),
            pl.BlockSpec((nq, MOBA_W, blk), lambda b, i: (b, 0, 0)),
            pl.BlockSpec((nq, MOBA_W), lambda b, i: (b, 0)),
        ],
        out_specs=pl.BlockSpec((blk, MOBA_W), lambda b, i: (b * nq + i, 0)),
        scratch_shapes=[pltpu.VMEM((MOBA_HEADS, nq, blk), F32)]
        + [pltpu.VMEM((HEAD_DIM + DENOM_ROWS, blk), F32) for _ in range(MOBA_HEADS)]
        + [pltpu.VMEM((2, blk, blk), F32) for _ in range(MOBA_HEADS)],
        compiler_params=_cparams(("parallel", "arbitrary")),
        name="moba_attn",
    )(q_mt, k_m, v_mt, kmean)


def _sortable_key(x):
    b = pltpu.bitcast(x, I32)
    return jnp.where(b >= 0, b, b ^ jnp.int32(0x7FFFFFFF))


def _dsa_kernel(qit_ref, wit_ref, qdt_ref, ki_ref, kd_ref, vdt_ref, o_ref, key_sc, high_sc, *pair_scratch, topk):
    accs, s_scs = pair_scratch[:DSA_HEADS // 2], pair_scratch[DSA_HEADS // 2:]
    t = pl.program_id(1)
    tq, kc = DSA_TQ, DSA_KC
    S = key_sc.shape[0]
    q0 = t * tq
    n_chunks = (q0 + tq + kc - 1) // kc
    key_io = lax.broadcasted_iota(I32, (kc, tq), 0)
    q_pos = q0 + lax.broadcasted_iota(I32, (kc, tq), 1)
    w = wit_ref[...]

    def score_chunk(c, carry):
        k0 = pl.multiple_of(c * kc, kc)
        kic = ki_ref[pl.ds(k0, kc), :]
        acc = jnp.zeros((kc, tq), F32)
        for hd in range(IDX_HEADS):
            lg = _dot(kic, qit_ref[hd * IDX_DIM:(hd + 1) * IDX_DIM, :])
            acc = acc + jnp.maximum(lg, 0.0) * w[hd:hd + 1, :]
        acc = jnp.where(acc == 0.0, 0.0, acc)
        key = _sortable_key(acc)
        key = jnp.where(k0 + key_io <= q_pos, key, INT_MIN)
        key_sc[pl.ds(k0, kc), :] = key
        high_sc[pl.ds(k0, kc), :] = lax.shift_right_arithmetic(key, 16).astype(jnp.int16)
        return carry

    lax.fori_loop(0, n_chunks, score_chunk, 0)

    ks = DSA_KS
    ks_io = lax.broadcasted_iota(I32, (ks, tq), 0)

    def column_total(slab_fn, dtype):
        rows = SUBLANES * (4 // jnp.dtype(dtype).itemsize)

        def cbody(c, accs):
            vals = slab_fn(pl.multiple_of(c * ks, ks))
            accs = list(accs)
            for i in range(ks // rows):
                j = i % REDUCE_CHAINS
                accs[j] = accs[j] + vals[i * rows:(i + 1) * rows]
            return tuple(accs)
        zero = jnp.zeros((rows, tq), dtype)
        accs = lax.fori_loop(0, n_chunks * (kc // ks), cbody, (zero,) * REDUCE_CHAINS)
        total = accs[0].astype(I32)
        for a in accs[1:]:
            total = total + a.astype(I32)
        return jnp.sum(total, axis=0, keepdims=True)

    def count(pred_fn):
        return column_total(lambda k0: jnp.where(pred_fn(key_sc[pl.ds(k0, ks), :], k0), 1, 0), I32)

    def count_ge_high(cand):
        c16 = lax.shift_right_arithmetic(cand, 16).astype(jnp.int16)
        one, zero = jnp.int16(1), jnp.int16(0)
        return column_total(lambda k0: jnp.where(high_sc[pl.ds(k0, ks), :] >= c16, one, zero), jnp.int16)

    thr0 = jnp.where(count_ge_high(jnp.zeros((1, tq), I32)) >= topk, 0, INT_MIN).astype(I32)

    def high_step(i, thr):
        cand = thr | lax.shift_left(jnp.int32(1), 30 - i)
        return jnp.where(count_ge_high(cand) >= topk, cand, thr)

    thr = lax.fori_loop(0, 15, high_step, thr0)

    high = lax.shift_right_arithmetic(thr, 16)
    h16 = high.astype(jnp.int16)
    one16, zero16 = jnp.int16(1), jnp.int16(0)
    n_above = column_total(lambda k0: jnp.where(high_sc[pl.ds(k0, ks), :] > h16, one16, zero16), jnp.int16)

    def repack(c, carry):
        k0 = pl.multiple_of(c * kc, kc)
        kk = key_sc[pl.ds(k0, kc), :]
        low = (kk & 0xFFFF) - 32768
        same = lax.shift_right_arithmetic(kk, 16) == high
        high_sc[pl.ds(k0, kc), :] = jnp.where(same, low, -32768).astype(jnp.int16)
        return carry

    lax.fori_loop(0, n_chunks, repack, 0)

    def low_step(i, low_bits):
        cand = low_bits | lax.shift_left(jnp.int32(1), 15 - i)
        c16 = (cand - 32768).astype(jnp.int16)
        cnt = n_above + column_total(
            lambda k0: jnp.where(high_sc[pl.ds(k0, ks), :] >= c16, one16, zero16), jnp.int16)
        return jnp.where(cnt >= topk, cand, low_bits)

    thr = thr | lax.fori_loop(0, 16, low_step, jnp.zeros((1, tq), I32))

    n_gt = count(lambda kk, k0: kk > thr)
    n_ge = count(lambda kk, k0: kk >= thr)
    need = topk - n_gt
    overflow = (n_ge > topk) & (thr != INT_MIN)
    any_overflow = jnp.max(jnp.where(overflow, 1, 0)) > 0
    nbits = max(1, (S - 1).bit_length())

    def cut_search():
        def step(i, lo):
            cand = lo | lax.shift_left(jnp.int32(1), nbits - 1 - i)
            cnt = count(lambda kk, k0: (kk == thr) & (k0 + ks_io < cand))
            return jnp.where(cnt >= need, lo, cand)
        return lax.fori_loop(0, nbits, step, jnp.zeros((1, tq), I32))

    jcut = lax.cond(any_overflow, cut_search, lambda: jnp.zeros((1, tq), I32))
    jcut = jnp.where(overflow, jcut, S)
    thr_ge = jnp.where(thr == INT_MIN, INT_MIN + 1, thr)

    pairs = range(DSA_HEADS // 2)
    qst = [jnp.concatenate([qdt_ref[(2 * g + j) * HEAD_DIM:(2 * g + j + 1) * HEAD_DIM, :] for j in range(2)],
                           axis=1) for g in pairs]
    for g in pairs:
        accs[g][...] = jnp.zeros(accs[g].shape, F32)

    ka = DSA_KA
    pos_io = lax.broadcasted_iota(I32, (ka, tq), 0)

    n_sub = (q0 + tq + ka - 1) // ka

    def masked_scores(c):
        k0 = pl.multiple_of(c * ka, ka)
        kk = key_sc[pl.ds(k0, ka), :]
        bias = lax.cond(
            any_overflow,
            lambda: jnp.where((kk > thr_ge) | ((kk == thr_ge) & (k0 + pos_io <= jcut)), 0.0, NEG),
            lambda: jnp.where(kk >= thr_ge, 0.0, NEG))
        bias2 = jnp.concatenate([bias, bias], axis=1)
        kdc = kd_ref[pl.ds(k0, ka), :]
        return [_dot(kdc, qst[g]) + bias2 for g in pairs]

    for g, s0 in enumerate(masked_scores(0)):
        s_scs[g][0] = s0

    def attn_chunk(c, carry):
        slot = lax.rem(c, 2)
        ss = [s_scs[g][slot] for g in pairs]
        ss_next = masked_scores(jnp.minimum(c + 1, n_sub - 1))
        vaug = jnp.concatenate([vdt_ref[c], ones_rows], axis=0)
        old = [accs[g][...] for g in pairs]
        out, new = [], []
        for g in pairs:
            m_old = carry[g]
            m_new = jnp.maximum(m_old, _col_max(ss[g]))
            alpha = jnp.exp2(m_old - m_new)
            p = jnp.exp2((ss[g] - m_new).astype(BF16))
            new.append(alpha * old[g] + _dot(vaug, p))
            out.append(m_new)
        for g in pairs:
            accs[g][...] = new[g]
            s_scs[g][1 - slot] = ss_next[g]
        return tuple(out)

    ones_rows = jnp.ones((DENOM_ROWS, ka), BF16)
    init = (jnp.full((1, 2 * tq), NEG, F32),) * len(pairs)
    lax.fori_loop(0, n_sub, attn_chunk, init)
    for g in pairs:
        acc = accs[g][...]
        out_t = acc[:HEAD_DIM] * (1.0 / acc[HEAD_DIM:HEAD_DIM + 1])
        for j in range(2):
            hd = 2 * g + j
            o_ref[:, hd * HEAD_DIM:(hd + 1) * HEAD_DIM] = out_t[:, j * tq:(j + 1) * tq].T


def _dsa_attn(q_it, w_it, q_dt, k_i, k_d, v_dt, B, S):
    T = B * S
    tq = DSA_TQ
    nq = S // tq
    topk = min(DSA_MAX_TOPK, S // 4)
    qcol = lambda b, t: (0, b * nq + t)
    seq = lambda b, t: (b, 0)
    return pl.pallas_call(
        functools.partial(_dsa_kernel, topk=topk),
        out_shape=jax.ShapeDtypeStruct((T, DSA_W), F32),
        grid=(B, nq),
        in_specs=[
            pl.BlockSpec((IDX_HEADS * IDX_DIM, tq), qcol),
            pl.BlockSpec((IDX_HEADS, tq), qcol),
            pl.BlockSpec((DSA_W, tq), qcol),
            pl.BlockSpec((S, IDX_DIM), seq),
            pl.BlockSpec((S, HEAD_DIM), seq),
            pl.BlockSpec((S // DSA_KA, HEAD_DIM, DSA_KA), lambda b, t: (b, 0, 0)),
        ],
        out_specs=pl.BlockSpec((tq, DSA_W), lambda b, t: (b * nq + t, 0)),
        scratch_shapes=[pltpu.VMEM((S, tq), I32), pltpu.VMEM((S, tq), jnp.int16)]
        + [pltpu.VMEM((HEAD_DIM + DENOM_ROWS, 2 * tq), F32) for _ in range(DSA_HEADS // 2)]
        + [pltpu.VMEM((2, DSA_KA, 2 * tq), F32) for _ in range(DSA_HEADS // 2)],
        compiler_params=_cparams(("parallel", "arbitrary")),
        name="dsa_attn",
    )(q_it, w_it, q_dt, k_i, k_d, v_dt)


def _first_index_of_max(v, row_io, n_rows):
    m = jnp.max(v, axis=0, keepdims=True)
    idx = jnp.min(jnp.where(v == m, row_io, n_rows), axis=0, keepdims=True)
    return m, idx


def _post_kernel(x_ref, om_ref, od_ref, mod_ref, gm_ref, gd_ref, wout_ref, gffn_ref, wgu_ref, wds_ref,
                 wrh_ref, wrl_ref, rb_ref, tri_ref,
                 xpart_ref, h2r_ref, eidx_ref, rank_ref, gate_ref, cnt_ref, base_sc):
    i = pl.program_id(0)
    tm = x_ref.shape[0]
    gt1 = mod_ref[0, 2:3, :]
    sh2 = mod_ref[0, 3:4, :]
    sc2 = mod_ref[0, 4:5, :]
    gt2 = mod_ref[0, 5:6, :]

    mixed = jnp.concatenate([_rms(om_ref[...], gm_ref[...]), _rms(od_ref[...], gd_ref[...])], axis=1)
    x1 = x_ref[...] + gt1 * _dot(mixed.astype(BF16), wout_ref[...])
    h2 = _rms(x1, gffn_ref[...]) * (1.0 + sc2) + sh2
    h2b = h2.astype(BF16)

    au = _dot(h2b, wgu_ref[...])
    hs = (_silu(au[:, :D_SHARED]) * au[:, D_SHARED:]).astype(BF16)
    xpart_ref[...] = x1 + gt2 * _dot(hs, wds_ref[...])

    for c, slab in enumerate(_pack_row_words(h2)):
        _store_word_slab(h2r_ref, 0, tm, c, slab)

    h2lo = (h2 - h2b.astype(F32)).astype(BF16)
    logits = _dot_nt(wrh_ref[...], h2b) + _dot_nt(wrl_ref[...], h2b) + _dot_nt(wrh_ref[...], h2lo)
    scores = 1.0 / (1.0 + jnp.exp(-logits))
    biased = scores + rb_ref[...]

    g_io = lax.broadcasted_iota(I32, (GROUP_SIZE, tm), 0)
    gs_rows = []
    for g in range(N_GROUPS):
        blk = biased[g * GROUP_SIZE:(g + 1) * GROUP_SIZE, :]
        m1, i1 = _first_index_of_max(blk, g_io, GROUP_SIZE)
        m2 = jnp.max(jnp.where(g_io == i1, -jnp.inf, blk), axis=0, keepdims=True)
        gs_rows.append(m1 + m2)
    gs = jnp.concatenate(gs_rows, axis=0)
    gi = lax.broadcasted_iota(I32, (N_GROUPS, tm), 0)
    grank = jnp.zeros((N_GROUPS, tm), I32)
    for m in range(N_GROUPS):
        gm = gs[m:m + 1, :]
        grank = grank + jnp.where((gm > gs) | ((gm == gs) & (m < gi)), 1, 0)
    gsel = grank < TOPK_GROUPS
    masked = jnp.concatenate(
        [jnp.where(gsel[g:g + 1, :], biased[g * GROUP_SIZE:(g + 1) * GROUP_SIZE, :], -jnp.inf)
         for g in range(N_GROUPS)], axis=0)

    e_io = lax.broadcasted_iota(I32, (N_EXPERTS, tm), 0)
    e_rows, s_rows = [], []
    for _ in range(EXPERT_TOPK):
        _, idx = _first_index_of_max(masked, e_io, N_EXPERTS)
        hit = e_io == idx
        e_rows.append(idx)
        s_rows.append(jnp.sum(jnp.where(hit, scores, 0.0), axis=0, keepdims=True))
        masked = jnp.where(hit, -jnp.inf, masked)
    eidx = jnp.concatenate(e_rows, axis=0)
    sk = jnp.concatenate(s_rows, axis=0)
    gate_ref[...] = sk / jnp.sum(sk, axis=0, keepdims=True) * ROUTED_SCALE
    eidx_ref[...] = eidx

    @pl.when(i == 0)
    def _():
        base_sc[...] = jnp.zeros(base_sc.shape, F32)

    chosen = jnp.zeros((N_EXPERTS, tm), F32)
    for k in range(EXPERT_TOPK):
        chosen = chosen + jnp.where(e_io == e_rows[k], 1.0, 0.0)
    incl = _dot(chosen.astype(BF16), tri_ref[...])
    pos = base_sc[...] + incl - 1.0
    rank_ref[...] = jnp.concatenate(
        [jnp.sum(jnp.where(e_io == e_rows[k], pos, 0.0), axis=0, keepdims=True)
         for k in range(EXPERT_TOPK)], axis=0).astype(I32)
    base_sc[...] = base_sc[...] + incl[:, tm - 1:tm]
    cnt_ref[...] = jnp.broadcast_to(base_sc[...], cnt_ref.shape)


def _post_attn(x2, o_m, o_d, mod3, g_moba, g_dsa, w_out, g_ffn, w_gu_s, w_down_s, wr_hi, wr_lo, rbias, S):
    T, D = x2.shape
    tm = POST_TM
    nt_per_seq = S // tm
    row = lambda i: (i, 0)
    full = lambda i: (0, 0)
    tri = (jnp.arange(tm)[:, None] <= jnp.arange(tm)[None, :]).astype(BF16)
    n_words = D // ROW_WORDS
    return pl.pallas_call(
        _post_kernel,
        out_shape=[
            jax.ShapeDtypeStruct((T, D), F32),
            jax.ShapeDtypeStruct((T, n_words, LANES), I32),
            jax.ShapeDtypeStruct((EXPERT_TOPK, T), I32),
            jax.ShapeDtypeStruct((EXPERT_TOPK, T), I32),
            jax.ShapeDtypeStruct((EXPERT_TOPK, T), F32),
            jax.ShapeDtypeStruct((N_EXPERTS, LANES), F32),
        ],
        grid=(T // tm,),
        in_specs=[
            pl.BlockSpec((tm, D), row),
            pl.BlockSpec((tm, MOBA_W), row),
            pl.BlockSpec((tm, DSA_W), row),
            pl.BlockSpec((1, 6, D), lambda i: (i // nt_per_seq, 0, 0)),
            pl.BlockSpec((1, MOBA_W), full),
            pl.BlockSpec((1, DSA_W), full),
            pl.BlockSpec(w_out.shape, full),
            pl.BlockSpec((1, D), full),
            pl.BlockSpec(w_gu_s.shape, full),
            pl.BlockSpec(w_down_s.shape, full),
            pl.BlockSpec(wr_hi.shape, full),
            pl.BlockSpec(wr_lo.shape, full),
            pl.BlockSpec((N_EXPERTS, 1), full),
            pl.BlockSpec((tm, tm), full),
        ],
        out_specs=[
            pl.BlockSpec((tm, D), row),
            pl.BlockSpec((tm, n_words, LANES), lambda i: (i, 0, 0)),
            pl.BlockSpec((EXPERT_TOPK, tm), lambda i: (0, i)),
            pl.BlockSpec((EXPERT_TOPK, tm), lambda i: (0, i)),
            pl.BlockSpec((EXPERT_TOPK, tm), lambda i: (0, i)),
            pl.BlockSpec((N_EXPERTS, LANES), full),
        ],
        scratch_shapes=[pltpu.VMEM((N_EXPERTS, 1), F32)],
        compiler_params=_cparams(("arbitrary",)),
        name="post_attn",
    )(x2, o_m, o_d, mod3, g_moba, g_dsa, w_out, g_ffn, w_gu_s, w_down_s, wr_hi, wr_lo, rbias, tri)


def _row_copy_wait(rows_hbm, n_rows, sem):
    blk = rows_hbm.at[pl.ds(0, n_rows)]
    pltpu.make_async_copy(blk, blk, sem).wait()


def _slots_kernel(ps_ref, e_ref, r_ref, d_ref):
    e = e_ref[...]

    def body(x, acc):
        return jnp.where(e == x, ps_ref[x], acc)

    d_ref[...] = lax.fori_loop(0, N_EXPERTS, body, jnp.zeros(e.shape, I32)) + r_ref[...]


def _slots(pad_starts, eidx, rank):
    K, T = eidx.shape
    tm = min(T, 4096)
    blk = lambda: pl.BlockSpec((K, tm), lambda i, ps: (0, i))
    return pl.pallas_call(
        _slots_kernel,
        out_shape=jax.ShapeDtypeStruct((K, T), I32),
        grid_spec=pltpu.PrefetchScalarGridSpec(num_scalar_prefetch=1, grid=(T // tm,),
                                               in_specs=[blk(), blk()], out_specs=blk()),
        compiler_params=_cparams(("parallel",)),
        name="slots",
    )(pad_starts, eidx, rank)


def _dispatch_kernel(d_ref, h2r_ref, xs_ref, sem):
    tm = h2r_ref.shape[0]

    def body(t, carry):
        for k in range(EXPERT_TOPK):
            pltpu.make_async_copy(h2r_ref.at[t], xs_ref.at[d_ref[t * EXPERT_TOPK + k]],
                                  sem).start(priority=k % 2)
        return carry

    lax.fori_loop(0, tm, body, 0)
    _row_copy_wait(xs_ref, tm * EXPERT_TOPK, sem)


def _dispatch(dest_flat, h2r, n_rows_padded):
    T, n_words, _ = h2r.shape
    tm = DISP_TM
    return pl.pallas_call(
        _dispatch_kernel,
        out_shape=jax.ShapeDtypeStruct((n_rows_padded, n_words, LANES), I32),
        grid=(T // tm,),
        in_specs=[pl.BlockSpec((tm * EXPERT_TOPK,), lambda i: (i,), memory_space=pltpu.SMEM),
                  pl.BlockSpec((tm, n_words, LANES), lambda i: (i, 0, 0))],
        out_specs=pl.BlockSpec(memory_space=pl.ANY),
        scratch_shapes=[pltpu.SemaphoreType.DMA],
        compiler_params=_cparams(("arbitrary",), disable_bounds_checks=True),
        name="dispatch",
    )(dest_flat, h2r)


def _expert_kernel(ps_ref, cnt_ref, wg_ref, wu_ref, wd_ref, xs_ref, ys_ref,
                   wg_sc, wu_sc, wd_sc, xbuf, ybuf, sem_in, sem_out):
    e = pl.program_id(0)
    bm = EXP_BM
    n_words = xbuf.shape[1]
    cnt = cnt_ref[e]
    g0 = ps_ref[e] // bm
    nb = (cnt + bm - 1) // bm
    n_used = (ps_ref[N_EXPERTS - 1] + cnt_ref[N_EXPERTS - 1] + bm - 1) // bm

    def in_copy(g):
        slot = lax.rem(g, EXP_IN_SLOTS)
        return pltpu.make_async_copy(xs_ref.at[pl.ds(g * bm, bm)], xbuf.at[pl.ds(slot * bm, bm)],
                                     sem_in.at[slot])

    def out_copy(g):
        slot = lax.rem(g, EXP_OUT_SLOTS)
        return pltpu.make_async_copy(ybuf.at[pl.ds(slot * bm, bm)], ys_ref.at[pl.ds(g * bm, bm)],
                                     sem_out.at[slot])

    @pl.when(e == 0)
    def _():
        for g in range(EXP_AHEAD):
            @pl.when(g < n_used)
            def _():
                in_copy(g).start()

    @pl.when(nb > 0)
    def _():
        wg_sc[...] = wg_ref[...].astype(BF16)
        wu_sc[...] = wu_ref[...].astype(BF16)
        wd_sc[...] = wd_ref[...].astype(BF16)

        def body(b, carry):
            g = g0 + b

            @pl.when(g + EXP_AHEAD < n_used)
            def _():
                in_copy(g + EXP_AHEAD).start()

            in_copy(g).wait()

            @pl.when(g >= EXP_OUT_SLOTS)
            def _():
                out_copy(g - EXP_OUT_SLOTS).wait()

            xrow = lax.rem(g, EXP_IN_SLOTS) * bm
            yrow = lax.rem(g, EXP_OUT_SLOTS) * bm
            feats = []
            for c in range(n_words):
                feats += _unpack_row_words(_load_word_slab(xbuf, xrow, bm, c))
            x = jnp.concatenate(feats, axis=1)
            valid = lax.broadcasted_iota(I32, (bm, 1), 0) < cnt - b * bm
            xb = jnp.where(valid, x, 0.0).astype(BF16)
            a = _dot(xb, wg_sc[...])
            u = _dot(xb, wu_sc[...])
            hmid = (_silu(a) * u).astype(BF16)
            ob = _dot(hmid, wd_sc[...])
            for c, slab in enumerate(_pack_row_words(ob)):
                _store_word_slab(ybuf, yrow, bm, c, slab)
            out_copy(g).start()
            return carry

        lax.fori_loop(0, nb, body, 0)

    @pl.when(e == N_EXPERTS - 1)
    def _():
        for back in range(EXP_OUT_SLOTS, 0, -1):
            @pl.when(n_used >= back)
            def _():
                out_copy(n_used - back).wait()


def _experts(pad_starts, counts, xs, w_gate_e, w_up_e, w_down_e):
    E, D, DE = w_gate_e.shape
    n_words = xs.shape[1]
    bm = EXP_BM
    wsel = lambda e, ps, cnt: (e, 0, 0)
    anyspec = pl.BlockSpec(memory_space=pl.ANY)
    return pl.pallas_call(
        _expert_kernel,
        out_shape=jax.ShapeDtypeStruct(xs.shape, I32),
        grid_spec=pltpu.PrefetchScalarGridSpec(
            num_scalar_prefetch=2,
            grid=(E,),
            in_specs=[pl.BlockSpec((None, D, DE), wsel), pl.BlockSpec((None, D, DE), wsel),
                      pl.BlockSpec((None, DE, D), wsel), anyspec],
            out_specs=anyspec,
            scratch_shapes=[pltpu.VMEM((D, DE), BF16), pltpu.VMEM((D, DE), BF16), pltpu.VMEM((DE, D), BF16),
                            pltpu.VMEM((EXP_IN_SLOTS * bm, n_words, LANES), I32),
                            pltpu.VMEM((EXP_OUT_SLOTS * bm, n_words, LANES), I32),
                            pltpu.SemaphoreType.DMA((EXP_IN_SLOTS,)), pltpu.SemaphoreType.DMA((EXP_OUT_SLOTS,))],
        ),
        compiler_params=_cparams(("arbitrary",)),
        name="experts",
    )(pad_starts, counts, w_gate_e, w_up_e, w_down_e, xs)


def _combine_kernel(dcur_ref, dnxt_ref, g_ref, xpart_ref, mod_ref, gfin_ref, ys_ref, o_ref,
                    buf0, buf1, sem0, sem1):
    i = pl.program_id(0)
    n_steps = pl.num_programs(0)
    tm = xpart_ref.shape[0]
    n_words = buf0.shape[1]

    def issue(dest_ref, buf, sem):
        for t in range(tm):
            for k in range(EXPERT_TOPK):
                pltpu.make_async_copy(ys_ref.at[dest_ref[t * EXPERT_TOPK + k]], buf.at[k * tm + t],
                                      sem).start(priority=k % 2)

    def reduce_tile(buf):
        gt2 = mod_ref[0, 5:6, :]
        g = g_ref[...]
        cols = []
        for c in range(n_words):
            lo = jnp.zeros((tm, LANES), F32)
            hi = jnp.zeros((tm, LANES), F32)
            for k in range(EXPERT_TOPK):
                a, b = _unpack_row_words(_load_word_slab(buf, k * tm, tm, c))
                gk = g[:, k:k + 1]
                lo = lo + gk * a
                hi = hi + gk * b
            cols += [lo, hi]
        routed = jnp.concatenate(cols, axis=1)
        o_ref[...] = _rms(xpart_ref[...] + gt2 * routed, gfin_ref[...])

    @pl.when(i == 0)
    def _():
        issue(dcur_ref, buf0, sem0)

    for parity, (cur, nxt) in enumerate((((buf0, sem0), (buf1, sem1)), ((buf1, sem1), (buf0, sem0)))):
        @pl.when(i % 2 == parity)
        def _():
            _row_copy_wait(ys_ref, tm * EXPERT_TOPK, cur[1])
            issue(dnxt_ref, *nxt)
            reduce_tile(cur[0])

            @pl.when(i + 1 == n_steps)
            def _():
                _row_copy_wait(ys_ref, tm * EXPERT_TOPK, nxt[1])


def _combine(dest, gates_t, xpart, mod3, g_final, ys, S):
    T, D = xpart.shape
    tm = COMB_TM
    n_steps = T // tm
    nt_per_seq = S // tm
    n_words = ys.shape[1]
    return pl.pallas_call(
        _combine_kernel,
        out_shape=jax.ShapeDtypeStruct((T, D), F32),
        grid=(n_steps,),
        in_specs=[
            pl.BlockSpec((tm * EXPERT_TOPK,), lambda i: (i,), memory_space=pltpu.SMEM),
            pl.BlockSpec((tm * EXPERT_TOPK,), lambda i: (jnp.minimum(i + 1, n_steps - 1),),
                         memory_space=pltpu.SMEM),
            pl.BlockSpec((tm, EXPERT_TOPK), lambda i: (i, 0)),
            pl.BlockSpec((tm, D), lambda i: (i, 0)),
            pl.BlockSpec((1, 6, D), lambda i: (i // nt_per_seq, 0, 0)),
            pl.BlockSpec((1, D), lambda i: (0, 0)),
            pl.BlockSpec(memory_space=pl.ANY),
        ],
        out_specs=pl.BlockSpec((tm, D), lambda i: (i, 0)),
        scratch_shapes=[pltpu.VMEM((EXPERT_TOPK * tm, n_words, LANES), I32),
                        pltpu.VMEM((EXPERT_TOPK * tm, n_words, LANES), I32),
                        pltpu.SemaphoreType.DMA, pltpu.SemaphoreType.DMA],
        compiler_params=_cparams(("arbitrary",), disable_bounds_checks=True),
        name="combine",
    )(dest, dest, gates_t, xpart, mod3, g_final, ys)


def _layer(x2, mod3, S, g_mix, w_in, g_kv, w_kv_up, g_moba_out, g_dsa_out, w_out, g_ffn, w_router,
           router_bias, w_gate_e, w_up_e, w_down_e, w_gate_s, w_up_s, w_down_s, g_final, tab_h, half_h,
           tab_i, half_i):
    T, D = x2.shape
    B = T // S
    w_in_p = jnp.pad(w_in, ((0, 0), (0, _C_END - w_in.shape[1]))).astype(BF16)
    (q_mt, k_m, v_mt, kmean, q_dt, k_d, v_dt, q_it, k_i, w_it) = _in_proj(
        x2, mod3, g_mix.reshape(1, D), w_in_p, g_kv.reshape(1, KV_LORA), w_kv_up.astype(BF16),
        tab_h, half_h, tab_i, half_i, S)
    o_m = _moba_attn(q_mt, k_m, v_mt, kmean, B, S)
    o_d = _dsa_attn(q_it, w_it, q_dt, k_i, k_d, v_dt, B, S)

    wr_t = w_router.T
    wr_hi = wr_t.astype(BF16)
    wr_lo = (wr_t - wr_hi.astype(F32)).astype(BF16)
    w_gu_s = jnp.concatenate([w_gate_s, w_up_s], axis=1).astype(BF16)
    xpart, h2r, eidx, rank, gates, cnt = _post_attn(
        x2, o_m, o_d, mod3, g_moba_out.reshape(1, MOBA_W), g_dsa_out.reshape(1, DSA_W), w_out.astype(BF16),
        g_ffn.reshape(1, D), w_gu_s, w_down_s.astype(BF16), wr_hi, wr_lo,
        router_bias.reshape(N_EXPERTS, 1), S)

    bm = EXP_BM
    n_blocks = T * EXPERT_TOPK // bm + N_EXPERTS
    counts = cnt[:, 0].astype(I32)
    padded = (counts + bm - 1) // bm * bm
    pad_ends = jnp.cumsum(padded)
    pad_starts = (pad_ends - padded).astype(I32)
    dest_flat = _slots(pad_starts, eidx, rank).T.reshape(-1)
    xs = _dispatch(dest_flat, h2r, n_blocks * bm)
    ys = _experts(pad_starts, counts, xs, w_gate_e, w_up_e, w_down_e)
    return _combine(dest_flat, gates.T, xpart, mod3, g_final.reshape(1, D), ys, S)


def kernel(x, c, w_ada, b_ada, g_mix, w_in, g_kv, w_kv_up, g_moba_out, g_dsa_out, w_out, g_ffn, w_router,
           router_bias, w_gate_e, w_up_e, w_down_e, w_gate_s, w_up_s, w_down_s, g_final):
    B, S, D = x.shape
    depth = w_ada.shape[0]
    assert depth == 1, "the final norm is fused into the single layer"
    assert S % PROJ_TM == 0 and S % DSA_KC == 0 and S % POST_TM == 0 and S >= 4 * DSA_MAX_TOPK
    tab_h, half_h = _rope_tables(S, HEAD_DIM, 1)
    tab_i, half_i = _rope_tables(S, IDX_DIM, LANES // IDX_DIM)
    x2 = x.reshape(B * S, D)
    sq = lambda a: a.reshape(a.shape[1:])
    mod3 = _ada_mod(c, sq(w_ada), sq(b_ada)).reshape(B, 6, D)
    out = _layer(x2, mod3, S, sq(g_mix), sq(w_in), sq(g_kv), sq(w_kv_up), sq(g_moba_out), sq(g_dsa_out),
                 sq(w_out), sq(g_ffn), sq(w_router), sq(router_bias), sq(w_gate_e), sq(w_up_e), sq(w_down_e),
                 sq(w_gate_s), sq(w_up_s), sq(w_down_s), g_final, tab_h, half_h, tab_i, half_i)
    return out.reshape(B, S, D)
```
